```python
import jax, jax.numpy as jnp
from jax import lax
import numpy as np

D_MODEL = 1024
BATCH = 16
SEQ = 2048
DEPTH = 4

PE_DIM = 256
D_MIX = D_MODEL
N_MIXERS = 4
GROUP_W = D_MIX // N_MIXERS
HEAD_DIM = 64
GROUP_HEADS = GROUP_W // HEAD_DIM
CONV_K = 4
MLSTM_CHUNK = 64
RG_C = 8.0
RWKV_DECAY_RANK = 32
RWKV_ICL_RANK = 32
RWKV_GATE_RANK = 64
RWKV_GN_EPS = 64e-5
SSD_STATE = 64
SSD_GROUPS = 2
SSD_CHUNK = 128
SSD_XBC = GROUP_W + 2 * SSD_GROUPS * SSD_STATE
M_COLS = 4 * GROUP_W + 2 * GROUP_HEADS
G_COLS = 2 * GROUP_W
R_COLS = 3 * GROUP_W + RWKV_DECAY_RANK + RWKV_ICL_RANK + RWKV_GATE_RANK
S_COLS = GROUP_W + SSD_XBC + GROUP_HEADS
N_IN = M_COLS + G_COLS + R_COLS + S_COLS
D_FF = 256 * ((8 * D_MODEL // 3 + 255) // 256)
N_EXPERTS = 8
TOP_K = 2
EXPERT_FF = D_FF
MOE_BLOCK = 256
N_DENSE = (DEPTH + 1) // 2
N_MOE = DEPTH // 2
ALPHA = (2 * DEPTH) ** 0.25
BETA = (8 * DEPTH) ** -0.25
LN_EPS = 1e-5
RMS_EPS = 1e-6

kernel_name = 'hymba_style_hybrid_mlstm_rglru_rwkv7_ssd_moe'


def _layer_norm(x, g, b):
    xf = x.astype(jnp.float32)
    xc = xf - jnp.mean(xf, -1, keepdims=True)
    var = jnp.mean(xc * xc, -1, keepdims=True)
    return (xc * lax.rsqrt(var + LN_EPS)).astype(x.dtype) * g + b


def _rms(x, w):
    xf = x.astype(jnp.float32)
    y = xf * lax.rsqrt(jnp.mean(xf * xf, -1, keepdims=True) + RMS_EPS)
    return y.astype(x.dtype) * w


def _causal_dwconv(u, w, b):
    k, c = w.shape
    out = lax.conv_general_dilated(u, w[:, None, :].astype(u.dtype), window_strides=(1,),
                                   padding=[(k - 1, 0)], dimension_numbers=('NWC', 'WIO', 'NWC'),
                                   feature_group_count=c)
    return out + b


def _linrec_combine(left, right):
    a_l, b_l = left
    a_r, b_r = right
    return a_l * a_r, a_r * b_l + b_r


def mlstm_group(cols, i_bias, f_bias, norm_w):
    bsz, seq, _ = cols.shape
    L, H, dh = MLSTM_CHUNK, GROUP_HEADS, HEAD_DIM
    nc = seq // L
    W = GROUP_W
    q, k, v, o, i_pre, f_pre = jnp.split(cols, [W, 2 * W, 3 * W, 4 * W, 4 * W + H], axis=-1)

    def heads(t):
        return t.reshape(bsz, nc, L, H, dh).transpose(0, 3, 1, 2, 4)

    def gates(t):
        return t.astype(jnp.float32).reshape(bsz, nc, L, H).transpose(0, 3, 1, 2)

    qh, kh, vh = heads(q), heads(k) * (dh ** -0.5), heads(v)
    log_i = gates(i_pre + i_bias)
    log_f = jax.nn.log_sigmoid(gates(f_pre + f_bias))
    b = jnp.cumsum(log_f, axis=-1)
    g = b[..., -1]
    a = g[..., None] - b + log_i

    def chunk_step(carry, xs):
        C, n, m = carry
        k_c, v_c, a_c, g_c = xs
        m_new = jnp.maximum(g_c + m, a_c.max(-1))
        decay = jnp.exp(g_c + m - m_new)
        w = jnp.exp(a_c - m_new[..., None])
        C_new = decay[..., None, None] * C + jnp.einsum('bhl,bhlv,bhlk->bhvk', w, v_c, k_c)
        n_new = decay[..., None] * n + jnp.einsum('bhl,bhlk->bhk', w, k_c)
        return (C_new, n_new, m_new), (C, n, m)

    init = (jnp.zeros((bsz, H, dh, dh), jnp.float32), jnp.zeros((bsz, H, dh), jnp.float32),
            jnp.zeros((bsz, H), jnp.float32))
    xs = (kh.transpose(2, 0, 1, 3, 4), vh.transpose(2, 0, 1, 3, 4), a.transpose(2, 0, 1, 3), g.transpose(2, 0, 1))
    _, (C_prev, n_prev, m_prev) = lax.scan(chunk_step, init, xs)
    C_prev = C_prev.transpose(1, 2, 0, 3, 4)
    n_prev = n_prev.transpose(1, 2, 0, 3)
    m_prev = m_prev.transpose(1, 2, 0)

    causal = jnp.tril(jnp.ones((L, L), bool))
    log_D = jnp.where(causal, b[..., :, None] - b[..., None, :] + log_i[..., None, :], -jnp.inf)
    inter_log = b + m_prev[..., None]
    m_t = jnp.maximum(inter_log, log_D.max(-1))
    s = jnp.einsum('bhcld,bhcsd->bhcls', qh, kh) * jnp.exp(log_D - m_t[..., None])
    inter_w = jnp.exp(inter_log - m_t)
    num = jnp.einsum('bhcls,bhcsd->bhcld', s, vh) + inter_w[..., None] * jnp.einsum('bhcvk,bhclk->bhclv', C_prev, qh)
    den = s.sum(-1) + inter_w * jnp.einsum('bhck,bhclk->bhcl', n_prev, qh)
    h = num / jnp.maximum(jnp.abs(den), jnp.exp(-m_t))[..., None]
    h = h.transpose(0, 2, 3, 1, 4).reshape(bsz, seq, H, dh)
    h = _rms(h.astype(cols.dtype), norm_w.reshape(H, dh)).reshape(bsz, seq, W)
    return h * jax.nn.sigmoid(o)


def rglru_group(cols, conv_w, conv_b, w_a, b_a, w_x, b_x, lam):
    bsz, seq, _ = cols.shape
    xb, gate = jnp.split(cols, [GROUP_W], axis=-1)
    xc = _causal_dwconv(xb, conv_w, conv_b)
    xh = xc.reshape(bsz, seq, GROUP_HEADS, HEAD_DIM)
    r = jax.nn.sigmoid(jnp.einsum('bshi,hij->bshj', xh, w_a).reshape(bsz, seq, GROUP_W) + b_a)
    ig = jax.nn.sigmoid(jnp.einsum('bshi,hij->bshj', xh, w_x).reshape(bsz, seq, GROUP_W) + b_x)
    log_a = (-RG_C * jax.nn.softplus(-lam) * r).astype(jnp.float32)
    a = jnp.exp(log_a)
    u = jnp.sqrt(-jnp.expm1(2.0 * log_a)) * (ig * xc)
    _, h = lax.associative_scan(_linrec_combine, (a, u), axis=1)
    return h.astype(cols.dtype) * jax.nn.gelu(gate)


def rwkv7_group(cols, mu, w0, w2, a0, a2, g2, k_k, k_a, r_k, ln_w, ln_b):
    bsz, seq, _ = cols.shape
    H, N, W = GROUP_HEADS, HEAD_DIM, GROUP_W
    prev = jnp.pad(cols[:, :-1], ((0, 0), (1, 0), (0, 0)))
    cols = cols + (prev - cols) * mu
    r, k, v, wd, ad, gd = jnp.split(
        cols, [W, 2 * W, 3 * W, 3 * W + RWKV_DECAY_RANK, 3 * W + RWKV_DECAY_RANK + RWKV_ICL_RANK], axis=-1)
    w_log = -jax.nn.softplus(-(w0 + jnp.tanh(wd) @ w2)) - 0.5
    decay = jnp.exp(-jnp.exp(w_log.astype(jnp.float32)))
    a = jax.nn.sigmoid(a0 + ad @ a2)
    g = jax.nn.sigmoid(gd) @ g2

    def hd(t):
        return t.reshape(bsz, seq, H, N)

    kk = hd(k * k_k).astype(jnp.float32)
    kk = kk * lax.rsqrt(jnp.maximum(jnp.sum(kk * kk, -1, keepdims=True), 1e-24))
    k = k * (1.0 + (a - 1.0) * k_a)
    rh, kh, vh = hd(r), hd(k), hd(v)

    def step(S, xs):
        r_t, d_t, k_t, v_t, kk_t, a_t = xs
        sa = jnp.einsum('bhvk,bhk->bhv', S, kk_t)
        S = (S * d_t[:, :, None, :] - sa[..., None] * (kk_t * a_t)[:, :, None, :]
             + v_t[..., None] * k_t[:, :, None, :])
        return S, jnp.einsum('bhvk,bhk->bhv', S, r_t)

    def tm(t):
        return t.astype(jnp.float32).transpose(1, 0, 2, 3)

    S0 = jnp.zeros((bsz, H, N, N), jnp.float32)
    _, y = lax.scan(step, S0, (tm(rh), tm(hd(decay)), tm(kh), tm(vh), tm(kk), tm(hd(a))))
    y = y.transpose(1, 0, 2, 3)
    yc = y - jnp.mean(y, -1, keepdims=True)
    yn = yc * lax.rsqrt(jnp.mean(yc * yc, -1, keepdims=True) + RWKV_GN_EPS)
    yn = yn.reshape(bsz, seq, W).astype(cols.dtype) * ln_w + ln_b
    bonus = jnp.sum(rh * kh * r_k, -1, keepdims=True) * vh
    return (yn + bonus.reshape(bsz, seq, W)) * g


def ssd_group(cols, conv_w, conv_b, dt_bias, a_log, d_skip, norm_w):
    bsz, seq, _ = cols.shape
    H, P, G, N, L = GROUP_HEADS, HEAD_DIM, SSD_GROUPS, SSD_STATE, SSD_CHUNK
    nc = seq // L
    z, xbc, dt = jnp.split(cols, [GROUP_W, GROUP_W + SSD_XBC], axis=-1)
    xbc = jax.nn.silu(_causal_dwconv(xbc, conv_w, conv_b))
    xs, bm, cm = jnp.split(xbc, [GROUP_W, GROUP_W + G * N], axis=-1)
    x = xs.reshape(bsz, nc, L, H, P)
    rep = H // G
    bm = jnp.repeat(bm.reshape(bsz, nc, L, G, N), rep, axis=3)
    cm = jnp.repeat(cm.reshape(bsz, nc, L, G, N), rep, axis=3)
    dt = jax.nn.softplus((dt + dt_bias).astype(jnp.float32)).reshape(bsz, nc, L, H)
    a_dt = (-jnp.exp(a_log.astype(jnp.float32)) * dt).transpose(0, 3, 1, 2)
    a_cs = jnp.cumsum(a_dt, axis=-1)
    causal = jnp.tril(jnp.ones((L, L), bool))
    decay_in = jnp.exp(jnp.where(causal, a_cs[..., :, None] - a_cs[..., None, :], -jnp.inf))
    xdt = x * dt[..., None]
    scores = jnp.einsum('bclhn,bcshn->bhcls', cm, bm) * decay_in
    y_diag = jnp.einsum('bhcls,bcshp->bclhp', scores, xdt)
    decay_states = jnp.exp(a_cs[..., -1:] - a_cs)
    states = jnp.einsum('bclhn,bhcl,bclhp->bchpn', bm, decay_states, xdt)
    tot_cs = jnp.cumsum(jnp.pad(a_cs[..., -1], ((0, 0), (0, 0), (1, 0))), axis=-1)
    chunk_causal = jnp.tril(jnp.ones((nc + 1, nc + 1), bool))
    decay_chunk = jnp.exp(jnp.where(chunk_causal, tot_cs[..., :, None] - tot_cs[..., None, :], -jnp.inf))
    states = jnp.pad(states, ((0, 0), (1, 0), (0, 0), (0, 0), (0, 0)))
    states_in = jnp.einsum('bhzc,bchpn->bzhpn', decay_chunk, states)[:, :-1]
    y_off = jnp.einsum('bclhn,bchpn,bhcl->bclhp', cm, states_in, jnp.exp(a_cs))
    y = y_diag + y_off + x * d_skip[:, None]
    y = y.reshape(bsz, seq, GROUP_W).astype(cols.dtype) * jax.nn.silu(z)
    y = _rms(y.reshape(bsz, seq, G, GROUP_W // G), norm_w.reshape(G, GROUP_W // G))
    return y.reshape(bsz, seq, GROUP_W)


def _swiglu(x, w1, w3, w2):
    return (jax.nn.silu(x @ w1) * (x @ w3)) @ w2


def _moe_swiglu(x, w_router, w1, w3, w2):
    bsz, seq, d = x.shape
    T = bsz * seq
    xf = x.reshape(T, d)
    logits = (xf @ w_router).astype(jnp.float32)
    top_logit, top_e = lax.top_k(logits, TOP_K)
    gate = jax.nn.softmax(top_logit, axis=-1).astype(x.dtype)
    flat_e = top_e.reshape(-1)
    flat_tok = jnp.repeat(jnp.arange(T, dtype=jnp.int32), TOP_K)
    order = jnp.argsort(flat_e)
    se, st, sg = flat_e[order], flat_tok[order], gate.reshape(-1)[order]
    counts = jnp.bincount(flat_e, length=N_EXPERTS)
    padded = (counts + MOE_BLOCK - 1) // MOE_BLOCK * MOE_BLOCK
    pad_end = jnp.cumsum(padded)
    start = jnp.cumsum(counts) - counts
    dest = (pad_end - padded)[se] + jnp.arange(T * TOP_K) - start[se]
    n_blocks = -(-(T * TOP_K) // MOE_BLOCK) + N_EXPERTS
    rows = n_blocks * MOE_BLOCK
    slot_tok = jnp.zeros((rows,), jnp.int32).at[dest].set(st)
    slot_gate = jnp.zeros((rows,), x.dtype).at[dest].set(sg)
    block_e = jnp.minimum(jnp.searchsorted(pad_end, jnp.arange(n_blocks) * MOE_BLOCK, side='right'), N_EXPERTS - 1)
    xb = xf[slot_tok].reshape(n_blocks, MOE_BLOCK, d)

    def expert_block(args):
        xblk, e = args
        return (jax.nn.silu(xblk @ w1[e]) * (xblk @ w3[e])) @ w2[e]

    yb = lax.map(expert_block, (xb, block_e))
    y = jnp.zeros((T, d), x.dtype).at[slot_tok].add(yb.reshape(rows, d) * slot_gate[:, None])
    return y.reshape(bsz, seq, d)


def setup_inputs(seed: int = 0) -> dict:
    key = jax.random.key(seed)
    ks = iter(jax.random.split(key, 64))

    def nrm(shape, scale):
        return jax.random.normal(next(ks), shape, jnp.float32) * scale

    def unif(shape, lo, hi):
        return jax.random.uniform(next(ks), shape, jnp.float32, lo, hi)

    Lyr, W, H, P = DEPTH, GROUP_W, GROUP_HEADS, HEAD_DIM
    x = nrm((BATCH, SEQ, D_MODEL), 1.0)
    p = nrm((DEPTH, BATCH, SEQ, PE_DIM), 1.0)
    w_in = nrm((Lyr, D_MODEL, N_IN), D_MODEL ** -0.5)
    m_i_bias = nrm((Lyr, H), 0.1)
    m_f_bias = jnp.linspace(3.0, 6.0, H)[None, :] + nrm((Lyr, H), 0.1)
    m_norm_w = 1.0 + nrm((Lyr, W), 0.02)
    g_conv_w = nrm((Lyr, CONV_K, W), CONV_K ** -0.5)
    g_conv_b = nrm((Lyr, W), 0.01)
    g_w_a = nrm((Lyr, H, P, P), P ** -0.5)
    g_b_a = nrm((Lyr, W), 0.01)
    g_w_x = nrm((Lyr, H, P, P), P ** -0.5)
    g_b_x = nrm((Lyr, W), 0.01)
    s_pow = unif((Lyr, W), 0.9, 0.999) ** (1.0 / RG_C)
    g_lambda = jnp.log(s_pow) - jnp.log1p(-s_pow)
    r_mu = unif((Lyr, R_COLS), 0.0, 1.0)
    r_w0 = (-6.0 + 5.0 * (jnp.arange(W, dtype=jnp.float32) / (W - 1)) ** 0.85)[None, :] + nrm((Lyr, W), 0.1)
    r_w2 = nrm((Lyr, RWKV_DECAY_RANK, W), 0.1 * RWKV_DECAY_RANK ** -0.5)
    r_a0 = nrm((Lyr, W), 0.1)
    r_a2 = nrm((Lyr, RWKV_ICL_RANK, W), RWKV_ICL_RANK ** -0.5)
    r_g2 = nrm((Lyr, RWKV_GATE_RANK, W), RWKV_GATE_RANK ** -0.5)
    r_k_k = 0.85 + nrm((Lyr, W), 0.02)
    r_k_a = 1.0 + nrm((Lyr, W), 0.02)
    r_r_k = -0.04 + nrm((Lyr, H, P), 0.01)
    r_ln_w = 1.0 + nrm((Lyr, W), 0.02)
    r_ln_b = nrm((Lyr, W), 0.01)
    s_conv_w = nrm((Lyr, CONV_K, SSD_XBC), CONV_K ** -0.5)
    s_conv_b = nrm((Lyr, SSD_XBC), 0.01)
    dt0 = jnp.exp(unif((Lyr, H), float(np.log(1e-3)), float(np.log(1e-1))))
    s_dt_bias = dt0 + jnp.log(-jnp.expm1(-dt0))
    s_a_log = jnp.log(unif((Lyr, H), 1.0, 16.0))
    s_d = 1.0 + nrm((Lyr, H), 0.1)
    s_norm_w = 1.0 + nrm((Lyr, W), 0.02)
    w_out = nrm((Lyr, D_MIX, D_MODEL), D_MIX ** -0.5 * BETA)
    ln1_g = 1.0 + nrm((Lyr, D_MODEL), 0.02)
    ln1_b = nrm((Lyr, D_MODEL), 0.01)
    ln2_g = 1.0 + nrm((Lyr, D_MODEL), 0.02)
    ln2_b = nrm((Lyr, D_MODEL), 0.01)
    f_w1 = nrm((N_DENSE, D_MODEL, D_FF), D_MODEL ** -0.5)
    f_w3 = nrm((N_DENSE, D_MODEL, D_FF), D_MODEL ** -0.5)
    f_w2 = nrm((N_DENSE, D_FF, D_MODEL), D_FF ** -0.5 * BETA)
    e_router = nrm((N_MOE, D_MODEL, N_EXPERTS), D_MODEL ** -0.5)
    e_w1 = nrm((N_MOE, N_EXPERTS, D_MODEL, EXPERT_FF), D_MODEL ** -0.5)
    e_w3 = nrm((N_MOE, N_EXPERTS, D_MODEL, EXPERT_FF), D_MODEL ** -0.5)
    e_w2 = nrm((N_MOE, N_EXPERTS, EXPERT_FF, D_MODEL), EXPERT_FF ** -0.5 * BETA)
    pe_proj = nrm((Lyr, PE_DIM, D_MODEL), PE_DIM ** -0.5 * BETA)
    pe_gate_w = nrm((Lyr, D_MODEL, D_MODEL), D_MODEL ** -0.5)
    pe_gate_b = nrm((Lyr, D_MODEL), 0.01)
    return {'x': x, 'p': p, 'w_in': w_in, 'm_i_bias': m_i_bias, 'm_f_bias': m_f_bias, 'm_norm_w': m_norm_w,
            'g_conv_w': g_conv_w, 'g_conv_b': g_conv_b, 'g_w_a': g_w_a, 'g_b_a': g_b_a, 'g_w_x': g_w_x,
            'g_b_x': g_b_x, 'g_lambda': g_lambda, 'r_mu': r_mu, 'r_w0': r_w0, 'r_w2': r_w2, 'r_a0': r_a0,
            'r_a2': r_a2, 'r_g2': r_g2, 'r_k_k': r_k_k, 'r_k_a': r_k_a, 'r_r_k': r_r_k, 'r_ln_w': r_ln_w,
            'r_ln_b': r_ln_b, 's_conv_w': s_conv_w, 's_conv_b': s_conv_b, 's_dt_bias': s_dt_bias,
            's_a_log': s_a_log, 's_d': s_d, 's_norm_w': s_norm_w, 'w_out': w_out, 'ln1_g': ln1_g,
            'ln1_b': ln1_b, 'ln2_g': ln2_g, 'ln2_b': ln2_b, 'f_w1': f_w1, 'f_w3': f_w3, 'f_w2': f_w2,
            'e_router': e_router, 'e_w1': e_w1, 'e_w3': e_w3, 'e_w2': e_w2, 'pe_proj': pe_proj,
            'pe_gate_w': pe_gate_w, 'pe_gate_b': pe_gate_b}


def reference(x, p, w_in, m_i_bias, m_f_bias, m_norm_w, g_conv_w, g_conv_b, g_w_a, g_b_a, g_w_x, g_b_x,
              g_lambda, r_mu, r_w0, r_w2, r_a0, r_a2, r_g2, r_k_k, r_k_a, r_r_k, r_ln_w, r_ln_b,
              s_conv_w, s_conv_b, s_dt_bias, s_a_log, s_d, s_norm_w, w_out, ln1_g, ln1_b, ln2_g, ln2_b,
              f_w1, f_w3, f_w2, e_router, e_w1, e_w3, e_w2, pe_proj, pe_gate_w, pe_gate_b):
    splits = [M_COLS, M_COLS + G_COLS, M_COLS + G_COLS + R_COLS]
    for i in range(DEPTH):
        proj = x @ w_in[i]
        m_cols, g_cols, r_cols, s_cols = jnp.split(proj, splits, axis=-1)
        mix = jnp.concatenate([
            mlstm_group(m_cols, m_i_bias[i], m_f_bias[i], m_norm_w[i]),
            rglru_group(g_cols, g_conv_w[i], g_conv_b[i], g_w_a[i], g_b_a[i], g_w_x[i], g_b_x[i], g_lambda[i]),
            rwkv7_group(r_cols, r_mu[i], r_w0[i], r_w2[i], r_a0[i], r_a2[i], r_g2[i], r_k_k[i], r_k_a[i],
                        r_r_k[i], r_ln_w[i], r_ln_b[i]),
            ssd_group(s_cols, s_conv_w[i], s_conv_b[i], s_dt_bias[i], s_a_log[i], s_d[i], s_norm_w[i]),
        ], axis=-1)
        x = _layer_norm(ALPHA * x + mix @ w_out[i], ln1_g[i], ln1_b[i])
        if i % 2 == 0:
            ff = _swiglu(x, f_w1[i // 2], f_w3[i // 2], f_w2[i // 2])
        else:
            ff = _moe_swiglu(x, e_router[i // 2], e_w1[i // 2], e_w3[i // 2], e_w2[i // 2])
        pe = (p[i] @ pe_proj[i]) * jax.nn.sigmoid(x @ pe_gate_w[i] + pe_gate_b[i])
        x = _layer_norm(ALPHA * x + ff + pe, ln2_g[i], ln2_b[i])
    return x
```

```python
import functools

import jax
import jax.numpy as jnp
from jax import lax
from jax.experimental import pallas as pl
from jax.experimental.pallas import tpu as pltpu

D_MODEL = 1024
DEPTH = 4
PE_DIM = 256
GROUP_W = 256
HEAD_DIM = 64
GROUP_HEADS = 4
CONV_K = 4
MLSTM_CHUNK = 64
RG_C = 8.0
RWKV_DECAY_RANK = 32
RWKV_ICL_RANK = 32
RWKV_GATE_RANK = 64
RWKV_GN_EPS = 64e-5
SSD_STATE = 64
SSD_GROUPS = 2
SSD_CHUNK = 128
SSD_XBC = GROUP_W + 2 * SSD_GROUPS * SSD_STATE
M_COLS = 4 * GROUP_W + 2 * GROUP_HEADS
G_COLS = 2 * GROUP_W
R_COLS = 3 * GROUP_W + RWKV_DECAY_RANK + RWKV_ICL_RANK + RWKV_GATE_RANK
S_COLS = GROUP_W + SSD_XBC + GROUP_HEADS
N_IN = M_COLS + G_COLS + R_COLS + S_COLS
N_EXPERTS = 8
TOP_K = 2
MOE_BLOCK = 256
ALPHA = (2 * DEPTH) ** 0.25
LN_EPS = 1e-5
RMS_EPS = 1e-6

LANES = 128
SUBLANES = 8
VMEM_LIMIT = 56 * 1024 * 1024


def _mm_kernel(a_ref, b_ref, o_ref):
    o_ref[...] = jnp.dot(a_ref[...].astype(jnp.bfloat16), b_ref[...],
                         preferred_element_type=jnp.float32)


def _matmul(a, b, tm=512, tn=512):
    m, k = a.shape
    _, n = b.shape
    n_pad = -(-n // tn) * tn
    b = b.astype(jnp.bfloat16)
    if n_pad != n:
        b = jnp.pad(b, ((0, 0), (0, n_pad - n)))
    out = pl.pallas_call(
        _mm_kernel,
        out_shape=jax.ShapeDtypeStruct((m, n_pad), jnp.float32),
        grid=(m // tm, n_pad // tn),
        in_specs=[pl.BlockSpec((tm, k), lambda i, j: (i, 0)),
                  pl.BlockSpec((k, tn), lambda i, j: (0, j))],
        out_specs=pl.BlockSpec((tm, tn), lambda i, j: (i, j)),
        compiler_params=pltpu.CompilerParams(
            dimension_semantics=("parallel", "parallel"), vmem_limit_bytes=VMEM_LIMIT),
        name="matmul",
    )(a, b)
    return out[:, :n] if n_pad != n else out


def _rwkv_scan_kernel(d_ref, kka_ref, km_ref, r_ref, kkn_ref, v_ref, y_ref, s_ref, sa_ref, *, tb, nk, nacc):
    @pl.when(pl.program_id(0) == 0)
    def _():
        s_ref[...] = jnp.zeros_like(s_ref)
        sa_ref[...] = jnp.zeros_like(sa_ref)

    vrows = v_ref.shape[1]

    def bcast(ref, t, k):
        return jnp.broadcast_to(ref[t, pl.ds(k, 1), :], (vrows, LANES))

    def step(t, sa):
        v_t = v_ref[t]
        y_acc = [None] * nacc
        sa_acc = [None] * nacc
        for k in range(nk):
            s_k = s_ref[k] * bcast(d_ref, t, k) - sa * bcast(kka_ref, t, k) + v_t * bcast(km_ref, t, k)
            s_ref[k] = s_k
            y_k = s_k * bcast(r_ref, t, k)
            n_k = s_k * bcast(kkn_ref, t, k)
            a = k % nacc
            y_acc[a] = y_k if y_acc[a] is None else y_acc[a] + y_k
            sa_acc[a] = n_k if sa_acc[a] is None else sa_acc[a] + n_k
        y_ref[t] = functools.reduce(lambda p, q: p + q, y_acc)
        return functools.reduce(lambda p, q: p + q, sa_acc)

    sa_ref[...] = lax.fori_loop(0, tb, step, sa_ref[...])


def _rwkv_scan(r, d, kmod, v, kk, a, tb=32):
    bsz, seq, w = r.shape
    h, n = GROUP_HEADS, HEAD_DIM
    chains = bsz * h
    assert 2 * chains == LANES
    half = n // 2

    def klay(t):
        t = t.reshape(bsz, seq, h, n).transpose(1, 3, 0, 2).reshape(seq, n, chains)
        return jnp.concatenate([t, t], axis=-1)

    kk_next = jnp.concatenate([kk[:, 1:], jnp.zeros_like(kk[:, :1])], axis=1)
    v_l = v.reshape(bsz, seq, h, 2, half).transpose(1, 4, 3, 0, 2).reshape(seq, half, LANES)
    kspec = pl.BlockSpec((tb, n, LANES), lambda i: (i, 0, 0))
    vspec = pl.BlockSpec((tb, half, LANES), lambda i: (i, 0, 0))
    y = pl.pallas_call(
        functools.partial(_rwkv_scan_kernel, tb=tb, nk=n, nacc=2),
        out_shape=jax.ShapeDtypeStruct((seq, half, LANES), jnp.float32),
        grid=(seq // tb,),
        in_specs=[kspec, kspec, kspec, kspec, kspec, vspec],
        out_specs=vspec,
        scratch_shapes=[pltpu.VMEM((n, half, LANES), jnp.float32),
                        pltpu.VMEM((half, LANES), jnp.float32)],
        compiler_params=pltpu.CompilerParams(
            dimension_semantics=("arbitrary",), vmem_limit_bytes=VMEM_LIMIT),
        name="rwkv_scan",
    )(klay(d), klay(kk * a), klay(kmod), klay(r), klay(kk_next), v_l)
    return y.reshape(seq, half, 2, bsz, h).transpose(3, 0, 4, 2, 1).reshape(bsz, seq, w)


def _layer_norm(x, g, b):
    xc = x - jnp.mean(x, -1, keepdims=True)
    var = jnp.mean(xc * xc, -1, keepdims=True)
    return (xc * lax.rsqrt(var + LN_EPS)) * g + b


def _rms(x, w):
    return x * lax.rsqrt(jnp.mean(x * x, -1, keepdims=True) + RMS_EPS) * w


def _causal_dwconv(u, w, b):
    k, c = w.shape
    out = lax.conv_general_dilated(u, w[:, None, :], window_strides=(1,), padding=[(k - 1, 0)],
                                   dimension_numbers=('NWC', 'WIO', 'NWC'), feature_group_count=c)
    return out + b


def _linrec_combine(left, right):
    a_l, b_l = left
    a_r, b_r = right
    return a_l * a_r, a_r * b_l + b_r


def _mlstm_group(cols, i_bias, f_bias, norm_w):
    bsz, seq, _ = cols.shape
    L, H, dh = MLSTM_CHUNK, GROUP_HEADS, HEAD_DIM
    nc = seq // L
    W = GROUP_W
    q, k, v, o, i_pre, f_pre = jnp.split(cols, [W, 2 * W, 3 * W, 4 * W, 4 * W + H], axis=-1)

    def heads(t):
        return t.reshape(bsz, nc, L, H, dh).transpose(0, 3, 1, 2, 4)

    def gates(t):
        return t.reshape(bsz, nc, L, H).transpose(0, 3, 1, 2)

    qh, kh, vh = heads(q), heads(k) * (dh ** -0.5), heads(v)
    log_i = gates(i_pre + i_bias)
    log_f = jax.nn.log_sigmoid(gates(f_pre + f_bias))
    b = jnp.cumsum(log_f, axis=-1)
    g = b[..., -1]
    a = g[..., None] - b + log_i

    def chunk_step(carry, xs):
        C, n, m = carry
        k_c, v_c, a_c, g_c = xs
        m_new = jnp.maximum(g_c + m, a_c.max(-1))
        decay = jnp.exp(g_c + m - m_new)
        w = jnp.exp(a_c - m_new[..., None])
        C_new = decay[..., None, None] * C + jnp.einsum('bhl,bhlv,bhlk->bhvk', w, v_c, k_c)
        n_new = decay[..., None] * n + jnp.einsum('bhl,bhlk->bhk', w, k_c)
        return (C_new, n_new, m_new), (C, n, m)

    init = (jnp.zeros((bsz, H, dh, dh), jnp.float32), jnp.zeros((bsz, H, dh), jnp.float32),
            jnp.zeros((bsz, H), jnp.float32))
    xs = (kh.transpose(2, 0, 1, 3, 4), vh.transpose(2, 0, 1, 3, 4), a.transpose(2, 0, 1, 3), g.transpose(2, 0, 1))
    _, (C_prev, n_prev, m_prev) = lax.scan(chunk_step, init, xs)
    C_prev = C_prev.transpose(1, 2, 0, 3, 4)
    n_prev = n_prev.transpose(1, 2, 0, 3)
    m_prev = m_prev.transpose(1, 2, 0)

    causal = jnp.tril(jnp.ones((L, L), bool))
    log_D = jnp.where(causal, b[..., :, None] - b[..., None, :] + log_i[..., None, :], -jnp.inf)
    inter_log = b + m_prev[..., None]
    m_t = jnp.maximum(inter_log, log_D.max(-1))
    s = jnp.einsum('bhcld,bhcsd->bhcls', qh, kh) * jnp.exp(log_D - m_t[..., None])
    inter_w = jnp.exp(inter_log - m_t)
    num = jnp.einsum('bhcls,bhcsd->bhcld', s, vh) + inter_w[..., None] * jnp.einsum('bhcvk,bhclk->bhclv', C_prev, qh)
    den = s.sum(-1) + inter_w * jnp.einsum('bhck,bhclk->bhcl', n_prev, qh)
    h = num / jnp.maximum(jnp.abs(den), jnp.exp(-m_t))[..., None]
    h = h.transpose(0, 2, 3, 1, 4).reshape(bsz, seq, H, dh)
    h = _rms(h, norm_w.reshape(H, dh)).reshape(bsz, seq, W)
    return h * jax.nn.sigmoid(o)


def _rglru_group(cols, conv_w, conv_b, w_a, b_a, w_x, b_x, lam):
    bsz, seq, _ = cols.shape
    xb, gate = jnp.split(cols, [GROUP_W], axis=-1)
    xc = _causal_dwconv(xb, conv_w, conv_b)
    xh = xc.reshape(bsz, seq, GROUP_HEADS, HEAD_DIM)
    r = jax.nn.sigmoid(jnp.einsum('bshi,hij->bshj', xh, w_a).reshape(bsz, seq, GROUP_W) + b_a)
    ig = jax.nn.sigmoid(jnp.einsum('bshi,hij->bshj', xh, w_x).reshape(bsz, seq, GROUP_W) + b_x)
    log_a = -RG_C * jax.nn.softplus(-lam) * r
    a = jnp.exp(log_a)
    u = jnp.sqrt(-jnp.expm1(2.0 * log_a)) * (ig * xc)
    _, h = lax.associative_scan(_linrec_combine, (a, u), axis=1)
    return h * jax.nn.gelu(gate)


def _rwkv7_group(cols, mu, w0, w2, a0, a2, g2, k_k, k_a, r_k, ln_w, ln_b):
    bsz, seq, _ = cols.shape
    H, N, W = GROUP_HEADS, HEAD_DIM, GROUP_W
    prev = jnp.pad(cols[:, :-1], ((0, 0), (1, 0), (0, 0)))
    cols = cols + (prev - cols) * mu
    r, k, v, wd, ad, gd = jnp.split(
        cols, [W, 2 * W, 3 * W, 3 * W + RWKV_DECAY_RANK, 3 * W + RWKV_DECAY_RANK + RWKV_ICL_RANK], axis=-1)
    w_log = -jax.nn.softplus(-(w0 + jnp.tanh(wd) @ w2)) - 0.5
    decay = jnp.exp(-jnp.exp(w_log))
    a = jax.nn.sigmoid(a0 + ad @ a2)
    g = jax.nn.sigmoid(gd) @ g2

    def hd(t):
        return t.reshape(bsz, seq, H, N)

    kk = hd(k * k_k)
    kk = kk * lax.rsqrt(jnp.maximum(jnp.sum(kk * kk, -1, keepdims=True), 1e-24))
    k = k * (1.0 + (a - 1.0) * k_a)
    rh, kh, vh = hd(r), hd(k), hd(v)
    y = _rwkv_scan(r, decay, k, v, kk.reshape(bsz, seq, W), a)
    y = hd(y)
    yc = y - jnp.mean(y, -1, keepdims=True)
    yn = yc * lax.rsqrt(jnp.mean(yc * yc, -1, keepdims=True) + RWKV_GN_EPS)
    yn = yn.reshape(bsz, seq, W) * ln_w + ln_b
    bonus = jnp.sum(rh * kh * r_k, -1, keepdims=True) * vh
    return (yn + bonus.reshape(bsz, seq, W)) * g


def _ssd_group(cols, conv_w, conv_b, dt_bias, a_log, d_skip, norm_w):
    bsz, seq, _ = cols.shape
    H, P, G, N, L = GROUP_HEADS, HEAD_DIM, SSD_GROUPS, SSD_STATE, SSD_CHUNK
    nc = seq // L
    z, xbc, dt = jnp.split(cols, [GROUP_W, GROUP_W + SSD_XBC], axis=-1)
    xbc = jax.nn.silu(_causal_dwconv(xbc, conv_w, conv_b))
    xs, bm, cm = jnp.split(xbc, [GROUP_W, GROUP_W + G * N], axis=-1)
    x = xs.reshape(bsz, nc, L, H, P)
    rep = H // G
    bm = jnp.repeat(bm.reshape(bsz, nc, L, G, N), rep, axis=3)
    cm = jnp.repeat(cm.reshape(bsz, nc, L, G, N), rep, axis=3)
    dt = jax.nn.softplus(dt + dt_bias).reshape(bsz, nc, L, H)
    a_dt = (-jnp.exp(a_log) * dt).transpose(0, 3, 1, 2)
    a_cs = jnp.cumsum(a_dt, axis=-1)
    causal = jnp.tril(jnp.ones((L, L), bool))
    decay_in = jnp.exp(jnp.where(causal, a_cs[..., :, None] - a_cs[..., None, :], -jnp.inf))
    xdt = x * dt[..., None]
    scores = jnp.einsum('bclhn,bcshn->bhcls', cm, bm) * decay_in
    y_diag = jnp.einsum('bhcls,bcshp->bclhp', scores, xdt)
    decay_states = jnp.exp(a_cs[..., -1:] - a_cs)
    states = jnp.einsum('bclhn,bhcl,bclhp->bchpn', bm, decay_states, xdt)
    tot_cs = jnp.cumsum(jnp.pad(a_cs[..., -1], ((0, 0), (0, 0), (1, 0))), axis=-1)
    chunk_causal = jnp.tril(jnp.ones((nc + 1, nc + 1), bool))
    decay_chunk = jnp.exp(jnp.where(chunk_causal, tot_cs[..., :, None] - tot_cs[..., None, :], -jnp.inf))
    states = jnp.pad(states, ((0, 0), (1, 0), (0, 0), (0, 0), (0, 0)))
    states_in = jnp.einsum('bhzc,bchpn->bzhpn', decay_chunk, states)[:, :-1]
    y_off = jnp.einsum('bclhn,bchpn,bhcl->bclhp', cm, states_in, jnp.exp(a_cs))
    y = y_diag + y_off + x * d_skip[:, None]
    y = y.reshape(bsz, seq, GROUP_W) * jax.nn.silu(z)
    y = _rms(y.reshape(bsz, seq, G, GROUP_W // G), norm_w.reshape(G, GROUP_W // G))
    return y.reshape(bsz, seq, GROUP_W)


def _swiglu(x2d, w1, w3, w2):
    return _matmul(jax.nn.silu(_matmul(x2d, w1)) * _matmul(x2d, w3), w2)


def _moe_swiglu(xf, w_router, w1, w3, w2):
    T, d = xf.shape
    logits = xf @ w_router
    top_logit, top_e = lax.top_k(logits, TOP_K)
    gate = jax.nn.softmax(top_logit, axis=-1)
    flat_e = top_e.reshape(-1)
    flat_tok = jnp.repeat(jnp.arange(T, dtype=jnp.int32), TOP_K)
    order = jnp.argsort(flat_e)
    se, st, sg = flat_e[order], flat_tok[order], gate.reshape(-1)[order]
    counts = jnp.bincount(flat_e, length=N_EXPERTS)
    padded = (counts + MOE_BLOCK - 1) // MOE_BLOCK * MOE_BLOCK
    pad_end = jnp.cumsum(padded)
    start = jnp.cumsum(counts) - counts
    dest = (pad_end - padded)[se] + jnp.arange(T * TOP_K) - start[se]
    n_blocks = -(-(T * TOP_K) // MOE_BLOCK) + N_EXPERTS
    rows = n_blocks * MOE_BLOCK
    slot_tok = jnp.zeros((rows,), jnp.int32).at[dest].set(st)
    slot_gate = jnp.zeros((rows,), xf.dtype).at[dest].set(sg)
    block_e = jnp.minimum(jnp.searchsorted(pad_end, jnp.arange(n_blocks) * MOE_BLOCK, side='right'), N_EXPERTS - 1)
    xb = xf[slot_tok].reshape(n_blocks, MOE_BLOCK, d)

    def expert_block(args):
        xblk, e = args
        return (jax.nn.silu(xblk @ w1[e]) * (xblk @ w3[e])) @ w2[e]

    yb = lax.map(expert_block, (xb, block_e))
    return jnp.zeros((T, d), xf.dtype).at[slot_tok].add(yb.reshape(rows, d) * slot_gate[:, None])


def kernel(x, p, w_in, m_i_bias, m_f_bias, m_norm_w, g_conv_w, g_conv_b, g_w_a, g_b_a, g_w_x, g_b_x,
           g_lambda, r_mu, r_w0, r_w2, r_a0, r_a2, r_g2, r_k_k, r_k_a, r_r_k, r_ln_w, r_ln_b,
           s_conv_w, s_conv_b, s_dt_bias, s_a_log, s_d, s_norm_w, w_out, ln1_g, ln1_b, ln2_g, ln2_b,
           f_w1, f_w3, f_w2, e_router, e_w1, e_w3, e_w2, pe_proj, pe_gate_w, pe_gate_b):
    bsz, seq, d = x.shape
    T = bsz * seq
    splits = [M_COLS, M_COLS + G_COLS, M_COLS + G_COLS + R_COLS]
    x = x.reshape(T, d)
    for i in range(DEPTH):
        proj = _matmul(x, w_in[i]).reshape(bsz, seq, N_IN)
        m_cols, g_cols, r_cols, s_cols = jnp.split(proj, splits, axis=-1)
        mix = jnp.concatenate([
            _mlstm_group(m_cols, m_i_bias[i], m_f_bias[i], m_norm_w[i]),
            _rglru_group(g_cols, g_conv_w[i], g_conv_b[i], g_w_a[i], g_b_a[i], g_w_x[i], g_b_x[i], g_lambda[i]),
            _rwkv7_group(r_cols, r_mu[i], r_w0[i], r_w2[i], r_a0[i], r_a2[i], r_g2[i], r_k_k[i], r_k_a[i],
                         r_r_k[i], r_ln_w[i], r_ln_b[i]),
            _ssd_group(s_cols, s_conv_w[i], s_conv_b[i], s_dt_bias[i], s_a_log[i], s_d[i], s_norm_w[i]),
        ], axis=-1).reshape(T, d)
        x = _layer_norm(ALPHA * x + _matmul(mix, w_out[i]), ln1_g[i], ln1_b[i])
        if i % 2 == 0:
            ff = _swiglu(x, f_w1[i // 2], f_w3[i // 2], f_w2[i // 2])
        else:
            ff = _moe_swiglu(x, e_router[i // 2], e_w1[i // 2], e_w3[i // 2], e_w2[i // 2])
        pe = _matmul(p[i].reshape(T, PE_DIM), pe_proj[i]) * jax.nn.sigmoid(_matmul(x, pe_gate_w[i]) + pe_gate_b[i])
        x = _layer_norm(ALPHA * x + ff + pe, ln2_g[i], ln2_b[i])
    return x.reshape(bsz, seq, d)
```

```python
import functools

import jax
import jax.numpy as jnp
from jax import lax
from jax.experimental import pallas as pl
from jax.experimental.pallas import tpu as pltpu

D_MODEL = 1024
DEPTH = 4
PE_DIM = 256
GROUP_W = 256
HEAD_DIM = 64
GROUP_HEADS = 4
CONV_K = 4
MLSTM_CHUNK = 64
RG_C = 8.0
RWKV_DECAY_RANK = 32
RWKV_ICL_RANK = 32
RWKV_GATE_RANK = 64
RWKV_GN_EPS = 64e-5
SSD_STATE = 64
SSD_GROUPS = 2
SSD_CHUNK = 128
SSD_XBC = GROUP_W + 2 * SSD_GROUPS * SSD_STATE
M_COLS = 4 * GROUP_W + 2 * GROUP_HEADS
G_COLS = 2 * GROUP_W
R_COLS = 3 * GROUP_W + RWKV_DECAY_RANK + RWKV_ICL_RANK + RWKV_GATE_RANK
S_COLS = GROUP_W + SSD_XBC + GROUP_HEADS
N_IN = M_COLS + G_COLS + R_COLS + S_COLS
N_EXPERTS = 8
TOP_K = 2
MOE_BLOCK = 256
ALPHA = (2 * DEPTH) ** 0.25
LN_EPS = 1e-5
RMS_EPS = 1e-6

LANES = 128
SUBLANES = 8
VMEM_LIMIT = 56 * 1024 * 1024


def _mm_kernel(a_ref, b_ref, o_ref):
    o_ref[...] = jnp.dot(a_ref[...].astype(jnp.bfloat16), b_ref[...],
                         preferred_element_type=jnp.float32)


def _matmul(a, b, tm=512, tn=512):
    m, k = a.shape
    _, n = b.shape
    n_pad = -(-n // tn) * tn
    b = b.astype(jnp.bfloat16)
    if n_pad != n:
        b = jnp.pad(b, ((0, 0), (0, n_pad - n)))
    out = pl.pallas_call(
        _mm_kernel,
        out_shape=jax.ShapeDtypeStruct((m, n_pad), jnp.float32),
        grid=(m // tm, n_pad // tn),
        in_specs=[pl.BlockSpec((tm, k), lambda i, j: (i, 0)),
                  pl.BlockSpec((k, tn), lambda i, j: (0, j))],
        out_specs=pl.BlockSpec((tm, tn), lambda i, j: (i, j)),
        compiler_params=pltpu.CompilerParams(
            dimension_semantics=("parallel", "parallel"), vmem_limit_bytes=VMEM_LIMIT),
        name="matmul",
    )(a, b)
    return out[:, :n] if n_pad != n else out


def _rwkv_scan_kernel(d_ref, kka_ref, km_ref, r_ref, kkn_ref, v_ref, y_ref, s_ref, sa_ref, *, tb, nk, nacc):
    @pl.when(pl.program_id(0) == 0)
    def _():
        s_ref[...] = jnp.zeros_like(s_ref)
        sa_ref[...] = jnp.zeros_like(sa_ref)

    vrows = v_ref.shape[1]

    def bcast(ref, t, k):
        return jnp.broadcast_to(ref[t, pl.ds(k, 1), :], (vrows, LANES))

    def step(t, sa):
        v_t = v_ref[t]
        y_acc = [None] * nacc
        sa_acc = [None] * nacc
        for k in range(nk):
            s_k = s_ref[k] * bcast(d_ref, t, k) - sa * bcast(kka_ref, t, k) + v_t * bcast(km_ref, t, k)
            s_ref[k] = s_k
            y_k = s_k * bcast(r_ref, t, k)
            n_k = s_k * bcast(kkn_ref, t, k)
            a = k % nacc
            y_acc[a] = y_k if y_acc[a] is None else y_acc[a] + y_k
            sa_acc[a] = n_k if sa_acc[a] is None else sa_acc[a] + n_k
        y_ref[t] = functools.reduce(lambda p, q: p + q, y_acc)
        return functools.reduce(lambda p, q: p + q, sa_acc)

    sa_ref[...] = lax.fori_loop(0, tb, step, sa_ref[...])


def _rwkv_scan(r, d, kmod, v, kk, a, tb=32):
    bsz, seq, w = r.shape
    h, n = GROUP_HEADS, HEAD_DIM
    chains = bsz * h
    assert 2 * chains == LANES
    half = n // 2

    def klay(t):
        t = t.reshape(bsz, seq, h, n).transpose(1, 3, 0, 2).reshape(seq, n, chains)
        return jnp.concatenate([t, t], axis=-1)

    kk_next = jnp.concatenate([kk[:, 1:], jnp.zeros_like(kk[:, :1])], axis=1)
    v_l = v.reshape(bsz, seq, h, 2, half).transpose(1, 4, 3, 0, 2).reshape(seq, half, LANES)
    kspec = pl.BlockSpec((tb, n, LANES), lambda i: (i, 0, 0))
    vspec = pl.BlockSpec((tb, half, LANES), lambda i: (i, 0, 0))
    y = pl.pallas_call(
        functools.partial(_rwkv_scan_kernel, tb=tb, nk=n, nacc=2),
        out_shape=jax.ShapeDtypeStruct((seq, half, LANES), jnp.float32),
        grid=(seq // tb,),
        in_specs=[kspec, kspec, kspec, kspec, kspec, vspec],
        out_specs=vspec,
        scratch_shapes=[pltpu.VMEM((n, half, LANES), jnp.float32),
                        pltpu.VMEM((half, LANES), jnp.float32)],
        compiler_params=pltpu.CompilerParams(
            dimension_semantics=("arbitrary",), vmem_limit_bytes=VMEM_LIMIT),
        name="rwkv_scan",
    )(klay(d), klay(kk * a), klay(kmod), klay(r), klay(kk_next), v_l)
    return y.reshape(seq, half, 2, bsz, h).transpose(3, 0, 4, 2, 1).reshape(bsz, seq, w)


def _inproj_kernel(x_ref, *refs):
    n = len(refs) // 2
    xb = x_ref[...].astype(jnp.bfloat16)
    for w_ref, o_ref in zip(refs[:n], refs[n:]):
        o_ref[...] = jnp.dot(xb, w_ref[...], preferred_element_type=jnp.float32)


def _inproj(x, weights, tm=512):
    t, d = x.shape
    return pl.pallas_call(
        _inproj_kernel,
        out_shape=[jax.ShapeDtypeStruct((t, w.shape[1]), jnp.float32) for w in weights],
        grid=(t // tm,),
        in_specs=[pl.BlockSpec((tm, d), lambda i: (i, 0))]
        + [pl.BlockSpec(w.shape, lambda i: (0, 0)) for w in weights],
        out_specs=[pl.BlockSpec((tm, w.shape[1]), lambda i: (i, 0)) for w in weights],
        compiler_params=pltpu.CompilerParams(
            dimension_semantics=("parallel",), vmem_limit_bytes=VMEM_LIMIT),
        name="inproj",
    )(x, *weights)


def _split_w_in(w):
    m0, g0, r0, s0 = 0, M_COLS, M_COLS + G_COLS, M_COLS + G_COLS + R_COLS
    w_m = w[:, m0:m0 + 4 * GROUP_W]
    w_g = w[:, g0:g0 + G_COLS]
    w_r = w[:, r0:r0 + R_COLS]
    w_s = w[:, s0:s0 + GROUP_W + SSD_XBC]
    small = jnp.concatenate([w[:, m0 + 4 * GROUP_W:g0], w[:, s0 + GROUP_W + SSD_XBC:]], axis=1)
    small = jnp.pad(small, ((0, 0), (0, LANES - small.shape[1])))
    return [t.astype(jnp.bfloat16) for t in (w_m, w_g, w_r, w_s, small)]


def _shift_rows(x, s, fill, row):
    return jnp.where(row >= s, pltpu.roll(x, s, 0), fill)


def _rglru_kernel(g_ref, cw_ref, cb_ref, wg_ref, bg_ref, c_ref, o_ref, xpad_ref, a_ref, u_ref, *, rc, unroll):
    seq, w = o_ref.shape
    pad = SUBLANES
    xpad_ref[0:pad, :] = jnp.zeros((pad, w), jnp.float32)
    xpad_ref[pad:, :] = g_ref[:, 0:w]
    cw = cw_ref[...]
    for c in range(seq // rc):
        r0 = c * rc
        xc = cb_ref[...] + cw[CONV_K - 1:CONV_K, :] * g_ref[r0:r0 + rc, 0:w]
        for j in range(1, CONV_K):
            xc = xc + cw[CONV_K - 1 - j:CONV_K - j, :] * xpad_ref[pad + r0 - j:pad + r0 - j + rc, :]
        z = jnp.dot(xc.astype(jnp.bfloat16), wg_ref[...], preferred_element_type=jnp.float32) + bg_ref[...]
        z = jax.nn.sigmoid(z)
        a = jnp.exp(c_ref[...] * z[:, 0:w])
        a_ref[r0:r0 + rc, :] = a
        u_ref[r0:r0 + rc, :] = jnp.sqrt(1.0 - a * a) * (z[:, w:] * xc)

    row = lax.broadcasted_iota(jnp.int32, (SUBLANES, w), 0)

    def tiles(i, h):
        for j in range(unroll):
            r = pl.multiple_of((i * unroll + j) * SUBLANES, SUBLANES)
            a = a_ref[pl.ds(r, SUBLANES), :]
            u = u_ref[pl.ds(r, SUBLANES), :]
            for s in (1, 2, 4):
                u = a * _shift_rows(u, s, 0.0, row) + u
                a = a * _shift_rows(a, s, 1.0, row)
            ht = a * h + u
            o_ref[pl.ds(r, SUBLANES), :] = ht * jax.nn.gelu(g_ref[pl.ds(r, SUBLANES), w:2 * w])
            h = jnp.broadcast_to(ht[SUBLANES - 1:SUBLANES, :], (SUBLANES, w))
        return h

    lax.fori_loop(0, seq // (SUBLANES * unroll), tiles, jnp.zeros((SUBLANES, w), jnp.float32))


def _block_diag(w):
    h, p, _ = w.shape
    eye = jnp.eye(h, dtype=w.dtype)
    return jnp.einsum('hij,hg->higj', w, eye).reshape(h * p, h * p)


def _rglru_group(g_cols, bsz, conv_w, conv_b, w_a, b_a, w_x, b_x, lam):
    t = g_cols.shape[0]
    seq = t // bsz
    w = GROUP_W
    wg = jnp.concatenate([_block_diag(w_a), _block_diag(w_x)], axis=1).astype(jnp.bfloat16)
    bg = jnp.concatenate([b_a, b_x])[None, :]
    cdec = (-RG_C * jax.nn.softplus(-lam))[None, :]
    const = lambda shape: pl.BlockSpec(shape, lambda b: (0, 0))
    return pl.pallas_call(
        functools.partial(_rglru_kernel, rc=256, unroll=8),
        out_shape=jax.ShapeDtypeStruct((t, w), jnp.float32),
        grid=(bsz,),
        in_specs=[pl.BlockSpec((seq, 2 * w), lambda b: (b, 0)), const((CONV_K, w)), const((1, w)),
                  const((w, 2 * w)), const((1, 2 * w)), const((1, w))],
        out_specs=pl.BlockSpec((seq, w), lambda b: (b, 0)),
        scratch_shapes=[pltpu.VMEM((seq + SUBLANES, w), jnp.float32),
                        pltpu.VMEM((seq, w), jnp.float32), pltpu.VMEM((seq, w), jnp.float32)],
        compiler_params=pltpu.CompilerParams(
            dimension_semantics=("parallel",), vmem_limit_bytes=VMEM_LIMIT),
        name="rglru",
    )(g_cols, conv_w, conv_b[None, :], wg, bg, cdec)


def _layer_norm(x, g, b):
    xc = x - jnp.mean(x, -1, keepdims=True)
    var = jnp.mean(xc * xc, -1, keepdims=True)
    return (xc * lax.rsqrt(var + LN_EPS)) * g + b


def _rms(x, w):
    return x * lax.rsqrt(jnp.mean(x * x, -1, keepdims=True) + RMS_EPS) * w


def _causal_dwconv(u, w, b):
    k, c = w.shape
    out = lax.conv_general_dilated(u, w[:, None, :], window_strides=(1,), padding=[(k - 1, 0)],
                                   dimension_numbers=('NWC', 'WIO', 'NWC'), feature_group_count=c)
    return out + b


def _linrec_combine(left, right):
    a_l, b_l = left
    a_r, b_r = right
    return a_l * a_r, a_r * b_l + b_r


def _mlstm_group(cols, i_bias, f_bias, norm_w):
    bsz, seq, _ = cols.shape
    L, H, dh = MLSTM_CHUNK, GROUP_HEADS, HEAD_DIM
    nc = seq // L
    W = GROUP_W
    q, k, v, o, i_pre, f_pre = jnp.split(cols, [W, 2 * W, 3 * W, 4 * W, 4 * W + H], axis=-1)

    def heads(t):
        return t.reshape(bsz, nc, L, H, dh).transpose(0, 3, 1, 2, 4)

    def gates(t):
        return t.reshape(bsz, nc, L, H).transpose(0, 3, 1, 2)

    qh, kh, vh = heads(q), heads(k) * (dh ** -0.5), heads(v)
    log_i = gates(i_pre + i_bias)
    log_f = jax.nn.log_sigmoid(gates(f_pre + f_bias))
    b = jnp.cumsum(log_f, axis=-1)
    g = b[..., -1]
    a = g[..., None] - b + log_i

    def chunk_step(carry, xs):
        C, n, m = carry
        k_c, v_c, a_c, g_c = xs
        m_new = jnp.maximum(g_c + m, a_c.max(-1))
        decay = jnp.exp(g_c + m - m_new)
        w = jnp.exp(a_c - m_new[..., None])
        C_new = decay[..., None, None] * C + jnp.einsum('bhl,bhlv,bhlk->bhvk', w, v_c, k_c)
        n_new = decay[..., None] * n + jnp.einsum('bhl,bhlk->bhk', w, k_c)
        return (C_new, n_new, m_new), (C, n, m)

    init = (jnp.zeros((bsz, H, dh, dh), jnp.float32), jnp.zeros((bsz, H, dh), jnp.float32),
            jnp.zeros((bsz, H), jnp.float32))
    xs = (kh.transpose(2, 0, 1, 3, 4), vh.transpose(2, 0, 1, 3, 4), a.transpose(2, 0, 1, 3), g.transpose(2, 0, 1))
    _, (C_prev, n_prev, m_prev) = lax.scan(chunk_step, init, xs)
    C_prev = C_prev.transpose(1, 2, 0, 3, 4)
    n_prev = n_prev.transpose(1, 2, 0, 3)
    m_prev = m_prev.transpose(1, 2, 0)

    causal = jnp.tril(jnp.ones((L, L), bool))
    log_D = jnp.where(causal, b[..., :, None] - b[..., None, :] + log_i[..., None, :], -jnp.inf)
    inter_log = b + m_prev[..., None]
    m_t = jnp.maximum(inter_log, log_D.max(-1))
    s = jnp.einsum('bhcld,bhcsd->bhcls', qh, kh) * jnp.exp(log_D - m_t[..., None])
    inter_w = jnp.exp(inter_log - m_t)
    num = jnp.einsum('bhcls,bhcsd->bhcld', s, vh) + inter_w[..., None] * jnp.einsum('bhcvk,bhclk->bhclv', C_prev, qh)
    den = s.sum(-1) + inter_w * jnp.einsum('bhck,bhclk->bhcl', n_prev, qh)
    h = num / jnp.maximum(jnp.abs(den), jnp.exp(-m_t))[..., None]
    h = h.transpose(0, 2, 3, 1, 4).reshape(bsz, seq, H, dh)
    h = _rms(h, norm_w.reshape(H, dh)).reshape(bsz, seq, W)
    return h * jax.nn.sigmoid(o)


def _rwkv7_group(cols, mu, w0, w2, a0, a2, g2, k_k, k_a, r_k, ln_w, ln_b):
    bsz, seq, _ = cols.shape
    H, N, W = GROUP_HEADS, HEAD_DIM, GROUP_W
    prev = jnp.pad(cols[:, :-1], ((0, 0), (1, 0), (0, 0)))
    cols = cols + (prev - cols) * mu
    r, k, v, wd, ad, gd = jnp.split(
        cols, [W, 2 * W, 3 * W, 3 * W + RWKV_DECAY_RANK, 3 * W + RWKV_DECAY_RANK + RWKV_ICL_RANK], axis=-1)
    w_log = -jax.nn.softplus(-(w0 + jnp.tanh(wd) @ w2)) - 0.5
    decay = jnp.exp(-jnp.exp(w_log))
    a = jax.nn.sigmoid(a0 + ad @ a2)
    g = jax.nn.sigmoid(gd) @ g2

    def hd(t):
        return t.reshape(bsz, seq, H, N)

    kk = hd(k * k_k)
    kk = kk * lax.rsqrt(jnp.maximum(jnp.sum(kk * kk, -1, keepdims=True), 1e-24))
    k = k * (1.0 + (a - 1.0) * k_a)
    rh, kh, vh = hd(r), hd(k), hd(v)
    y = _rwkv_scan(r, decay, k, v, kk.reshape(bsz, seq, W), a)
    y = hd(y)
    yc = y - jnp.mean(y, -1, keepdims=True)
    yn = yc * lax.rsqrt(jnp.mean(yc * yc, -1, keepdims=True) + RWKV_GN_EPS)
    yn = yn.reshape(bsz, seq, W) * ln_w + ln_b
    bonus = jnp.sum(rh * kh * r_k, -1, keepdims=True) * vh
    return (yn + bonus.reshape(bsz, seq, W)) * g


def _ssd_group(cols, conv_w, conv_b, dt_bias, a_log, d_skip, norm_w):
    bsz, seq, _ = cols.shape
    H, P, G, N, L = GROUP_HEADS, HEAD_DIM, SSD_GROUPS, SSD_STATE, SSD_CHUNK
    nc = seq // L
    z, xbc, dt = jnp.split(cols, [GROUP_W, GROUP_W + SSD_XBC], axis=-1)
    xbc = jax.nn.silu(_causal_dwconv(xbc, conv_w, conv_b))
    xs, bm, cm = jnp.split(xbc, [GROUP_W, GROUP_W + G * N], axis=-1)
    x = xs.reshape(bsz, nc, L, H, P)
    rep = H // G
    bm = jnp.repeat(bm.reshape(bsz, nc, L, G, N), rep, axis=3)
    cm = jnp.repeat(cm.reshape(bsz, nc, L, G, N), rep, axis=3)
    dt = jax.nn.softplus(dt + dt_bias).reshape(bsz, nc, L, H)
    a_dt = (-jnp.exp(a_log) * dt).transpose(0, 3, 1, 2)
    a_cs = jnp.cumsum(a_dt, axis=-1)
    causal = jnp.tril(jnp.ones((L, L), bool))
    decay_in = jnp.exp(jnp.where(causal, a_cs[..., :, None] - a_cs[..., None, :], -jnp.inf))
    xdt = x * dt[..., None]
    scores = jnp.einsum('bclhn,bcshn->bhcls', cm, bm) * decay_in
    y_diag = jnp.einsum('bhcls,bcshp->bclhp', scores, xdt)
    decay_states = jnp.exp(a_cs[..., -1:] - a_cs)
    states = jnp.einsum('bclhn,bhcl,bclhp->bchpn', bm, decay_states, xdt)
    tot_cs = jnp.cumsum(jnp.pad(a_cs[..., -1], ((0, 0), (0, 0), (1, 0))), axis=-1)
    chunk_causal = jnp.tril(jnp.ones((nc + 1, nc + 1), bool))
    decay_chunk = jnp.exp(jnp.where(chunk_causal, tot_cs[..., :, None] - tot_cs[..., None, :], -jnp.inf))
    states = jnp.pad(states, ((0, 0), (1, 0), (0, 0), (0, 0), (0, 0)))
    states_in = jnp.einsum('bhzc,bchpn->bzhpn', decay_chunk, states)[:, :-1]
    y_off = jnp.einsum('bclhn,bchpn,bhcl->bclhp', cm, states_in, jnp.exp(a_cs))
    y = y_diag + y_off + x * d_skip[:, None]
    y = y.reshape(bsz, seq, GROUP_W) * jax.nn.silu(z)
    y = _rms(y.reshape(bsz, seq, G, GROUP_W // G), norm_w.reshape(G, GROUP_W // G))
    return y.reshape(bsz, seq, GROUP_W)


FF_CHUNK = 512


def _swiglu_kernel(be_ref, x_ref, w1_ref, w3_ref, w2_ref, g_ref, o_ref):
    del be_ref
    xb = x_ref[...].astype(jnp.bfloat16)
    dff = w1_ref.shape[-1]
    acc = None
    for c0 in range(0, dff, FF_CHUNK):
        c1 = min(c0 + FF_CHUNK, dff)
        h1 = jnp.dot(xb, w1_ref[0, :, c0:c1], preferred_element_type=jnp.float32)
        h3 = jnp.dot(xb, w3_ref[0, :, c0:c1], preferred_element_type=jnp.float32)
        h = (jax.nn.silu(h1) * h3).astype(jnp.bfloat16)
        part = jnp.dot(h, w2_ref[0, c0:c1, :], preferred_element_type=jnp.float32)
        acc = part if acc is None else acc + part
    o_ref[...] = acc * g_ref[...]


def _grouped_swiglu(xrows, block_e, row_gate, w1, w3, w2, bm):
    rows, d = xrows.shape
    dff = w1.shape[-1]
    grid_spec = pltpu.PrefetchScalarGridSpec(
        num_scalar_prefetch=1,
        grid=(rows // bm,),
        in_specs=[pl.BlockSpec((bm, d), lambda i, be: (i, 0)),
                  pl.BlockSpec((1, d, dff), lambda i, be: (be[i], 0, 0)),
                  pl.BlockSpec((1, d, dff), lambda i, be: (be[i], 0, 0)),
                  pl.BlockSpec((1, dff, d), lambda i, be: (be[i], 0, 0)),
                  pl.BlockSpec((bm, 1), lambda i, be: (i, 0))],
        out_specs=pl.BlockSpec((bm, d), lambda i, be: (i, 0)),
    )
    return pl.pallas_call(
        _swiglu_kernel,
        out_shape=jax.ShapeDtypeStruct((rows, d), jnp.float32),
        grid_spec=grid_spec,
        compiler_params=pltpu.CompilerParams(
            dimension_semantics=("arbitrary",), vmem_limit_bytes=VMEM_LIMIT),
        name="swiglu",
    )(block_e, xrows, w1, w3, w2, row_gate)


def _swiglu(x2d, w1, w3, w2, bm=256):
    rows = x2d.shape[0]
    bf = lambda t: t.astype(jnp.bfloat16)[None]
    return _grouped_swiglu(x2d, jnp.zeros((rows // bm,), jnp.int32), jnp.ones((rows, 1), jnp.float32),
                           bf(w1), bf(w3), bf(w2), bm)


def _moe_swiglu(xf, w_router, w1, w3, w2):
    T, d = xf.shape
    logits = xf @ w_router
    top_logit, top_e = lax.top_k(logits, TOP_K)
    gate = jax.nn.softmax(top_logit, axis=-1)
    flat_e = top_e.reshape(-1)
    flat_tok = jnp.repeat(jnp.arange(T, dtype=jnp.int32), TOP_K)
    order = jnp.argsort(flat_e)
    se, st, sg = flat_e[order], flat_tok[order], gate.reshape(-1)[order]
    counts = jnp.bincount(flat_e, length=N_EXPERTS)
    padded = (counts + MOE_BLOCK - 1) // MOE_BLOCK * MOE_BLOCK
    pad_end = jnp.cumsum(padded)
    start = jnp.cumsum(counts) - counts
    dest = (pad_end - padded)[se] + jnp.arange(T * TOP_K) - start[se]
    n_blocks = -(-(T * TOP_K) // MOE_BLOCK) + N_EXPERTS
    rows = n_blocks * MOE_BLOCK
    slot_tok = jnp.zeros((rows,), jnp.int32).at[dest].set(st)
    slot_gate = jnp.zeros((rows,), xf.dtype).at[dest].set(sg)
    block_e = jnp.minimum(jnp.searchsorted(pad_end, jnp.arange(n_blocks) * MOE_BLOCK, side='right'), N_EXPERTS - 1)
    bf = lambda t: t.astype(jnp.bfloat16)
    yb = _grouped_swiglu(xf[slot_tok], block_e.astype(jnp.int32), slot_gate[:, None], bf(w1), bf(w3), bf(w2),
                         MOE_BLOCK)
    return jnp.zeros((T, d), xf.dtype).at[slot_tok].add(yb)


def kernel(x, p, w_in, m_i_bias, m_f_bias, m_norm_w, g_conv_w, g_conv_b, g_w_a, g_b_a, g_w_x, g_b_x,
           g_lambda, r_mu, r_w0, r_w2, r_a0, r_a2, r_g2, r_k_k, r_k_a, r_r_k, r_ln_w, r_ln_b,
           s_conv_w, s_conv_b, s_dt_bias, s_a_log, s_d, s_norm_w, w_out, ln1_g, ln1_b, ln2_g, ln2_b,
           f_w1, f_w3, f_w2, e_router, e_w1, e_w3, e_w2, pe_proj, pe_gate_w, pe_gate_b):
    bsz, seq, d = x.shape
    T = bsz * seq
    x = x.reshape(T, d)
    nh = GROUP_HEADS
    for i in range(DEPTH):
        pm, pg, pr, ps, psm = _inproj(x, _split_w_in(w_in[i]))
        m_cols = jnp.concatenate([pm, psm[:, :2 * nh]], axis=-1).reshape(bsz, seq, M_COLS)
        s_cols = jnp.concatenate([ps, psm[:, 2 * nh:3 * nh]], axis=-1).reshape(bsz, seq, S_COLS)
        mix = jnp.concatenate([
            _mlstm_group(m_cols, m_i_bias[i], m_f_bias[i], m_norm_w[i]).reshape(T, GROUP_W),
            _rglru_group(pg, bsz, g_conv_w[i], g_conv_b[i], g_w_a[i], g_b_a[i], g_w_x[i], g_b_x[i], g_lambda[i]),
            _rwkv7_group(pr.reshape(bsz, seq, R_COLS), r_mu[i], r_w0[i], r_w2[i], r_a0[i], r_a2[i], r_g2[i],
                         r_k_k[i], r_k_a[i], r_r_k[i], r_ln_w[i], r_ln_b[i]).reshape(T, GROUP_W),
            _ssd_group(s_cols, s_conv_w[i], s_conv_b[i], s_dt_bias[i], s_a_log[i], s_d[i],
                       s_norm_w[i]).reshape(T, GROUP_W),
        ], axis=-1)
        x = _layer_norm(ALPHA * x + _matmul(mix, w_out[i]), ln1_g[i], ln1_b[i])
        if i % 2 == 0:
            ff = _swiglu(x, f_w1[i // 2], f_w3[i // 2], f_w2[i // 2])
        else:
            ff = _moe_swiglu(x, e_router[i // 2], e_w1[i // 2], e_w3[i // 2], e_w2[i // 2])
        pe = _matmul(p[i].reshape(T, PE_DIM), pe_proj[i]) * jax.nn.sigmoid(_matmul(x, pe_gate_w[i]) + pe_gate_b[i])
        x = _layer_norm(ALPHA * x + ff + pe, ln2_g[i], ln2_b[i])
    return x.reshape(bsz, seq, d)
```

```python
import functools

import jax
import jax.numpy as jnp
from jax import lax
from jax.experimental import pallas as pl
from jax.experimental.pallas import tpu as pltpu

D_MODEL = 1024
DEPTH = 4
PE_DIM = 256
GROUP_W = 256
HEAD_DIM = 64
GROUP_HEADS = 4
CONV_K = 4
MLSTM_CHUNK = 64
RG_C = 8.0
RWKV_DECAY_RANK = 32
RWKV_ICL_RANK = 32
RWKV_GATE_RANK = 64
RWKV_GN_EPS = 64e-5
SSD_STATE = 64
SSD_GROUPS = 2
SSD_CHUNK = 128
SSD_XBC = GROUP_W + 2 * SSD_GROUPS * SSD_STATE
M_COLS = 4 * GROUP_W + 2 * GROUP_HEADS
G_COLS = 2 * GROUP_W
R_COLS = 3 * GROUP_W + RWKV_DECAY_RANK + RWKV_ICL_RANK + RWKV_GATE_RANK
S_COLS = GROUP_W + SSD_XBC + GROUP_HEADS
N_IN = M_COLS + G_COLS + R_COLS + S_COLS
N_EXPERTS = 8
TOP_K = 2
MOE_BLOCK = 256
ALPHA = (2 * DEPTH) ** 0.25
LN_EPS = 1e-5
RMS_EPS = 1e-6

LANES = 128
SUBLANES = 8
VMEM_LIMIT = 56 * 1024 * 1024


def _rwkv_scan_kernel(d_ref, kka_ref, km_ref, r_ref, kkn_ref, v_ref, y_ref, s_ref, sa_ref, *, tb, nk, nacc):
    @pl.when(pl.program_id(0) == 0)
    def _():
        s_ref[...] = jnp.zeros_like(s_ref)
        sa_ref[...] = jnp.zeros_like(sa_ref)

    vrows = v_ref.shape[1]

    def bcast(ref, t, k):
        return jnp.broadcast_to(ref[t, pl.ds(k, 1), :], (vrows, LANES))

    def step(t, sa):
        v_t = v_ref[t]
        y_acc = [None] * nacc
        sa_acc = [None] * nacc
        for k in range(nk):
            s_k = s_ref[k] * bcast(d_ref, t, k) - sa * bcast(kka_ref, t, k) + v_t * bcast(km_ref, t, k)
            s_ref[k] = s_k
            y_k = s_k * bcast(r_ref, t, k)
            n_k = s_k * bcast(kkn_ref, t, k)
            a = k % nacc
            y_acc[a] = y_k if y_acc[a] is None else y_acc[a] + y_k
            sa_acc[a] = n_k if sa_acc[a] is None else sa_acc[a] + n_k
        y_ref[t] = functools.reduce(lambda p, q: p + q, y_acc)
        return functools.reduce(lambda p, q: p + q, sa_acc)

    sa_ref[...] = lax.fori_loop(0, tb, step, sa_ref[...])


def _rwkv_scan(r, d, kmod, v, kk, kka, tb=32):
    bsz, seq, w = r.shape
    h, n = GROUP_HEADS, HEAD_DIM
    chains = bsz * h
    assert 2 * chains == LANES
    half = n // 2

    def klay(t):
        t = t.reshape(bsz, seq, h, n).transpose(1, 3, 0, 2).reshape(seq, n, chains)
        return jnp.concatenate([t, t], axis=-1)

    kk_next = jnp.concatenate([kk[:, 1:], jnp.zeros_like(kk[:, :1])], axis=1)
    v_l = v.reshape(bsz, seq, h, 2, half).transpose(1, 4, 3, 0, 2).reshape(seq, half, LANES)
    kspec = pl.BlockSpec((tb, n, LANES), lambda i: (i, 0, 0))
    vspec = pl.BlockSpec((tb, half, LANES), lambda i: (i, 0, 0))
    y = pl.pallas_call(
        functools.partial(_rwkv_scan_kernel, tb=tb, nk=n, nacc=2),
        out_shape=jax.ShapeDtypeStruct((seq, half, LANES), jnp.float32),
        grid=(seq // tb,),
        in_specs=[kspec, kspec, kspec, kspec, kspec, vspec],
        out_specs=vspec,
        scratch_shapes=[pltpu.VMEM((n, half, LANES), jnp.float32),
                        pltpu.VMEM((half, LANES), jnp.float32)],
        compiler_params=pltpu.CompilerParams(
            dimension_semantics=("arbitrary",), vmem_limit_bytes=VMEM_LIMIT),
        name="rwkv_scan",
    )(klay(d), klay(kka), klay(kmod), klay(r), klay(kk_next), v_l)
    return y.reshape(seq, half, 2, bsz, h).transpose(3, 0, 4, 2, 1).reshape(bsz, seq, w)


def _inproj_kernel(x_ref, *refs):
    n = len(refs) // 2
    xb = x_ref[...].astype(jnp.bfloat16)
    for w_ref, o_ref in zip(refs[:n], refs[n:]):
        o_ref[...] = jnp.dot(xb, w_ref[...], preferred_element_type=jnp.float32)


def _inproj(x, weights, tm=512):
    t, d = x.shape
    return pl.pallas_call(
        _inproj_kernel,
        out_shape=[jax.ShapeDtypeStruct((t, w.shape[1]), jnp.float32) for w in weights],
        grid=(t // tm,),
        in_specs=[pl.BlockSpec((tm, d), lambda i: (i, 0))]
        + [pl.BlockSpec(w.shape, lambda i: (0, 0)) for w in weights],
        out_specs=[pl.BlockSpec((tm, w.shape[1]), lambda i: (i, 0)) for w in weights],
        compiler_params=pltpu.CompilerParams(
            dimension_semantics=("parallel",), vmem_limit_bytes=VMEM_LIMIT),
        name="inproj",
    )(x, *weights)


def _split_w_in(w):
    m0, g0, r0, s0 = 0, M_COLS, M_COLS + G_COLS, M_COLS + G_COLS + R_COLS
    w_m = w[:, m0:m0 + 4 * GROUP_W]
    w_g = w[:, g0:g0 + G_COLS]
    w_r = w[:, r0:r0 + R_COLS]
    w_s = w[:, s0:s0 + GROUP_W + SSD_XBC]
    small = jnp.concatenate([w[:, m0 + 4 * GROUP_W:g0], w[:, s0 + GROUP_W + SSD_XBC:]], axis=1)
    small = jnp.pad(small, ((0, 0), (0, LANES - small.shape[1])))
    return [t.astype(jnp.bfloat16) for t in (w_m, w_g, w_r, w_s, small)]


def _shift_rows(x, s, fill, row):
    return jnp.where(row >= s, pltpu.roll(x, s, 0), fill)


def _rglru_kernel(g_ref, cw_ref, cb_ref, wg_ref, bg_ref, c_ref, o_ref, xpad_ref, a_ref, u_ref, *, rc, unroll):
    seq, w = o_ref.shape
    pad = SUBLANES
    xpad_ref[0:pad, :] = jnp.zeros((pad, w), jnp.float32)
    xpad_ref[pad:, :] = g_ref[:, 0:w]
    cw = cw_ref[...]
    for c in range(seq // rc):
        r0 = c * rc
        xc = cb_ref[...] + cw[CONV_K - 1:CONV_K, :] * g_ref[r0:r0 + rc, 0:w]
        for j in range(1, CONV_K):
            xc = xc + cw[CONV_K - 1 - j:CONV_K - j, :] * xpad_ref[pad + r0 - j:pad + r0 - j + rc, :]
        z = jnp.dot(xc.astype(jnp.bfloat16), wg_ref[...], preferred_element_type=jnp.float32) + bg_ref[...]
        z = jax.nn.sigmoid(z)
        a = jnp.exp(c_ref[...] * z[:, 0:w])
        a_ref[r0:r0 + rc, :] = a
        u_ref[r0:r0 + rc, :] = jnp.sqrt(1.0 - a * a) * (z[:, w:] * xc)

    row = lax.broadcasted_iota(jnp.int32, (SUBLANES, w), 0)

    def tiles(i, h):
        for j in range(unroll):
            r = pl.multiple_of((i * unroll + j) * SUBLANES, SUBLANES)
            a = a_ref[pl.ds(r, SUBLANES), :]
            u = u_ref[pl.ds(r, SUBLANES), :]
            for s in (1, 2, 4):
                u = a * _shift_rows(u, s, 0.0, row) + u
                a = a * _shift_rows(a, s, 1.0, row)
            ht = a * h + u
            o_ref[pl.ds(r, SUBLANES), :] = ht * jax.nn.gelu(g_ref[pl.ds(r, SUBLANES), w:2 * w])
            h = jnp.broadcast_to(ht[SUBLANES - 1:SUBLANES, :], (SUBLANES, w))
        return h

    lax.fori_loop(0, seq // (SUBLANES * unroll), tiles, jnp.zeros((SUBLANES, w), jnp.float32))


def _block_diag(w):
    h, p, _ = w.shape
    eye = jnp.eye(h, dtype=w.dtype)
    return jnp.einsum('hij,hg->higj', w, eye).reshape(h * p, h * p)


def _rglru_group(g_cols, bsz, conv_w, conv_b, w_a, b_a, w_x, b_x, lam):
    t = g_cols.shape[0]
    seq = t // bsz
    w = GROUP_W
    wg = jnp.concatenate([_block_diag(w_a), _block_diag(w_x)], axis=1).astype(jnp.bfloat16)
    bg = jnp.concatenate([b_a, b_x])[None, :]
    cdec = (-RG_C * jax.nn.softplus(-lam))[None, :]
    const = lambda shape: pl.BlockSpec(shape, lambda b: (0, 0))
    return pl.pallas_call(
        functools.partial(_rglru_kernel, rc=256, unroll=8),
        out_shape=jax.ShapeDtypeStruct((t, w), jnp.float32),
        grid=(bsz,),
        in_specs=[pl.BlockSpec((seq, 2 * w), lambda b: (b, 0)), const((CONV_K, w)), const((1, w)),
                  const((w, 2 * w)), const((1, 2 * w)), const((1, w))],
        out_specs=pl.BlockSpec((seq, w), lambda b: (b, 0)),
        scratch_shapes=[pltpu.VMEM((seq + SUBLANES, w), jnp.float32),
                        pltpu.VMEM((seq, w), jnp.float32), pltpu.VMEM((seq, w), jnp.float32)],
        compiler_params=pltpu.CompilerParams(
            dimension_semantics=("parallel",), vmem_limit_bytes=VMEM_LIMIT),
        name="rglru",
    )(g_cols, conv_w, conv_b[None, :], wg, bg, cdec)


_NT = (((1,), (1,)), ((), ()))
_TN = (((0,), (0,)), ((), ()))


def _bdot(a, b, dims=(((1,), (0,)), ((), ()))):
    return lax.dot_general(a.astype(jnp.bfloat16), b.astype(jnp.bfloat16), dims,
                           preferred_element_type=jnp.float32)


def _cumsum_rows(x):
    n = x.shape[0]
    tri = (lax.broadcasted_iota(jnp.int32, (n, n), 0) >= lax.broadcasted_iota(jnp.int32, (n, n), 1))
    return jnp.dot(tri.astype(jnp.float32), x, precision=lax.Precision.HIGHEST,
                   preferred_element_type=jnp.float32)


def _log_sigmoid(x):
    return jnp.minimum(x, 0.0) - jnp.log1p(jnp.exp(-jnp.abs(x)))


def _softplus(x):
    return jnp.maximum(x, 0.0) + jnp.log1p(jnp.exp(-jnp.abs(x)))


def _mlstm_kernel(pm_ref, sm_ref, bias_ref, nw_ref, o_ref, c_ref, n_ref, m_ref):
    seq, w = o_ref.shape
    nh, dh, L = GROUP_HEADS, HEAD_DIM, MLSTM_CHUNK
    c_ref[...] = jnp.zeros_like(c_ref)
    n_ref[...] = jnp.zeros_like(n_ref)
    m_ref[...] = jnp.zeros_like(m_ref)
    col = lax.broadcasted_iota(jnp.int32, (L, LANES), 1)
    causal = lax.broadcasted_iota(jnp.int32, (L, L), 0) >= lax.broadcasted_iota(jnp.int32, (L, L), 1)

    def chunk(c, carry):
        r0 = pl.multiple_of(c * L, L)
        rows = pl.ds(r0, L)
        gts = sm_ref[rows, :] + bias_ref[...]
        gts = jnp.where(col < nh, gts, jnp.where(col < 2 * nh, _log_sigmoid(gts), 0.0))
        bcs = _cumsum_rows(gts)
        gts_t = gts.T
        bcs_t = bcs.T
        for h in range(nh):
            b_col, li_col = bcs[:, nh + h:nh + h + 1], gts[:, h:h + 1]
            b_row, li_row = bcs_t[nh + h:nh + h + 1, :], gts_t[h:h + 1, :]
            g = b_col[L - 1:L, :]
            m = m_ref[h]
            q = pm_ref[rows, h * dh:(h + 1) * dh]
            k = pm_ref[rows, w + h * dh:w + (h + 1) * dh] * (dh ** -0.5)
            v = pm_ref[rows, 2 * w + h * dh:2 * w + (h + 1) * dh]
            o = pm_ref[rows, 3 * w + h * dh:3 * w + (h + 1) * dh]
            cst, nst = c_ref[h], n_ref[h]
            log_d = jnp.where(causal, b_col - b_row + li_row, -jnp.inf)
            m_t = jnp.maximum(b_col + m, jnp.max(log_d, axis=1, keepdims=True))
            s = _bdot(q, k, _NT) * jnp.exp(log_d - m_t)
            inter_w = jnp.exp(b_col + m - m_t)
            num = _bdot(s, v) + inter_w * _bdot(q, cst, _NT)
            den = jnp.sum(s, axis=1, keepdims=True) + inter_w * jnp.sum(q * nst, axis=1, keepdims=True)
            hh = num / jnp.maximum(jnp.abs(den), jnp.exp(-m_t))
            hh = hh * lax.rsqrt(jnp.mean(hh * hh, axis=1, keepdims=True) + RMS_EPS) * nw_ref[:, h * dh:(h + 1) * dh]
            o_ref[rows, h * dh:(h + 1) * dh] = hh * jax.nn.sigmoid(o)
            a_row = g - b_row + li_row
            m_new = jnp.maximum(g + m, jnp.max(a_row, axis=1, keepdims=True))
            decay = jnp.exp(g + m - m_new)
            w_col = jnp.exp(g - b_col + li_col - m_new)
            c_ref[h] = decay * cst + _bdot(w_col * v, k, _TN)
            n_ref[h] = decay * nst + jnp.sum(w_col * k, axis=0, keepdims=True)
            m_ref[h] = m_new
        return carry

    lax.fori_loop(0, seq // L, chunk, 0)


def _mlstm_group(pm, psm, bsz, i_bias, f_bias, norm_w):
    t = pm.shape[0]
    seq = t // bsz
    w, nh, dh = GROUP_W, GROUP_HEADS, HEAD_DIM
    bias = jnp.pad(jnp.concatenate([i_bias, f_bias]), (0, LANES - 2 * nh))[None, :]
    const = lambda shape: pl.BlockSpec(shape, lambda b: (0,) * len(shape))
    return pl.pallas_call(
        _mlstm_kernel,
        out_shape=jax.ShapeDtypeStruct((t, w), jnp.float32),
        grid=(bsz,),
        in_specs=[pl.BlockSpec((seq, 4 * w), lambda b: (b, 0)), pl.BlockSpec((seq, LANES), lambda b: (b, 0)),
                  const((1, LANES)), const((1, w))],
        out_specs=pl.BlockSpec((seq, w), lambda b: (b, 0)),
        scratch_shapes=[pltpu.VMEM((nh, dh, dh), jnp.float32), pltpu.VMEM((nh, 1, dh), jnp.float32),
                        pltpu.VMEM((nh, 1, 1), jnp.float32)],
        compiler_params=pltpu.CompilerParams(
            dimension_semantics=("parallel",), vmem_limit_bytes=VMEM_LIMIT),
        name="mlstm",
    )(pm, psm, bias, norm_w[None, :])


def _ssd_kernel(ps_ref, sm_ref, cw_ref, cb_ref, dtb_ref, aneg_ref, dsk_ref, nw_ref, o_ref,
                xpad_ref, xbc_ref, st_ref, *, rc):
    seq, w = o_ref.shape
    nh, hp, ng, ns, L = GROUP_HEADS, HEAD_DIM, SSD_GROUPS, SSD_STATE, SSD_CHUNK
    cx = SSD_XBC
    pad = SUBLANES
    xpad_ref[0:pad, :] = jnp.zeros((pad, cx), jnp.float32)
    xpad_ref[pad:, :] = ps_ref[:, w:w + cx]
    cw = cw_ref[...]
    for c in range(seq // rc):
        r0 = c * rc
        acc = cb_ref[...] + cw[CONV_K - 1:CONV_K, :] * ps_ref[r0:r0 + rc, w:w + cx]
        for j in range(1, CONV_K):
            acc = acc + cw[CONV_K - 1 - j:CONV_K - j, :] * xpad_ref[pad + r0 - j:pad + r0 - j + rc, :]
        xbc_ref[r0:r0 + rc, :] = jax.nn.silu(acc)
    st_ref[...] = jnp.zeros_like(st_ref)
    causal = lax.broadcasted_iota(jnp.int32, (L, L), 0) >= lax.broadcasted_iota(jnp.int32, (L, L), 1)
    gsz = w // ng
    rep = nh // ng

    def chunk(c, carry):
        r0 = pl.multiple_of(c * L, L)
        rows = pl.ds(r0, L)
        dt_all = _softplus(sm_ref[rows, :] + dtb_ref[...])
        acs_all = _cumsum_rows(aneg_ref[...] * dt_all)
        acs_t = acs_all.T
        ys = []
        for g in range(ng):
            bm = xbc_ref[rows, w + g * ns:w + (g + 1) * ns]
            cm = xbc_ref[rows, w + ng * ns + g * ns:w + ng * ns + (g + 1) * ns]
            cb = _bdot(cm, bm, _NT)
            for hh in range(rep):
                h = g * rep + hh
                cidx = 2 * nh + h
                acs_col, acs_row = acs_all[:, cidx:cidx + 1], acs_t[cidx:cidx + 1, :]
                dt_col = dt_all[:, cidx:cidx + 1]
                a_last = acs_col[L - 1:L, :]
                xh = xbc_ref[rows, h * hp:(h + 1) * hp]
                xdt = xh * dt_col
                state = st_ref[h]
                scores = cb * jnp.exp(jnp.where(causal, acs_col - acs_row, -jnp.inf))
                y = _bdot(scores, xdt) + jnp.exp(acs_col) * _bdot(cm, state, _NT) + xh * dsk_ref[:, h * hp:(h + 1) * hp]
                st_ref[h] = jnp.exp(a_last) * state + _bdot(xdt * jnp.exp(a_last - acs_col), bm, _TN)
                ys.append(y)
        z = ps_ref[rows, 0:w]
        for g in range(ng):
            yg = jnp.concatenate(ys[g * rep:(g + 1) * rep], axis=1) * jax.nn.silu(z[:, g * gsz:(g + 1) * gsz])
            yg = yg * lax.rsqrt(jnp.mean(yg * yg, axis=1, keepdims=True) + RMS_EPS) * nw_ref[:, g * gsz:(g + 1) * gsz]
            o_ref[rows, g * gsz:(g + 1) * gsz] = yg
        return carry

    lax.fori_loop(0, seq // L, chunk, 0)


def _ssd_group(ps, psm, bsz, conv_w, conv_b, dt_bias, a_log, d_skip, norm_w):
    t = ps.shape[0]
    seq = t // bsz
    w, nh = GROUP_W, GROUP_HEADS
    lane_vec = lambda v: jnp.pad(v, (2 * nh, LANES - 3 * nh))[None, :]
    dsk = jnp.repeat(d_skip, HEAD_DIM)[None, :]
    const = lambda shape: pl.BlockSpec(shape, lambda b: (0,) * len(shape))
    return pl.pallas_call(
        functools.partial(_ssd_kernel, rc=256),
        out_shape=jax.ShapeDtypeStruct((t, w), jnp.float32),
        grid=(bsz,),
        in_specs=[pl.BlockSpec((seq, w + SSD_XBC), lambda b: (b, 0)), pl.BlockSpec((seq, LANES), lambda b: (b, 0)),
                  const((CONV_K, SSD_XBC)), const((1, SSD_XBC)), const((1, LANES)), const((1, LANES)),
                  const((1, w)), const((1, w))],
        out_specs=pl.BlockSpec((seq, w), lambda b: (b, 0)),
        scratch_shapes=[pltpu.VMEM((seq + SUBLANES, SSD_XBC), jnp.float32),
                        pltpu.VMEM((seq, SSD_XBC), jnp.float32),
                        pltpu.VMEM((nh, HEAD_DIM, SSD_STATE), jnp.float32)],
        compiler_params=pltpu.CompilerParams(
            dimension_semantics=("parallel",), vmem_limit_bytes=VMEM_LIMIT),
        name="ssd",
    )(ps, psm, conv_w, conv_b[None, :], lane_vec(dt_bias), lane_vec(-jnp.exp(a_log)), dsk, norm_w[None, :])


def _layer_norm(x, g, b):
    xc = x - jnp.mean(x, -1, keepdims=True)
    var = jnp.mean(xc * xc, -1, keepdims=True)
    return (xc * lax.rsqrt(var + LN_EPS)) * g + b


def _outproj_ln_kernel(x_ref, m0_ref, m1_ref, m2_ref, m3_ref, w_ref, g_ref, b_ref, *rest):
    gw = m0_ref.shape[1]
    acc = ALPHA * x_ref[...]
    for j, m_ref in enumerate((m0_ref, m1_ref, m2_ref, m3_ref)):
        acc = acc + _bdot(m_ref[...], w_ref[j * gw:(j + 1) * gw, :])
    x1 = _layer_norm(acc, g_ref[...], b_ref[...])
    if len(rest) == 1:
        rest[0][...] = x1
    else:
        wr_ref, o_ref, lg_ref = rest
        o_ref[...] = x1
        lg_ref[...] = _bdot(x1, wr_ref[...])


def _outproj_ln(x, mixers, w_out, g, b, w_router=None, tm=512):
    t, d = x.shape
    gw = mixers[0].shape[1]
    row = lambda n: pl.BlockSpec((tm, n), lambda i: (i, 0))
    const = lambda shape: pl.BlockSpec(shape, lambda i: (0, 0))
    in_specs = [row(d)] + [row(gw)] * 4 + [const((d, d)), const((1, d)), const((1, d))]
    args = [x, *mixers, w_out.astype(jnp.bfloat16), g[None, :], b[None, :]]
    out_shape = [jax.ShapeDtypeStruct((t, d), jnp.float32)]
    out_specs = [row(d)]
    if w_router is not None:
        wr = jnp.pad(w_router, ((0, 0), (0, LANES - w_router.shape[1]))).astype(jnp.bfloat16)
        in_specs.append(const((d, LANES)))
        args.append(wr)
        out_shape.append(jax.ShapeDtypeStruct((t, LANES), jnp.float32))
        out_specs.append(row(LANES))
    return pl.pallas_call(
        _outproj_ln_kernel,
        out_shape=out_shape, grid=(t // tm,), in_specs=in_specs, out_specs=out_specs,
        compiler_params=pltpu.CompilerParams(
            dimension_semantics=("parallel",), vmem_limit_bytes=VMEM_LIMIT),
        name="outproj_ln",
    )(*args)


def _pe_ln_kernel(x_ref, ff_ref, p_ref, wp_ref, wg_ref, bg_ref, g_ref, b_ref, o_ref):
    x1 = x_ref[...]
    pe = _bdot(p_ref[...], wp_ref[...]) * jax.nn.sigmoid(_bdot(x1, wg_ref[...]) + bg_ref[...])
    o_ref[...] = _layer_norm(ALPHA * x1 + ff_ref[...] + pe, g_ref[...], b_ref[...])


def _pe_ln(x1, ff, p, pe_proj, gate_w, gate_b, g, b, tm=512):
    t, d = x1.shape
    pd = p.shape[1]
    row = lambda n: pl.BlockSpec((tm, n), lambda i: (i, 0))
    const = lambda shape: pl.BlockSpec(shape, lambda i: (0, 0))
    return pl.pallas_call(
        _pe_ln_kernel,
        out_shape=jax.ShapeDtypeStruct((t, d), jnp.float32),
        grid=(t // tm,),
        in_specs=[row(d), row(d), row(pd), const((pd, d)), const((d, d)), const((1, d)), const((1, d)), const((1, d))],
        out_specs=row(d),
        compiler_params=pltpu.CompilerParams(
            dimension_semantics=("parallel",), vmem_limit_bytes=VMEM_LIMIT),
        name="pe_ln",
    )(x1, ff, p, pe_proj.astype(jnp.bfloat16), gate_w.astype(jnp.bfloat16), gate_b[None, :], g[None, :], b[None, :])


def _head_sums(x):
    rows = x.shape[0]
    parts = []
    for h in range(GROUP_HEADS):
        s = jnp.sum(x[:, h * HEAD_DIM:(h + 1) * HEAD_DIM], axis=1, keepdims=True)
        parts.append(jnp.broadcast_to(s, (rows, HEAD_DIM)))
    return jnp.concatenate(parts, axis=1)


def _rwkv_pre_kernel(c_ref, mu_ref, w0_ref, w2_ref, a0_ref, a2_ref, g2_ref, kk_ref, ka_ref,
                     r_ref, km_ref, v_ref, d_ref, kkn_ref, kka_ref, g_ref, *, rc):
    seq = c_ref.shape[0]
    w = GROUP_W
    o1 = 3 * w + RWKV_DECAY_RANK
    o2 = o1 + RWKV_ICL_RANK
    row = lax.broadcasted_iota(jnp.int32, (rc, c_ref.shape[1]), 0)
    for c in range(seq // rc):
        r0 = c * rc
        cur = c_ref[r0:r0 + rc, :]
        if c == 0:
            prev = jnp.where(row == 0, 0.0, pltpu.roll(cur, 1, 0))
        else:
            prev = c_ref[r0 - 1:r0 - 1 + rc, :]
        cols = cur + (prev - cur) * mu_ref[...]
        k = cols[:, w:2 * w]
        w_log = -_softplus(-(w0_ref[...] + _bdot(jnp.tanh(cols[:, 3 * w:o1]), w2_ref[...]))) - 0.5
        a = jax.nn.sigmoid(a0_ref[...] + _bdot(cols[:, o1:o2], a2_ref[...]))
        kk = k * kk_ref[...]
        kk = kk * lax.rsqrt(jnp.maximum(_head_sums(kk * kk), 1e-24))
        rows = slice(r0, r0 + rc)
        r_ref[rows, :] = cols[:, 0:w]
        km_ref[rows, :] = k * (1.0 + (a - 1.0) * ka_ref[...])
        v_ref[rows, :] = cols[:, 2 * w:3 * w]
        d_ref[rows, :] = jnp.exp(-jnp.exp(w_log))
        kkn_ref[rows, :] = kk
        kka_ref[rows, :] = kk * a
        g_ref[rows, :] = _bdot(jax.nn.sigmoid(cols[:, o2:]), g2_ref[...])


def _rwkv_post_kernel(y_ref, r_ref, km_ref, v_ref, g_ref, rk_ref, lw_ref, lb_ref, o_ref):
    inv = 1.0 / HEAD_DIM
    y = y_ref[...]
    yc = y - _head_sums(y) * inv
    yn = yc * lax.rsqrt(_head_sums(yc * yc) * inv + RWKV_GN_EPS) * lw_ref[...] + lb_ref[...]
    bonus = _head_sums(r_ref[...] * km_ref[...] * rk_ref[...]) * v_ref[...]
    o_ref[...] = (yn + bonus) * g_ref[...]


def _rwkv7_group(pr, bsz, mu, w0, w2, a0, a2, g2, k_k, k_a, r_k, ln_w, ln_b, tm=512):
    t = pr.shape[0]
    seq = t // bsz
    w = GROUP_W
    bf = lambda x: x.astype(jnp.bfloat16)
    const = lambda shape: pl.BlockSpec(shape, lambda b: (0, 0))
    vec = lambda x: x.reshape(1, -1)
    seq_spec = pl.BlockSpec((seq, w), lambda b: (b, 0))
    nat = jax.ShapeDtypeStruct((t, w), jnp.float32)
    r, kmod, v, decay, kk, kka, g = pl.pallas_call(
        functools.partial(_rwkv_pre_kernel, rc=256),
        out_shape=[nat] * 7,
        grid=(bsz,),
        in_specs=[pl.BlockSpec((seq, R_COLS), lambda b: (b, 0)), const((1, R_COLS)), const((1, w)),
                  const((RWKV_DECAY_RANK, w)), const((1, w)), const((RWKV_ICL_RANK, w)),
                  const((RWKV_GATE_RANK, w)), const((1, w)), const((1, w))],
        out_specs=[seq_spec] * 7,
        compiler_params=pltpu.CompilerParams(
            dimension_semantics=("parallel",), vmem_limit_bytes=VMEM_LIMIT),
        name="rwkv_pre",
    )(pr, vec(mu), vec(w0), bf(w2), vec(a0), bf(a2), bf(g2), vec(k_k), vec(k_a))
    sh = lambda x: x.reshape(bsz, seq, w)
    y = _rwkv_scan(sh(r), sh(decay), sh(kmod), sh(v), sh(kk), sh(kka)).reshape(t, w)
    row = pl.BlockSpec((tm, w), lambda i: (i, 0))
    return pl.pallas_call(
        _rwkv_post_kernel,
        out_shape=nat,
        grid=(t // tm,),
        in_specs=[row] * 5 + [const((1, w))] * 3,
        out_specs=row,
        compiler_params=pltpu.CompilerParams(
            dimension_semantics=("parallel",), vmem_limit_bytes=VMEM_LIMIT),
        name="rwkv_post",
    )(y, r, kmod, v, g, vec(r_k), vec(ln_w), vec(ln_b))


FF_CHUNK = 512


def _swiglu_kernel(be_ref, x_ref, w1_ref, w3_ref, w2_ref, g_ref, o_ref):
    del be_ref
    xb = x_ref[...].astype(jnp.bfloat16)
    dff = w1_ref.shape[-1]
    acc = None
    for c0 in range(0, dff, FF_CHUNK):
        c1 = min(c0 + FF_CHUNK, dff)
        h1 = jnp.dot(xb, w1_ref[0, :, c0:c1], preferred_element_type=jnp.float32)
        h3 = jnp.dot(xb, w3_ref[0, :, c0:c1], preferred_element_type=jnp.float32)
        h = (jax.nn.silu(h1) * h3).astype(jnp.bfloat16)
        part = jnp.dot(h, w2_ref[0, c0:c1, :], preferred_element_type=jnp.float32)
        acc = part if acc is None else acc + part
    o_ref[...] = acc * g_ref[...]


def _grouped_swiglu(xrows, block_e, row_gate, w1, w3, w2, bm):
    rows, d = xrows.shape
    dff = w1.shape[-1]
    grid_spec = pltpu.PrefetchScalarGridSpec(
        num_scalar_prefetch=1,
        grid=(rows // bm,),
        in_specs=[pl.BlockSpec((bm, d), lambda i, be: (i, 0)),
                  pl.BlockSpec((1, d, dff), lambda i, be: (be[i], 0, 0)),
                  pl.BlockSpec((1, d, dff), lambda i, be: (be[i], 0, 0)),
                  pl.BlockSpec((1, dff, d), lambda i, be: (be[i], 0, 0)),
                  pl.BlockSpec((bm, 1), lambda i, be: (i, 0))],
        out_specs=pl.BlockSpec((bm, d), lambda i, be: (i, 0)),
    )
    return pl.pallas_call(
        _swiglu_kernel,
        out_shape=jax.ShapeDtypeStruct((rows, d), jnp.float32),
        grid_spec=grid_spec,
        compiler_params=pltpu.CompilerParams(
            dimension_semantics=("arbitrary",), vmem_limit_bytes=VMEM_LIMIT),
        name="swiglu",
    )(block_e, xrows, w1, w3, w2, row_gate)


def _swiglu(x2d, w1, w3, w2, bm=256):
    rows = x2d.shape[0]
    bf = lambda t: t.astype(jnp.bfloat16)[None]
    return _grouped_swiglu(x2d, jnp.zeros((rows // bm,), jnp.int32), jnp.ones((rows, 1), jnp.float32),
                           bf(w1), bf(w3), bf(w2), bm)


def _moe_swiglu(xf, logits, w1, w3, w2):
    T, d = xf.shape
    top_logit, top_e = lax.top_k(logits, TOP_K)
    gate = jax.nn.softmax(top_logit, axis=-1)
    flat_e = top_e.reshape(-1)
    flat_tok = jnp.repeat(jnp.arange(T, dtype=jnp.int32), TOP_K)
    order = jnp.argsort(flat_e)
    se, st, sg = flat_e[order], flat_tok[order], gate.reshape(-1)[order]
    counts = jnp.bincount(flat_e, length=N_EXPERTS)
    padded = (counts + MOE_BLOCK - 1) // MOE_BLOCK * MOE_BLOCK
    pad_end = jnp.cumsum(padded)
    start = jnp.cumsum(counts) - counts
    dest = (pad_end - padded)[se] + jnp.arange(T * TOP_K) - start[se]
    n_blocks = -(-(T * TOP_K) // MOE_BLOCK) + N_EXPERTS
    rows = n_blocks * MOE_BLOCK
    slot_tok = jnp.zeros((rows,), jnp.int32).at[dest].set(st)
    slot_gate = jnp.zeros((rows,), xf.dtype).at[dest].set(sg)
    block_e = jnp.minimum(jnp.searchsorted(pad_end, jnp.arange(n_blocks) * MOE_BLOCK, side='right'), N_EXPERTS - 1)
    bf = lambda t: t.astype(jnp.bfloat16)
    yb = _grouped_swiglu(xf[slot_tok], block_e.astype(jnp.int32), slot_gate[:, None], bf(w1), bf(w3), bf(w2),
                         MOE_BLOCK)
    return jnp.zeros((T, d), xf.dtype).at[slot_tok].add(yb)


def kernel(x, p, w_in, m_i_bias, m_f_bias, m_norm_w, g_conv_w, g_conv_b, g_w_a, g_b_a, g_w_x, g_b_x,
           g_lambda, r_mu, r_w0, r_w2, r_a0, r_a2, r_g2, r_k_k, r_k_a, r_r_k, r_ln_w, r_ln_b,
           s_conv_w, s_conv_b, s_dt_bias, s_a_log, s_d, s_norm_w, w_out, ln1_g, ln1_b, ln2_g, ln2_b,
           f_w1, f_w3, f_w2, e_router, e_w1, e_w3, e_w2, pe_proj, pe_gate_w, pe_gate_b):
    bsz, seq, d = x.shape
    T = bsz * seq
    x = x.reshape(T, d)
    for i in range(DEPTH):
        pm, pg, pr, ps, psm = _inproj(x, _split_w_in(w_in[i]))
        mixers = [
            _mlstm_group(pm, psm, bsz, m_i_bias[i], m_f_bias[i], m_norm_w[i]),
            _rglru_group(pg, bsz, g_conv_w[i], g_conv_b[i], g_w_a[i], g_b_a[i], g_w_x[i], g_b_x[i], g_lambda[i]),
            _rwkv7_group(pr, bsz, r_mu[i], r_w0[i], r_w2[i], r_a0[i], r_a2[i], r_g2[i],
                         r_k_k[i], r_k_a[i], r_r_k[i], r_ln_w[i], r_ln_b[i]),
            _ssd_group(ps, psm, bsz, s_conv_w[i], s_conv_b[i], s_dt_bias[i], s_a_log[i], s_d[i], s_norm_w[i]),
        ]
        if i % 2 == 0:
            (x,) = _outproj_ln(x, mixers, w_out[i], ln1_g[i], ln1_b[i])
            ff = _swiglu(x, f_w1[i // 2], f_w3[i // 2], f_w2[i // 2])
        else:
            x, logits = _outproj_ln(x, mixers, w_out[i], ln1_g[i], ln1_b[i], e_router[i // 2])
            ff = _moe_swiglu(x, logits[:, :N_EXPERTS], e_w1[i // 2], e_w3[i // 2], e_w2[i // 2])
        x = _pe_ln(x, ff, p[i].reshape(T, PE_DIM), pe_proj[i], pe_gate_w[i], pe_gate_b[i], ln2_g[i], ln2_b[i])
    return x.reshape(bsz, seq, d)
```

```python
import functools

import jax
import jax.numpy as jnp
from jax import lax
from jax.experimental import pallas as pl
from jax.experimental.pallas import tpu as pltpu

D_MODEL = 1024
DEPTH = 4
PE_DIM = 256
GROUP_W = 256
HEAD_DIM = 64
GROUP_HEADS = 4
CONV_K = 4
MLSTM_CHUNK = 64
RG_C = 8.0
RWKV_DECAY_RANK = 32
RWKV_ICL_RANK = 32
RWKV_GATE_RANK = 64
RWKV_GN_EPS = 64e-5
SSD_STATE = 64
SSD_GROUPS = 2
SSD_CHUNK = 128
SSD_XBC = GROUP_W + 2 * SSD_GROUPS * SSD_STATE
M_COLS = 4 * GROUP_W + 2 * GROUP_HEADS
G_COLS = 2 * GROUP_W
R_COLS = 3 * GROUP_W + RWKV_DECAY_RANK + RWKV_ICL_RANK + RWKV_GATE_RANK
S_COLS = GROUP_W + SSD_XBC + GROUP_HEADS
N_IN = M_COLS + G_COLS + R_COLS + S_COLS
N_EXPERTS = 8
TOP_K = 2
MOE_BLOCK = 256
ALPHA = (2 * DEPTH) ** 0.25
LN_EPS = 1e-5
RMS_EPS = 1e-6

LANES = 128
SUBLANES = 8
VMEM_LIMIT = 56 * 1024 * 1024


RWKV_TB = 32


def _rwkv_scan_kernel(pa_ref, pb_ref, pc_ref, y_ref, s_ref, sa_ref, ta_ref, tb_ref, tc_ref, yb_ref,
                      *, tb, nb, nk, nacc):
    @pl.when(pl.program_id(0) == 0)
    def _():
        s_ref[...] = jnp.zeros_like(s_ref)
        sa_ref[...] = jnp.zeros_like(sa_ref)

    nh = GROUP_HEADS
    half = nk // 2
    lane = lax.broadcasted_iota(jnp.int32, (half, LANES), 1)
    rows = lambda t, h: pl.ds(h * nb * tb + t, nb, stride=tb)

    def prepare(t, carry):
        for src, dst in ((pa_ref, ta_ref), (pb_ref, tb_ref), (pc_ref, tc_ref)):
            parts = [src[rows(t, h), :] for h in range(nh)]
            dst[t] = jnp.concatenate(parts + parts, axis=0).T
        return carry

    lax.fori_loop(0, tb, prepare, 0, unroll=8)

    def step(t, sa):
        bc = lambda ref, r: jnp.broadcast_to(ref[t, pl.ds(r, 1), :], (half, LANES))
        v_t = jnp.where(lane < nk, tc_ref[t, nk:nk + half, :], tc_ref[t, nk + half:2 * nk, :])
        y_acc = [None] * nacc
        sa_acc = [None] * nacc
        for k in range(nk):
            s_k = s_ref[k] * bc(ta_ref, k) - sa * bc(ta_ref, nk + k) + v_t * bc(tb_ref, k)
            s_ref[k] = s_k
            y_k = s_k * bc(tb_ref, nk + k)
            n_k = s_k * bc(tc_ref, k)
            a = k % nacc
            y_acc[a] = y_k if y_acc[a] is None else y_acc[a] + y_k
            sa_acc[a] = n_k if sa_acc[a] is None else sa_acc[a] + n_k
        yb_ref[t] = functools.reduce(lambda p, q: p + q, y_acc)
        return functools.reduce(lambda p, q: p + q, sa_acc)

    sa_ref[...] = lax.fori_loop(0, tb, step, sa_ref[...])

    def emit(t, carry):
        y = yb_ref[t]
        y_sw = pltpu.roll(y, nk, 1)
        y_nat = jnp.concatenate([y, y_sw, y, y_sw], axis=0).T
        for h in range(nh):
            y_ref[rows(t, h), :] = y_nat[h * nb:(h + 1) * nb, :]
        return carry

    lax.fori_loop(0, tb, emit, 0, unroll=8)


def _rwkv_scan(pa, pb, pc, bsz):
    nblk, nh, _, tb, _ = pa.shape
    n = HEAD_DIM
    assert 2 * bsz * nh == LANES and 2 * n == LANES
    flat = lambda t: t.reshape(nblk * nh * bsz * tb, LANES)
    spec = pl.BlockSpec((nh * bsz * tb, LANES), lambda i: (i, 0))
    y = pl.pallas_call(
        functools.partial(_rwkv_scan_kernel, tb=tb, nb=bsz, nk=n, nacc=2),
        out_shape=jax.ShapeDtypeStruct((nblk * nh * bsz * tb, LANES), jnp.float32),
        grid=(nblk,),
        in_specs=[spec, spec, spec],
        out_specs=spec,
        scratch_shapes=[pltpu.VMEM((n, n // 2, LANES), jnp.float32), pltpu.VMEM((n // 2, LANES), jnp.float32)]
        + [pltpu.VMEM((tb, LANES, LANES), jnp.float32)] * 3 + [pltpu.VMEM((tb, n // 2, LANES), jnp.float32)],
        compiler_params=pltpu.CompilerParams(
            dimension_semantics=("arbitrary",), vmem_limit_bytes=VMEM_LIMIT),
        name="rwkv_scan",
    )(flat(pa), flat(pb), flat(pc))
    return y.reshape(pa.shape)


def _inproj_kernel(x_ref, *refs):
    n = len(refs) // 2
    xb = x_ref[...].astype(jnp.bfloat16)
    for w_ref, o_ref in zip(refs[:n], refs[n:]):
        o_ref[...] = jnp.dot(xb, w_ref[...], preferred_element_type=jnp.float32)


def _inproj(x, weights, tm=512):
    t, d = x.shape
    return pl.pallas_call(
        _inproj_kernel,
        out_shape=[jax.ShapeDtypeStruct((t, w.shape[1]), jnp.float32) for w in weights],
        grid=(t // tm,),
        in_specs=[pl.BlockSpec((tm, d), lambda i: (i, 0))]
        + [pl.BlockSpec(w.shape, lambda i: (0, 0)) for w in weights],
        out_specs=[pl.BlockSpec((tm, w.shape[1]), lambda i: (i, 0)) for w in weights],
        compiler_params=pltpu.CompilerParams(
            dimension_semantics=("parallel",), vmem_limit_bytes=VMEM_LIMIT),
        name="inproj",
    )(x, *weights)


def _split_w_in(w):
    m0, g0, r0, s0 = 0, M_COLS, M_COLS + G_COLS, M_COLS + G_COLS + R_COLS
    w_m = w[:, m0:m0 + 4 * GROUP_W]
    w_g = w[:, g0:g0 + G_COLS]
    w_r = w[:, r0:r0 + R_COLS]
    w_s = w[:, s0:s0 + GROUP_W + SSD_XBC]
    small = jnp.concatenate([w[:, m0 + 4 * GROUP_W:g0], w[:, s0 + GROUP_W + SSD_XBC:]], axis=1)
    small = jnp.pad(small, ((0, 0), (0, LANES - small.shape[1])))
    return [t.astype(jnp.bfloat16) for t in (w_m, w_g, w_r, w_s, small)]


def _shift_rows(x, s, fill, row):
    return jnp.where(row >= s, pltpu.roll(x, s, 0), fill)


def _rglru_kernel(g_ref, cw_ref, cb_ref, wg_ref, bg_ref, c_ref, o_ref, xpad_ref, a_ref, u_ref, *, rc, unroll):
    seq, w = o_ref.shape
    pad = SUBLANES
    xpad_ref[0:pad, :] = jnp.zeros((pad, w), jnp.float32)
    xpad_ref[pad:, :] = g_ref[:, 0:w]
    cw = cw_ref[...]
    for c in range(seq // rc):
        r0 = c * rc
        xc = cb_ref[...] + cw[CONV_K - 1:CONV_K, :] * g_ref[r0:r0 + rc, 0:w]
        for j in range(1, CONV_K):
            xc = xc + cw[CONV_K - 1 - j:CONV_K - j, :] * xpad_ref[pad + r0 - j:pad + r0 - j + rc, :]
        z = jnp.dot(xc.astype(jnp.bfloat16), wg_ref[...], preferred_element_type=jnp.float32) + bg_ref[...]
        z = jax.nn.sigmoid(z)
        a = jnp.exp(c_ref[...] * z[:, 0:w])
        a_ref[r0:r0 + rc, :] = a
        u_ref[r0:r0 + rc, :] = jnp.sqrt(1.0 - a * a) * (z[:, w:] * xc)

    row = lax.broadcasted_iota(jnp.int32, (SUBLANES, w), 0)

    def tiles(i, h):
        for j in range(unroll):
            r = pl.multiple_of((i * unroll + j) * SUBLANES, SUBLANES)
            a = a_ref[pl.ds(r, SUBLANES), :]
            u = u_ref[pl.ds(r, SUBLANES), :]
            for s in (1, 2, 4):
                u = a * _shift_rows(u, s, 0.0, row) + u
                a = a * _shift_rows(a, s, 1.0, row)
            ht = a * h + u
            o_ref[pl.ds(r, SUBLANES), :] = ht * jax.nn.gelu(g_ref[pl.ds(r, SUBLANES), w:2 * w])
            h = jnp.broadcast_to(ht[SUBLANES - 1:SUBLANES, :], (SUBLANES, w))
        return h

    lax.fori_loop(0, seq // (SUBLANES * unroll), tiles, jnp.zeros((SUBLANES, w), jnp.float32))


def _block_diag(w):
    h, p, _ = w.shape
    eye = jnp.eye(h, dtype=w.dtype)
    return jnp.einsum('hij,hg->higj', w, eye).reshape(h * p, h * p)


def _rglru_group(g_cols, bsz, conv_w, conv_b, w_a, b_a, w_x, b_x, lam):
    t = g_cols.shape[0]
    seq = t // bsz
    w = GROUP_W
    wg = jnp.concatenate([_block_diag(w_a), _block_diag(w_x)], axis=1).astype(jnp.bfloat16)
    bg = jnp.concatenate([b_a, b_x])[None, :]
    cdec = (-RG_C * jax.nn.softplus(-lam))[None, :]
    const = lambda shape: pl.BlockSpec(shape, lambda b: (0, 0))
    return pl.pallas_call(
        functools.partial(_rglru_kernel, rc=256, unroll=8),
        out_shape=jax.ShapeDtypeStruct((t, w), jnp.float32),
        grid=(bsz,),
        in_specs=[pl.BlockSpec((seq, 2 * w), lambda b: (b, 0)), const((CONV_K, w)), const((1, w)),
                  const((w, 2 * w)), const((1, 2 * w)), const((1, w))],
        out_specs=pl.BlockSpec((seq, w), lambda b: (b, 0)),
        scratch_shapes=[pltpu.VMEM((seq + SUBLANES, w), jnp.float32),
                        pltpu.VMEM((seq, w), jnp.float32), pltpu.VMEM((seq, w), jnp.float32)],
        compiler_params=pltpu.CompilerParams(
            dimension_semantics=("parallel",), vmem_limit_bytes=VMEM_LIMIT),
        name="rglru",
    )(g_cols, conv_w, conv_b[None, :], wg, bg, cdec)


_NT = (((1,), (1,)), ((), ()))
_TN = (((0,), (0,)), ((), ()))


def _bdot(a, b, dims=(((1,), (0,)), ((), ()))):
    return lax.dot_general(a.astype(jnp.bfloat16), b.astype(jnp.bfloat16), dims,
                           preferred_element_type=jnp.float32)


def _cumsum_rows(x):
    n = x.shape[0]
    tri = (lax.broadcasted_iota(jnp.int32, (n, n), 0) >= lax.broadcasted_iota(jnp.int32, (n, n), 1))
    return jnp.dot(tri.astype(jnp.float32), x, precision=lax.Precision.HIGHEST,
                   preferred_element_type=jnp.float32)


def _log_sigmoid(x):
    return jnp.minimum(x, 0.0) - jnp.log1p(jnp.exp(-jnp.abs(x)))


def _softplus(x):
    return jnp.maximum(x, 0.0) + jnp.log1p(jnp.exp(-jnp.abs(x)))


def _mlstm_kernel(pm_ref, sm_ref, bias_ref, nw_ref, o_ref, c_ref, n_ref, m_ref):
    seq, w = o_ref.shape
    nh, dh, L = GROUP_HEADS, HEAD_DIM, MLSTM_CHUNK
    c_ref[...] = jnp.zeros_like(c_ref)
    n_ref[...] = jnp.zeros_like(n_ref)
    m_ref[...] = jnp.zeros_like(m_ref)
    col = lax.broadcasted_iota(jnp.int32, (L, LANES), 1)
    causal = lax.broadcasted_iota(jnp.int32, (L, L), 0) >= lax.broadcasted_iota(jnp.int32, (L, L), 1)

    def chunk(c, carry):
        r0 = pl.multiple_of(c * L, L)
        rows = pl.ds(r0, L)
        gts = sm_ref[rows, :] + bias_ref[...]
        gts = jnp.where(col < nh, gts, jnp.where(col < 2 * nh, _log_sigmoid(gts), 0.0))
        bcs = _cumsum_rows(gts)
        gts_t = gts.T
        bcs_t = bcs.T
        for h in range(nh):
            b_col, li_col = bcs[:, nh + h:nh + h + 1], gts[:, h:h + 1]
            b_row, li_row = bcs_t[nh + h:nh + h + 1, :], gts_t[h:h + 1, :]
            g = b_col[L - 1:L, :]
            m = m_ref[h]
            q = pm_ref[rows, h * dh:(h + 1) * dh]
            k = pm_ref[rows, w + h * dh:w + (h + 1) * dh] * (dh ** -0.5)
            v = pm_ref[rows, 2 * w + h * dh:2 * w + (h + 1) * dh]
            o = pm_ref[rows, 3 * w + h * dh:3 * w + (h + 1) * dh]
            cst, nst = c_ref[h], n_ref[h]
            log_d = jnp.where(causal, b_col - b_row + li_row, -jnp.inf)
            m_t = jnp.maximum(b_col + m, jnp.max(log_d, axis=1, keepdims=True))
            s = _bdot(q, k, _NT) * jnp.exp(log_d - m_t)
            inter_w = jnp.exp(b_col + m - m_t)
            num = _bdot(s, v) + inter_w * _bdot(q, cst, _NT)
            den = jnp.sum(s, axis=1, keepdims=True) + inter_w * jnp.sum(q * nst, axis=1, keepdims=True)
            hh = num / jnp.maximum(jnp.abs(den), jnp.exp(-m_t))
            hh = hh * lax.rsqrt(jnp.mean(hh * hh, axis=1, keepdims=True) + RMS_EPS) * nw_ref[:, h * dh:(h + 1) * dh]
            o_ref[rows, h * dh:(h + 1) * dh] = hh * jax.nn.sigmoid(o)
            a_row = g - b_row + li_row
            m_new = jnp.maximum(g + m, jnp.max(a_row, axis=1, keepdims=True))
            decay = jnp.exp(g + m - m_new)
            w_col = jnp.exp(g - b_col + li_col - m_new)
            c_ref[h] = decay * cst + _bdot(w_col * v, k, _TN)
            n_ref[h] = decay * nst + jnp.sum(w_col * k, axis=0, keepdims=True)
            m_ref[h] = m_new
        return carry

    lax.fori_loop(0, seq // L, chunk, 0)


def _mlstm_group(pm, psm, bsz, i_bias, f_bias, norm_w):
    t = pm.shape[0]
    seq = t // bsz
    w, nh, dh = GROUP_W, GROUP_HEADS, HEAD_DIM
    bias = jnp.pad(jnp.concatenate([i_bias, f_bias]), (0, LANES - 2 * nh))[None, :]
    const = lambda shape: pl.BlockSpec(shape, lambda b: (0,) * len(shape))
    return pl.pallas_call(
        _mlstm_kernel,
        out_shape=jax.ShapeDtypeStruct((t, w), jnp.float32),
        grid=(bsz,),
        in_specs=[pl.BlockSpec((seq, 4 * w), lambda b: (b, 0)), pl.BlockSpec((seq, LANES), lambda b: (b, 0)),
                  const((1, LANES)), const((1, w))],
        out_specs=pl.BlockSpec((seq, w), lambda b: (b, 0)),
        scratch_shapes=[pltpu.VMEM((nh, dh, dh), jnp.float32), pltpu.VMEM((nh, 1, dh), jnp.float32),
                        pltpu.VMEM((nh, 1, 1), jnp.float32)],
        compiler_params=pltpu.CompilerParams(
            dimension_semantics=("parallel",), vmem_limit_bytes=VMEM_LIMIT),
        name="mlstm",
    )(pm, psm, bias, norm_w[None, :])


def _ssd_kernel(ps_ref, sm_ref, cw_ref, cb_ref, dtb_ref, aneg_ref, dsk_ref, nw_ref, o_ref,
                xpad_ref, xbc_ref, st_ref, *, rc):
    seq, w = o_ref.shape
    nh, hp, ng, ns, L = GROUP_HEADS, HEAD_DIM, SSD_GROUPS, SSD_STATE, SSD_CHUNK
    cx = SSD_XBC
    pad = SUBLANES
    xpad_ref[0:pad, :] = jnp.zeros((pad, cx), jnp.float32)
    xpad_ref[pad:, :] = ps_ref[:, w:w + cx]
    cw = cw_ref[...]
    for c in range(seq // rc):
        r0 = c * rc
        acc = cb_ref[...] + cw[CONV_K - 1:CONV_K, :] * ps_ref[r0:r0 + rc, w:w + cx]
        for j in range(1, CONV_K):
            acc = acc + cw[CONV_K - 1 - j:CONV_K - j, :] * xpad_ref[pad + r0 - j:pad + r0 - j + rc, :]
        xbc_ref[r0:r0 + rc, :] = jax.nn.silu(acc)
    st_ref[...] = jnp.zeros_like(st_ref)
    causal = lax.broadcasted_iota(jnp.int32, (L, L), 0) >= lax.broadcasted_iota(jnp.int32, (L, L), 1)
    gsz = w // ng
    rep = nh // ng

    def chunk(c, carry):
        r0 = pl.multiple_of(c * L, L)
        rows = pl.ds(r0, L)
        dt_all = _softplus(sm_ref[rows, :] + dtb_ref[...])
        acs_all = _cumsum_rows(aneg_ref[...] * dt_all)
        acs_t = acs_all.T
        ys = []
        for g in range(ng):
            bm = xbc_ref[rows, w + g * ns:w + (g + 1) * ns]
            cm = xbc_ref[rows, w + ng * ns + g * ns:w + ng * ns + (g + 1) * ns]
            cb = _bdot(cm, bm, _NT)
            for hh in range(rep):
                h = g * rep + hh
                cidx = 2 * nh + h
                acs_col, acs_row = acs_all[:, cidx:cidx + 1], acs_t[cidx:cidx + 1, :]
                dt_col = dt_all[:, cidx:cidx + 1]
                a_last = acs_col[L - 1:L, :]
                xh = xbc_ref[rows, h * hp:(h + 1) * hp]
                xdt = xh * dt_col
                state = st_ref[h]
                scores = cb * jnp.exp(jnp.where(causal, acs_col - acs_row, -jnp.inf))
                y = _bdot(scores, xdt) + jnp.exp(acs_col) * _bdot(cm, state, _NT) + xh * dsk_ref[:, h * hp:(h + 1) * hp]
                st_ref[h] = jnp.exp(a_last) * state + _bdot(xdt * jnp.exp(a_last - acs_col), bm, _TN)
                ys.append(y)
        z = ps_ref[rows, 0:w]
        for g in range(ng):
            yg = jnp.concatenate(ys[g * rep:(g + 1) * rep], axis=1) * jax.nn.silu(z[:, g * gsz:(g + 1) * gsz])
            yg = yg * lax.rsqrt(jnp.mean(yg * yg, axis=1, keepdims=True) + RMS_EPS) * nw_ref[:, g * gsz:(g + 1) * gsz]
            o_ref[rows, g * gsz:(g + 1) * gsz] = yg
        return carry

    lax.fori_loop(0, seq // L, chunk, 0)


def _ssd_group(ps, psm, bsz, conv_w, conv_b, dt_bias, a_log, d_skip, norm_w):
    t = ps.shape[0]
    seq = t // bsz
    w, nh = GROUP_W, GROUP_HEADS
    lane_vec = lambda v: jnp.pad(v, (2 * nh, LANES - 3 * nh))[None, :]
    dsk = jnp.repeat(d_skip, HEAD_DIM)[None, :]
    const = lambda shape: pl.BlockSpec(shape, lambda b: (0,) * len(shape))
    return pl.pallas_call(
        functools.partial(_ssd_kernel, rc=256),
        out_shape=jax.ShapeDtypeStruct((t, w), jnp.float32),
        grid=(bsz,),
        in_specs=[pl.BlockSpec((seq, w + SSD_XBC), lambda b: (b, 0)), pl.BlockSpec((seq, LANES), lambda b: (b, 0)),
                  const((CONV_K, SSD_XBC)), const((1, SSD_XBC)), const((1, LANES)), const((1, LANES)),
                  const((1, w)), const((1, w))],
        out_specs=pl.BlockSpec((seq, w), lambda b: (b, 0)),
        scratch_shapes=[pltpu.VMEM((seq + SUBLANES, SSD_XBC), jnp.float32),
                        pltpu.VMEM((seq, SSD_XBC), jnp.float32),
                        pltpu.VMEM((nh, HEAD_DIM, SSD_STATE), jnp.float32)],
        compiler_params=pltpu.CompilerParams(
            dimension_semantics=("parallel",), vmem_limit_bytes=VMEM_LIMIT),
        name="ssd",
    )(ps, psm, conv_w, conv_b[None, :], lane_vec(dt_bias), lane_vec(-jnp.exp(a_log)), dsk, norm_w[None, :])


def _layer_norm(x, g, b):
    xc = x - jnp.mean(x, -1, keepdims=True)
    var = jnp.mean(xc * xc, -1, keepdims=True)
    return (xc * lax.rsqrt(var + LN_EPS)) * g + b


def _outproj_ln_kernel(x_ref, m0_ref, m1_ref, m2_ref, m3_ref, w_ref, g_ref, b_ref, *rest):
    gw = m0_ref.shape[1]
    acc = ALPHA * x_ref[...]
    for j, m_ref in enumerate((m0_ref, m1_ref, m2_ref, m3_ref)):
        acc = acc + _bdot(m_ref[...], w_ref[j * gw:(j + 1) * gw, :])
    x1 = _layer_norm(acc, g_ref[...], b_ref[...])
    if len(rest) == 1:
        rest[0][...] = x1
    else:
        wr_ref, o_ref, lg_ref = rest
        o_ref[...] = x1
        lg_ref[...] = _bdot(x1, wr_ref[...])


def _outproj_ln(x, mixers, w_out, g, b, w_router=None, tm=512):
    t, d = x.shape
    gw = mixers[0].shape[1]
    row = lambda n: pl.BlockSpec((tm, n), lambda i: (i, 0))
    const = lambda shape: pl.BlockSpec(shape, lambda i: (0, 0))
    in_specs = [row(d)] + [row(gw)] * 4 + [const((d, d)), const((1, d)), const((1, d))]
    args = [x, *mixers, w_out.astype(jnp.bfloat16), g[None, :], b[None, :]]
    out_shape = [jax.ShapeDtypeStruct((t, d), jnp.float32)]
    out_specs = [row(d)]
    if w_router is not None:
        wr = jnp.pad(w_router, ((0, 0), (0, LANES - w_router.shape[1]))).astype(jnp.bfloat16)
        in_specs.append(const((d, LANES)))
        args.append(wr)
        out_shape.append(jax.ShapeDtypeStruct((t, LANES), jnp.float32))
        out_specs.append(row(LANES))
    return pl.pallas_call(
        _outproj_ln_kernel,
        out_shape=out_shape, grid=(t // tm,), in_specs=in_specs, out_specs=out_specs,
        compiler_params=pltpu.CompilerParams(
            dimension_semantics=("parallel",), vmem_limit_bytes=VMEM_LIMIT),
        name="outproj_ln",
    )(*args)


def _pe_ln_kernel(x_ref, ff_ref, p_ref, wp_ref, wg_ref, bg_ref, g_ref, b_ref, o_ref):
    x1 = x_ref[...]
    pe = _bdot(p_ref[...], wp_ref[...]) * jax.nn.sigmoid(_bdot(x1, wg_ref[...]) + bg_ref[...])
    o_ref[...] = _layer_norm(ALPHA * x1 + ff_ref[...] + pe, g_ref[...], b_ref[...])


def _pe_ln(x1, ff, p, pe_proj, gate_w, gate_b, g, b, tm=512):
    t, d = x1.shape
    pd = p.shape[1]
    row = lambda n: pl.BlockSpec((tm, n), lambda i: (i, 0))
    const = lambda shape: pl.BlockSpec(shape, lambda i: (0, 0))
    return pl.pallas_call(
        _pe_ln_kernel,
        out_shape=jax.ShapeDtypeStruct((t, d), jnp.float32),
        grid=(t // tm,),
        in_specs=[row(d), row(d), row(pd), const((pd, d)), const((d, d)), const((1, d)), const((1, d)), const((1, d))],
        out_specs=row(d),
        compiler_params=pltpu.CompilerParams(
            dimension_semantics=("parallel",), vmem_limit_bytes=VMEM_LIMIT),
        name="pe_ln",
    )(x1, ff, p, pe_proj.astype(jnp.bfloat16), gate_w.astype(jnp.bfloat16), gate_b[None, :], g[None, :], b[None, :])


def _head_sums(x):
    rows = x.shape[0]
    parts = []
    for h in range(GROUP_HEADS):
        s = jnp.sum(x[:, h * HEAD_DIM:(h + 1) * HEAD_DIM], axis=1, keepdims=True)
        parts.append(jnp.broadcast_to(s, (rows, HEAD_DIM)))
    return jnp.concatenate(parts, axis=1)


def _rwkv_pre_kernel(c_ref, mu_ref, w0_ref, w2_ref, a0_ref, a2_ref, g2_ref, kk_ref, ka_ref,
                     pa_ref, pb_ref, pc_ref, g_ref, *, rc):
    seq = c_ref.shape[0]
    w, p = GROUP_W, HEAD_DIM
    o1 = 3 * w + RWKV_DECAY_RANK
    o2 = o1 + RWKV_ICL_RANK
    nchunks = seq // rc
    blk = rc // RWKV_TB
    row = lax.broadcasted_iota(jnp.int32, (rc, c_ref.shape[1]), 0)
    lane = lax.broadcasted_iota(jnp.int32, (rc, LANES), 1)

    def unit_keys(cols):
        kk = cols[:, w:2 * w] * kk_ref[...]
        return kk * lax.rsqrt(jnp.maximum(_head_sums(kk * kk), 1e-24))

    def pack(dst, c, x, z):
        for hp in range(w // LANES):
            x2, z2 = x[:, hp * LANES:(hp + 1) * LANES], z[:, hp * LANES:(hp + 1) * LANES]
            even = jnp.where(lane < p, x2, pltpu.roll(z2, p, 1))
            odd = jnp.where(lane < p, pltpu.roll(x2, p, 1), z2)
            for h, val in ((2 * hp, even), (2 * hp + 1, odd)):
                dst[c * blk:(c + 1) * blk, h, 0] = val.reshape(blk, RWKV_TB, LANES)

    for c in range(nchunks):
        r0 = c * rc
        cur = c_ref[r0:r0 + rc, :]
        if c == 0:
            prev = jnp.where(row == 0, 0.0, pltpu.roll(cur, 1, 0))
        else:
            prev = c_ref[r0 - 1:r0 - 1 + rc, :]
        if c == nchunks - 1:
            nxt = jnp.where(row == rc - 1, 0.0, pltpu.roll(cur, rc - 1, 0))
        else:
            nxt = c_ref[r0 + 1:r0 + 1 + rc, :]
        cols = cur + (prev - cur) * mu_ref[...]
        cols_next = nxt + (cur - nxt) * mu_ref[...]
        k = cols[:, w:2 * w]
        w_log = -_softplus(-(w0_ref[...] + _bdot(jnp.tanh(cols[:, 3 * w:o1]), w2_ref[...]))) - 0.5
        a = jax.nn.sigmoid(a0_ref[...] + _bdot(cols[:, o1:o2], a2_ref[...]))
        kk = unit_keys(cols)
        pack(pa_ref, c, jnp.exp(-jnp.exp(w_log)), kk * a)
        pack(pb_ref, c, k * (1.0 + (a - 1.0) * ka_ref[...]), cols[:, 0:w])
        pack(pc_ref, c, unit_keys(cols_next), cols[:, 2 * w:3 * w])
        g_ref[r0:r0 + rc, :] = _bdot(jax.nn.sigmoid(cols[:, o2:]), g2_ref[...])


def _rwkv_post_kernel(y_ref, pb_ref, pc_ref, g_ref, rk_ref, lw_ref, lb_ref, o_ref):
    p = HEAD_DIM
    rows = o_ref.shape[0]
    inv = 1.0 / p
    outs = []
    for h in range(GROUP_HEADS):
        hs = slice(h * p, (h + 1) * p)
        y = y_ref[:, h, 0].reshape(rows, LANES)[:, 0:p]
        kr = pb_ref[:, h, 0].reshape(rows, LANES)
        v = pc_ref[:, h, 0].reshape(rows, LANES)[:, p:2 * p]
        yc = y - jnp.sum(y, axis=1, keepdims=True) * inv
        yn = yc * lax.rsqrt(jnp.sum(yc * yc, axis=1, keepdims=True) * inv + RWKV_GN_EPS) * lw_ref[:, hs] + lb_ref[:, hs]
        rk = (kr * pltpu.roll(kr, p, 1))[:, 0:p] * rk_ref[:, hs]
        outs.append((yn + jnp.sum(rk, axis=1, keepdims=True) * v) * g_ref[:, hs])
    o_ref[...] = jnp.concatenate(outs, axis=1)


def _rwkv7_group(pr, bsz, mu, w0, w2, a0, a2, g2, k_k, k_a, r_k, ln_w, ln_b, tm=512):
    t = pr.shape[0]
    seq = t // bsz
    w = GROUP_W
    bf = lambda x: x.astype(jnp.bfloat16)
    const = lambda shape: pl.BlockSpec(shape, lambda b: (0, 0))
    vec = lambda x: x.reshape(1, -1)
    nh, tb = GROUP_HEADS, RWKV_TB
    nblk = seq // tb
    nat = jax.ShapeDtypeStruct((t, w), jnp.float32)
    packed = jax.ShapeDtypeStruct((nblk, nh, bsz, tb, LANES), jnp.float32)
    pspec = pl.BlockSpec((nblk, nh, 1, tb, LANES), lambda b: (0, 0, b, 0, 0))
    pa, pb, pc, g = pl.pallas_call(
        functools.partial(_rwkv_pre_kernel, rc=256),
        out_shape=[packed] * 3 + [nat],
        grid=(bsz,),
        in_specs=[pl.BlockSpec((seq, R_COLS), lambda b: (b, 0)), const((1, R_COLS)), const((1, w)),
                  const((RWKV_DECAY_RANK, w)), const((1, w)), const((RWKV_ICL_RANK, w)),
                  const((RWKV_GATE_RANK, w)), const((1, w)), const((1, w))],
        out_specs=[pspec] * 3 + [pl.BlockSpec((seq, w), lambda b: (b, 0))],
        compiler_params=pltpu.CompilerParams(
            dimension_semantics=("parallel",), vmem_limit_bytes=VMEM_LIMIT),
        name="rwkv_pre",
    )(pr, vec(mu), vec(w0), bf(w2), vec(a0), bf(a2), bf(g2), vec(k_k), vec(k_a))
    y = _rwkv_scan(pa, pb, pc, bsz)
    gb = tm // tb
    per_b = seq // tm
    bspec = pl.BlockSpec((gb, nh, 1, tb, LANES), lambda b, i: (i, 0, b, 0, 0))
    row = pl.BlockSpec((tm, w), lambda b, i: (b * per_b + i, 0))
    const2 = lambda shape: pl.BlockSpec(shape, lambda b, i: (0, 0))
    return pl.pallas_call(
        _rwkv_post_kernel,
        out_shape=nat,
        grid=(bsz, per_b),
        in_specs=[bspec, bspec, bspec, row] + [const2((1, w))] * 3,
        out_specs=row,
        compiler_params=pltpu.CompilerParams(
            dimension_semantics=("parallel", "parallel"), vmem_limit_bytes=VMEM_LIMIT),
        name="rwkv_post",
    )(y, pb, pc, g, vec(r_k), vec(ln_w), vec(ln_b))


FF_CHUNK = 512


def _swiglu_kernel(be_ref, x_ref, w1_ref, w3_ref, w2_ref, g_ref, o_ref):
    del be_ref
    xb = x_ref[...].astype(jnp.bfloat16)
    dff = w1_ref.shape[-1]
    acc = None
    for c0 in range(0, dff, FF_CHUNK):
        c1 = min(c0 + FF_CHUNK, dff)
        h1 = jnp.dot(xb, w1_ref[0, :, c0:c1], preferred_element_type=jnp.float32)
        h3 = jnp.dot(xb, w3_ref[0, :, c0:c1], preferred_element_type=jnp.float32)
        h = (jax.nn.silu(h1) * h3).astype(jnp.bfloat16)
        part = jnp.dot(h, w2_ref[0, c0:c1, :], preferred_element_type=jnp.float32)
        acc = part if acc is None else acc + part
    o_ref[...] = acc * g_ref[...]


def _grouped_swiglu(xrows, block_e, row_gate, w1, w3, w2, bm):
    rows, d = xrows.shape
    dff = w1.shape[-1]
    grid_spec = pltpu.PrefetchScalarGridSpec(
        num_scalar_prefetch=1,
        grid=(rows // bm,),
        in_specs=[pl.BlockSpec((bm, d), lambda i, be: (i, 0)),
                  pl.BlockSpec((1, d, dff), lambda i, be: (be[i], 0, 0)),
                  pl.BlockSpec((1, d, dff), lambda i, be: (be[i], 0, 0)),
                  pl.BlockSpec((1, dff, d), lambda i, be: (be[i], 0, 0)),
                  pl.BlockSpec((bm, 1), lambda i, be: (i, 0))],
        out_specs=pl.BlockSpec((bm, d), lambda i, be: (i, 0)),
    )
    return pl.pallas_call(
        _swiglu_kernel,
        out_shape=jax.ShapeDtypeStruct((rows, d), jnp.float32),
        grid_spec=grid_spec,
        compiler_params=pltpu.CompilerParams(
            dimension_semantics=("arbitrary",), vmem_limit_bytes=VMEM_LIMIT),
        name="swiglu",
    )(block_e, xrows, w1, w3, w2, row_gate)


def _swiglu(x2d, w1, w3, w2, bm=256):
    rows = x2d.shape[0]
    bf = lambda t: t.astype(jnp.bfloat16)[None]
    return _grouped_swiglu(x2d, jnp.zeros((rows // bm,), jnp.int32), jnp.ones((rows, 1), jnp.float32),
                           bf(w1), bf(w3), bf(w2), bm)


def _moe_swiglu(xf, logits, w1, w3, w2):
    T, d = xf.shape
    top_logit, top_e = lax.top_k(logits, TOP_K)
    gate = jax.nn.softmax(top_logit, axis=-1)
    flat_e = top_e.reshape(-1).astype(jnp.int32)
    flat_tok = jnp.repeat(jnp.arange(T, dtype=jnp.int32), TOP_K)
    onehot = (flat_e[:, None] == jnp.arange(N_EXPERTS, dtype=jnp.int32)[None, :]).astype(jnp.int32)
    csum = jnp.cumsum(onehot, axis=0)
    counts = csum[-1]
    rank = jnp.sum(onehot * (csum - 1), axis=1)
    padded = (counts + MOE_BLOCK - 1) // MOE_BLOCK * MOE_BLOCK
    pad_end = jnp.cumsum(padded)
    pad_start = pad_end - padded
    start = jnp.cumsum(counts) - counts
    pos = (pad_start[flat_e] + rank).reshape(T, TOP_K)
    _, st, sg = lax.sort((flat_e, flat_tok, gate.reshape(-1)), num_keys=1, is_stable=True)
    n_blocks = -(-(T * TOP_K) // MOE_BLOCK) + N_EXPERTS
    rows = n_blocks * MOE_BLOCK
    block_e = jnp.minimum(jnp.searchsorted(pad_end, jnp.arange(n_blocks) * MOE_BLOCK, side='right'),
                          N_EXPERTS - 1).astype(jnp.int32)
    row_e = jnp.repeat(block_e, MOE_BLOCK)
    q = jnp.arange(rows, dtype=jnp.int32) - pad_start[row_e]
    valid = q < counts[row_e]
    src = jnp.clip(start[row_e] + q, 0, T * TOP_K - 1)
    slot_tok = jnp.where(valid, st[src], 0)
    slot_gate = jnp.where(valid, sg[src], 0.0)
    bf = lambda t: t.astype(jnp.bfloat16)
    yb = _grouped_swiglu(xf[slot_tok], block_e, slot_gate[:, None], bf(w1), bf(w3), bf(w2), MOE_BLOCK)
    return yb[pos[:, 0]] + yb[pos[:, 1]]


def kernel(x, p, w_in, m_i_bias, m_f_bias, m_norm_w, g_conv_w, g_conv_b, g_w_a, g_b_a, g_w_x, g_b_x,
           g_lambda, r_mu, r_w0, r_w2, r_a0, r_a2, r_g2, r_k_k, r_k_a, r_r_k, r_ln_w, r_ln_b,
           s_conv_w, s_conv_b, s_dt_bias, s_a_log, s_d, s_norm_w, w_out, ln1_g, ln1_b, ln2_g, ln2_b,
           f_w1, f_w3, f_w2, e_router, e_w1, e_w3, e_w2, pe_proj, pe_gate_w, pe_gate_b):
    bsz, seq, d = x.shape
    T = bsz * seq
    x = x.reshape(T, d)
    for i in range(DEPTH):
        pm, pg, pr, ps, psm = _inproj(x, _split_w_in(w_in[i]))
        mixers = [
            _mlstm_group(pm, psm, bsz, m_i_bias[i], m_f_bias[i], m_norm_w[i]),
            _rglru_group(pg, bsz, g_conv_w[i], g_conv_b[i], g_w_a[i], g_b_a[i], g_w_x[i], g_b_x[i], g_lambda[i]),
            _rwkv7_group(pr, bsz, r_mu[i], r_w0[i], r_w2[i], r_a0[i], r_a2[i], r_g2[i],
                         r_k_k[i], r_k_a[i], r_r_k[i], r_ln_w[i], r_ln_b[i]),
            _ssd_group(ps, psm, bsz, s_conv_w[i], s_conv_b[i], s_dt_bias[i], s_a_log[i], s_d[i], s_norm_w[i]),
        ]
        if i % 2 == 0:
            (x,) = _outproj_ln(x, mixers, w_out[i], ln1_g[i], ln1_b[i])
            ff = _swiglu(x, f_w1[i // 2], f_w3[i // 2], f_w2[i // 2])
        else:
            x, logits = _outproj_ln(x, mixers, w_out[i], ln1_g[i], ln1_b[i], e_router[i // 2])
            ff = _moe_swiglu(x, logits[:, :N_EXPERTS], e_w1[i // 2], e_w3[i // 2], e_w2[i // 2])
        x = _pe_ln(x, ff, p[i].reshape(T, PE_DIM), pe_proj[i], pe_gate_w[i], pe_gate_b[i], ln2_g[i], ln2_b[i])
    return x.reshape(bsz, seq, d)
```

```python
import functools

import jax
import jax.numpy as jnp
from jax import lax
from jax.experimental import pallas as pl
from jax.experimental.pallas import tpu as pltpu

D_MODEL = 1024
DEPTH = 4
PE_DIM = 256
GROUP_W = 256
HEAD_DIM = 64
GROUP_HEADS = 4
CONV_K = 4
MLSTM_CHUNK = 64
RG_C = 8.0
RWKV_DECAY_RANK = 32
RWKV_ICL_RANK = 32
RWKV_GATE_RANK = 64
RWKV_GN_EPS = 64e-5
SSD_STATE = 64
SSD_GROUPS = 2
SSD_CHUNK = 128
SSD_XBC = GROUP_W + 2 * SSD_GROUPS * SSD_STATE
M_COLS = 4 * GROUP_W + 2 * GROUP_HEADS
G_COLS = 2 * GROUP_W
R_COLS = 3 * GROUP_W + RWKV_DECAY_RANK + RWKV_ICL_RANK + RWKV_GATE_RANK
S_COLS = GROUP_W + SSD_XBC + GROUP_HEADS
N_IN = M_COLS + G_COLS + R_COLS + S_COLS
N_EXPERTS = 8
TOP_K = 2
MOE_BLOCK = 256
ALPHA = (2 * DEPTH) ** 0.25
LN_EPS = 1e-5
RMS_EPS = 1e-6

LANES = 128
SUBLANES = 8
VMEM_LIMIT = 56 * 1024 * 1024


RWKV_TB = 32


def _rwkv_scan_kernel(pa_ref, pb_ref, pc_ref, y_ref, s_ref, sa_ref, ta_ref, tb_ref, tc_ref, yb_ref,
                      *, tb, nb, nk, nacc):
    @pl.when(pl.program_id(0) == 0)
    def _():
        s_ref[...] = jnp.zeros_like(s_ref)
        sa_ref[...] = jnp.zeros_like(sa_ref)

    nh = GROUP_HEADS
    half = nk // 2
    lane = lax.broadcasted_iota(jnp.int32, (half, LANES), 1)
    rows = lambda t, h: pl.ds(h * nb * tb + t, nb, stride=tb)

    def prepare(t, carry):
        for src, dst in ((pa_ref, ta_ref), (pb_ref, tb_ref), (pc_ref, tc_ref)):
            parts = [src[rows(t, h), :] for h in range(nh)]
            dst[t] = jnp.concatenate(parts + parts, axis=0).T
        return carry

    lax.fori_loop(0, tb, prepare, 0, unroll=8)

    def step(t, sa):
        bc = lambda ref, r: jnp.broadcast_to(ref[t, pl.ds(r, 1), :], (half, LANES))
        v_t = jnp.where(lane < nk, tc_ref[t, nk:nk + half, :], tc_ref[t, nk + half:2 * nk, :])
        y_acc = [None] * nacc
        sa_acc = [None] * nacc
        for k in range(nk):
            s_k = s_ref[k] * bc(ta_ref, k) - sa * bc(ta_ref, nk + k) + v_t * bc(tb_ref, k)
            s_ref[k] = s_k
            y_k = s_k * bc(tb_ref, nk + k)
            n_k = s_k * bc(tc_ref, k)
            a = k % nacc
            y_acc[a] = y_k if y_acc[a] is None else y_acc[a] + y_k
            sa_acc[a] = n_k if sa_acc[a] is None else sa_acc[a] + n_k
        yb_ref[t] = functools.reduce(lambda p, q: p + q, y_acc)
        return functools.reduce(lambda p, q: p + q, sa_acc)

    sa_ref[...] = lax.fori_loop(0, tb, step, sa_ref[...])

    def emit(t, carry):
        y = yb_ref[t]
        y_sw = pltpu.roll(y, nk, 1)
        y_nat = jnp.concatenate([y, y_sw, y, y_sw], axis=0).T
        for h in range(nh):
            y_ref[rows(t, h), :] = y_nat[h * nb:(h + 1) * nb, :]
        return carry

    lax.fori_loop(0, tb, emit, 0, unroll=8)


def _rwkv_scan(pa, pb, pc, bsz):
    nblk, nh, _, tb, _ = pa.shape
    n = HEAD_DIM
    assert 2 * bsz * nh == LANES and 2 * n == LANES
    flat = lambda t: t.reshape(nblk * nh * bsz * tb, LANES)
    spec = pl.BlockSpec((nh * bsz * tb, LANES), lambda i: (i, 0))
    y = pl.pallas_call(
        functools.partial(_rwkv_scan_kernel, tb=tb, nb=bsz, nk=n, nacc=2),
        out_shape=jax.ShapeDtypeStruct((nblk * nh * bsz * tb, LANES), jnp.float32),
        grid=(nblk,),
        in_specs=[spec, spec, spec],
        out_specs=spec,
        scratch_shapes=[pltpu.VMEM((n, n // 2, LANES), jnp.float32), pltpu.VMEM((n // 2, LANES), jnp.float32)]
        + [pltpu.VMEM((tb, LANES, LANES), jnp.float32)] * 3 + [pltpu.VMEM((tb, n // 2, LANES), jnp.float32)],
        compiler_params=pltpu.CompilerParams(
            dimension_semantics=("arbitrary",), vmem_limit_bytes=VMEM_LIMIT),
        name="rwkv_scan",
    )(flat(pa), flat(pb), flat(pc))
    return y.reshape(pa.shape)


def _inproj_kernel(x_ref, *refs):
    n = len(refs) // 2
    xb = x_ref[...].astype(jnp.bfloat16)
    for w_ref, o_ref in zip(refs[:n], refs[n:]):
        o_ref[...] = jnp.dot(xb, w_ref[...], preferred_element_type=jnp.float32)


def _inproj(x, weights, tm=512):
    t, d = x.shape
    return pl.pallas_call(
        _inproj_kernel,
        out_shape=[jax.ShapeDtypeStruct((t, w.shape[1]), jnp.float32) for w in weights],
        grid=(t // tm,),
        in_specs=[pl.BlockSpec((tm, d), lambda i: (i, 0))]
        + [pl.BlockSpec(w.shape, lambda i: (0, 0)) for w in weights],
        out_specs=[pl.BlockSpec((tm, w.shape[1]), lambda i: (i, 0)) for w in weights],
        compiler_params=pltpu.CompilerParams(
            dimension_semantics=("parallel",), vmem_limit_bytes=VMEM_LIMIT),
        name="inproj",
    )(x, *weights)


def _split_w_in(w):
    m0, g0, r0, s0 = 0, M_COLS, M_COLS + G_COLS, M_COLS + G_COLS + R_COLS
    w_m = w[:, m0:m0 + 4 * GROUP_W]
    w_g = w[:, g0:g0 + G_COLS]
    w_r = w[:, r0:r0 + R_COLS]
    w_s = w[:, s0:s0 + GROUP_W + SSD_XBC]
    small = jnp.concatenate([w[:, m0 + 4 * GROUP_W:g0], w[:, s0 + GROUP_W + SSD_XBC:]], axis=1)
    small = jnp.pad(small, ((0, 0), (0, LANES - small.shape[1])))
    return [t.astype(jnp.bfloat16) for t in (w_m, w_g, w_r, w_s, small)]


def _shift_rows(x, s, fill, row):
    return jnp.where(row >= s, pltpu.roll(x, s, 0), fill)


def _rglru_kernel(g_ref, cw_ref, cb_ref, wg_ref, bg_ref, c_ref, o_ref, xpad_ref, a_ref, u_ref, *, rc, unroll):
    seq, w = o_ref.shape
    pad = SUBLANES
    xpad_ref[0:pad, :] = jnp.zeros((pad, w), jnp.float32)
    xpad_ref[pad:, :] = g_ref[:, 0:w]
    cw = cw_ref[...]
    for c in range(seq // rc):
        r0 = c * rc
        xc = cb_ref[...] + cw[CONV_K - 1:CONV_K, :] * g_ref[r0:r0 + rc, 0:w]
        for j in range(1, CONV_K):
            xc = xc + cw[CONV_K - 1 - j:CONV_K - j, :] * xpad_ref[pad + r0 - j:pad + r0 - j + rc, :]
        z = jnp.dot(xc.astype(jnp.bfloat16), wg_ref[...], preferred_element_type=jnp.float32) + bg_ref[...]
        z = jax.nn.sigmoid(z)
        a = jnp.exp(c_ref[...] * z[:, 0:w])
        a_ref[r0:r0 + rc, :] = a
        u_ref[r0:r0 + rc, :] = jnp.sqrt(1.0 - a * a) * (z[:, w:] * xc)

    row = lax.broadcasted_iota(jnp.int32, (SUBLANES, w), 0)

    def tiles(i, h):
        for j in range(unroll):
            r = pl.multiple_of((i * unroll + j) * SUBLANES, SUBLANES)
            a = a_ref[pl.ds(r, SUBLANES), :]
            u = u_ref[pl.ds(r, SUBLANES), :]
            for s in (1, 2, 4):
                u = a * _shift_rows(u, s, 0.0, row) + u
                a = a * _shift_rows(a, s, 1.0, row)
            ht = a * h + u
            o_ref[pl.ds(r, SUBLANES), :] = ht * jax.nn.gelu(g_ref[pl.ds(r, SUBLANES), w:2 * w])
            h = jnp.broadcast_to(ht[SUBLANES - 1:SUBLANES, :], (SUBLANES, w))
        return h

    lax.fori_loop(0, seq // (SUBLANES * unroll), tiles, jnp.zeros((SUBLANES, w), jnp.float32))


def _block_diag(w):
    h, p, _ = w.shape
    eye = jnp.eye(h, dtype=w.dtype)
    return jnp.einsum('hij,hg->higj', w, eye).reshape(h * p, h * p)


def _rglru_group(g_cols, bsz, conv_w, conv_b, w_a, b_a, w_x, b_x, lam):
    t = g_cols.shape[0]
    seq = t // bsz
    w = GROUP_W
    wg = jnp.concatenate([_block_diag(w_a), _block_diag(w_x)], axis=1).astype(jnp.bfloat16)
    bg = jnp.concatenate([b_a, b_x])[None, :]
    cdec = (-RG_C * jax.nn.softplus(-lam))[None, :]
    const = lambda shape: pl.BlockSpec(shape, lambda b: (0, 0))
    return pl.pallas_call(
        functools.partial(_rglru_kernel, rc=256, unroll=8),
        out_shape=jax.ShapeDtypeStruct((t, w), jnp.float32),
        grid=(bsz,),
        in_specs=[pl.BlockSpec((seq, 2 * w), lambda b: (b, 0)), const((CONV_K, w)), const((1, w)),
                  const((w, 2 * w)), const((1, 2 * w)), const((1, w))],
        out_specs=pl.BlockSpec((seq, w), lambda b: (b, 0)),
        scratch_shapes=[pltpu.VMEM((seq + SUBLANES, w), jnp.float32),
                        pltpu.VMEM((seq, w), jnp.float32), pltpu.VMEM((seq, w), jnp.float32)],
        compiler_params=pltpu.CompilerParams(
            dimension_semantics=("parallel",), vmem_limit_bytes=VMEM_LIMIT),
        name="rglru",
    )(g_cols, conv_w, conv_b[None, :], wg, bg, cdec)


_NT = (((1,), (1,)), ((), ()))
_TN = (((0,), (0,)), ((), ()))


def _bdot(a, b, dims=(((1,), (0,)), ((), ()))):
    return lax.dot_general(a.astype(jnp.bfloat16), b.astype(jnp.bfloat16), dims,
                           preferred_element_type=jnp.float32)


def _cumsum_rows(x):
    n = x.shape[0]
    tri = (lax.broadcasted_iota(jnp.int32, (n, n), 0) >= lax.broadcasted_iota(jnp.int32, (n, n), 1))
    return jnp.dot(tri.astype(jnp.float32), x, precision=lax.Precision.HIGHEST,
                   preferred_element_type=jnp.float32)


def _log_sigmoid(x):
    return jnp.minimum(x, 0.0) - jnp.log1p(jnp.exp(-jnp.abs(x)))


def _softplus(x):
    return jnp.maximum(x, 0.0) + jnp.log1p(jnp.exp(-jnp.abs(x)))


def _mlstm_kernel(pm_ref, sm_ref, bias_ref, nw_ref, o_ref, c_ref, n_ref, m_ref):
    seq, w = o_ref.shape
    nh, dh, L = GROUP_HEADS, HEAD_DIM, MLSTM_CHUNK
    c_ref[...] = jnp.zeros_like(c_ref)
    n_ref[...] = jnp.zeros_like(n_ref)
    m_ref[...] = jnp.zeros_like(m_ref)
    f32 = jnp.float32
    hi = lax.Precision.HIGHEST
    col = lax.broadcasted_iota(jnp.int32, (L, LANES), 1)
    lrow = lax.broadcasted_iota(jnp.int32, (L, LANES), 0)
    scol = col & (dh - 1)
    lo = col < dh
    lo1 = lo[0:1]
    sq = lax.broadcasted_iota(jnp.int32, (LANES, LANES), 0)
    sr = lax.broadcasted_iota(jnp.int32, (LANES, LANES), 1)
    bd = (sq < dh) == (sr < dh)
    e_row = lax.broadcasted_iota(jnp.int32, (LANES, 2 * w), 0)
    e_col = lax.broadcasted_iota(jnp.int32, (LANES, 2 * w), 1)
    e_head = jnp.where(e_col < w, 0, nh) + ((e_col & (w - 1)) >> (dh.bit_length() - 1))
    spread = (e_row == e_head).astype(f32)
    tri = (lax.broadcasted_iota(jnp.int32, (L, L), 0) >= lax.broadcasted_iota(jnp.int32, (L, L), 1)).astype(f32)

    def seg_max(x, first):
        a = jnp.max(jnp.where(first, x, -jnp.inf), axis=1, keepdims=True)
        b = jnp.max(jnp.where(first, -jnp.inf, x), axis=1, keepdims=True)
        return jnp.where(first, a, b)

    def seg_sum(x):
        a = jnp.sum(jnp.where(lo, x, 0.0), axis=1, keepdims=True)
        b = jnp.sum(jnp.where(lo, 0.0, x), axis=1, keepdims=True)
        return jnp.where(lo, a, b)

    def chunk(c, carry):
        r0 = pl.multiple_of(c * L, L)
        rows = pl.ds(r0, L)
        gts = sm_ref[rows, :] + bias_ref[...]
        gts = jnp.where(col < nh, gts, jnp.where(col < 2 * nh, _log_sigmoid(gts), 0.0))
        rep = jnp.dot(gts, spread, precision=hi, preferred_element_type=f32)
        b_rep = jnp.dot(tri, rep[:, w:], precision=hi, preferred_element_type=f32)
        for p in range(nh // 2):
            ls = slice(p * LANES, (p + 1) * LANES)
            bc, lic, lfc = b_rep[:, ls], rep[:, ls], rep[:, w + p * LANES:w + (p + 1) * LANES]
            b_row = jnp.sum(jnp.where(lrow <= scol, lfc, 0.0), axis=0, keepdims=True)
            li_row = jnp.sum(jnp.where(lrow == scol, lic, 0.0), axis=0, keepdims=True)
            g = bc[L - 1:L, :]
            q = pm_ref[rows, ls]
            k = pm_ref[rows, w + p * LANES:w + (p + 1) * LANES] * (dh ** -0.5)
            v = pm_ref[rows, 2 * w + p * LANES:2 * w + (p + 1) * LANES]
            o = pm_ref[rows, 3 * w + p * LANES:3 * w + (p + 1) * LANES]
            cst, nst, m = c_ref[p], n_ref[p], m_ref[p]
            log_d = jnp.where(lrow >= scol, bc - b_row + li_row, -jnp.inf)
            m_t = jnp.maximum(bc + m, seg_max(log_d, lo))
            k2 = jnp.concatenate([k, k], axis=0)
            qk = jnp.where(lo, _bdot(jnp.where(lo, q, 0.0), k2, _NT), _bdot(jnp.where(lo, 0.0, q), k2, _NT))
            s = qk * jnp.exp(log_d - m_t)
            inter_w = jnp.exp(bc + m - m_t)
            v_bd = jnp.where(bd, jnp.concatenate([v, v], axis=0), 0.0)
            num = _bdot(s, v_bd) + inter_w * _bdot(q, cst)
            den = seg_sum(s) + inter_w * seg_sum(q * nst)
            hh = num / jnp.maximum(jnp.abs(den), jnp.exp(-m_t))
            hh = hh * lax.rsqrt(seg_sum(hh * hh) * (1.0 / dh) + RMS_EPS) * nw_ref[:, ls]
            o_ref[rows, ls] = hh * jax.nn.sigmoid(o)
            m_new = jnp.maximum(g + m, seg_max(g - b_row + li_row, lo1))
            decay = jnp.exp(g + m - m_new)
            w_col = jnp.exp(g - bc + lic - m_new)
            c_ref[p] = decay * cst + jnp.where(bd, _bdot(k, w_col * v, _TN), 0.0)
            n_ref[p] = decay * nst + jnp.sum(w_col * k, axis=0, keepdims=True)
            m_ref[p] = m_new
        return carry

    lax.fori_loop(0, seq // L, chunk, 0, unroll=2)


def _mlstm_group(pm, psm, bsz, i_bias, f_bias, norm_w):
    t = pm.shape[0]
    seq = t // bsz
    w, nh, dh = GROUP_W, GROUP_HEADS, HEAD_DIM
    bias = jnp.pad(jnp.concatenate([i_bias, f_bias]), (0, LANES - 2 * nh))[None, :]
    const = lambda shape: pl.BlockSpec(shape, lambda b: (0,) * len(shape))
    return pl.pallas_call(
        _mlstm_kernel,
        out_shape=jax.ShapeDtypeStruct((t, w), jnp.float32),
        grid=(bsz,),
        in_specs=[pl.BlockSpec((seq, 4 * w), lambda b: (b, 0)), pl.BlockSpec((seq, LANES), lambda b: (b, 0)),
                  const((1, LANES)), const((1, w))],
        out_specs=pl.BlockSpec((seq, w), lambda b: (b, 0)),
        scratch_shapes=[pltpu.VMEM((nh // 2, LANES, LANES), jnp.float32), pltpu.VMEM((nh // 2, 1, LANES), jnp.float32),
                        pltpu.VMEM((nh // 2, 1, LANES), jnp.float32)],
        compiler_params=pltpu.CompilerParams(
            dimension_semantics=("parallel",), vmem_limit_bytes=VMEM_LIMIT),
        name="mlstm",
    )(pm, psm, bias, norm_w[None, :])


def _ssd_kernel(ps_ref, sm_ref, cw_ref, cb_ref, dtb_ref, aneg_ref, dsk_ref, nw_ref, o_ref,
                xpad_ref, xbc_ref, st_ref, *, rc):
    seq, w = o_ref.shape
    nh, hp, ng, ns, L = GROUP_HEADS, HEAD_DIM, SSD_GROUPS, SSD_STATE, SSD_CHUNK
    cx = SSD_XBC
    pad = SUBLANES
    xpad_ref[0:pad, :] = jnp.zeros((pad, cx), jnp.float32)
    xpad_ref[pad:, :] = ps_ref[:, w:w + cx]
    cw = cw_ref[...]
    for c in range(seq // rc):
        r0 = c * rc
        acc = cb_ref[...] + cw[CONV_K - 1:CONV_K, :] * ps_ref[r0:r0 + rc, w:w + cx]
        for j in range(1, CONV_K):
            acc = acc + cw[CONV_K - 1 - j:CONV_K - j, :] * xpad_ref[pad + r0 - j:pad + r0 - j + rc, :]
        xbc_ref[r0:r0 + rc, :] = jax.nn.silu(acc)
    st_ref[...] = jnp.zeros_like(st_ref)
    causal = lax.broadcasted_iota(jnp.int32, (L, L), 0) >= lax.broadcasted_iota(jnp.int32, (L, L), 1)
    gsz = w // ng
    rep = nh // ng

    def chunk(c, carry):
        r0 = pl.multiple_of(c * L, L)
        rows = pl.ds(r0, L)
        dt_all = _softplus(sm_ref[rows, :] + dtb_ref[...])
        acs_all = _cumsum_rows(aneg_ref[...] * dt_all)
        acs_t = acs_all.T
        ys = []
        for g in range(ng):
            bm = xbc_ref[rows, w + g * ns:w + (g + 1) * ns]
            cm = xbc_ref[rows, w + ng * ns + g * ns:w + ng * ns + (g + 1) * ns]
            cb = _bdot(cm, bm, _NT)
            for hh in range(rep):
                h = g * rep + hh
                cidx = 2 * nh + h
                acs_col, acs_row = acs_all[:, cidx:cidx + 1], acs_t[cidx:cidx + 1, :]
                dt_col = dt_all[:, cidx:cidx + 1]
                a_last = acs_col[L - 1:L, :]
                xh = xbc_ref[rows, h * hp:(h + 1) * hp]
                xdt = xh * dt_col
                state = st_ref[h]
                scores = cb * jnp.exp(jnp.where(causal, acs_col - acs_row, -jnp.inf))
                y = _bdot(scores, xdt) + jnp.exp(acs_col) * _bdot(cm, state, _NT) + xh * dsk_ref[:, h * hp:(h + 1) * hp]
                st_ref[h] = jnp.exp(a_last) * state + _bdot(xdt * jnp.exp(a_last - acs_col), bm, _TN)
                ys.append(y)
        z = ps_ref[rows, 0:w]
        for g in range(ng):
            yg = jnp.concatenate(ys[g * rep:(g + 1) * rep], axis=1) * jax.nn.silu(z[:, g * gsz:(g + 1) * gsz])
            yg = yg * lax.rsqrt(jnp.mean(yg * yg, axis=1, keepdims=True) + RMS_EPS) * nw_ref[:, g * gsz:(g + 1) * gsz]
            o_ref[rows, g * gsz:(g + 1) * gsz] = yg
        return carry

    lax.fori_loop(0, seq // L, chunk, 0)


def _ssd_group(ps, psm, bsz, conv_w, conv_b, dt_bias, a_log, d_skip, norm_w):
    t = ps.shape[0]
    seq = t // bsz
    w, nh = GROUP_W, GROUP_HEADS
    lane_vec = lambda v: jnp.pad(v, (2 * nh, LANES - 3 * nh))[None, :]
    dsk = jnp.repeat(d_skip, HEAD_DIM)[None, :]
    const = lambda shape: pl.BlockSpec(shape, lambda b: (0,) * len(shape))
    return pl.pallas_call(
        functools.partial(_ssd_kernel, rc=256),
        out_shape=jax.ShapeDtypeStruct((t, w), jnp.float32),
        grid=(bsz,),
        in_specs=[pl.BlockSpec((seq, w + SSD_XBC), lambda b: (b, 0)), pl.BlockSpec((seq, LANES), lambda b: (b, 0)),
                  const((CONV_K, SSD_XBC)), const((1, SSD_XBC)), const((1, LANES)), const((1, LANES)),
                  const((1, w)), const((1, w))],
        out_specs=pl.BlockSpec((seq, w), lambda b: (b, 0)),
        scratch_shapes=[pltpu.VMEM((seq + SUBLANES, SSD_XBC), jnp.float32),
                        pltpu.VMEM((seq, SSD_XBC), jnp.float32),
                        pltpu.VMEM((nh, HEAD_DIM, SSD_STATE), jnp.float32)],
        compiler_params=pltpu.CompilerParams(
            dimension_semantics=("parallel",), vmem_limit_bytes=VMEM_LIMIT),
        name="ssd",
    )(ps, psm, conv_w, conv_b[None, :], lane_vec(dt_bias), lane_vec(-jnp.exp(a_log)), dsk, norm_w[None, :])


def _layer_norm(x, g, b):
    xc = x - jnp.mean(x, -1, keepdims=True)
    var = jnp.mean(xc * xc, -1, keepdims=True)
    return (xc * lax.rsqrt(var + LN_EPS)) * g + b


def _outproj_ln_kernel(x_ref, m0_ref, m1_ref, m2_ref, m3_ref, w_ref, g_ref, b_ref, *rest):
    gw = m0_ref.shape[1]
    acc = ALPHA * x_ref[...]
    for j, m_ref in enumerate((m0_ref, m1_ref, m2_ref, m3_ref)):
        acc = acc + _bdot(m_ref[...], w_ref[j * gw:(j + 1) * gw, :])
    x1 = _layer_norm(acc, g_ref[...], b_ref[...])
    if len(rest) == 1:
        rest[0][...] = x1
    else:
        wr_ref, o_ref, lg_ref = rest
        o_ref[...] = x1
        lg_ref[...] = _bdot(x1, wr_ref[...])


def _outproj_ln(x, mixers, w_out, g, b, w_router=None, tm=512):
    t, d = x.shape
    gw = mixers[0].shape[1]
    row = lambda n: pl.BlockSpec((tm, n), lambda i: (i, 0))
    const = lambda shape: pl.BlockSpec(shape, lambda i: (0, 0))
    in_specs = [row(d)] + [row(gw)] * 4 + [const((d, d)), const((1, d)), const((1, d))]
    args = [x, *mixers, w_out.astype(jnp.bfloat16), g[None, :], b[None, :]]
    out_shape = [jax.ShapeDtypeStruct((t, d), jnp.float32)]
    out_specs = [row(d)]
    if w_router is not None:
        wr = jnp.pad(w_router, ((0, 0), (0, LANES - w_router.shape[1]))).astype(jnp.bfloat16)
        in_specs.append(const((d, LANES)))
        args.append(wr)
        out_shape.append(jax.ShapeDtypeStruct((t, LANES), jnp.float32))
        out_specs.append(row(LANES))
    return pl.pallas_call(
        _outproj_ln_kernel,
        out_shape=out_shape, grid=(t // tm,), in_specs=in_specs, out_specs=out_specs,
        compiler_params=pltpu.CompilerParams(
            dimension_semantics=("parallel",), vmem_limit_bytes=VMEM_LIMIT),
        name="outproj_ln",
    )(*args)


def _pe_ln_kernel(x_ref, ff_ref, p_ref, wp_ref, wg_ref, bg_ref, g_ref, b_ref, o_ref):
    x1 = x_ref[...]
    pe = _bdot(p_ref[...], wp_ref[...]) * jax.nn.sigmoid(_bdot(x1, wg_ref[...]) + bg_ref[...])
    o_ref[...] = _layer_norm(ALPHA * x1 + ff_ref[...] + pe, g_ref[...], b_ref[...])


def _pe_ln(x1, ff, p, pe_proj, gate_w, gate_b, g, b, tm=512):
    t, d = x1.shape
    pd = p.shape[1]
    row = lambda n: pl.BlockSpec((tm, n), lambda i: (i, 0))
    const = lambda shape: pl.BlockSpec(shape, lambda i: (0, 0))
    return pl.pallas_call(
        _pe_ln_kernel,
        out_shape=jax.ShapeDtypeStruct((t, d), jnp.float32),
        grid=(t // tm,),
        in_specs=[row(d), row(d), row(pd), const((pd, d)), const((d, d)), const((1, d)), const((1, d)), const((1, d))],
        out_specs=row(d),
        compiler_params=pltpu.CompilerParams(
            dimension_semantics=("parallel",), vmem_limit_bytes=VMEM_LIMIT),
        name="pe_ln",
    )(x1, ff, p, pe_proj.astype(jnp.bfloat16), gate_w.astype(jnp.bfloat16), gate_b[None, :], g[None, :], b[None, :])


def _head_sums(x):
    rows = x.shape[0]
    parts = []
    for h in range(GROUP_HEADS):
        s = jnp.sum(x[:, h * HEAD_DIM:(h + 1) * HEAD_DIM], axis=1, keepdims=True)
        parts.append(jnp.broadcast_to(s, (rows, HEAD_DIM)))
    return jnp.concatenate(parts, axis=1)


def _rwkv_pre_kernel(c_ref, mu_ref, w0_ref, w2_ref, a0_ref, a2_ref, g2_ref, kk_ref, ka_ref,
                     pa_ref, pb_ref, pc_ref, g_ref, *, rc):
    seq = c_ref.shape[0]
    w, p = GROUP_W, HEAD_DIM
    o1 = 3 * w + RWKV_DECAY_RANK
    o2 = o1 + RWKV_ICL_RANK
    nchunks = seq // rc
    blk = rc // RWKV_TB
    row = lax.broadcasted_iota(jnp.int32, (rc, c_ref.shape[1]), 0)
    lane = lax.broadcasted_iota(jnp.int32, (rc, LANES), 1)

    def unit_keys(cols):
        kk = cols[:, w:2 * w] * kk_ref[...]
        return kk * lax.rsqrt(jnp.maximum(_head_sums(kk * kk), 1e-24))

    def pack(dst, c, x, z):
        for hp in range(w // LANES):
            x2, z2 = x[:, hp * LANES:(hp + 1) * LANES], z[:, hp * LANES:(hp + 1) * LANES]
            even = jnp.where(lane < p, x2, pltpu.roll(z2, p, 1))
            odd = jnp.where(lane < p, pltpu.roll(x2, p, 1), z2)
            for h, val in ((2 * hp, even), (2 * hp + 1, odd)):
                dst[c * blk:(c + 1) * blk, h, 0] = val.reshape(blk, RWKV_TB, LANES)

    for c in range(nchunks):
        r0 = c * rc
        cur = c_ref[r0:r0 + rc, :]
        if c == 0:
            prev = jnp.where(row == 0, 0.0, pltpu.roll(cur, 1, 0))
        else:
            prev = c_ref[r0 - 1:r0 - 1 + rc, :]
        if c == nchunks - 1:
            nxt = jnp.where(row == rc - 1, 0.0, pltpu.roll(cur, rc - 1, 0))
        else:
            nxt = c_ref[r0 + 1:r0 + 1 + rc, :]
        cols = cur + (prev - cur) * mu_ref[...]
        cols_next = nxt + (cur - nxt) * mu_ref[...]
        k = cols[:, w:2 * w]
        w_log = -_softplus(-(w0_ref[...] + _bdot(jnp.tanh(cols[:, 3 * w:o1]), w2_ref[...]))) - 0.5
        a = jax.nn.sigmoid(a0_ref[...] + _bdot(cols[:, o1:o2], a2_ref[...]))
        kk = unit_keys(cols)
        pack(pa_ref, c, jnp.exp(-jnp.exp(w_log)), kk * a)
        pack(pb_ref, c, k * (1.0 + (a - 1.0) * ka_ref[...]), cols[:, 0:w])
        pack(pc_ref, c, unit_keys(cols_next), cols[:, 2 * w:3 * w])
        g_ref[r0:r0 + rc, :] = _bdot(jax.nn.sigmoid(cols[:, o2:]), g2_ref[...])


def _rwkv_post_kernel(y_ref, pb_ref, pc_ref, g_ref, rk_ref, lw_ref, lb_ref, o_ref):
    p = HEAD_DIM
    rows = o_ref.shape[0]
    inv = 1.0 / p
    outs = []
    for h in range(GROUP_HEADS):
        hs = slice(h * p, (h + 1) * p)
        y = y_ref[:, h, 0].reshape(rows, LANES)[:, 0:p]
        kr = pb_ref[:, h, 0].reshape(rows, LANES)
        v = pc_ref[:, h, 0].reshape(rows, LANES)[:, p:2 * p]
        yc = y - jnp.sum(y, axis=1, keepdims=True) * inv
        yn = yc * lax.rsqrt(jnp.sum(yc * yc, axis=1, keepdims=True) * inv + RWKV_GN_EPS) * lw_ref[:, hs] + lb_ref[:, hs]
        rk = (kr * pltpu.roll(kr, p, 1))[:, 0:p] * rk_ref[:, hs]
        outs.append((yn + jnp.sum(rk, axis=1, keepdims=True) * v) * g_ref[:, hs])
    o_ref[...] = jnp.concatenate(outs, axis=1)


def _rwkv7_group(pr, bsz, mu, w0, w2, a0, a2, g2, k_k, k_a, r_k, ln_w, ln_b, tm=512):
    t = pr.shape[0]
    seq = t // bsz
    w = GROUP_W
    bf = lambda x: x.astype(jnp.bfloat16)
    const = lambda shape: pl.BlockSpec(shape, lambda b: (0, 0))
    vec = lambda x: x.reshape(1, -1)
    nh, tb = GROUP_HEADS, RWKV_TB
    nblk = seq // tb
    nat = jax.ShapeDtypeStruct((t, w), jnp.float32)
    packed = jax.ShapeDtypeStruct((nblk, nh, bsz, tb, LANES), jnp.float32)
    pspec = pl.BlockSpec((nblk, nh, 1, tb, LANES), lambda b: (0, 0, b, 0, 0))
    pa, pb, pc, g = pl.pallas_call(
        functools.partial(_rwkv_pre_kernel, rc=256),
        out_shape=[packed] * 3 + [nat],
        grid=(bsz,),
        in_specs=[pl.BlockSpec((seq, R_COLS), lambda b: (b, 0)), const((1, R_COLS)), const((1, w)),
                  const((RWKV_DECAY_RANK, w)), const((1, w)), const((RWKV_ICL_RANK, w)),
                  const((RWKV_GATE_RANK, w)), const((1, w)), const((1, w))],
        out_specs=[pspec] * 3 + [pl.BlockSpec((seq, w), lambda b: (b, 0))],
        compiler_params=pltpu.CompilerParams(
            dimension_semantics=("parallel",), vmem_limit_bytes=VMEM_LIMIT),
        name="rwkv_pre",
    )(pr, vec(mu), vec(w0), bf(w2), vec(a0), bf(a2), bf(g2), vec(k_k), vec(k_a))
    y = _rwkv_scan(pa, pb, pc, bsz)
    gb = tm // tb
    per_b = seq // tm
    bspec = pl.BlockSpec((gb, nh, 1, tb, LANES), lambda b, i: (i, 0, b, 0, 0))
    row = pl.BlockSpec((tm, w), lambda b, i: (b * per_b + i, 0))
    const2 = lambda shape: pl.BlockSpec(shape, lambda b, i: (0, 0))
    return pl.pallas_call(
        _rwkv_post_kernel,
        out_shape=nat,
        grid=(bsz, per_b),
        in_specs=[bspec, bspec, bspec, row] + [const2((1, w))] * 3,
        out_specs=row,
        compiler_params=pltpu.CompilerParams(
            dimension_semantics=("parallel", "parallel"), vmem_limit_bytes=VMEM_LIMIT),
        name="rwkv_post",
    )(y, pb, pc, g, vec(r_k), vec(ln_w), vec(ln_b))


FF_CHUNK = 512


def _swiglu_kernel(be_ref, x_ref, w1_ref, w3_ref, w2_ref, g_ref, o_ref):
    del be_ref
    xb = x_ref[...].astype(jnp.bfloat16)
    dff = w1_ref.shape[-1]
    acc = None
    for c0 in range(0, dff, FF_CHUNK):
        c1 = min(c0 + FF_CHUNK, dff)
        h1 = jnp.dot(xb, w1_ref[0, :, c0:c1], preferred_element_type=jnp.float32)
        h3 = jnp.dot(xb, w3_ref[0, :, c0:c1], preferred_element_type=jnp.float32)
        h = (jax.nn.silu(h1) * h3).astype(jnp.bfloat16)
        part = jnp.dot(h, w2_ref[0, c0:c1, :], preferred_element_type=jnp.float32)
        acc = part if acc is None else acc + part
    o_ref[...] = acc * g_ref[...]


def _grouped_swiglu(xrows, block_e, row_gate, w1, w3, w2, bm):
    rows, d = xrows.shape
    dff = w1.shape[-1]
    grid_spec = pltpu.PrefetchScalarGridSpec(
        num_scalar_prefetch=1,
        grid=(rows // bm,),
        in_specs=[pl.BlockSpec((bm, d), lambda i, be: (i, 0)),
                  pl.BlockSpec((1, d, dff), lambda i, be: (be[i], 0, 0)),
                  pl.BlockSpec((1, d, dff), lambda i, be: (be[i], 0, 0)),
                  pl.BlockSpec((1, dff, d), lambda i, be: (be[i], 0, 0)),
                  pl.BlockSpec((bm, 1), lambda i, be: (i, 0))],
        out_specs=pl.BlockSpec((bm, d), lambda i, be: (i, 0)),
    )
    return pl.pallas_call(
        _swiglu_kernel,
        out_shape=jax.ShapeDtypeStruct((rows, d), jnp.float32),
        grid_spec=grid_spec,
        compiler_params=pltpu.CompilerParams(
            dimension_semantics=("arbitrary",), vmem_limit_bytes=VMEM_LIMIT),
        name="swiglu",
    )(block_e, xrows, w1, w3, w2, row_gate)


def _cast_kernel(x_ref, o_ref):
    o_ref[0] = x_ref[0, 0].astype(o_ref.dtype)


def _layer_weights_bf16(w, layer, rb=256):
    _, ne, r, c = w.shape
    assert r % rb == 0
    return pl.pallas_call(
        _cast_kernel,
        out_shape=jax.ShapeDtypeStruct((ne, r, c), jnp.bfloat16),
        grid=(ne, r // rb),
        in_specs=[pl.BlockSpec((1, 1, rb, c), lambda e, j: (layer, e, j, 0))],
        out_specs=pl.BlockSpec((1, rb, c), lambda e, j: (e, j, 0)),
        compiler_params=pltpu.CompilerParams(
            dimension_semantics=("parallel", "parallel"), vmem_limit_bytes=VMEM_LIMIT),
        name="cast_bf16",
    )(w).reshape(ne, r, c)


def _swiglu(x2d, w1, w3, w2, layer, bm=512):
    rows = x2d.shape[0]
    bf = lambda t: _layer_weights_bf16(t[:, None], layer)
    return _grouped_swiglu(x2d, jnp.zeros((rows // bm,), jnp.int32), jnp.ones((rows, 1), jnp.float32),
                           bf(w1), bf(w3), bf(w2), bm)


def _moe_swiglu(xf, logits, w1, w3, w2, layer):
    T, d = xf.shape
    top_logit, top_e = lax.top_k(logits, TOP_K)
    gate = jax.nn.softmax(top_logit, axis=-1)
    flat_e = top_e.reshape(-1).astype(jnp.int32)
    flat_tok = jnp.repeat(jnp.arange(T, dtype=jnp.int32), TOP_K)
    onehot = (flat_e[:, None] == jnp.arange(N_EXPERTS, dtype=jnp.int32)[None, :]).astype(jnp.int32)
    csum = jnp.cumsum(onehot, axis=0)
    counts = csum[-1]
    rank = jnp.sum(onehot * (csum - 1), axis=1)
    padded = (counts + MOE_BLOCK - 1) // MOE_BLOCK * MOE_BLOCK
    pad_end = jnp.cumsum(padded)
    pad_start = pad_end - padded
    start = jnp.cumsum(counts) - counts
    pos = (pad_start[flat_e] + rank).reshape(T, TOP_K)
    _, st, sg = lax.sort((flat_e, flat_tok, gate.reshape(-1)), num_keys=1, is_stable=True)
    n_blocks = -(-(T * TOP_K) // MOE_BLOCK) + N_EXPERTS
    rows = n_blocks * MOE_BLOCK
    block_e = jnp.minimum(jnp.searchsorted(pad_end, jnp.arange(n_blocks) * MOE_BLOCK, side='right'),
                          N_EXPERTS - 1).astype(jnp.int32)
    row_e = jnp.repeat(block_e, MOE_BLOCK)
    q = jnp.arange(rows, dtype=jnp.int32) - pad_start[row_e]
    valid = q < counts[row_e]
    src = jnp.clip(start[row_e] + q, 0, T * TOP_K - 1)
    slot_tok = jnp.where(valid, st[src], 0)
    slot_gate = jnp.where(valid, sg[src], 0.0)
    bf = lambda t: _layer_weights_bf16(t, layer)
    yb = _grouped_swiglu(xf[slot_tok], block_e, slot_gate[:, None], bf(w1), bf(w3), bf(w2), MOE_BLOCK)
    return yb[pos[:, 0]] + yb[pos[:, 1]]


def kernel(x, p, w_in, m_i_bias, m_f_bias, m_norm_w, g_conv_w, g_conv_b, g_w_a, g_b_a, g_w_x, g_b_x,
           g_lambda, r_mu, r_w0, r_w2, r_a0, r_a2, r_g2, r_k_k, r_k_a, r_r_k, r_ln_w, r_ln_b,
           s_conv_w, s_conv_b, s_dt_bias, s_a_log, s_d, s_norm_w, w_out, ln1_g, ln1_b, ln2_g, ln2_b,
           f_w1, f_w3, f_w2, e_router, e_w1, e_w3, e_w2, pe_proj, pe_gate_w, pe_gate_b):
    bsz, seq, d = x.shape
    T = bsz * seq
    x = x.reshape(T, d)
    for i in range(DEPTH):
        pm, pg, pr, ps, psm = _inproj(x, _split_w_in(w_in[i]))
        mixers = [
            _mlstm_group(pm, psm, bsz, m_i_bias[i], m_f_bias[i], m_norm_w[i]),
            _rglru_group(pg, bsz, g_conv_w[i], g_conv_b[i], g_w_a[i], g_b_a[i], g_w_x[i], g_b_x[i], g_lambda[i]),
            _rwkv7_group(pr, bsz, r_mu[i], r_w0[i], r_w2[i], r_a0[i], r_a2[i], r_g2[i],
                         r_k_k[i], r_k_a[i], r_r_k[i], r_ln_w[i], r_ln_b[i]),
            _ssd_group(ps, psm, bsz, s_conv_w[i], s_conv_b[i], s_dt_bias[i], s_a_log[i], s_d[i], s_norm_w[i]),
        ]
        if i % 2 == 0:
            (x,) = _outproj_ln(x, mixers, w_out[i], ln1_g[i], ln1_b[i])
            ff = _swiglu(x, f_w1, f_w3, f_w2, i // 2)
        else:
            x, logits = _outproj_ln(x, mixers, w_out[i], ln1_g[i], ln1_b[i], e_router[i // 2])
            ff = _moe_swiglu(x, logits[:, :N_EXPERTS], e_w1, e_w3, e_w2, i // 2)
        x = _pe_ln(x, ff, p[i].reshape(T, PE_DIM), pe_proj[i], pe_gate_w[i], pe_gate_b[i], ln2_g[i], ln2_b[i])
    return x.reshape(bsz, seq, d)
```

```python
import functools

import jax
import jax.numpy as jnp
from jax import lax
from jax.experimental import pallas as pl
from jax.experimental.pallas import tpu as pltpu

D_MODEL = 1024
DEPTH = 4
PE_DIM = 256
GROUP_W = 256
HEAD_DIM = 64
GROUP_HEADS = 4
CONV_K = 4
MLSTM_CHUNK = 64
RG_C = 8.0
RWKV_DECAY_RANK = 32
RWKV_ICL_RANK = 32
RWKV_GATE_RANK = 64
RWKV_GN_EPS = 64e-5
SSD_STATE = 64
SSD_GROUPS = 2
SSD_CHUNK = 128
SSD_XBC = GROUP_W + 2 * SSD_GROUPS * SSD_STATE
M_COLS = 4 * GROUP_W + 2 * GROUP_HEADS
G_COLS = 2 * GROUP_W
R_COLS = 3 * GROUP_W + RWKV_DECAY_RANK + RWKV_ICL_RANK + RWKV_GATE_RANK
S_COLS = GROUP_W + SSD_XBC + GROUP_HEADS
N_IN = M_COLS + G_COLS + R_COLS + S_COLS
N_EXPERTS = 8
TOP_K = 2
MOE_BLOCK = 512
ALPHA = (2 * DEPTH) ** 0.25
LN_EPS = 1e-5
RMS_EPS = 1e-6

LANES = 128
SUBLANES = 8
VMEM_LIMIT = 56 * 1024 * 1024


RWKV_TB = 32


def _rwkv_scan_kernel(pa_ref, pb_ref, pc_ref, y_ref, s_ref, sa_ref, ta_ref, tb_ref, tc_ref, yb_ref,
                      *, tb, nb, nk, nacc):
    @pl.when(pl.program_id(0) == 0)
    def _():
        s_ref[...] = jnp.zeros_like(s_ref)
        sa_ref[...] = jnp.zeros_like(sa_ref)

    nh = GROUP_HEADS
    half = nk // 2
    lane = lax.broadcasted_iota(jnp.int32, (half, LANES), 1)
    rows = lambda t, h: pl.ds(h * nb * tb + t, nb, stride=tb)

    def prepare(t, carry):
        for src, dst in ((pa_ref, ta_ref), (pb_ref, tb_ref), (pc_ref, tc_ref)):
            parts = [src[rows(t, h), :] for h in range(nh)]
            dst[t] = jnp.concatenate(parts + parts, axis=0).T
        return carry

    lax.fori_loop(0, tb, prepare, 0, unroll=8)

    def step(t, sa):
        bc = lambda ref, r: jnp.broadcast_to(ref[t, pl.ds(r, 1), :], (half, LANES))
        v_t = jnp.where(lane < nk, tc_ref[t, nk:nk + half, :], tc_ref[t, nk + half:2 * nk, :])
        y_acc = [None] * nacc
        sa_acc = [None] * nacc
        for k in range(nk):
            s_k = s_ref[k] * bc(ta_ref, k) - sa * bc(ta_ref, nk + k) + v_t * bc(tb_ref, k)
            s_ref[k] = s_k
            y_k = s_k * bc(tb_ref, nk + k)
            n_k = s_k * bc(tc_ref, k)
            a = k % nacc
            y_acc[a] = y_k if y_acc[a] is None else y_acc[a] + y_k
            sa_acc[a] = n_k if sa_acc[a] is None else sa_acc[a] + n_k
        yb_ref[t] = functools.reduce(lambda p, q: p + q, y_acc)
        return functools.reduce(lambda p, q: p + q, sa_acc)

    sa_ref[...] = lax.fori_loop(0, tb, step, sa_ref[...])

    def emit(t, carry):
        y = yb_ref[t]
        y_sw = pltpu.roll(y, nk, 1)
        y_nat = jnp.concatenate([y, y_sw, y, y_sw], axis=0).T
        for h in range(nh):
            y_ref[rows(t, h), :] = y_nat[h * nb:(h + 1) * nb, :]
        return carry

    lax.fori_loop(0, tb, emit, 0, unroll=8)


def _rwkv_scan(pa, pb, pc, bsz):
    nblk, nh, _, tb, _ = pa.shape
    n = HEAD_DIM
    assert 2 * bsz * nh == LANES and 2 * n == LANES
    flat = lambda t: t.reshape(nblk * nh * bsz * tb, LANES)
    spec = pl.BlockSpec((nh * bsz * tb, LANES), lambda i: (i, 0))
    y = pl.pallas_call(
        functools.partial(_rwkv_scan_kernel, tb=tb, nb=bsz, nk=n, nacc=2),
        out_shape=jax.ShapeDtypeStruct((nblk * nh * bsz * tb, LANES), jnp.float32),
        grid=(nblk,),
        in_specs=[spec, spec, spec],
        out_specs=spec,
        scratch_shapes=[pltpu.VMEM((n, n // 2, LANES), jnp.float32), pltpu.VMEM((n // 2, LANES), jnp.float32)]
        + [pltpu.VMEM((tb, LANES, LANES), jnp.float32)] * 3 + [pltpu.VMEM((tb, n // 2, LANES), jnp.float32)],
        compiler_params=pltpu.CompilerParams(
            dimension_semantics=("arbitrary",), vmem_limit_bytes=VMEM_LIMIT),
        name="rwkv_scan",
    )(flat(pa), flat(pb), flat(pc))
    return y.reshape(pa.shape)


def _inproj_kernel(x_ref, *refs):
    n = len(refs) // 2
    xb = x_ref[...].astype(jnp.bfloat16)
    for w_ref, o_ref in zip(refs[:n], refs[n:]):
        o_ref[...] = jnp.dot(xb, w_ref[...], preferred_element_type=jnp.float32)


def _inproj(x, weights, tm=512):
    t, d = x.shape
    return pl.pallas_call(
        _inproj_kernel,
        out_shape=[jax.ShapeDtypeStruct((t, w.shape[1]), jnp.float32) for w in weights],
        grid=(t // tm,),
        in_specs=[pl.BlockSpec((tm, d), lambda i: (i, 0))]
        + [pl.BlockSpec(w.shape, lambda i: (0, 0)) for w in weights],
        out_specs=[pl.BlockSpec((tm, w.shape[1]), lambda i: (i, 0)) for w in weights],
        compiler_params=pltpu.CompilerParams(
            dimension_semantics=("parallel",), vmem_limit_bytes=VMEM_LIMIT),
        name="inproj",
    )(x, *weights)


def _split_w_in(w):
    m0, g0, r0, s0 = 0, M_COLS, M_COLS + G_COLS, M_COLS + G_COLS + R_COLS
    w_m = w[:, m0:m0 + 4 * GROUP_W]
    w_g = w[:, g0:g0 + G_COLS]
    w_r = w[:, r0:r0 + R_COLS]
    w_s = w[:, s0:s0 + GROUP_W + SSD_XBC]
    small = jnp.concatenate([w[:, m0 + 4 * GROUP_W:g0], w[:, s0 + GROUP_W + SSD_XBC:]], axis=1)
    small = jnp.pad(small, ((0, 0), (0, LANES - small.shape[1])))
    return [t.astype(jnp.bfloat16) for t in (w_m, w_g, w_r, w_s, small)]


def _shift_rows(x, s, fill, row):
    return jnp.where(row >= s, pltpu.roll(x, s, 0), fill)


def _rglru_kernel(g_ref, cw_ref, cb_ref, wg_ref, bg_ref, c_ref, o_ref, xpad_ref, a_ref, u_ref, *, rc, unroll):
    seq, w = o_ref.shape
    pad = SUBLANES
    xpad_ref[0:pad, :] = jnp.zeros((pad, w), jnp.float32)
    xpad_ref[pad:, :] = g_ref[:, 0:w]
    cw = cw_ref[...]
    for c in range(seq // rc):
        r0 = c * rc
        xc = cb_ref[...] + cw[CONV_K - 1:CONV_K, :] * g_ref[r0:r0 + rc, 0:w]
        for j in range(1, CONV_K):
            xc = xc + cw[CONV_K - 1 - j:CONV_K - j, :] * xpad_ref[pad + r0 - j:pad + r0 - j + rc, :]
        z = jnp.dot(xc.astype(jnp.bfloat16), wg_ref[...], preferred_element_type=jnp.float32) + bg_ref[...]
        z = jax.nn.sigmoid(z)
        a = jnp.exp(c_ref[...] * z[:, 0:w])
        a_ref[r0:r0 + rc, :] = a
        u_ref[r0:r0 + rc, :] = jnp.sqrt(1.0 - a * a) * (z[:, w:] * xc)

    row = lax.broadcasted_iota(jnp.int32, (SUBLANES, w), 0)

    def tiles(i, h):
        for j in range(unroll):
            r = pl.multiple_of((i * unroll + j) * SUBLANES, SUBLANES)
            a = a_ref[pl.ds(r, SUBLANES), :]
            u = u_ref[pl.ds(r, SUBLANES), :]
            for s in (1, 2, 4):
                u = a * _shift_rows(u, s, 0.0, row) + u
                a = a * _shift_rows(a, s, 1.0, row)
            ht = a * h + u
            o_ref[pl.ds(r, SUBLANES), :] = ht * jax.nn.gelu(g_ref[pl.ds(r, SUBLANES), w:2 * w])
            h = jnp.broadcast_to(ht[SUBLANES - 1:SUBLANES, :], (SUBLANES, w))
        return h

    lax.fori_loop(0, seq // (SUBLANES * unroll), tiles, jnp.zeros((SUBLANES, w), jnp.float32))


def _block_diag(w):
    h, p, _ = w.shape
    eye = jnp.eye(h, dtype=w.dtype)
    return jnp.einsum('hij,hg->higj', w, eye).reshape(h * p, h * p)


def _rglru_group(g_cols, bsz, conv_w, conv_b, w_a, b_a, w_x, b_x, lam):
    t = g_cols.shape[0]
    seq = t // bsz
    w = GROUP_W
    wg = jnp.concatenate([_block_diag(w_a), _block_diag(w_x)], axis=1).astype(jnp.bfloat16)
    bg = jnp.concatenate([b_a, b_x])[None, :]
    cdec = (-RG_C * jax.nn.softplus(-lam))[None, :]
    const = lambda shape: pl.BlockSpec(shape, lambda b: (0, 0))
    return pl.pallas_call(
        functools.partial(_rglru_kernel, rc=256, unroll=8),
        out_shape=jax.ShapeDtypeStruct((t, w), jnp.float32),
        grid=(bsz,),
        in_specs=[pl.BlockSpec((seq, 2 * w), lambda b: (b, 0)), const((CONV_K, w)), const((1, w)),
                  const((w, 2 * w)), const((1, 2 * w)), const((1, w))],
        out_specs=pl.BlockSpec((seq, w), lambda b: (b, 0)),
        scratch_shapes=[pltpu.VMEM((seq + SUBLANES, w), jnp.float32),
                        pltpu.VMEM((seq, w), jnp.float32), pltpu.VMEM((seq, w), jnp.float32)],
        compiler_params=pltpu.CompilerParams(
            dimension_semantics=("parallel",), vmem_limit_bytes=VMEM_LIMIT),
        name="rglru",
    )(g_cols, conv_w, conv_b[None, :], wg, bg, cdec)


_NT = (((1,), (1,)), ((), ()))
_TN = (((0,), (0,)), ((), ()))


def _bdot(a, b, dims=(((1,), (0,)), ((), ()))):
    return lax.dot_general(a.astype(jnp.bfloat16), b.astype(jnp.bfloat16), dims,
                           preferred_element_type=jnp.float32)


def _cumsum_rows(x):
    n = x.shape[0]
    tri = (lax.broadcasted_iota(jnp.int32, (n, n), 0) >= lax.broadcasted_iota(jnp.int32, (n, n), 1))
    return jnp.dot(tri.astype(jnp.float32), x, precision=lax.Precision.HIGHEST,
                   preferred_element_type=jnp.float32)


def _log_sigmoid(x):
    return jnp.minimum(x, 0.0) - jnp.log1p(jnp.exp(-jnp.abs(x)))


def _softplus(x):
    return jnp.maximum(x, 0.0) + jnp.log1p(jnp.exp(-jnp.abs(x)))


def _mlstm_kernel(pm_ref, sm_ref, bias_ref, nw_ref, o_ref, c_ref, n_ref, m_ref):
    seq, w = o_ref.shape
    nh, dh, L = GROUP_HEADS, HEAD_DIM, MLSTM_CHUNK
    c_ref[...] = jnp.zeros_like(c_ref)
    n_ref[...] = jnp.zeros_like(n_ref)
    m_ref[...] = jnp.zeros_like(m_ref)
    f32 = jnp.float32
    hi = lax.Precision.HIGHEST
    col = lax.broadcasted_iota(jnp.int32, (L, LANES), 1)
    lrow = lax.broadcasted_iota(jnp.int32, (L, LANES), 0)
    scol = col & (dh - 1)
    lo = col < dh
    lo1 = lo[0:1]
    sq = lax.broadcasted_iota(jnp.int32, (LANES, LANES), 0)
    sr = lax.broadcasted_iota(jnp.int32, (LANES, LANES), 1)
    bd = (sq < dh) == (sr < dh)
    e_row = lax.broadcasted_iota(jnp.int32, (LANES, 2 * w), 0)
    e_col = lax.broadcasted_iota(jnp.int32, (LANES, 2 * w), 1)
    e_head = jnp.where(e_col < w, 0, nh) + ((e_col & (w - 1)) >> (dh.bit_length() - 1))
    spread = (e_row == e_head).astype(f32)
    tri = (lax.broadcasted_iota(jnp.int32, (L, L), 0) >= lax.broadcasted_iota(jnp.int32, (L, L), 1)).astype(f32)

    def seg_max(x, first):
        a = jnp.max(jnp.where(first, x, -jnp.inf), axis=1, keepdims=True)
        b = jnp.max(jnp.where(first, -jnp.inf, x), axis=1, keepdims=True)
        return jnp.where(first, a, b)

    def seg_sum(x):
        a = jnp.sum(jnp.where(lo, x, 0.0), axis=1, keepdims=True)
        b = jnp.sum(jnp.where(lo, 0.0, x), axis=1, keepdims=True)
        return jnp.where(lo, a, b)

    def chunk(c, carry):
        r0 = pl.multiple_of(c * L, L)
        rows = pl.ds(r0, L)
        gts = sm_ref[rows, :] + bias_ref[...]
        gts = jnp.where(col < nh, gts, jnp.where(col < 2 * nh, _log_sigmoid(gts), 0.0))
        rep = jnp.dot(gts, spread, precision=hi, preferred_element_type=f32)
        b_rep = jnp.dot(tri, rep[:, w:], precision=hi, preferred_element_type=f32)
        for p in range(nh // 2):
            ls = slice(p * LANES, (p + 1) * LANES)
            bc, lic, lfc = b_rep[:, ls], rep[:, ls], rep[:, w + p * LANES:w + (p + 1) * LANES]
            b_row = jnp.sum(jnp.where(lrow <= scol, lfc, 0.0), axis=0, keepdims=True)
            li_row = jnp.sum(jnp.where(lrow == scol, lic, 0.0), axis=0, keepdims=True)
            g = bc[L - 1:L, :]
            q = pm_ref[rows, ls]
            k = pm_ref[rows, w + p * LANES:w + (p + 1) * LANES] * (dh ** -0.5)
            v = pm_ref[rows, 2 * w + p * LANES:2 * w + (p + 1) * LANES]
            o = pm_ref[rows, 3 * w + p * LANES:3 * w + (p + 1) * LANES]
            cst, nst, m = c_ref[p], n_ref[p], m_ref[p]
            log_d = jnp.where(lrow >= scol, bc - b_row + li_row, -jnp.inf)
            m_t = jnp.maximum(bc + m, seg_max(log_d, lo))
            k2 = jnp.concatenate([k, k], axis=0)
            qk = jnp.where(lo, _bdot(jnp.where(lo, q, 0.0), k2, _NT), _bdot(jnp.where(lo, 0.0, q), k2, _NT))
            s = qk * jnp.exp(log_d - m_t)
            inter_w = jnp.exp(bc + m - m_t)
            v_bd = jnp.where(bd, jnp.concatenate([v, v], axis=0), 0.0)
            num = _bdot(s, v_bd) + inter_w * _bdot(q, cst)
            den = seg_sum(s) + inter_w * seg_sum(q * nst)
            hh = num / jnp.maximum(jnp.abs(den), jnp.exp(-m_t))
            hh = hh * lax.rsqrt(seg_sum(hh * hh) * (1.0 / dh) + RMS_EPS) * nw_ref[:, ls]
            o_ref[rows, ls] = hh * jax.nn.sigmoid(o)
            m_new = jnp.maximum(g + m, seg_max(g - b_row + li_row, lo1))
            decay = jnp.exp(g + m - m_new)
            w_col = jnp.exp(g - bc + lic - m_new)
            c_ref[p] = decay * cst + jnp.where(bd, _bdot(k, w_col * v, _TN), 0.0)
            n_ref[p] = decay * nst + jnp.sum(w_col * k, axis=0, keepdims=True)
            m_ref[p] = m_new
        return carry

    lax.fori_loop(0, seq // L, chunk, 0, unroll=2)


def _mlstm_group(pm, psm, bsz, i_bias, f_bias, norm_w):
    t = pm.shape[0]
    seq = t // bsz
    w, nh, dh = GROUP_W, GROUP_HEADS, HEAD_DIM
    bias = jnp.pad(jnp.concatenate([i_bias, f_bias]), (0, LANES - 2 * nh))[None, :]
    const = lambda shape: pl.BlockSpec(shape, lambda b: (0,) * len(shape))
    return pl.pallas_call(
        _mlstm_kernel,
        out_shape=jax.ShapeDtypeStruct((t, w), jnp.float32),
        grid=(bsz,),
        in_specs=[pl.BlockSpec((seq, 4 * w), lambda b: (b, 0)), pl.BlockSpec((seq, LANES), lambda b: (b, 0)),
                  const((1, LANES)), const((1, w))],
        out_specs=pl.BlockSpec((seq, w), lambda b: (b, 0)),
        scratch_shapes=[pltpu.VMEM((nh // 2, LANES, LANES), jnp.float32), pltpu.VMEM((nh // 2, 1, LANES), jnp.float32),
                        pltpu.VMEM((nh // 2, 1, LANES), jnp.float32)],
        compiler_params=pltpu.CompilerParams(
            dimension_semantics=("parallel",), vmem_limit_bytes=VMEM_LIMIT),
        name="mlstm",
    )(pm, psm, bias, norm_w[None, :])


def _ssd_kernel(ps_ref, sm_ref, cw_ref, cb_ref, dtb_ref, aneg_ref, dsk_ref, nw_ref, o_ref,
                xpad_ref, xbc_ref, st_ref, *, rc):
    seq, w = o_ref.shape
    nh, hp, ng, ns, L = GROUP_HEADS, HEAD_DIM, SSD_GROUPS, SSD_STATE, SSD_CHUNK
    cx = SSD_XBC
    pad = SUBLANES
    xpad_ref[0:pad, :] = jnp.zeros((pad, cx), jnp.float32)
    xpad_ref[pad:, :] = ps_ref[:, w:w + cx]
    cw = cw_ref[...]
    for c in range(seq // rc):
        r0 = c * rc
        acc = cb_ref[...] + cw[CONV_K - 1:CONV_K, :] * ps_ref[r0:r0 + rc, w:w + cx]
        for j in range(1, CONV_K):
            acc = acc + cw[CONV_K - 1 - j:CONV_K - j, :] * xpad_ref[pad + r0 - j:pad + r0 - j + rc, :]
        xbc_ref[r0:r0 + rc, :] = jax.nn.silu(acc)
    st_ref[...] = jnp.zeros_like(st_ref)
    causal = lax.broadcasted_iota(jnp.int32, (L, L), 0) >= lax.broadcasted_iota(jnp.int32, (L, L), 1)
    gsz = w // ng
    rep = nh // ng

    def chunk(c, carry):
        r0 = pl.multiple_of(c * L, L)
        rows = pl.ds(r0, L)
        dt_all = _softplus(sm_ref[rows, :] + dtb_ref[...])
        acs_all = _cumsum_rows(aneg_ref[...] * dt_all)
        acs_t = acs_all.T
        ys = []
        for g in range(ng):
            bm = xbc_ref[rows, w + g * ns:w + (g + 1) * ns]
            cm = xbc_ref[rows, w + ng * ns + g * ns:w + ng * ns + (g + 1) * ns]
            cb = _bdot(cm, bm, _NT)
            for hh in range(rep):
                h = g * rep + hh
                cidx = 2 * nh + h
                acs_col, acs_row = acs_all[:, cidx:cidx + 1], acs_t[cidx:cidx + 1, :]
                dt_col = dt_all[:, cidx:cidx + 1]
                a_last = acs_col[L - 1:L, :]
                xh = xbc_ref[rows, h * hp:(h + 1) * hp]
                xdt = xh * dt_col
                state = st_ref[h]
                scores = cb * jnp.exp(jnp.where(causal, acs_col - acs_row, -jnp.inf))
                y = _bdot(scores, xdt) + jnp.exp(acs_col) * _bdot(cm, state, _NT) + xh * dsk_ref[:, h * hp:(h + 1) * hp]
                st_ref[h] = jnp.exp(a_last) * state + _bdot(xdt * jnp.exp(a_last - acs_col), bm, _TN)
                ys.append(y)
        z = ps_ref[rows, 0:w]
        for g in range(ng):
            yg = jnp.concatenate(ys[g * rep:(g + 1) * rep], axis=1) * jax.nn.silu(z[:, g * gsz:(g + 1) * gsz])
            yg = yg * lax.rsqrt(jnp.mean(yg * yg, axis=1, keepdims=True) + RMS_EPS) * nw_ref[:, g * gsz:(g + 1) * gsz]
            o_ref[rows, g * gsz:(g + 1) * gsz] = yg
        return carry

    lax.fori_loop(0, seq // L, chunk, 0)


def _ssd_group(ps, psm, bsz, conv_w, conv_b, dt_bias, a_log, d_skip, norm_w):
    t = ps.shape[0]
    seq = t // bsz
    w, nh = GROUP_W, GROUP_HEADS
    lane_vec = lambda v: jnp.pad(v, (2 * nh, LANES - 3 * nh))[None, :]
    dsk = jnp.repeat(d_skip, HEAD_DIM)[None, :]
    const = lambda shape: pl.BlockSpec(shape, lambda b: (0,) * len(shape))
    return pl.pallas_call(
        functools.partial(_ssd_kernel, rc=256),
        out_shape=jax.ShapeDtypeStruct((t, w), jnp.float32),
        grid=(bsz,),
        in_specs=[pl.BlockSpec((seq, w + SSD_XBC), lambda b: (b, 0)), pl.BlockSpec((seq, LANES), lambda b: (b, 0)),
                  const((CONV_K, SSD_XBC)), const((1, SSD_XBC)), const((1, LANES)), const((1, LANES)),
                  const((1, w)), const((1, w))],
        out_specs=pl.BlockSpec((seq, w), lambda b: (b, 0)),
        scratch_shapes=[pltpu.VMEM((seq + SUBLANES, SSD_XBC), jnp.float32),
                        pltpu.VMEM((seq, SSD_XBC), jnp.float32),
                        pltpu.VMEM((nh, HEAD_DIM, SSD_STATE), jnp.float32)],
        compiler_params=pltpu.CompilerParams(
            dimension_semantics=("parallel",), vmem_limit_bytes=VMEM_LIMIT),
        name="ssd",
    )(ps, psm, conv_w, conv_b[None, :], lane_vec(dt_bias), lane_vec(-jnp.exp(a_log)), dsk, norm_w[None, :])


def _layer_norm(x, g, b):
    xc = x - jnp.mean(x, -1, keepdims=True)
    var = jnp.mean(xc * xc, -1, keepdims=True)
    return (xc * lax.rsqrt(var + LN_EPS)) * g + b


def _outproj_ln_kernel(x_ref, m0_ref, m1_ref, m2_ref, m3_ref, w_ref, g_ref, b_ref, *rest):
    gw = m0_ref.shape[1]
    acc = ALPHA * x_ref[...]
    for j, m_ref in enumerate((m0_ref, m1_ref, m2_ref, m3_ref)):
        acc = acc + _bdot(m_ref[...], w_ref[j * gw:(j + 1) * gw, :])
    x1 = _layer_norm(acc, g_ref[...], b_ref[...])
    if len(rest) == 1:
        rest[0][...] = x1
    else:
        wr_ref, o_ref, lg_ref = rest
        o_ref[...] = x1
        lg_ref[...] = _bdot(x1, wr_ref[...])


def _outproj_ln(x, mixers, w_out, g, b, w_router=None, tm=512):
    t, d = x.shape
    gw = mixers[0].shape[1]
    row = lambda n: pl.BlockSpec((tm, n), lambda i: (i, 0))
    const = lambda shape: pl.BlockSpec(shape, lambda i: (0, 0))
    in_specs = [row(d)] + [row(gw)] * 4 + [const((d, d)), const((1, d)), const((1, d))]
    args = [x, *mixers, w_out.astype(jnp.bfloat16), g[None, :], b[None, :]]
    out_shape = [jax.ShapeDtypeStruct((t, d), jnp.float32)]
    out_specs = [row(d)]
    if w_router is not None:
        wr = jnp.pad(w_router, ((0, 0), (0, LANES - w_router.shape[1]))).astype(jnp.bfloat16)
        in_specs.append(const((d, LANES)))
        args.append(wr)
        out_shape.append(jax.ShapeDtypeStruct((t, LANES), jnp.float32))
        out_specs.append(row(LANES))
    return pl.pallas_call(
        _outproj_ln_kernel,
        out_shape=out_shape, grid=(t // tm,), in_specs=in_specs, out_specs=out_specs,
        compiler_params=pltpu.CompilerParams(
            dimension_semantics=("parallel",), vmem_limit_bytes=VMEM_LIMIT),
        name="outproj_ln",
    )(*args)


def _pe_ln_kernel(x_ref, p_ref, wp_ref, wg_ref, bg_ref, g_ref, b_ref, *rest):
    *ff_refs, o_ref = rest
    x1 = x_ref[...]
    pe = _bdot(p_ref[...], wp_ref[...]) * jax.nn.sigmoid(_bdot(x1, wg_ref[...]) + bg_ref[...])
    ff = functools.reduce(lambda a, c: a + c, [r[...] for r in ff_refs])
    o_ref[...] = _layer_norm(ALPHA * x1 + ff + pe, g_ref[...], b_ref[...])


def _pe_ln(x1, ff_parts, p, pe_proj, gate_w, gate_b, g, b, tm=512):
    t, d = x1.shape
    pd = p.shape[1]
    row = lambda n: pl.BlockSpec((tm, n), lambda i: (i, 0))
    const = lambda shape: pl.BlockSpec(shape, lambda i: (0, 0))
    return pl.pallas_call(
        _pe_ln_kernel,
        out_shape=jax.ShapeDtypeStruct((t, d), jnp.float32),
        grid=(t // tm,),
        in_specs=[row(d), row(pd), const((pd, d)), const((d, d)), const((1, d)), const((1, d)), const((1, d))]
        + [row(d)] * len(ff_parts),
        out_specs=row(d),
        compiler_params=pltpu.CompilerParams(
            dimension_semantics=("parallel",), vmem_limit_bytes=VMEM_LIMIT),
        name="pe_ln",
    )(x1, p, pe_proj.astype(jnp.bfloat16), gate_w.astype(jnp.bfloat16), gate_b[None, :], g[None, :], b[None, :],
      *ff_parts)


def _head_sums(x):
    rows = x.shape[0]
    parts = []
    for h in range(GROUP_HEADS):
        s = jnp.sum(x[:, h * HEAD_DIM:(h + 1) * HEAD_DIM], axis=1, keepdims=True)
        parts.append(jnp.broadcast_to(s, (rows, HEAD_DIM)))
    return jnp.concatenate(parts, axis=1)


def _rwkv_pre_kernel(c_ref, mu_ref, w0_ref, w2_ref, a0_ref, a2_ref, g2_ref, kk_ref, ka_ref,
                     pa_ref, pb_ref, pc_ref, g_ref, *, rc):
    seq = c_ref.shape[0]
    w, p = GROUP_W, HEAD_DIM
    o1 = 3 * w + RWKV_DECAY_RANK
    o2 = o1 + RWKV_ICL_RANK
    nchunks = seq // rc
    blk = rc // RWKV_TB
    row = lax.broadcasted_iota(jnp.int32, (rc, c_ref.shape[1]), 0)
    lane = lax.broadcasted_iota(jnp.int32, (rc, LANES), 1)

    def unit_keys(cols):
        kk = cols[:, w:2 * w] * kk_ref[...]
        return kk * lax.rsqrt(jnp.maximum(_head_sums(kk * kk), 1e-24))

    def pack(dst, c, x, z):
        for hp in range(w // LANES):
            x2, z2 = x[:, hp * LANES:(hp + 1) * LANES], z[:, hp * LANES:(hp + 1) * LANES]
            even = jnp.where(lane < p, x2, pltpu.roll(z2, p, 1))
            odd = jnp.where(lane < p, pltpu.roll(x2, p, 1), z2)
            for h, val in ((2 * hp, even), (2 * hp + 1, odd)):
                dst[c * blk:(c + 1) * blk, h, 0] = val.reshape(blk, RWKV_TB, LANES)

    for c in range(nchunks):
        r0 = c * rc
        cur = c_ref[r0:r0 + rc, :]
        if c == 0:
            prev = jnp.where(row == 0, 0.0, pltpu.roll(cur, 1, 0))
        else:
            prev = c_ref[r0 - 1:r0 - 1 + rc, :]
        if c == nchunks - 1:
            nxt = jnp.where(row == rc - 1, 0.0, pltpu.roll(cur, rc - 1, 0))
        else:
            nxt = c_ref[r0 + 1:r0 + 1 + rc, :]
        cols = cur + (prev - cur) * mu_ref[...]
        cols_next = nxt + (cur - nxt) * mu_ref[...]
        k = cols[:, w:2 * w]
        w_log = -_softplus(-(w0_ref[...] + _bdot(jnp.tanh(cols[:, 3 * w:o1]), w2_ref[...]))) - 0.5
        a = jax.nn.sigmoid(a0_ref[...] + _bdot(cols[:, o1:o2], a2_ref[...]))
        kk = unit_keys(cols)
        pack(pa_ref, c, jnp.exp(-jnp.exp(w_log)), kk * a)
        pack(pb_ref, c, k * (1.0 + (a - 1.0) * ka_ref[...]), cols[:, 0:w])
        pack(pc_ref, c, unit_keys(cols_next), cols[:, 2 * w:3 * w])
        g_ref[r0:r0 + rc, :] = _bdot(jax.nn.sigmoid(cols[:, o2:]), g2_ref[...])


def _rwkv_post_kernel(y_ref, pb_ref, pc_ref, g_ref, rk_ref, lw_ref, lb_ref, o_ref):
    p = HEAD_DIM
    rows = o_ref.shape[0]
    inv = 1.0 / p
    outs = []
    for h in range(GROUP_HEADS):
        hs = slice(h * p, (h + 1) * p)
        y = y_ref[:, h, 0].reshape(rows, LANES)[:, 0:p]
        kr = pb_ref[:, h, 0].reshape(rows, LANES)
        v = pc_ref[:, h, 0].reshape(rows, LANES)[:, p:2 * p]
        yc = y - jnp.sum(y, axis=1, keepdims=True) * inv
        yn = yc * lax.rsqrt(jnp.sum(yc * yc, axis=1, keepdims=True) * inv + RWKV_GN_EPS) * lw_ref[:, hs] + lb_ref[:, hs]
        rk = (kr * pltpu.roll(kr, p, 1))[:, 0:p] * rk_ref[:, hs]
        outs.append((yn + jnp.sum(rk, axis=1, keepdims=True) * v) * g_ref[:, hs])
    o_ref[...] = jnp.concatenate(outs, axis=1)


def _rwkv7_group(pr, bsz, mu, w0, w2, a0, a2, g2, k_k, k_a, r_k, ln_w, ln_b, tm=512):
    t = pr.shape[0]
    seq = t // bsz
    w = GROUP_W
    bf = lambda x: x.astype(jnp.bfloat16)
    const = lambda shape: pl.BlockSpec(shape, lambda b: (0, 0))
    vec = lambda x: x.reshape(1, -1)
    nh, tb = GROUP_HEADS, RWKV_TB
    nblk = seq // tb
    nat = jax.ShapeDtypeStruct((t, w), jnp.float32)
    packed = jax.ShapeDtypeStruct((nblk, nh, bsz, tb, LANES), jnp.float32)
    pspec = pl.BlockSpec((nblk, nh, 1, tb, LANES), lambda b: (0, 0, b, 0, 0))
    pa, pb, pc, g = pl.pallas_call(
        functools.partial(_rwkv_pre_kernel, rc=256),
        out_shape=[packed] * 3 + [nat],
        grid=(bsz,),
        in_specs=[pl.BlockSpec((seq, R_COLS), lambda b: (b, 0)), const((1, R_COLS)), const((1, w)),
                  const((RWKV_DECAY_RANK, w)), const((1, w)), const((RWKV_ICL_RANK, w)),
                  const((RWKV_GATE_RANK, w)), const((1, w)), const((1, w))],
        out_specs=[pspec] * 3 + [pl.BlockSpec((seq, w), lambda b: (b, 0))],
        compiler_params=pltpu.CompilerParams(
            dimension_semantics=("parallel",), vmem_limit_bytes=VMEM_LIMIT),
        name="rwkv_pre",
    )(pr, vec(mu), vec(w0), bf(w2), vec(a0), bf(a2), bf(g2), vec(k_k), vec(k_a))
    y = _rwkv_scan(pa, pb, pc, bsz)
    gb = tm // tb
    per_b = seq // tm
    bspec = pl.BlockSpec((gb, nh, 1, tb, LANES), lambda b, i: (i, 0, b, 0, 0))
    row = pl.BlockSpec((tm, w), lambda b, i: (b * per_b + i, 0))
    const2 = lambda shape: pl.BlockSpec(shape, lambda b, i: (0, 0))
    return pl.pallas_call(
        _rwkv_post_kernel,
        out_shape=nat,
        grid=(bsz, per_b),
        in_specs=[bspec, bspec, bspec, row] + [const2((1, w))] * 3,
        out_specs=row,
        compiler_params=pltpu.CompilerParams(
            dimension_semantics=("parallel", "parallel"), vmem_limit_bytes=VMEM_LIMIT),
        name="rwkv_post",
    )(y, pb, pc, g, vec(r_k), vec(ln_w), vec(ln_b))


FF_CHUNK = 512


def _swiglu_kernel(be_ref, x_ref, w1_ref, w3_ref, w2_ref, g_ref, o_ref):
    del be_ref
    xb = x_ref[...].astype(jnp.bfloat16)
    dff = w1_ref.shape[-1]
    acc = None
    for c0 in range(0, dff, FF_CHUNK):
        c1 = min(c0 + FF_CHUNK, dff)
        h1 = jnp.dot(xb, w1_ref[0, :, c0:c1], preferred_element_type=jnp.float32)
        h3 = jnp.dot(xb, w3_ref[0, :, c0:c1], preferred_element_type=jnp.float32)
        h = (jax.nn.silu(h1) * h3).astype(jnp.bfloat16)
        part = jnp.dot(h, w2_ref[0, c0:c1, :], preferred_element_type=jnp.float32)
        acc = part if acc is None else acc + part
    o_ref[...] = acc * g_ref[...]


def _grouped_swiglu(xrows, block_e, row_gate, w1, w3, w2, bm):
    rows, d = xrows.shape
    dff = w1.shape[-1]
    grid_spec = pltpu.PrefetchScalarGridSpec(
        num_scalar_prefetch=1,
        grid=(rows // bm,),
        in_specs=[pl.BlockSpec((bm, d), lambda i, be: (i, 0)),
                  pl.BlockSpec((1, d, dff), lambda i, be: (be[i], 0, 0)),
                  pl.BlockSpec((1, d, dff), lambda i, be: (be[i], 0, 0)),
                  pl.BlockSpec((1, dff, d), lambda i, be: (be[i], 0, 0)),
                  pl.BlockSpec((bm, 1), lambda i, be: (i, 0))],
        out_specs=pl.BlockSpec((bm, d), lambda i, be: (i, 0)),
    )
    return pl.pallas_call(
        _swiglu_kernel,
        out_shape=jax.ShapeDtypeStruct((rows, d), jnp.float32),
        grid_spec=grid_spec,
        compiler_params=pltpu.CompilerParams(
            dimension_semantics=("arbitrary",), vmem_limit_bytes=VMEM_LIMIT),
        name="swiglu",
    )(block_e, xrows, w1, w3, w2, row_gate)


def _cast_kernel(x_ref, o_ref):
    o_ref[0] = x_ref[0, 0].astype(o_ref.dtype)


def _layer_weights_bf16(w, layer):
    _, ne, r, c = w.shape
    rb = r
    return pl.pallas_call(
        _cast_kernel,
        out_shape=jax.ShapeDtypeStruct((ne, r, c), jnp.bfloat16),
        grid=(ne, r // rb),
        in_specs=[pl.BlockSpec((1, 1, rb, c), lambda e, j: (layer, e, j, 0))],
        out_specs=pl.BlockSpec((1, rb, c), lambda e, j: (e, j, 0)),
        compiler_params=pltpu.CompilerParams(
            dimension_semantics=("parallel", "parallel"), vmem_limit_bytes=VMEM_LIMIT),
        name="cast_bf16",
    )(w).reshape(ne, r, c)


def _swiglu(x2d, w1, w3, w2, layer, bm=512):
    rows = x2d.shape[0]
    bf = lambda t: _layer_weights_bf16(t[:, None], layer)
    return _grouped_swiglu(x2d, jnp.zeros((rows // bm,), jnp.int32), jnp.ones((rows, 1), jnp.float32),
                           bf(w1), bf(w3), bf(w2), bm)


def _moe_swiglu(xf, logits, w1, w3, w2, layer):
    T, d = xf.shape
    top_logit, top_e = lax.top_k(logits, TOP_K)
    gate = jax.nn.softmax(top_logit, axis=-1)
    flat_e = top_e.reshape(-1).astype(jnp.int32)
    flat_tok = jnp.repeat(jnp.arange(T, dtype=jnp.int32), TOP_K)
    onehot = (flat_e[:, None] == jnp.arange(N_EXPERTS, dtype=jnp.int32)[None, :]).astype(jnp.int32)
    csum = jnp.cumsum(onehot, axis=0)
    counts = csum[-1]
    rank = jnp.sum(onehot * (csum - 1), axis=1)
    padded = (counts + MOE_BLOCK - 1) // MOE_BLOCK * MOE_BLOCK
    pad_end = jnp.cumsum(padded)
    pad_start = pad_end - padded
    start = jnp.cumsum(counts) - counts
    pos = (pad_start[flat_e] + rank).reshape(T, TOP_K)
    _, st, sg = lax.sort((flat_e, flat_tok, gate.reshape(-1)), num_keys=1, is_stable=True)
    n_blocks = -(-(T * TOP_K) // MOE_BLOCK) + N_EXPERTS
    rows = n_blocks * MOE_BLOCK
    block_e = jnp.minimum(jnp.searchsorted(pad_end, jnp.arange(n_blocks) * MOE_BLOCK, side='right'),
                          N_EXPERTS - 1).astype(jnp.int32)
    row_e = jnp.repeat(block_e, MOE_BLOCK)
    q = jnp.arange(rows, dtype=jnp.int32) - pad_start[row_e]
    valid = q < counts[row_e]
    src = jnp.clip(start[row_e] + q, 0, T * TOP_K - 1)
    slot_tok = jnp.where(valid, st[src], 0)
    slot_gate = jnp.where(valid, sg[src], 0.0)
    bf = lambda t: _layer_weights_bf16(t, layer)
    yb = _grouped_swiglu(xf[slot_tok], block_e, slot_gate[:, None], bf(w1), bf(w3), bf(w2), MOE_BLOCK)
    return [yb[pos[:, 0]], yb[pos[:, 1]]]


def kernel(x, p, w_in, m_i_bias, m_f_bias, m_norm_w, g_conv_w, g_conv_b, g_w_a, g_b_a, g_w_x, g_b_x,
           g_lambda, r_mu, r_w0, r_w2, r_a0, r_a2, r_g2, r_k_k, r_k_a, r_r_k, r_ln_w, r_ln_b,
           s_conv_w, s_conv_b, s_dt_bias, s_a_log, s_d, s_norm_w, w_out, ln1_g, ln1_b, ln2_g, ln2_b,
           f_w1, f_w3, f_w2, e_router, e_w1, e_w3, e_w2, pe_proj, pe_gate_w, pe_gate_b):
    bsz, seq, d = x.shape
    T = bsz * seq
    x = x.reshape(T, d)
    for i in range(DEPTH):
        pm, pg, pr, ps, psm = _inproj(x, _split_w_in(w_in[i]))
        mixers = [
            _mlstm_group(pm, psm, bsz, m_i_bias[i], m_f_bias[i], m_norm_w[i]),
            _rglru_group(pg, bsz, g_conv_w[i], g_conv_b[i], g_w_a[i], g_b_a[i], g_w_x[i], g_b_x[i], g_lambda[i]),
            _rwkv7_group(pr, bsz, r_mu[i], r_w0[i], r_w2[i], r_a0[i], r_a2[i], r_g2[i],
                         r_k_k[i], r_k_a[i], r_r_k[i], r_ln_w[i], r_ln_b[i]),
            _ssd_group(ps, psm, bsz, s_conv_w[i], s_conv_b[i], s_dt_bias[i], s_a_log[i], s_d[i], s_norm_w[i]),
        ]
        if i % 2 == 0:
            (x,) = _outproj_ln(x, mixers, w_out[i], ln1_g[i], ln1_b[i])
            ff = [_swiglu(x, f_w1, f_w3, f_w2, i // 2)]
        else:
            x, logits = _outproj_ln(x, mixers, w_out[i], ln1_g[i], ln1_b[i], e_router[i // 2])
            ff = _moe_swiglu(x, logits[:, :N_EXPERTS], e_w1, e_w3, e_w2, i // 2)
        x = _pe_ln(x, ff, p[i].reshape(T, PE_DIM), pe_proj[i], pe_gate_w[i], pe_gate_b[i], ln2_g[i], ln2_b[i])
    return x.reshape(bsz, seq, d)
```

```python
import functools

import jax
import jax.numpy as jnp
from jax import lax
from jax.experimental import pallas as pl
from jax.experimental.pallas import tpu as pltpu

D_MODEL = 1024
DEPTH = 4
PE_DIM = 256
GROUP_W = 256
HEAD_DIM = 64
GROUP_HEADS = 4
CONV_K = 4
MLSTM_CHUNK = 64
RG_C = 8.0
RWKV_DECAY_RANK = 32
RWKV_ICL_RANK = 32
RWKV_GATE_RANK = 64
RWKV_GN_EPS = 64e-5
SSD_STATE = 64
SSD_GROUPS = 2
SSD_CHUNK = 128
SSD_XBC = GROUP_W + 2 * SSD_GROUPS * SSD_STATE
M_COLS = 4 * GROUP_W + 2 * GROUP_HEADS
G_COLS = 2 * GROUP_W
R_COLS = 3 * GROUP_W + RWKV_DECAY_RANK + RWKV_ICL_RANK + RWKV_GATE_RANK
S_COLS = GROUP_W + SSD_XBC + GROUP_HEADS
N_IN = M_COLS + G_COLS + R_COLS + S_COLS
N_EXPERTS = 8
TOP_K = 2
MOE_BLOCK = 512
ALPHA = (2 * DEPTH) ** 0.25
LN_EPS = 1e-5
RMS_EPS = 1e-6

LANES = 128
SUBLANES = 8
VMEM_LIMIT = 56 * 1024 * 1024


RWKV_TB = 32


def _rwkv_scan_kernel(pa_ref, pb_ref, pc_ref, y_ref, s_ref, sa_ref, ta_ref, tb_ref, tc_ref, yb_ref,
                      *, tb, nb, nk, nacc):
    @pl.when(pl.program_id(0) == 0)
    def _():
        s_ref[...] = jnp.zeros_like(s_ref)
        sa_ref[...] = jnp.zeros_like(sa_ref)

    nh = GROUP_HEADS
    half = nk // 2
    lane = lax.broadcasted_iota(jnp.int32, (half, LANES), 1)
    rows = lambda t, h: pl.ds(h * nb * tb + t, nb, stride=tb)

    def prepare(t, carry):
        for src, dst in ((pa_ref, ta_ref), (pb_ref, tb_ref), (pc_ref, tc_ref)):
            parts = [src[rows(t, h), :] for h in range(nh)]
            dst[t] = jnp.concatenate(parts + parts, axis=0).T
        return carry

    lax.fori_loop(0, tb, prepare, 0, unroll=8)

    def step(t, sa):
        bc = lambda ref, r: jnp.broadcast_to(ref[t, pl.ds(r, 1), :], (half, LANES))
        v_t = jnp.where(lane < nk, tc_ref[t, nk:nk + half, :], tc_ref[t, nk + half:2 * nk, :])
        y_acc = [None] * nacc
        sa_acc = [None] * nacc
        for k in range(nk):
            s_k = s_ref[k] * bc(ta_ref, k) - sa * bc(ta_ref, nk + k) + v_t * bc(tb_ref, k)
            s_ref[k] = s_k
            y_k = s_k * bc(tb_ref, nk + k)
            n_k = s_k * bc(tc_ref, k)
            a = k % nacc
            y_acc[a] = y_k if y_acc[a] is None else y_acc[a] + y_k
            sa_acc[a] = n_k if sa_acc[a] is None else sa_acc[a] + n_k
        yb_ref[t] = functools.reduce(lambda p, q: p + q, y_acc)
        return functools.reduce(lambda p, q: p + q, sa_acc)

    sa_ref[...] = lax.fori_loop(0, tb, step, sa_ref[...])

    def emit(t, carry):
        y = yb_ref[t]
        y_sw = pltpu.roll(y, nk, 1)
        y_nat = jnp.concatenate([y, y_sw, y, y_sw], axis=0).T
        for h in range(nh):
            y_ref[rows(t, h), :] = y_nat[h * nb:(h + 1) * nb, :]
        return carry

    lax.fori_loop(0, tb, emit, 0, unroll=8)


def _rwkv_scan(pa, pb, pc, bsz):
    nblk, nh, _, tb, _ = pa.shape
    n = HEAD_DIM
    assert 2 * bsz * nh == LANES and 2 * n == LANES
    flat = lambda t: t.reshape(nblk * nh * bsz * tb, LANES)
    spec = pl.BlockSpec((nh * bsz * tb, LANES), lambda i: (i, 0))
    y = pl.pallas_call(
        functools.partial(_rwkv_scan_kernel, tb=tb, nb=bsz, nk=n, nacc=2),
        out_shape=jax.ShapeDtypeStruct((nblk * nh * bsz * tb, LANES), jnp.float32),
        grid=(nblk,),
        in_specs=[spec, spec, spec],
        out_specs=spec,
        scratch_shapes=[pltpu.VMEM((n, n // 2, LANES), jnp.float32), pltpu.VMEM((n // 2, LANES), jnp.float32)]
        + [pltpu.VMEM((tb, LANES, LANES), jnp.float32)] * 3 + [pltpu.VMEM((tb, n // 2, LANES), jnp.float32)],
        compiler_params=pltpu.CompilerParams(
            dimension_semantics=("arbitrary",), vmem_limit_bytes=VMEM_LIMIT),
        name="rwkv_scan",
    )(flat(pa), flat(pb), flat(pc))
    return y.reshape(pa.shape)


def _inproj_kernel(x_ref, *refs):
    n = len(refs) // 2
    xb = x_ref[...].astype(jnp.bfloat16)
    for w_ref, o_ref in zip(refs[:n], refs[n:]):
        o_ref[...] = jnp.dot(xb, w_ref[...], preferred_element_type=jnp.float32)


def _inproj(x, weights, tm=512):
    t, d = x.shape
    return pl.pallas_call(
        _inproj_kernel,
        out_shape=[jax.ShapeDtypeStruct((t, w.shape[1]), jnp.float32) for w in weights],
        grid=(t // tm,),
        in_specs=[pl.BlockSpec((tm, d), lambda i: (i, 0))]
        + [pl.BlockSpec(w.shape, lambda i: (0, 0)) for w in weights],
        out_specs=[pl.BlockSpec((tm, w.shape[1]), lambda i: (i, 0)) for w in weights],
        compiler_params=pltpu.CompilerParams(
            dimension_semantics=("parallel",), vmem_limit_bytes=VMEM_LIMIT),
        name="inproj",
    )(x, *weights)


def _split_w_in(w):
    m0, g0, r0, s0 = 0, M_COLS, M_COLS + G_COLS, M_COLS + G_COLS + R_COLS
    w_m = w[:, m0:m0 + 4 * GROUP_W]
    w_g = w[:, g0:g0 + G_COLS]
    w_r = w[:, r0:r0 + R_COLS]
    w_s = w[:, s0:s0 + GROUP_W + SSD_XBC]
    small = jnp.concatenate([w[:, m0 + 4 * GROUP_W:g0], w[:, s0 + GROUP_W + SSD_XBC:]], axis=1)
    small = jnp.pad(small, ((0, 0), (0, LANES - small.shape[1])))
    return [t.astype(jnp.bfloat16) for t in (w_m, w_g, w_r, w_s, small)]


def _shift_rows(x, s, fill, row):
    return jnp.where(row >= s, pltpu.roll(x, s, 0), fill)


def _rglru_kernel(g_ref, cw_ref, cb_ref, wg_ref, bg_ref, c_ref, o_ref, xpad_ref, a_ref, u_ref, *, rc, unroll):
    seq, w = o_ref.shape
    pad = SUBLANES
    xpad_ref[0:pad, :] = jnp.zeros((pad, w), jnp.float32)
    xpad_ref[pad:, :] = g_ref[:, 0:w]
    cw = cw_ref[...]
    for c in range(seq // rc):
        r0 = c * rc
        xc = cb_ref[...] + cw[CONV_K - 1:CONV_K, :] * g_ref[r0:r0 + rc, 0:w]
        for j in range(1, CONV_K):
            xc = xc + cw[CONV_K - 1 - j:CONV_K - j, :] * xpad_ref[pad + r0 - j:pad + r0 - j + rc, :]
        z = jnp.dot(xc.astype(jnp.bfloat16), wg_ref[...], preferred_element_type=jnp.float32) + bg_ref[...]
        z = jax.nn.sigmoid(z)
        a = jnp.exp(c_ref[...] * z[:, 0:w])
        a_ref[r0:r0 + rc, :] = a
        u_ref[r0:r0 + rc, :] = jnp.sqrt(1.0 - a * a) * (z[:, w:] * xc)

    row = lax.broadcasted_iota(jnp.int32, (SUBLANES, w), 0)

    def tiles(i, h):
        for j in range(unroll):
            r = pl.multiple_of((i * unroll + j) * SUBLANES, SUBLANES)
            a = a_ref[pl.ds(r, SUBLANES), :]
            u = u_ref[pl.ds(r, SUBLANES), :]
            for s in (1, 2, 4):
                u = a * _shift_rows(u, s, 0.0, row) + u
                a = a * _shift_rows(a, s, 1.0, row)
            ht = a * h + u
            o_ref[pl.ds(r, SUBLANES), :] = ht * jax.nn.gelu(g_ref[pl.ds(r, SUBLANES), w:2 * w])
            h = jnp.broadcast_to(ht[SUBLANES - 1:SUBLANES, :], (SUBLANES, w))
        return h

    lax.fori_loop(0, seq // (SUBLANES * unroll), tiles, jnp.zeros((SUBLANES, w), jnp.float32))


def _block_diag(w):
    h, p, _ = w.shape
    eye = jnp.eye(h, dtype=w.dtype)
    return jnp.einsum('hij,hg->higj', w, eye).reshape(h * p, h * p)


def _rglru_group(g_cols, bsz, conv_w, conv_b, w_a, b_a, w_x, b_x, lam):
    t = g_cols.shape[0]
    seq = t // bsz
    w = GROUP_W
    wg = jnp.concatenate([_block_diag(w_a), _block_diag(w_x)], axis=1).astype(jnp.bfloat16)
    bg = jnp.concatenate([b_a, b_x])[None, :]
    cdec = (-RG_C * jax.nn.softplus(-lam))[None, :]
    const = lambda shape: pl.BlockSpec(shape, lambda b: (0, 0))
    return pl.pallas_call(
        functools.partial(_rglru_kernel, rc=256, unroll=8),
        out_shape=jax.ShapeDtypeStruct((t, w), jnp.float32),
        grid=(bsz,),
        in_specs=[pl.BlockSpec((seq, 2 * w), lambda b: (b, 0)), const((CONV_K, w)), const((1, w)),
                  const((w, 2 * w)), const((1, 2 * w)), const((1, w))],
        out_specs=pl.BlockSpec((seq, w), lambda b: (b, 0)),
        scratch_shapes=[pltpu.VMEM((seq + SUBLANES, w), jnp.float32),
                        pltpu.VMEM((seq, w), jnp.float32), pltpu.VMEM((seq, w), jnp.float32)],
        compiler_params=pltpu.CompilerParams(
            dimension_semantics=("parallel",), vmem_limit_bytes=VMEM_LIMIT),
        name="rglru",
    )(g_cols, conv_w, conv_b[None, :], wg, bg, cdec)


MIXER_ROWS_PER_STEP = 2
_NT = (((1,), (1,)), ((), ()))
_TN = (((0,), (0,)), ((), ()))


def _bdot(a, b, dims=(((1,), (0,)), ((), ()))):
    return lax.dot_general(a.astype(jnp.bfloat16), b.astype(jnp.bfloat16), dims,
                           preferred_element_type=jnp.float32)


def _cumsum_rows(x):
    n = x.shape[0]
    tri = (lax.broadcasted_iota(jnp.int32, (n, n), 0) >= lax.broadcasted_iota(jnp.int32, (n, n), 1))
    return jnp.dot(tri.astype(jnp.float32), x, precision=lax.Precision.HIGHEST,
                   preferred_element_type=jnp.float32)


def _log_sigmoid(x):
    return jnp.minimum(x, 0.0) - jnp.log1p(jnp.exp(-jnp.abs(x)))


def _softplus(x):
    return jnp.maximum(x, 0.0) + jnp.log1p(jnp.exp(-jnp.abs(x)))


def _mlstm_kernel(pm_ref, sm_ref, bias_ref, nw_ref, o_ref, c_ref, n_ref, m_ref):
    _, seq, w = o_ref.shape
    nh, dh, L = GROUP_HEADS, HEAD_DIM, MLSTM_CHUNK
    c_ref[...] = jnp.zeros_like(c_ref)
    n_ref[...] = jnp.zeros_like(n_ref)
    m_ref[...] = jnp.zeros_like(m_ref)
    f32 = jnp.float32
    hi = lax.Precision.HIGHEST
    col = lax.broadcasted_iota(jnp.int32, (L, LANES), 1)
    lrow = lax.broadcasted_iota(jnp.int32, (L, LANES), 0)
    scol = col & (dh - 1)
    lo = col < dh
    lo1 = lo[0:1]
    sq = lax.broadcasted_iota(jnp.int32, (LANES, LANES), 0)
    sr = lax.broadcasted_iota(jnp.int32, (LANES, LANES), 1)
    bd = (sq < dh) == (sr < dh)
    e_row = lax.broadcasted_iota(jnp.int32, (LANES, 2 * w), 0)
    e_col = lax.broadcasted_iota(jnp.int32, (LANES, 2 * w), 1)
    e_head = jnp.where(e_col < w, 0, nh) + ((e_col & (w - 1)) >> (dh.bit_length() - 1))
    spread = (e_row == e_head).astype(f32)
    tri = (lax.broadcasted_iota(jnp.int32, (L, L), 0) >= lax.broadcasted_iota(jnp.int32, (L, L), 1)).astype(f32)

    def seg_max(x, first):
        a = jnp.max(jnp.where(first, x, -jnp.inf), axis=1, keepdims=True)
        b = jnp.max(jnp.where(first, -jnp.inf, x), axis=1, keepdims=True)
        return jnp.where(first, a, b)

    def seg_sum(x):
        a = jnp.sum(jnp.where(lo, x, 0.0), axis=1, keepdims=True)
        b = jnp.sum(jnp.where(lo, 0.0, x), axis=1, keepdims=True)
        return jnp.where(lo, a, b)

    nb = pm_ref.shape[0]
    items = [(b, p) for b in range(nb) for p in range(nh // 2)]
    lanes = lambda p, sec: slice(sec * w + p * LANES, sec * w + (p + 1) * LANES)

    def chunk(c, carry):
        r0 = pl.multiple_of(c * L, L)
        rows = pl.ds(r0, L)
        each = lambda f, *xs: [f(*args) for args in zip(*xs)]
        gts = [sm_ref[b, rows, :] + bias_ref[...] for b in range(nb)]
        gts = [jnp.where(col < nh, t, jnp.where(col < 2 * nh, _log_sigmoid(t), 0.0)) for t in gts]
        rep = [jnp.dot(t, spread, precision=hi, preferred_element_type=f32) for t in gts]
        b_rep = [jnp.dot(tri, t[:, w:], precision=hi, preferred_element_type=f32) for t in rep]
        bc = [b_rep[b][:, lanes(p, 0)] for b, p in items]
        lic = [rep[b][:, lanes(p, 0)] for b, p in items]
        lfc = [rep[b][:, lanes(p, 1)] for b, p in items]
        b_row = [jnp.sum(jnp.where(lrow <= scol, t, 0.0), axis=0, keepdims=True) for t in lfc]
        li_row = [jnp.sum(jnp.where(lrow == scol, t, 0.0), axis=0, keepdims=True) for t in lic]
        g = [t[L - 1:L, :] for t in bc]
        q = [pm_ref[b, rows, lanes(p, 0)] for b, p in items]
        k = [pm_ref[b, rows, lanes(p, 1)] * (dh ** -0.5) for b, p in items]
        v = [pm_ref[b, rows, lanes(p, 2)] for b, p in items]
        cst = [c_ref[b, p] for b, p in items]
        nst = [n_ref[b, p] for b, p in items]
        m = [m_ref[b, p] for b, p in items]
        log_d = each(lambda bc_, br, lr: jnp.where(lrow >= scol, bc_ - br + lr, -jnp.inf), bc, b_row, li_row)
        m_t = each(lambda bc_, m_, ld: jnp.maximum(bc_ + m_, seg_max(ld, lo)), bc, m, log_d)
        k2 = [jnp.concatenate([t, t], axis=0) for t in k]
        qk = each(lambda q_, k2_: jnp.where(lo, _bdot(jnp.where(lo, q_, 0.0), k2_, _NT),
                                            _bdot(jnp.where(lo, 0.0, q_), k2_, _NT)), q, k2)
        s = each(lambda qk_, ld, mt: qk_ * jnp.exp(ld - mt), qk, log_d, m_t)
        inter_w = each(lambda bc_, m_, mt: jnp.exp(bc_ + m_ - mt), bc, m, m_t)
        v_bd = [jnp.where(bd, jnp.concatenate([t, t], axis=0), 0.0) for t in v]
        num = each(lambda s_, vb, iw, q_, c_: _bdot(s_, vb) + iw * _bdot(q_, c_), s, v_bd, inter_w, q, cst)
        den = each(lambda s_, iw, q_, n_: seg_sum(s_) + iw * seg_sum(q_ * n_), s, inter_w, q, nst)
        hh = each(lambda nu, de, mt: nu / jnp.maximum(jnp.abs(de), jnp.exp(-mt)), num, den, m_t)
        ms = [seg_sum(t * t) for t in hh]
        for (b, p), h_, ms_ in zip(items, hh, ms):
            o = pm_ref[b, rows, lanes(p, 3)]
            o_ref[b, rows, lanes(p, 0)] = (h_ * lax.rsqrt(ms_ * (1.0 / dh) + RMS_EPS) * nw_ref[:, lanes(p, 0)]
                                           * jax.nn.sigmoid(o))
        m_new = each(lambda g_, m_, br, lr: jnp.maximum(g_ + m_, seg_max(g_ - br + lr, lo1)), g, m, b_row, li_row)
        decay = each(lambda g_, m_, mn: jnp.exp(g_ + m_ - mn), g, m, m_new)
        w_col = each(lambda g_, bc_, li_, mn: jnp.exp(g_ - bc_ + li_ - mn), g, bc, lic, m_new)
        upd = each(lambda k_, wc, v_: jnp.where(bd, _bdot(k_, wc * v_, _TN), 0.0), k, w_col, v)
        for i, (b, p) in enumerate(items):
            c_ref[b, p] = decay[i] * cst[i] + upd[i]
            n_ref[b, p] = decay[i] * nst[i] + jnp.sum(w_col[i] * k[i], axis=0, keepdims=True)
            m_ref[b, p] = m_new[i]
        return carry

    lax.fori_loop(0, seq // L, chunk, 0)


def _mlstm_group(pm, psm, bsz, i_bias, f_bias, norm_w):
    t = pm.shape[0]
    seq = t // bsz
    w, nh, dh = GROUP_W, GROUP_HEADS, HEAD_DIM
    bias = jnp.pad(jnp.concatenate([i_bias, f_bias]), (0, LANES - 2 * nh))[None, :]
    const = lambda shape: pl.BlockSpec(shape, lambda b: (0,) * len(shape))
    nb = MIXER_ROWS_PER_STEP
    assert bsz % nb == 0
    seq_block = lambda n: pl.BlockSpec((nb, seq, n), lambda b: (b, 0, 0))
    return pl.pallas_call(
        _mlstm_kernel,
        out_shape=jax.ShapeDtypeStruct((bsz, seq, w), jnp.float32),
        grid=(bsz // nb,),
        in_specs=[seq_block(4 * w), seq_block(LANES), const((1, LANES)), const((1, w))],
        out_specs=seq_block(w),
        scratch_shapes=[pltpu.VMEM((nb, nh // 2, LANES, LANES), jnp.float32),
                        pltpu.VMEM((nb, nh // 2, 1, LANES), jnp.float32),
                        pltpu.VMEM((nb, nh // 2, 1, LANES), jnp.float32)],
        compiler_params=pltpu.CompilerParams(
            dimension_semantics=("parallel",), vmem_limit_bytes=VMEM_LIMIT),
        name="mlstm",
    )(pm.reshape(bsz, seq, 4 * w), psm.reshape(bsz, seq, LANES), bias, norm_w[None, :]).reshape(t, w)


def _ssd_kernel(ps_ref, sm_ref, cw_ref, cb_ref, dtb_ref, aneg_ref, dsk_ref, nw_ref, o_ref,
                xbc_ref, st_ref, *, rc):
    nb, seq, w = o_ref.shape
    nh, hp, ng, ns, L = GROUP_HEADS, HEAD_DIM, SSD_GROUPS, SSD_STATE, SSD_CHUNK
    cx = SSD_XBC
    cw = cw_ref[...]
    row = lax.broadcasted_iota(jnp.int32, (rc, cx), 0)
    for b in range(nb):
        for c in range(seq // rc):
            r0 = c * rc
            cur = ps_ref[b, r0:r0 + rc, w:w + cx]
            acc = cb_ref[...] + cw[CONV_K - 1:CONV_K, :] * cur
            for j in range(1, CONV_K):
                if c == 0:
                    sh = jnp.where(row >= j, pltpu.roll(cur, j, 0), 0.0)
                else:
                    sh = ps_ref[b, r0 - j:r0 - j + rc, w:w + cx]
                acc = acc + cw[CONV_K - 1 - j:CONV_K - j, :] * sh
            xbc_ref[b, r0:r0 + rc, :] = jax.nn.silu(acc)
    st_ref[...] = jnp.zeros_like(st_ref)
    causal = lax.broadcasted_iota(jnp.int32, (L, L), 0) >= lax.broadcasted_iota(jnp.int32, (L, L), 1)
    gsz = w // ng
    rep = nh // ng
    groups = [(b, g) for b in range(nb) for g in range(ng)]
    heads = [(b, h) for b in range(nb) for h in range(nh)]

    def chunk(c, carry):
        r0 = pl.multiple_of(c * L, L)
        rows = pl.ds(r0, L)
        each = lambda f, *xs: [f(*args) for args in zip(*xs)]
        dt_all = [_softplus(sm_ref[b, rows, :] + dtb_ref[...]) for b in range(nb)]
        acs_all = [_cumsum_rows(aneg_ref[...] * t) for t in dt_all]
        acs_t = [t.T for t in acs_all]
        bm = {bg: xbc_ref[bg[0], rows, w + bg[1] * ns:w + (bg[1] + 1) * ns] for bg in groups}
        cm = {bg: xbc_ref[bg[0], rows, w + (ng + bg[1]) * ns:w + (ng + bg[1] + 1) * ns] for bg in groups}
        cb = {bg: _bdot(cm[bg], bm[bg], _NT) for bg in groups}
        grp = lambda b, h: (b, h // rep)
        cidx = lambda h: 2 * nh + h
        acs_col = [acs_all[b][:, cidx(h):cidx(h) + 1] for b, h in heads]
        acs_row = [acs_t[b][cidx(h):cidx(h) + 1, :] for b, h in heads]
        dt_col = [dt_all[b][:, cidx(h):cidx(h) + 1] for b, h in heads]
        a_last = [t[L - 1:L, :] for t in acs_col]
        xh = [xbc_ref[b, rows, h * hp:(h + 1) * hp] for b, h in heads]
        xdt = each(lambda x_, d_: x_ * d_, xh, dt_col)
        state = [st_ref[b, h] for b, h in heads]
        scores = [cb[grp(b, h)] * jnp.exp(jnp.where(causal, ac - ar, -jnp.inf))
                  for (b, h), ac, ar in zip(heads, acs_col, acs_row)]
        y_diag = each(_bdot, scores, xdt)
        y_off = [_bdot(cm[grp(b, h)], st, _NT) for (b, h), st in zip(heads, state)]
        ys = [yd + jnp.exp(ac) * yo + x_ * dsk_ref[:, h * hp:(h + 1) * hp]
              for (b, h), yd, ac, yo, x_ in zip(heads, y_diag, acs_col, y_off, xh)]
        upd = [_bdot(xd * jnp.exp(al - ac), bm[grp(b, h)], _TN)
               for (b, h), xd, al, ac in zip(heads, xdt, a_last, acs_col)]
        for i, (b, h) in enumerate(heads):
            st_ref[b, h] = jnp.exp(a_last[i]) * state[i] + upd[i]
        ysd = dict(zip(heads, ys))
        ygs = [jnp.concatenate([ysd[(b, g * rep + j)] for j in range(rep)], axis=1)
               * jax.nn.silu(ps_ref[b, rows, g * gsz:(g + 1) * gsz]) for b, g in groups]
        ms = [jnp.mean(t * t, axis=1, keepdims=True) for t in ygs]
        for (b, g), yg, m_ in zip(groups, ygs, ms):
            o_ref[b, rows, g * gsz:(g + 1) * gsz] = yg * lax.rsqrt(m_ + RMS_EPS) * nw_ref[:, g * gsz:(g + 1) * gsz]
        return carry

    lax.fori_loop(0, seq // L, chunk, 0)


def _ssd_group(ps, psm, bsz, conv_w, conv_b, dt_bias, a_log, d_skip, norm_w):
    t = ps.shape[0]
    seq = t // bsz
    w, nh = GROUP_W, GROUP_HEADS
    lane_vec = lambda v: jnp.pad(v, (2 * nh, LANES - 3 * nh))[None, :]
    dsk = jnp.repeat(d_skip, HEAD_DIM)[None, :]
    const = lambda shape: pl.BlockSpec(shape, lambda b: (0,) * len(shape))
    nb = MIXER_ROWS_PER_STEP
    rc = 256
    assert bsz % nb == 0 and seq % rc == 0
    seq_block = lambda n: pl.BlockSpec((nb, seq, n), lambda b: (b, 0, 0))
    return pl.pallas_call(
        functools.partial(_ssd_kernel, rc=rc),
        out_shape=jax.ShapeDtypeStruct((bsz, seq, w), jnp.float32),
        grid=(bsz // nb,),
        in_specs=[seq_block(w + SSD_XBC), seq_block(LANES),
                  const((CONV_K, SSD_XBC)), const((1, SSD_XBC)), const((1, LANES)), const((1, LANES)),
                  const((1, w)), const((1, w))],
        out_specs=seq_block(w),
        scratch_shapes=[pltpu.VMEM((nb, seq, SSD_XBC), jnp.float32),
                        pltpu.VMEM((nb, nh, HEAD_DIM, SSD_STATE), jnp.float32)],
        compiler_params=pltpu.CompilerParams(
            dimension_semantics=("parallel",), vmem_limit_bytes=VMEM_LIMIT),
        name="ssd",
    )(ps.reshape(bsz, seq, w + SSD_XBC), psm.reshape(bsz, seq, LANES), conv_w, conv_b[None, :], lane_vec(dt_bias),
      lane_vec(-jnp.exp(a_log)), dsk, norm_w[None, :]).reshape(t, w)


def _layer_norm(x, g, b):
    xc = x - jnp.mean(x, -1, keepdims=True)
    var = jnp.mean(xc * xc, -1, keepdims=True)
    return (xc * lax.rsqrt(var + LN_EPS)) * g + b


def _outproj_ln_kernel(x_ref, m0_ref, m1_ref, m2_ref, m3_ref, w_ref, g_ref, b_ref, *rest):
    gw = m0_ref.shape[1]
    acc = ALPHA * x_ref[...]
    for j, m_ref in enumerate((m0_ref, m1_ref, m2_ref, m3_ref)):
        acc = acc + _bdot(m_ref[...], w_ref[j * gw:(j + 1) * gw, :])
    x1 = _layer_norm(acc, g_ref[...], b_ref[...])
    if len(rest) == 1:
        rest[0][...] = x1
    else:
        wr_ref, o_ref, lg_ref = rest
        o_ref[...] = x1
        lg_ref[...] = _bdot(x1, wr_ref[...])


def _outproj_ln(x, mixers, w_out, g, b, w_router=None, tm=512):
    t, d = x.shape
    gw = mixers[0].shape[1]
    row = lambda n: pl.BlockSpec((tm, n), lambda i: (i, 0))
    const = lambda shape: pl.BlockSpec(shape, lambda i: (0, 0))
    in_specs = [row(d)] + [row(gw)] * 4 + [const((d, d)), const((1, d)), const((1, d))]
    args = [x, *mixers, w_out.astype(jnp.bfloat16), g[None, :], b[None, :]]
    out_shape = [jax.ShapeDtypeStruct((t, d), jnp.float32)]
    out_specs = [row(d)]
    if w_router is not None:
        wr = jnp.pad(w_router, ((0, 0), (0, LANES - w_router.shape[1]))).astype(jnp.bfloat16)
        in_specs.append(const((d, LANES)))
        args.append(wr)
        out_shape.append(jax.ShapeDtypeStruct((t, LANES), jnp.float32))
        out_specs.append(row(LANES))
    return pl.pallas_call(
        _outproj_ln_kernel,
        out_shape=out_shape, grid=(t // tm,), in_specs=in_specs, out_specs=out_specs,
        compiler_params=pltpu.CompilerParams(
            dimension_semantics=("parallel",), vmem_limit_bytes=VMEM_LIMIT),
        name="outproj_ln",
    )(*args)


def _pe_ln_kernel(x_ref, p_ref, wp_ref, wg_ref, bg_ref, g_ref, b_ref, *rest):
    *ff_refs, o_ref = rest
    x1 = x_ref[...]
    pe = _bdot(p_ref[...], wp_ref[...]) * jax.nn.sigmoid(_bdot(x1, wg_ref[...]) + bg_ref[...])
    ff = functools.reduce(lambda a, c: a + c, [r[...] for r in ff_refs])
    o_ref[...] = _layer_norm(ALPHA * x1 + ff + pe, g_ref[...], b_ref[...])


def _pe_ln(x1, ff_parts, p, pe_proj, gate_w, gate_b, g, b, tm=512):
    t, d = x1.shape
    pd = p.shape[1]
    row = lambda n: pl.BlockSpec((tm, n), lambda i: (i, 0))
    const = lambda shape: pl.BlockSpec(shape, lambda i: (0, 0))
    return pl.pallas_call(
        _pe_ln_kernel,
        out_shape=jax.ShapeDtypeStruct((t, d), jnp.float32),
        grid=(t // tm,),
        in_specs=[row(d), row(pd), const((pd, d)), const((d, d)), const((1, d)), const((1, d)), const((1, d))]
        + [row(d)] * len(ff_parts),
        out_specs=row(d),
        compiler_params=pltpu.CompilerParams(
            dimension_semantics=("parallel",), vmem_limit_bytes=VMEM_LIMIT),
        name="pe_ln",
    )(x1, p, pe_proj.astype(jnp.bfloat16), gate_w.astype(jnp.bfloat16), gate_b[None, :], g[None, :], b[None, :],
      *ff_parts)


def _head_sums(x):
    rows = x.shape[0]
    parts = []
    for h in range(GROUP_HEADS):
        s = jnp.sum(x[:, h * HEAD_DIM:(h + 1) * HEAD_DIM], axis=1, keepdims=True)
        parts.append(jnp.broadcast_to(s, (rows, HEAD_DIM)))
    return jnp.concatenate(parts, axis=1)


def _rwkv_pre_kernel(c_ref, mu_ref, w0_ref, w2_ref, a0_ref, a2_ref, g2_ref, kk_ref, ka_ref,
                     pa_ref, pb_ref, pc_ref, g_ref, *, rc):
    seq = c_ref.shape[0]
    w, p = GROUP_W, HEAD_DIM
    o1 = 3 * w + RWKV_DECAY_RANK
    o2 = o1 + RWKV_ICL_RANK
    nchunks = seq // rc
    blk = rc // RWKV_TB
    row = lax.broadcasted_iota(jnp.int32, (rc, c_ref.shape[1]), 0)
    lane = lax.broadcasted_iota(jnp.int32, (rc, LANES), 1)

    def unit_keys(cols):
        kk = cols[:, w:2 * w] * kk_ref[...]
        return kk * lax.rsqrt(jnp.maximum(_head_sums(kk * kk), 1e-24))

    def pack(dst, c, x, z):
        for hp in range(w // LANES):
            x2, z2 = x[:, hp * LANES:(hp + 1) * LANES], z[:, hp * LANES:(hp + 1) * LANES]
            even = jnp.where(lane < p, x2, pltpu.roll(z2, p, 1))
            odd = jnp.where(lane < p, pltpu.roll(x2, p, 1), z2)
            for h, val in ((2 * hp, even), (2 * hp + 1, odd)):
                dst[c * blk:(c + 1) * blk, h, 0] = val.reshape(blk, RWKV_TB, LANES)

    for c in range(nchunks):
        r0 = c * rc
        cur = c_ref[r0:r0 + rc, :]
        if c == 0:
            prev = jnp.where(row == 0, 0.0, pltpu.roll(cur, 1, 0))
        else:
            prev = c_ref[r0 - 1:r0 - 1 + rc, :]
        if c == nchunks - 1:
            nxt = jnp.where(row == rc - 1, 0.0, pltpu.roll(cur, rc - 1, 0))
        else:
            nxt = c_ref[r0 + 1:r0 + 1 + rc, :]
        cols = cur + (prev - cur) * mu_ref[...]
        cols_next = nxt + (cur - nxt) * mu_ref[...]
        k = cols[:, w:2 * w]
        w_log = -_softplus(-(w0_ref[...] + _bdot(jnp.tanh(cols[:, 3 * w:o1]), w2_ref[...]))) - 0.5
        a = jax.nn.sigmoid(a0_ref[...] + _bdot(cols[:, o1:o2], a2_ref[...]))
        kk = unit_keys(cols)
        pack(pa_ref, c, jnp.exp(-jnp.exp(w_log)), kk * a)
        pack(pb_ref, c, k * (1.0 + (a - 1.0) * ka_ref[...]), cols[:, 0:w])
        pack(pc_ref, c, unit_keys(cols_next), cols[:, 2 * w:3 * w])
        g_ref[r0:r0 + rc, :] = _bdot(jax.nn.sigmoid(cols[:, o2:]), g2_ref[...])


def _rwkv_post_kernel(y_ref, pb_ref, pc_ref, g_ref, rk_ref, lw_ref, lb_ref, o_ref):
    p = HEAD_DIM
    rows = o_ref.shape[0]
    inv = 1.0 / p
    outs = []
    for h in range(GROUP_HEADS):
        hs = slice(h * p, (h + 1) * p)
        y = y_ref[:, h, 0].reshape(rows, LANES)[:, 0:p]
        kr = pb_ref[:, h, 0].reshape(rows, LANES)
        v = pc_ref[:, h, 0].reshape(rows, LANES)[:, p:2 * p]
        yc = y - jnp.sum(y, axis=1, keepdims=True) * inv
        yn = yc * lax.rsqrt(jnp.sum(yc * yc, axis=1, keepdims=True) * inv + RWKV_GN_EPS) * lw_ref[:, hs] + lb_ref[:, hs]
        rk = (kr * pltpu.roll(kr, p, 1))[:, 0:p] * rk_ref[:, hs]
        outs.append((yn + jnp.sum(rk, axis=1, keepdims=True) * v) * g_ref[:, hs])
    o_ref[...] = jnp.concatenate(outs, axis=1)


def _rwkv7_group(pr, bsz, mu, w0, w2, a0, a2, g2, k_k, k_a, r_k, ln_w, ln_b, tm=512):
    t = pr.shape[0]
    seq = t // bsz
    w = GROUP_W
    bf = lambda x: x.astype(jnp.bfloat16)
    const = lambda shape: pl.BlockSpec(shape, lambda b: (0, 0))
    vec = lambda x: x.reshape(1, -1)
    nh, tb = GROUP_HEADS, RWKV_TB
    nblk = seq // tb
    nat = jax.ShapeDtypeStruct((t, w), jnp.float32)
    packed = jax.ShapeDtypeStruct((nblk, nh, bsz, tb, LANES), jnp.float32)
    pspec = pl.BlockSpec((nblk, nh, 1, tb, LANES), lambda b: (0, 0, b, 0, 0))
    pa, pb, pc, g = pl.pallas_call(
        functools.partial(_rwkv_pre_kernel, rc=256),
        out_shape=[packed] * 3 + [nat],
        grid=(bsz,),
        in_specs=[pl.BlockSpec((seq, R_COLS), lambda b: (b, 0)), const((1, R_COLS)), const((1, w)),
                  const((RWKV_DECAY_RANK, w)), const((1, w)), const((RWKV_ICL_RANK, w)),
                  const((RWKV_GATE_RANK, w)), const((1, w)), const((1, w))],
        out_specs=[pspec] * 3 + [pl.BlockSpec((seq, w), lambda b: (b, 0))],
        compiler_params=pltpu.CompilerParams(
            dimension_semantics=("parallel",), vmem_limit_bytes=VMEM_LIMIT),
        name="rwkv_pre",
    )(pr, vec(mu), vec(w0), bf(w2), vec(a0), bf(a2), bf(g2), vec(k_k), vec(k_a))
    y = _rwkv_scan(pa, pb, pc, bsz)
    gb = tm // tb
    per_b = seq // tm
    bspec = pl.BlockSpec((gb, nh, 1, tb, LANES), lambda b, i: (i, 0, b, 0, 0))
    row = pl.BlockSpec((tm, w), lambda b, i: (b * per_b + i, 0))
    const2 = lambda shape: pl.BlockSpec(shape, lambda b, i: (0, 0))
    return pl.pallas_call(
        _rwkv_post_kernel,
        out_shape=nat,
        grid=(bsz, per_b),
        in_specs=[bspec, bspec, bspec, row] + [const2((1, w))] * 3,
        out_specs=row,
        compiler_params=pltpu.CompilerParams(
            dimension_semantics=("parallel", "parallel"), vmem_limit_bytes=VMEM_LIMIT),
        name="rwkv_post",
    )(y, pb, pc, g, vec(r_k), vec(ln_w), vec(ln_b))


FF_CHUNK = 512


def _swiglu_kernel(be_ref, x_ref, w1_ref, w3_ref, w2_ref, g_ref, o_ref):
    del be_ref
    xb = x_ref[...].astype(jnp.bfloat16)
    dff = w1_ref.shape[-1]
    acc = None
    for c0 in range(0, dff, FF_CHUNK):
        c1 = min(c0 + FF_CHUNK, dff)
        h1 = jnp.dot(xb, w1_ref[0, :, c0:c1], preferred_element_type=jnp.float32)
        h3 = jnp.dot(xb, w3_ref[0, :, c0:c1], preferred_element_type=jnp.float32)
        h = (jax.nn.silu(h1) * h3).astype(jnp.bfloat16)
        part = jnp.dot(h, w2_ref[0, c0:c1, :], preferred_element_type=jnp.float32)
        acc = part if acc is None else acc + part
    o_ref[...] = acc * g_ref[...]


def _grouped_swiglu(xrows, block_e, row_gate, w1, w3, w2, bm):
    rows, d = xrows.shape
    dff = w1.shape[-1]
    grid_spec = pltpu.PrefetchScalarGridSpec(
        num_scalar_prefetch=1,
        grid=(rows // bm,),
        in_specs=[pl.BlockSpec((bm, d), lambda i, be: (i, 0)),
                  pl.BlockSpec((1, d, dff), lambda i, be: (be[i], 0, 0)),
                  pl.BlockSpec((1, d, dff), lambda i, be: (be[i], 0, 0)),
                  pl.BlockSpec((1, dff, d), lambda i, be: (be[i], 0, 0)),
                  pl.BlockSpec((bm, 1), lambda i, be: (i, 0))],
        out_specs=pl.BlockSpec((bm, d), lambda i, be: (i, 0)),
    )
    return pl.pallas_call(
        _swiglu_kernel,
        out_shape=jax.ShapeDtypeStruct((rows, d), jnp.float32),
        grid_spec=grid_spec,
        compiler_params=pltpu.CompilerParams(
            dimension_semantics=("arbitrary",), vmem_limit_bytes=VMEM_LIMIT),
        name="swiglu",
    )(block_e, xrows, w1, w3, w2, row_gate)


def _cast_kernel(x_ref, o_ref):
    o_ref[0] = x_ref[0, 0].astype(o_ref.dtype)


def _layer_weights_bf16(w, layer):
    _, ne, r, c = w.shape
    rb = r
    return pl.pallas_call(
        _cast_kernel,
        out_shape=jax.ShapeDtypeStruct((ne, r, c), jnp.bfloat16),
        grid=(ne, r // rb),
        in_specs=[pl.BlockSpec((1, 1, rb, c), lambda e, j: (layer, e, j, 0))],
        out_specs=pl.BlockSpec((1, rb, c), lambda e, j: (e, j, 0)),
        compiler_params=pltpu.CompilerParams(
            dimension_semantics=("parallel", "parallel"), vmem_limit_bytes=VMEM_LIMIT),
        name="cast_bf16",
    )(w).reshape(ne, r, c)


def _swiglu(x2d, w1, w3, w2, layer, bm=512):
    rows = x2d.shape[0]
    bf = lambda t: _layer_weights_bf16(t[:, None], layer)
    return _grouped_swiglu(x2d, jnp.zeros((rows // bm,), jnp.int32), jnp.ones((rows, 1), jnp.float32),
                           bf(w1), bf(w3), bf(w2), bm)


def _moe_swiglu(xf, logits, w1, w3, w2, layer):
    T, d = xf.shape
    top_logit, top_e = lax.top_k(logits, TOP_K)
    gate = jax.nn.softmax(top_logit, axis=-1)
    flat_e = top_e.reshape(-1).astype(jnp.int32)
    flat_tok = jnp.repeat(jnp.arange(T, dtype=jnp.int32), TOP_K)
    onehot = (flat_e[:, None] == jnp.arange(N_EXPERTS, dtype=jnp.int32)[None, :]).astype(jnp.int32)
    csum = jnp.cumsum(onehot, axis=0)
    counts = csum[-1]
    rank = jnp.sum(onehot * (csum - 1), axis=1)
    padded = (counts + MOE_BLOCK - 1) // MOE_BLOCK * MOE_BLOCK
    pad_end = jnp.cumsum(padded)
    pad_start = pad_end - padded
    start = jnp.cumsum(counts) - counts
    pos = (pad_start[flat_e] + rank).reshape(T, TOP_K)
    _, st, sg = lax.sort((flat_e, flat_tok, gate.reshape(-1)), num_keys=1, is_stable=True)
    n_blocks = -(-(T * TOP_K) // MOE_BLOCK) + N_EXPERTS
    rows = n_blocks * MOE_BLOCK
    block_e = jnp.minimum(jnp.searchsorted(pad_end, jnp.arange(n_blocks) * MOE_BLOCK, side='right'),
                          N_EXPERTS - 1).astype(jnp.int32)
    row_e = jnp.repeat(block_e, MOE_BLOCK)
    q = jnp.arange(rows, dtype=jnp.int32) - pad_start[row_e]
    valid = q < counts[row_e]
    src = jnp.clip(start[row_e] + q, 0, T * TOP_K - 1)
    slot_tok = jnp.where(valid, st[src], 0)
    slot_gate = jnp.where(valid, sg[src], 0.0)
    bf = lambda t: _layer_weights_bf16(t, layer)
    yb = _grouped_swiglu(xf[slot_tok], block_e, slot_gate[:, None], bf(w1), bf(w3), bf(w2), MOE_BLOCK)
    return [yb[pos[:, 0]], yb[pos[:, 1]]]


def kernel(x, p, w_in, m_i_bias, m_f_bias, m_norm_w, g_conv_w, g_conv_b, g_w_a, g_b_a, g_w_x, g_b_x,
           g_lambda, r_mu, r_w0, r_w2, r_a0, r_a2, r_g2, r_k_k, r_k_a, r_r_k, r_ln_w, r_ln_b,
           s_conv_w, s_conv_b, s_dt_bias, s_a_log, s_d, s_norm_w, w_out, ln1_g, ln1_b, ln2_g, ln2_b,
           f_w1, f_w3, f_w2, e_router, e_w1, e_w3, e_w2, pe_proj, pe_gate_w, pe_gate_b):
    bsz, seq, d = x.shape
    T = bsz * seq
    x = x.reshape(T, d)
    for i in range(DEPTH):
        pm, pg, pr, ps, psm = _inproj(x, _split_w_in(w_in[i]))
        mixers = [
            _mlstm_group(pm, psm, bsz, m_i_bias[i], m_f_bias[i], m_norm_w[i]),
            _rglru_group(pg, bsz, g_conv_w[i], g_conv_b[i], g_w_a[i], g_b_a[i], g_w_x[i], g_b_x[i], g_lambda[i]),
            _rwkv7_group(pr, bsz, r_mu[i], r_w0[i], r_w2[i], r_a0[i], r_a2[i], r_g2[i],
                         r_k_k[i], r_k_a[i], r_r_k[i], r_ln_w[i], r_ln_b[i]),
            _ssd_group(ps, psm, bsz, s_conv_w[i], s_conv_b[i], s_dt_bias[i], s_a_log[i], s_d[i], s_norm_w[i]),
        ]
        if i % 2 == 0:
            (x,) = _outproj_ln(x, mixers, w_out[i], ln1_g[i], ln1_b[i])
            ff = [_swiglu(x, f_w1, f_w3, f_w2, i // 2)]
        else:
            x, logits = _outproj_ln(x, mixers, w_out[i], ln1_g[i], ln1_b[i], e_router[i // 2])
            ff = _moe_swiglu(x, logits[:, :N_EXPERTS], e_w1, e_w3, e_w2, i // 2)
        x = _pe_ln(x, ff, p[i].reshape(T, PE_DIM), pe_proj[i], pe_gate_w[i], pe_gate_b[i], ln2_g[i], ln2_b[i])
    return x.reshape(bsz, seq, d)
```

```python
import functools

import jax
import jax.numpy as jnp
from jax import lax
from jax.experimental import pallas as pl
from jax.experimental.pallas import tpu as pltpu

D_MODEL = 1024
DEPTH = 4
PE_DIM = 256
GROUP_W = 256
HEAD_DIM = 64
GROUP_HEADS = 4
CONV_K = 4
MLSTM_CHUNK = 64
RG_C = 8.0
RWKV_DECAY_RANK = 32
RWKV_ICL_RANK = 32
RWKV_GATE_RANK = 64
RWKV_GN_EPS = 64e-5
SSD_STATE = 64
SSD_GROUPS = 2
SSD_CHUNK = 128
SSD_XBC = GROUP_W + 2 * SSD_GROUPS * SSD_STATE
M_COLS = 4 * GROUP_W + 2 * GROUP_HEADS
G_COLS = 2 * GROUP_W
R_COLS = 3 * GROUP_W + RWKV_DECAY_RANK + RWKV_ICL_RANK + RWKV_GATE_RANK
S_COLS = GROUP_W + SSD_XBC + GROUP_HEADS
N_IN = M_COLS + G_COLS + R_COLS + S_COLS
N_EXPERTS = 8
TOP_K = 2
MOE_BLOCK = 512
ALPHA = (2 * DEPTH) ** 0.25
LN_EPS = 1e-5
RMS_EPS = 1e-6

LANES = 128
SUBLANES = 8
VMEM_LIMIT = 56 * 1024 * 1024


RWKV_TB = 32


def _rwkv_scan_kernel(pa_ref, pb_ref, pc_ref, y_ref, s_ref, sa_ref, ta_ref, tb_ref, tc_ref, yb_ref,
                      *, tb, nb, nk, nacc):
    @pl.when(pl.program_id(0) == 0)
    def _():
        s_ref[...] = jnp.zeros_like(s_ref)
        sa_ref[...] = jnp.zeros_like(sa_ref)

    nh = GROUP_HEADS
    half = nk // 2
    lane = lax.broadcasted_iota(jnp.int32, (half, LANES), 1)
    rows = lambda t, h: pl.ds(h * nb * tb + t, nb, stride=tb)

    def prepare(t, carry):
        for src, dst in ((pa_ref, ta_ref), (pb_ref, tb_ref), (pc_ref, tc_ref)):
            parts = [src[rows(t, h), :] for h in range(nh)]
            dst[t] = jnp.concatenate(parts + parts, axis=0).T
        return carry

    lax.fori_loop(0, tb, prepare, 0, unroll=8)

    def step(t, sa):
        bc = lambda ref, r: jnp.broadcast_to(ref[t, pl.ds(r, 1), :], (half, LANES))
        v_t = jnp.where(lane < nk, tc_ref[t, nk:nk + half, :], tc_ref[t, nk + half:2 * nk, :])
        y_acc = [None] * nacc
        sa_acc = [None] * nacc
        for k in range(nk):
            s_k = s_ref[k] * bc(ta_ref, k) - sa * bc(ta_ref, nk + k) + v_t * bc(tb_ref, k)
            s_ref[k] = s_k
            y_k = s_k * bc(tb_ref, nk + k)
            n_k = s_k * bc(tc_ref, k)
            a = k % nacc
            y_acc[a] = y_k if y_acc[a] is None else y_acc[a] + y_k
            sa_acc[a] = n_k if sa_acc[a] is None else sa_acc[a] + n_k
        yb_ref[t] = functools.reduce(lambda p, q: p + q, y_acc)
        return functools.reduce(lambda p, q: p + q, sa_acc)

    sa_ref[...] = lax.fori_loop(0, tb, step, sa_ref[...])

    def emit(t, carry):
        y = yb_ref[t]
        y_sw = pltpu.roll(y, nk, 1)
        y_nat = jnp.concatenate([y, y_sw, y, y_sw], axis=0).T
        for h in range(nh):
            y_ref[rows(t, h), :] = y_nat[h * nb:(h + 1) * nb, :]
        return carry

    lax.fori_loop(0, tb, emit, 0, unroll=8)


def _rwkv_scan(pa, pb, pc, bsz):
    nblk, nh, _, tb, _ = pa.shape
    n = HEAD_DIM
    assert 2 * bsz * nh == LANES and 2 * n == LANES
    flat = lambda t: t.reshape(nblk * nh * bsz * tb, LANES)
    spec = pl.BlockSpec((nh * bsz * tb, LANES), lambda i: (i, 0))
    y = pl.pallas_call(
        functools.partial(_rwkv_scan_kernel, tb=tb, nb=bsz, nk=n, nacc=2),
        out_shape=jax.ShapeDtypeStruct((nblk * nh * bsz * tb, LANES), jnp.float32),
        grid=(nblk,),
        in_specs=[spec, spec, spec],
        out_specs=spec,
        scratch_shapes=[pltpu.VMEM((n, n // 2, LANES), jnp.float32), pltpu.VMEM((n // 2, LANES), jnp.float32)]
        + [pltpu.VMEM((tb, LANES, LANES), jnp.float32)] * 3 + [pltpu.VMEM((tb, n // 2, LANES), jnp.float32)],
        compiler_params=pltpu.CompilerParams(
            dimension_semantics=("arbitrary",), vmem_limit_bytes=VMEM_LIMIT),
        name="rwkv_scan",
    )(flat(pa), flat(pb), flat(pc))
    return y.reshape(pa.shape)


def _inproj_kernel(x_ref, *refs):
    n = len(refs) // 2
    xb = x_ref[...].astype(jnp.bfloat16)
    for w_ref, o_ref in zip(refs[:n], refs[n:]):
        o_ref[...] = jnp.dot(xb, w_ref[...], preferred_element_type=jnp.float32)


def _inproj(x, weights, tm=512):
    t, d = x.shape
    return pl.pallas_call(
        _inproj_kernel,
        out_shape=[jax.ShapeDtypeStruct((t, w.shape[1]), jnp.float32) for w in weights],
        grid=(t // tm,),
        in_specs=[pl.BlockSpec((tm, d), lambda i: (i, 0))]
        + [pl.BlockSpec(w.shape, lambda i: (0, 0)) for w in weights],
        out_specs=[pl.BlockSpec((tm, w.shape[1]), lambda i: (i, 0)) for w in weights],
        compiler_params=pltpu.CompilerParams(
            dimension_semantics=("parallel",), vmem_limit_bytes=VMEM_LIMIT),
        name="inproj",
    )(x, *weights)


def _split_w_in_kernel(w_ref, m_ref, g_ref, r_ref, s_ref, sm_ref):
    m0, g0, r0, s0 = 0, M_COLS, M_COLS + G_COLS, M_COLS + G_COLS + R_COLS
    bf = jnp.bfloat16
    m_ref[...] = w_ref[0, :, m0:m0 + 4 * GROUP_W].astype(bf)
    g_ref[...] = w_ref[0, :, g0:g0 + G_COLS].astype(bf)
    r_ref[...] = w_ref[0, :, r0:r0 + R_COLS].astype(bf)
    s_ref[...] = w_ref[0, :, s0:s0 + GROUP_W + SSD_XBC].astype(bf)
    gates = jnp.concatenate([w_ref[0, :, m0 + 4 * GROUP_W:g0], w_ref[0, :, s0 + GROUP_W + SSD_XBC:N_IN]], axis=1)
    pad = jnp.zeros((gates.shape[0], LANES - gates.shape[1]), jnp.float32)
    sm_ref[...] = jnp.concatenate([gates, pad], axis=1).astype(bf)


def _split_w_in(w_in, layer):
    _, d, n = w_in.shape
    widths = (4 * GROUP_W, G_COLS, R_COLS, GROUP_W + SSD_XBC, LANES)
    return pl.pallas_call(
        _split_w_in_kernel,
        out_shape=[jax.ShapeDtypeStruct((d, c), jnp.bfloat16) for c in widths],
        grid=(1,),
        in_specs=[pl.BlockSpec((1, d, n), lambda i: (layer, 0, 0))],
        out_specs=[pl.BlockSpec((d, c), lambda i: (0, 0)) for c in widths],
        compiler_params=pltpu.CompilerParams(vmem_limit_bytes=VMEM_LIMIT),
        name="split_w_in",
    )(w_in)


def _shift_rows(x, s, fill, row):
    return jnp.where(row >= s, pltpu.roll(x, s, 0), fill)


def _rglru_kernel(g_ref, cw_ref, cb_ref, wg_ref, bg_ref, c_ref, o_ref, xpad_ref, a_ref, u_ref, *, rc, unroll):
    seq, w = o_ref.shape
    pad = SUBLANES
    xpad_ref[0:pad, :] = jnp.zeros((pad, w), jnp.float32)
    xpad_ref[pad:, :] = g_ref[:, 0:w]
    cw = cw_ref[...]
    for c in range(seq // rc):
        r0 = c * rc
        xc = cb_ref[...] + cw[CONV_K - 1:CONV_K, :] * g_ref[r0:r0 + rc, 0:w]
        for j in range(1, CONV_K):
            xc = xc + cw[CONV_K - 1 - j:CONV_K - j, :] * xpad_ref[pad + r0 - j:pad + r0 - j + rc, :]
        z = jnp.dot(xc.astype(jnp.bfloat16), wg_ref[...], preferred_element_type=jnp.float32) + bg_ref[...]
        z = jax.nn.sigmoid(z)
        a = jnp.exp(c_ref[...] * z[:, 0:w])
        a_ref[r0:r0 + rc, :] = a
        u_ref[r0:r0 + rc, :] = jnp.sqrt(1.0 - a * a) * (z[:, w:] * xc)

    row = lax.broadcasted_iota(jnp.int32, (SUBLANES, w), 0)

    def tiles(i, h):
        for j in range(unroll):
            r = pl.multiple_of((i * unroll + j) * SUBLANES, SUBLANES)
            a = a_ref[pl.ds(r, SUBLANES), :]
            u = u_ref[pl.ds(r, SUBLANES), :]
            for s in (1, 2, 4):
                u = a * _shift_rows(u, s, 0.0, row) + u
                a = a * _shift_rows(a, s, 1.0, row)
            ht = a * h + u
            o_ref[pl.ds(r, SUBLANES), :] = ht * jax.nn.gelu(g_ref[pl.ds(r, SUBLANES), w:2 * w])
            h = jnp.broadcast_to(ht[SUBLANES - 1:SUBLANES, :], (SUBLANES, w))
        return h

    lax.fori_loop(0, seq // (SUBLANES * unroll), tiles, jnp.zeros((SUBLANES, w), jnp.float32))


def _block_diag(w):
    h, p, _ = w.shape
    eye = jnp.eye(h, dtype=w.dtype)
    return jnp.einsum('hij,hg->higj', w, eye).reshape(h * p, h * p)


def _rglru_group(g_cols, bsz, conv_w, conv_b, w_a, b_a, w_x, b_x, lam):
    t = g_cols.shape[0]
    seq = t // bsz
    w = GROUP_W
    wg = jnp.concatenate([_block_diag(w_a), _block_diag(w_x)], axis=1).astype(jnp.bfloat16)
    bg = jnp.concatenate([b_a, b_x])[None, :]
    cdec = (-RG_C * jax.nn.softplus(-lam))[None, :]
    const = lambda shape: pl.BlockSpec(shape, lambda b: (0, 0))
    return pl.pallas_call(
        functools.partial(_rglru_kernel, rc=256, unroll=8),
        out_shape=jax.ShapeDtypeStruct((t, w), jnp.float32),
        grid=(bsz,),
        in_specs=[pl.BlockSpec((seq, 2 * w), lambda b: (b, 0)), const((CONV_K, w)), const((1, w)),
                  const((w, 2 * w)), const((1, 2 * w)), const((1, w))],
        out_specs=pl.BlockSpec((seq, w), lambda b: (b, 0)),
        scratch_shapes=[pltpu.VMEM((seq + SUBLANES, w), jnp.float32),
                        pltpu.VMEM((seq, w), jnp.float32), pltpu.VMEM((seq, w), jnp.float32)],
        compiler_params=pltpu.CompilerParams(
            dimension_semantics=("parallel",), vmem_limit_bytes=VMEM_LIMIT),
        name="rglru",
    )(g_cols, conv_w, conv_b[None, :], wg, bg, cdec)


MIXER_ROWS_PER_STEP = 2
_NT = (((1,), (1,)), ((), ()))
_TN = (((0,), (0,)), ((), ()))


def _bdot(a, b, dims=(((1,), (0,)), ((), ()))):
    return lax.dot_general(a.astype(jnp.bfloat16), b.astype(jnp.bfloat16), dims,
                           preferred_element_type=jnp.float32)


def _cumsum_rows(x):
    n = x.shape[0]
    tri = (lax.broadcasted_iota(jnp.int32, (n, n), 0) >= lax.broadcasted_iota(jnp.int32, (n, n), 1))
    return jnp.dot(tri.astype(jnp.float32), x, precision=lax.Precision.HIGHEST,
                   preferred_element_type=jnp.float32)


def _log_sigmoid(x):
    return jnp.minimum(x, 0.0) - jnp.log1p(jnp.exp(-jnp.abs(x)))


def _softplus(x):
    return jnp.maximum(x, 0.0) + jnp.log1p(jnp.exp(-jnp.abs(x)))


def _mlstm_kernel(pm_ref, sm_ref, bias_ref, nw_ref, o_ref, c_ref, n_ref, m_ref):
    _, seq, w = o_ref.shape
    nh, dh, L = GROUP_HEADS, HEAD_DIM, MLSTM_CHUNK
    c_ref[...] = jnp.zeros_like(c_ref)
    n_ref[...] = jnp.zeros_like(n_ref)
    m_ref[...] = jnp.zeros_like(m_ref)
    f32 = jnp.float32
    hi = lax.Precision.HIGHEST
    col = lax.broadcasted_iota(jnp.int32, (L, LANES), 1)
    lrow = lax.broadcasted_iota(jnp.int32, (L, LANES), 0)
    scol = col & (dh - 1)
    lo = col < dh
    lo1 = lo[0:1]
    sq = lax.broadcasted_iota(jnp.int32, (LANES, LANES), 0)
    sr = lax.broadcasted_iota(jnp.int32, (LANES, LANES), 1)
    bd = (sq < dh) == (sr < dh)
    e_row = lax.broadcasted_iota(jnp.int32, (LANES, 2 * w), 0)
    e_col = lax.broadcasted_iota(jnp.int32, (LANES, 2 * w), 1)
    e_head = jnp.where(e_col < w, 0, nh) + ((e_col & (w - 1)) >> (dh.bit_length() - 1))
    spread = (e_row == e_head).astype(f32)
    tri = (lax.broadcasted_iota(jnp.int32, (L, L), 0) >= lax.broadcasted_iota(jnp.int32, (L, L), 1)).astype(f32)

    def seg_max(x, first):
        a = jnp.max(jnp.where(first, x, -jnp.inf), axis=1, keepdims=True)
        b = jnp.max(jnp.where(first, -jnp.inf, x), axis=1, keepdims=True)
        return jnp.where(first, a, b)

    def seg_sum(x):
        a = jnp.sum(jnp.where(lo, x, 0.0), axis=1, keepdims=True)
        b = jnp.sum(jnp.where(lo, 0.0, x), axis=1, keepdims=True)
        return jnp.where(lo, a, b)

    nb = pm_ref.shape[0]
    items = [(b, p) for b in range(nb) for p in range(nh // 2)]
    lanes = lambda p, sec: slice(sec * w + p * LANES, sec * w + (p + 1) * LANES)

    def chunk(c, carry):
        r0 = pl.multiple_of(c * L, L)
        rows = pl.ds(r0, L)
        each = lambda f, *xs: [f(*args) for args in zip(*xs)]
        gts = [sm_ref[b, rows, :] + bias_ref[...] for b in range(nb)]
        gts = [jnp.where(col < nh, t, jnp.where(col < 2 * nh, _log_sigmoid(t), 0.0)) for t in gts]
        rep = [jnp.dot(t, spread, precision=hi, preferred_element_type=f32) for t in gts]
        b_rep = [jnp.dot(tri, t[:, w:], precision=hi, preferred_element_type=f32) for t in rep]
        bc = [b_rep[b][:, lanes(p, 0)] for b, p in items]
        lic = [rep[b][:, lanes(p, 0)] for b, p in items]
        lfc = [rep[b][:, lanes(p, 1)] for b, p in items]
        b_row = [jnp.sum(jnp.where(lrow <= scol, t, 0.0), axis=0, keepdims=True) for t in lfc]
        li_row = [jnp.sum(jnp.where(lrow == scol, t, 0.0), axis=0, keepdims=True) for t in lic]
        g = [t[L - 1:L, :] for t in bc]
        q = [pm_ref[b, rows, lanes(p, 0)] for b, p in items]
        k = [pm_ref[b, rows, lanes(p, 1)] * (dh ** -0.5) for b, p in items]
        v = [pm_ref[b, rows, lanes(p, 2)] for b, p in items]
        cst = [c_ref[b, p] for b, p in items]
        nst = [n_ref[b, p] for b, p in items]
        m = [m_ref[b, p] for b, p in items]
        log_d = each(lambda bc_, br, lr: jnp.where(lrow >= scol, bc_ - br + lr, -jnp.inf), bc, b_row, li_row)
        m_t = each(lambda bc_, m_, ld: jnp.maximum(bc_ + m_, seg_max(ld, lo)), bc, m, log_d)
        k2 = [jnp.concatenate([t, t], axis=0) for t in k]
        qk = each(lambda q_, k2_: jnp.where(lo, _bdot(jnp.where(lo, q_, 0.0), k2_, _NT),
                                            _bdot(jnp.where(lo, 0.0, q_), k2_, _NT)), q, k2)
        s = each(lambda qk_, ld, mt: qk_ * jnp.exp(ld - mt), qk, log_d, m_t)
        inter_w = each(lambda bc_, m_, mt: jnp.exp(bc_ + m_ - mt), bc, m, m_t)
        v_bd = [jnp.where(bd, jnp.concatenate([t, t], axis=0), 0.0) for t in v]
        num = each(lambda s_, vb, iw, q_, c_: _bdot(s_, vb) + iw * _bdot(q_, c_), s, v_bd, inter_w, q, cst)
        den = each(lambda s_, iw, q_, n_: seg_sum(s_) + iw * seg_sum(q_ * n_), s, inter_w, q, nst)
        hh = each(lambda nu, de, mt: nu / jnp.maximum(jnp.abs(de), jnp.exp(-mt)), num, den, m_t)
        ms = [seg_sum(t * t) for t in hh]
        for (b, p), h_, ms_ in zip(items, hh, ms):
            o = pm_ref[b, rows, lanes(p, 3)]
            o_ref[b, rows, lanes(p, 0)] = (h_ * lax.rsqrt(ms_ * (1.0 / dh) + RMS_EPS) * nw_ref[:, lanes(p, 0)]
                                           * jax.nn.sigmoid(o))
        m_new = each(lambda g_, m_, br, lr: jnp.maximum(g_ + m_, seg_max(g_ - br + lr, lo1)), g, m, b_row, li_row)
        decay = each(lambda g_, m_, mn: jnp.exp(g_ + m_ - mn), g, m, m_new)
        w_col = each(lambda g_, bc_, li_, mn: jnp.exp(g_ - bc_ + li_ - mn), g, bc, lic, m_new)
        upd = each(lambda k_, wc, v_: jnp.where(bd, _bdot(k_, wc * v_, _TN), 0.0), k, w_col, v)
        for i, (b, p) in enumerate(items):
            c_ref[b, p] = decay[i] * cst[i] + upd[i]
            n_ref[b, p] = decay[i] * nst[i] + jnp.sum(w_col[i] * k[i], axis=0, keepdims=True)
            m_ref[b, p] = m_new[i]
        return carry

    lax.fori_loop(0, seq // L, chunk, 0)


def _mlstm_group(pm, psm, bsz, i_bias, f_bias, norm_w):
    t = pm.shape[0]
    seq = t // bsz
    w, nh, dh = GROUP_W, GROUP_HEADS, HEAD_DIM
    bias = jnp.pad(jnp.concatenate([i_bias, f_bias]), (0, LANES - 2 * nh))[None, :]
    const = lambda shape: pl.BlockSpec(shape, lambda b: (0,) * len(shape))
    nb = MIXER_ROWS_PER_STEP
    assert bsz % nb == 0
    seq_block = lambda n: pl.BlockSpec((nb, seq, n), lambda b: (b, 0, 0))
    return pl.pallas_call(
        _mlstm_kernel,
        out_shape=jax.ShapeDtypeStruct((bsz, seq, w), jnp.float32),
        grid=(bsz // nb,),
        in_specs=[seq_block(4 * w), seq_block(LANES), const((1, LANES)), const((1, w))],
        out_specs=seq_block(w),
        scratch_shapes=[pltpu.VMEM((nb, nh // 2, LANES, LANES), jnp.float32),
                        pltpu.VMEM((nb, nh // 2, 1, LANES), jnp.float32),
                        pltpu.VMEM((nb, nh // 2, 1, LANES), jnp.float32)],
        compiler_params=pltpu.CompilerParams(
            dimension_semantics=("parallel",), vmem_limit_bytes=VMEM_LIMIT),
        name="mlstm",
    )(pm.reshape(bsz, seq, 4 * w), psm.reshape(bsz, seq, LANES), bias, norm_w[None, :]).reshape(t, w)


def _ssd_kernel(ps_ref, sm_ref, cw_ref, cb_ref, dtb_ref, aneg_ref, dsk_ref, nw_ref, o_ref,
                xbc_ref, st_ref, *, rc):
    nb, seq, w = o_ref.shape
    nh, hp, ng, ns, L = GROUP_HEADS, HEAD_DIM, SSD_GROUPS, SSD_STATE, SSD_CHUNK
    cx = SSD_XBC
    cw = cw_ref[...]
    row = lax.broadcasted_iota(jnp.int32, (rc, cx), 0)
    for b in range(nb):
        for c in range(seq // rc):
            r0 = c * rc
            cur = ps_ref[b, r0:r0 + rc, w:w + cx]
            acc = cb_ref[...] + cw[CONV_K - 1:CONV_K, :] * cur
            for j in range(1, CONV_K):
                if c == 0:
                    sh = jnp.where(row >= j, pltpu.roll(cur, j, 0), 0.0)
                else:
                    sh = ps_ref[b, r0 - j:r0 - j + rc, w:w + cx]
                acc = acc + cw[CONV_K - 1 - j:CONV_K - j, :] * sh
            xbc_ref[b, r0:r0 + rc, :] = jax.nn.silu(acc)
    st_ref[...] = jnp.zeros_like(st_ref)
    causal = lax.broadcasted_iota(jnp.int32, (L, L), 0) >= lax.broadcasted_iota(jnp.int32, (L, L), 1)
    gsz = w // ng
    rep = nh // ng
    groups = [(b, g) for b in range(nb) for g in range(ng)]
    heads = [(b, h) for b in range(nb) for h in range(nh)]

    def chunk(c, carry):
        r0 = pl.multiple_of(c * L, L)
        rows = pl.ds(r0, L)
        each = lambda f, *xs: [f(*args) for args in zip(*xs)]
        dt_all = [_softplus(sm_ref[b, rows, :] + dtb_ref[...]) for b in range(nb)]
        acs_all = [_cumsum_rows(aneg_ref[...] * t) for t in dt_all]
        acs_t = [t.T for t in acs_all]
        bm = {bg: xbc_ref[bg[0], rows, w + bg[1] * ns:w + (bg[1] + 1) * ns] for bg in groups}
        cm = {bg: xbc_ref[bg[0], rows, w + (ng + bg[1]) * ns:w + (ng + bg[1] + 1) * ns] for bg in groups}
        cb = {bg: _bdot(cm[bg], bm[bg], _NT) for bg in groups}
        grp = lambda b, h: (b, h // rep)
        cidx = lambda h: 2 * nh + h
        acs_col = [acs_all[b][:, cidx(h):cidx(h) + 1] for b, h in heads]
        acs_row = [acs_t[b][cidx(h):cidx(h) + 1, :] for b, h in heads]
        dt_col = [dt_all[b][:, cidx(h):cidx(h) + 1] for b, h in heads]
        a_last = [t[L - 1:L, :] for t in acs_col]
        xh = [xbc_ref[b, rows, h * hp:(h + 1) * hp] for b, h in heads]
        xdt = each(lambda x_, d_: x_ * d_, xh, dt_col)
        state = [st_ref[b, h] for b, h in heads]
        scores = [cb[grp(b, h)] * jnp.exp(jnp.where(causal, ac - ar, -jnp.inf))
                  for (b, h), ac, ar in zip(heads, acs_col, acs_row)]
        y_diag = each(_bdot, scores, xdt)
        y_off = [_bdot(cm[grp(b, h)], st, _NT) for (b, h), st in zip(heads, state)]
        ys = [yd + jnp.exp(ac) * yo + x_ * dsk_ref[:, h * hp:(h + 1) * hp]
              for (b, h), yd, ac, yo, x_ in zip(heads, y_diag, acs_col, y_off, xh)]
        upd = [_bdot(xd * jnp.exp(al - ac), bm[grp(b, h)], _TN)
               for (b, h), xd, al, ac in zip(heads, xdt, a_last, acs_col)]
        for i, (b, h) in enumerate(heads):
            st_ref[b, h] = jnp.exp(a_last[i]) * state[i] + upd[i]
        ysd = dict(zip(heads, ys))
        ygs = [jnp.concatenate([ysd[(b, g * rep + j)] for j in range(rep)], axis=1)
               * jax.nn.silu(ps_ref[b, rows, g * gsz:(g + 1) * gsz]) for b, g in groups]
        ms = [jnp.mean(t * t, axis=1, keepdims=True) for t in ygs]
        for (b, g), yg, m_ in zip(groups, ygs, ms):
            o_ref[b, rows, g * gsz:(g + 1) * gsz] = yg * lax.rsqrt(m_ + RMS_EPS) * nw_ref[:, g * gsz:(g + 1) * gsz]
        return carry

    lax.fori_loop(0, seq // L, chunk, 0)


def _ssd_group(ps, psm, bsz, conv_w, conv_b, dt_bias, a_log, d_skip, norm_w):
    t = ps.shape[0]
    seq = t // bsz
    w, nh = GROUP_W, GROUP_HEADS
    lane_vec = lambda v: jnp.pad(v, (2 * nh, LANES - 3 * nh))[None, :]
    dsk = jnp.repeat(d_skip, HEAD_DIM)[None, :]
    const = lambda shape: pl.BlockSpec(shape, lambda b: (0,) * len(shape))
    nb = MIXER_ROWS_PER_STEP
    rc = 256
    assert bsz % nb == 0 and seq % rc == 0
    seq_block = lambda n: pl.BlockSpec((nb, seq, n), lambda b: (b, 0, 0))
    return pl.pallas_call(
        functools.partial(_ssd_kernel, rc=rc),
        out_shape=jax.ShapeDtypeStruct((bsz, seq, w), jnp.float32),
        grid=(bsz // nb,),
        in_specs=[seq_block(w + SSD_XBC), seq_block(LANES),
                  const((CONV_K, SSD_XBC)), const((1, SSD_XBC)), const((1, LANES)), const((1, LANES)),
                  const((1, w)), const((1, w))],
        out_specs=seq_block(w),
        scratch_shapes=[pltpu.VMEM((nb, seq, SSD_XBC), jnp.float32),
                        pltpu.VMEM((nb, nh, HEAD_DIM, SSD_STATE), jnp.float32)],
        compiler_params=pltpu.CompilerParams(
            dimension_semantics=("parallel",), vmem_limit_bytes=VMEM_LIMIT),
        name="ssd",
    )(ps.reshape(bsz, seq, w + SSD_XBC), psm.reshape(bsz, seq, LANES), conv_w, conv_b[None, :], lane_vec(dt_bias),
      lane_vec(-jnp.exp(a_log)), dsk, norm_w[None, :]).reshape(t, w)


def _layer_norm(x, g, b):
    xc = x - jnp.mean(x, -1, keepdims=True)
    var = jnp.mean(xc * xc, -1, keepdims=True)
    return (xc * lax.rsqrt(var + LN_EPS)) * g + b


def _outproj_ln_kernel(x_ref, m0_ref, m1_ref, m2_ref, m3_ref, w_ref, g_ref, b_ref, *rest):
    gw = m0_ref.shape[1]
    acc = ALPHA * x_ref[...]
    for j, m_ref in enumerate((m0_ref, m1_ref, m2_ref, m3_ref)):
        acc = acc + _bdot(m_ref[...], w_ref[j * gw:(j + 1) * gw, :])
    x1 = _layer_norm(acc, g_ref[...], b_ref[...])
    if len(rest) == 1:
        rest[0][...] = x1
    else:
        wr_ref, o_ref, lg_ref = rest
        o_ref[...] = x1
        lg_ref[...] = _bdot(x1, wr_ref[...])


def _outproj_ln(x, mixers, w_out, g, b, w_router=None, tm=512):
    t, d = x.shape
    gw = mixers[0].shape[1]
    row = lambda n: pl.BlockSpec((tm, n), lambda i: (i, 0))
    const = lambda shape: pl.BlockSpec(shape, lambda i: (0, 0))
    in_specs = [row(d)] + [row(gw)] * 4 + [const((d, d)), const((1, d)), const((1, d))]
    args = [x, *mixers, w_out.astype(jnp.bfloat16), g[None, :], b[None, :]]
    out_shape = [jax.ShapeDtypeStruct((t, d), jnp.float32)]
    out_specs = [row(d)]
    if w_router is not None:
        wr = jnp.pad(w_router, ((0, 0), (0, LANES - w_router.shape[1]))).astype(jnp.bfloat16)
        in_specs.append(const((d, LANES)))
        args.append(wr)
        out_shape.append(jax.ShapeDtypeStruct((t, LANES), jnp.float32))
        out_specs.append(row(LANES))
    return pl.pallas_call(
        _outproj_ln_kernel,
        out_shape=out_shape, grid=(t // tm,), in_specs=in_specs, out_specs=out_specs,
        compiler_params=pltpu.CompilerParams(
            dimension_semantics=("parallel",), vmem_limit_bytes=VMEM_LIMIT),
        name="outproj_ln",
    )(*args)


def _pe_ln_kernel(x_ref, p_ref, wp_ref, wg_ref, bg_ref, g_ref, b_ref, *rest):
    *ff_refs, o_ref = rest
    x1 = x_ref[...]
    pe = _bdot(p_ref[0], wp_ref[...]) * jax.nn.sigmoid(_bdot(x1, wg_ref[...]) + bg_ref[...])
    ff = functools.reduce(lambda a, c: a + c, [r[...] for r in ff_refs])
    o_ref[...] = _layer_norm(ALPHA * x1 + ff + pe, g_ref[...], b_ref[...])


def _pe_ln(x1, ff_parts, p, layer, pe_proj, gate_w, gate_b, g, b, tm=512):
    t, d = x1.shape
    pd = p.shape[2]
    row = lambda n: pl.BlockSpec((tm, n), lambda i: (i, 0))
    const = lambda shape: pl.BlockSpec(shape, lambda i: (0, 0))
    return pl.pallas_call(
        _pe_ln_kernel,
        out_shape=jax.ShapeDtypeStruct((t, d), jnp.float32),
        grid=(t // tm,),
        in_specs=[row(d), pl.BlockSpec((1, tm, pd), lambda i: (layer, i, 0)), const((pd, d)), const((d, d)),
                  const((1, d)), const((1, d)), const((1, d))]
        + [row(d)] * len(ff_parts),
        out_specs=row(d),
        compiler_params=pltpu.CompilerParams(
            dimension_semantics=("parallel",), vmem_limit_bytes=VMEM_LIMIT),
        name="pe_ln",
    )(x1, p, pe_proj.astype(jnp.bfloat16), gate_w.astype(jnp.bfloat16), gate_b[None, :], g[None, :], b[None, :],
      *ff_parts)


def _head_sums(x):
    rows = x.shape[0]
    parts = []
    for h in range(GROUP_HEADS):
        s = jnp.sum(x[:, h * HEAD_DIM:(h + 1) * HEAD_DIM], axis=1, keepdims=True)
        parts.append(jnp.broadcast_to(s, (rows, HEAD_DIM)))
    return jnp.concatenate(parts, axis=1)


def _rwkv_pre_kernel(c_ref, mu_ref, w0_ref, w2_ref, a0_ref, a2_ref, g2_ref, kk_ref, ka_ref,
                     pa_ref, pb_ref, pc_ref, g_ref, *, rc):
    seq = c_ref.shape[0]
    w, p = GROUP_W, HEAD_DIM
    o1 = 3 * w + RWKV_DECAY_RANK
    o2 = o1 + RWKV_ICL_RANK
    nchunks = seq // rc
    blk = rc // RWKV_TB
    row = lax.broadcasted_iota(jnp.int32, (rc, c_ref.shape[1]), 0)
    lane = lax.broadcasted_iota(jnp.int32, (rc, LANES), 1)

    def unit_keys(cols):
        kk = cols[:, w:2 * w] * kk_ref[...]
        return kk * lax.rsqrt(jnp.maximum(_head_sums(kk * kk), 1e-24))

    def pack(dst, c, x, z):
        for hp in range(w // LANES):
            x2, z2 = x[:, hp * LANES:(hp + 1) * LANES], z[:, hp * LANES:(hp + 1) * LANES]
            even = jnp.where(lane < p, x2, pltpu.roll(z2, p, 1))
            odd = jnp.where(lane < p, pltpu.roll(x2, p, 1), z2)
            for h, val in ((2 * hp, even), (2 * hp + 1, odd)):
                dst[c * blk:(c + 1) * blk, h, 0] = val.reshape(blk, RWKV_TB, LANES)

    for c in range(nchunks):
        r0 = c * rc
        cur = c_ref[r0:r0 + rc, :]
        if c == 0:
            prev = jnp.where(row == 0, 0.0, pltpu.roll(cur, 1, 0))
        else:
            prev = c_ref[r0 - 1:r0 - 1 + rc, :]
        if c == nchunks - 1:
            nxt = jnp.where(row == rc - 1, 0.0, pltpu.roll(cur, rc - 1, 0))
        else:
            nxt = c_ref[r0 + 1:r0 + 1 + rc, :]
        cols = cur + (prev - cur) * mu_ref[...]
        cols_next = nxt + (cur - nxt) * mu_ref[...]
        k = cols[:, w:2 * w]
        w_log = -_softplus(-(w0_ref[...] + _bdot(jnp.tanh(cols[:, 3 * w:o1]), w2_ref[...]))) - 0.5
        a = jax.nn.sigmoid(a0_ref[...] + _bdot(cols[:, o1:o2], a2_ref[...]))
        kk = unit_keys(cols)
        pack(pa_ref, c, jnp.exp(-jnp.exp(w_log)), kk * a)
        pack(pb_ref, c, k * (1.0 + (a - 1.0) * ka_ref[...]), cols[:, 0:w])
        pack(pc_ref, c, unit_keys(cols_next), cols[:, 2 * w:3 * w])
        g_ref[r0:r0 + rc, :] = _bdot(jax.nn.sigmoid(cols[:, o2:]), g2_ref[...])


def _rwkv_post_kernel(y_ref, pb_ref, pc_ref, g_ref, rk_ref, lw_ref, lb_ref, o_ref):
    p = HEAD_DIM
    rows = o_ref.shape[0]
    inv = 1.0 / p
    outs = []
    for h in range(GROUP_HEADS):
        hs = slice(h * p, (h + 1) * p)
        y = y_ref[:, h, 0].reshape(rows, LANES)[:, 0:p]
        kr = pb_ref[:, h, 0].reshape(rows, LANES)
        v = pc_ref[:, h, 0].reshape(rows, LANES)[:, p:2 * p]
        yc = y - jnp.sum(y, axis=1, keepdims=True) * inv
        yn = yc * lax.rsqrt(jnp.sum(yc * yc, axis=1, keepdims=True) * inv + RWKV_GN_EPS) * lw_ref[:, hs] + lb_ref[:, hs]
        rk = (kr * pltpu.roll(kr, p, 1))[:, 0:p] * rk_ref[:, hs]
        outs.append((yn + jnp.sum(rk, axis=1, keepdims=True) * v) * g_ref[:, hs])
    o_ref[...] = jnp.concatenate(outs, axis=1)


def _rwkv7_group(pr, bsz, mu, w0, w2, a0, a2, g2, k_k, k_a, r_k, ln_w, ln_b, tm=512):
    t = pr.shape[0]
    seq = t // bsz
    w = GROUP_W
    bf = lambda x: x.astype(jnp.bfloat16)
    const = lambda shape: pl.BlockSpec(shape, lambda b: (0, 0))
    vec = lambda x: x.reshape(1, -1)
    nh, tb = GROUP_HEADS, RWKV_TB
    nblk = seq // tb
    nat = jax.ShapeDtypeStruct((t, w), jnp.float32)
    packed = jax.ShapeDtypeStruct((nblk, nh, bsz, tb, LANES), jnp.float32)
    pspec = pl.BlockSpec((nblk, nh, 1, tb, LANES), lambda b: (0, 0, b, 0, 0))
    pa, pb, pc, g = pl.pallas_call(
        functools.partial(_rwkv_pre_kernel, rc=256),
        out_shape=[packed] * 3 + [nat],
        grid=(bsz,),
        in_specs=[pl.BlockSpec((seq, R_COLS), lambda b: (b, 0)), const((1, R_COLS)), const((1, w)),
                  const((RWKV_DECAY_RANK, w)), const((1, w)), const((RWKV_ICL_RANK, w)),
                  const((RWKV_GATE_RANK, w)), const((1, w)), const((1, w))],
        out_specs=[pspec] * 3 + [pl.BlockSpec((seq, w), lambda b: (b, 0))],
        compiler_params=pltpu.CompilerParams(
            dimension_semantics=("parallel",), vmem_limit_bytes=VMEM_LIMIT),
        name="rwkv_pre",
    )(pr, vec(mu), vec(w0), bf(w2), vec(a0), bf(a2), bf(g2), vec(k_k), vec(k_a))
    y = _rwkv_scan(pa, pb, pc, bsz)
    gb = tm // tb
    per_b = seq // tm
    bspec = pl.BlockSpec((gb, nh, 1, tb, LANES), lambda b, i: (i, 0, b, 0, 0))
    row = pl.BlockSpec((tm, w), lambda b, i: (b * per_b + i, 0))
    const2 = lambda shape: pl.BlockSpec(shape, lambda b, i: (0, 0))
    return pl.pallas_call(
        _rwkv_post_kernel,
        out_shape=nat,
        grid=(bsz, per_b),
        in_specs=[bspec, bspec, bspec, row] + [const2((1, w))] * 3,
        out_specs=row,
        compiler_params=pltpu.CompilerParams(
            dimension_semantics=("parallel", "parallel"), vmem_limit_bytes=VMEM_LIMIT),
        name="rwkv_post",
    )(y, pb, pc, g, vec(r_k), vec(ln_w), vec(ln_b))


FF_CHUNK = 512


def _swiglu_kernel(be_ref, x_ref, w1_ref, w3_ref, w2_ref, g_ref, o_ref):
    del be_ref
    xb = x_ref[...].astype(jnp.bfloat16)
    dff = w1_ref.shape[-1]
    acc = None
    for c0 in range(0, dff, FF_CHUNK):
        c1 = min(c0 + FF_CHUNK, dff)
        h1 = jnp.dot(xb, w1_ref[0, :, c0:c1], preferred_element_type=jnp.float32)
        h3 = jnp.dot(xb, w3_ref[0, :, c0:c1], preferred_element_type=jnp.float32)
        h = (jax.nn.silu(h1) * h3).astype(jnp.bfloat16)
        part = jnp.dot(h, w2_ref[0, c0:c1, :], preferred_element_type=jnp.float32)
        acc = part if acc is None else acc + part
    o_ref[...] = acc * g_ref[...]


def _grouped_swiglu(xrows, block_e, row_gate, w1, w3, w2, bm):
    rows, d = xrows.shape
    dff = w1.shape[-1]
    grid_spec = pltpu.PrefetchScalarGridSpec(
        num_scalar_prefetch=1,
        grid=(rows // bm,),
        in_specs=[pl.BlockSpec((bm, d), lambda i, be: (i, 0)),
                  pl.BlockSpec((1, d, dff), lambda i, be: (be[i], 0, 0)),
                  pl.BlockSpec((1, d, dff), lambda i, be: (be[i], 0, 0)),
                  pl.BlockSpec((1, dff, d), lambda i, be: (be[i], 0, 0)),
                  pl.BlockSpec((bm, 1), lambda i, be: (i, 0))],
        out_specs=pl.BlockSpec((bm, d), lambda i, be: (i, 0)),
    )
    return pl.pallas_call(
        _swiglu_kernel,
        out_shape=jax.ShapeDtypeStruct((rows, d), jnp.float32),
        grid_spec=grid_spec,
        compiler_params=pltpu.CompilerParams(
            dimension_semantics=("arbitrary",), vmem_limit_bytes=VMEM_LIMIT),
        name="swiglu",
    )(block_e, xrows, w1, w3, w2, row_gate)


def _cast_kernel(x_ref, o_ref):
    o_ref[0] = x_ref[0, 0].astype(o_ref.dtype)


def _layer_weights_bf16(w, layer):
    _, ne, r, c = w.shape
    rb = r
    return pl.pallas_call(
        _cast_kernel,
        out_shape=jax.ShapeDtypeStruct((ne, r, c), jnp.bfloat16),
        grid=(ne, r // rb),
        in_specs=[pl.BlockSpec((1, 1, rb, c), lambda e, j: (layer, e, j, 0))],
        out_specs=pl.BlockSpec((1, rb, c), lambda e, j: (e, j, 0)),
        compiler_params=pltpu.CompilerParams(
            dimension_semantics=("parallel", "parallel"), vmem_limit_bytes=VMEM_LIMIT),
        name="cast_bf16",
    )(w).reshape(ne, r, c)


def _swiglu(x2d, w1, w3, w2, layer, bm=512):
    rows = x2d.shape[0]
    bf = lambda t: _layer_weights_bf16(t[:, None], layer)
    return _grouped_swiglu(x2d, jnp.zeros((rows // bm,), jnp.int32), jnp.ones((rows, 1), jnp.float32),
                           bf(w1), bf(w3), bf(w2), bm)


def _moe_swiglu(xf, logits, w1, w3, w2, layer):
    T, d = xf.shape
    top_logit, top_e = lax.top_k(logits, TOP_K)
    gate = jax.nn.softmax(top_logit, axis=-1)
    flat_e = top_e.reshape(-1).astype(jnp.int32)
    flat_tok = jnp.repeat(jnp.arange(T, dtype=jnp.int32), TOP_K)
    onehot = (flat_e[:, None] == jnp.arange(N_EXPERTS, dtype=jnp.int32)[None, :]).astype(jnp.int32)
    csum = jnp.cumsum(onehot, axis=0)
    counts = csum[-1]
    rank = jnp.sum(onehot * (csum - 1), axis=1)
    padded = (counts + MOE_BLOCK - 1) // MOE_BLOCK * MOE_BLOCK
    pad_end = jnp.cumsum(padded)
    pad_start = pad_end - padded
    start = jnp.cumsum(counts) - counts
    pos = (pad_start[flat_e] + rank).reshape(T, TOP_K)
    _, st, sg = lax.sort((flat_e, flat_tok, gate.reshape(-1)), num_keys=1, is_stable=True)
    n_blocks = -(-(T * TOP_K) // MOE_BLOCK) + N_EXPERTS
    rows = n_blocks * MOE_BLOCK
    block_e = jnp.minimum(jnp.searchsorted(pad_end, jnp.arange(n_blocks) * MOE_BLOCK, side='right'),
                          N_EXPERTS - 1).astype(jnp.int32)
    row_e = jnp.repeat(block_e, MOE_BLOCK)
    q = jnp.arange(rows, dtype=jnp.int32) - pad_start[row_e]
    valid = q < counts[row_e]
    src = jnp.clip(start[row_e] + q, 0, T * TOP_K - 1)
    slot_tok = jnp.where(valid, st[src], 0)
    slot_gate = jnp.where(valid, sg[src], 0.0)
    bf = lambda t: _layer_weights_bf16(t, layer)
    yb = _grouped_swiglu(xf[slot_tok], block_e, slot_gate[:, None], bf(w1), bf(w3), bf(w2), MOE_BLOCK)
    return [yb[pos[:, 0]], yb[pos[:, 1]]]


def kernel(x, p, w_in, m_i_bias, m_f_bias, m_norm_w, g_conv_w, g_conv_b, g_w_a, g_b_a, g_w_x, g_b_x,
           g_lambda, r_mu, r_w0, r_w2, r_a0, r_a2, r_g2, r_k_k, r_k_a, r_r_k, r_ln_w, r_ln_b,
           s_conv_w, s_conv_b, s_dt_bias, s_a_log, s_d, s_norm_w, w_out, ln1_g, ln1_b, ln2_g, ln2_b,
           f_w1, f_w3, f_w2, e_router, e_w1, e_w3, e_w2, pe_proj, pe_gate_w, pe_gate_b):
    bsz, seq, d = x.shape
    T = bsz * seq
    x = x.reshape(T, d)
    for i in range(DEPTH):
        pm, pg, pr, ps, psm = _inproj(x, _split_w_in(w_in, i))
        mixers = [
            _mlstm_group(pm, psm, bsz, m_i_bias[i], m_f_bias[i], m_norm_w[i]),
            _rglru_group(pg, bsz, g_conv_w[i], g_conv_b[i], g_w_a[i], g_b_a[i], g_w_x[i], g_b_x[i], g_lambda[i]),
            _rwkv7_group(pr, bsz, r_mu[i], r_w0[i], r_w2[i], r_a0[i], r_a2[i], r_g2[i],
                         r_k_k[i], r_k_a[i], r_r_k[i], r_ln_w[i], r_ln_b[i]),
            _ssd_group(ps, psm, bsz, s_conv_w[i], s_conv_b[i], s_dt_bias[i], s_a_log[i], s_d[i], s_norm_w[i]),
        ]
        if i % 2 == 0:
            (x,) = _outproj_ln(x, mixers, w_out[i], ln1_g[i], ln1_b[i])
            ff = [_swiglu(x, f_w1, f_w3, f_w2, i // 2)]
        else:
            x, logits = _outproj_ln(x, mixers, w_out[i], ln1_g[i], ln1_b[i], e_router[i // 2])
            ff = _moe_swiglu(x, logits[:, :N_EXPERTS], e_w1, e_w3, e_w2, i // 2)
        x = _pe_ln(x, ff, p.reshape(DEPTH, T, PE_DIM), i, pe_proj[i], pe_gate_w[i], pe_gate_b[i], ln2_g[i], ln2_b[i])
    return x.reshape(bsz, seq, d)
```

```python
import functools

import jax
import jax.numpy as jnp
from jax import lax
from jax.experimental import pallas as pl
from jax.experimental.pallas import tpu as pltpu

D_MODEL = 1024
DEPTH = 4
PE_DIM = 256
GROUP_W = 256
HEAD_DIM = 64
GROUP_HEADS = 4
CONV_K = 4
MLSTM_CHUNK = 64
RG_C = 8.0
RWKV_DECAY_RANK = 32
RWKV_ICL_RANK = 32
RWKV_GATE_RANK = 64
RWKV_GN_EPS = 64e-5
SSD_STATE = 64
SSD_GROUPS = 2
SSD_CHUNK = 128
SSD_XBC = GROUP_W + 2 * SSD_GROUPS * SSD_STATE
M_COLS = 4 * GROUP_W + 2 * GROUP_HEADS
G_COLS = 2 * GROUP_W
R_COLS = 3 * GROUP_W + RWKV_DECAY_RANK + RWKV_ICL_RANK + RWKV_GATE_RANK
S_COLS = GROUP_W + SSD_XBC + GROUP_HEADS
N_IN = M_COLS + G_COLS + R_COLS + S_COLS
N_EXPERTS = 8
TOP_K = 2
MOE_BLOCK = 512
ALPHA = (2 * DEPTH) ** 0.25
LN_EPS = 1e-5
RMS_EPS = 1e-6

LANES = 128
SUBLANES = 8
VMEM_LIMIT = 56 * 1024 * 1024


RWKV_TB = 32


def _rwkv_scan_kernel(pa_ref, pb_ref, pc_ref, y_ref, s_ref, sa_ref, ta_ref, tb_ref, tc_ref, yb_ref,
                      *, tb, nb, nk, nacc):
    @pl.when(pl.program_id(0) == 0)
    def _():
        s_ref[...] = jnp.zeros_like(s_ref)
        sa_ref[...] = jnp.zeros_like(sa_ref)

    nh = GROUP_HEADS
    half = nk // 2
    lane = lax.broadcasted_iota(jnp.int32, (half, LANES), 1)
    rows = lambda t, h: pl.ds(h * nb * tb + t, nb, stride=tb)

    def prepare(t, carry):
        for src, dst in ((pa_ref, ta_ref), (pb_ref, tb_ref), (pc_ref, tc_ref)):
            parts = [src[rows(t, h), :] for h in range(nh)]
            dst[t] = jnp.concatenate(parts + parts, axis=0).T
        return carry

    lax.fori_loop(0, tb, prepare, 0, unroll=8)

    def step(t, sa):
        bc = lambda ref, r: jnp.broadcast_to(ref[t, pl.ds(r, 1), :], (half, LANES))
        v_t = jnp.where(lane < nk, tc_ref[t, nk:nk + half, :], tc_ref[t, nk + half:2 * nk, :])
        y_acc = [None] * nacc
        sa_acc = [None] * nacc
        for k in range(nk):
            s_k = s_ref[k] * bc(ta_ref, k) - sa * bc(ta_ref, nk + k) + v_t * bc(tb_ref, k)
            s_ref[k] = s_k
            y_k = s_k * bc(tb_ref, nk + k)
            n_k = s_k * bc(tc_ref, k)
            a = k % nacc
            y_acc[a] = y_k if y_acc[a] is None else y_acc[a] + y_k
            sa_acc[a] = n_k if sa_acc[a] is None else sa_acc[a] + n_k
        yb_ref[t] = functools.reduce(lambda p, q: p + q, y_acc)
        return functools.reduce(lambda p, q: p + q, sa_acc)

    sa_ref[...] = lax.fori_loop(0, tb, step, sa_ref[...])

    def emit(t, carry):
        y = yb_ref[t]
        y_sw = pltpu.roll(y, nk, 1)
        y_nat = jnp.concatenate([y, y_sw, y, y_sw], axis=0).T
        for h in range(nh):
            y_ref[rows(t, h), :] = y_nat[h * nb:(h + 1) * nb, :]
        return carry

    lax.fori_loop(0, tb, emit, 0, unroll=8)


def _rwkv_scan(pa, pb, pc, bsz):
    nblk, nh, _, tb, _ = pa.shape
    n = HEAD_DIM
    assert 2 * bsz * nh == LANES and 2 * n == LANES
    flat = lambda t: t.reshape(nblk * nh * bsz * tb, LANES)
    spec = pl.BlockSpec((nh * bsz * tb, LANES), lambda i: (i, 0))
    y = pl.pallas_call(
        functools.partial(_rwkv_scan_kernel, tb=tb, nb=bsz, nk=n, nacc=2),
        out_shape=jax.ShapeDtypeStruct((nblk * nh * bsz * tb, LANES), jnp.float32),
        grid=(nblk,),
        in_specs=[spec, spec, spec],
        out_specs=spec,
        scratch_shapes=[pltpu.VMEM((n, n // 2, LANES), jnp.float32), pltpu.VMEM((n // 2, LANES), jnp.float32)]
        + [pltpu.VMEM((tb, LANES, LANES), jnp.float32)] * 3 + [pltpu.VMEM((tb, n // 2, LANES), jnp.float32)],
        compiler_params=pltpu.CompilerParams(
            dimension_semantics=("arbitrary",), vmem_limit_bytes=VMEM_LIMIT),
        name="rwkv_scan",
    )(flat(pa), flat(pb), flat(pc))
    return y.reshape(pa.shape)


def _inproj_kernel(x_ref, *refs):
    n = len(refs) // 2
    xb = x_ref[...].astype(jnp.bfloat16)
    for w_ref, o_ref in zip(refs[:n], refs[n:]):
        o_ref[...] = jnp.dot(xb, w_ref[...], preferred_element_type=jnp.float32)


def _inproj(x, weights, tm=512):
    t, d = x.shape
    return pl.pallas_call(
        _inproj_kernel,
        out_shape=[jax.ShapeDtypeStruct((t, w.shape[1]), jnp.float32) for w in weights],
        grid=(t // tm,),
        in_specs=[pl.BlockSpec((tm, d), lambda i: (i, 0))]
        + [pl.BlockSpec(w.shape, lambda i: (0, 0)) for w in weights],
        out_specs=[pl.BlockSpec((tm, w.shape[1]), lambda i: (i, 0)) for w in weights],
        compiler_params=pltpu.CompilerParams(
            dimension_semantics=("parallel",), vmem_limit_bytes=VMEM_LIMIT),
        name="inproj",
    )(x, *weights)


def _split_w_in_kernel(w_ref, m_ref, g_ref, r_ref, s_ref, sm_ref):
    m0, g0, r0, s0 = 0, M_COLS, M_COLS + G_COLS, M_COLS + G_COLS + R_COLS
    bf = jnp.bfloat16
    m_ref[...] = w_ref[0, :, m0:m0 + 4 * GROUP_W].astype(bf)
    g_ref[...] = w_ref[0, :, g0:g0 + G_COLS].astype(bf)
    r_ref[...] = w_ref[0, :, r0:r0 + R_COLS].astype(bf)
    s_ref[...] = w_ref[0, :, s0:s0 + GROUP_W + SSD_XBC].astype(bf)
    gates = jnp.concatenate([w_ref[0, :, m0 + 4 * GROUP_W:g0], w_ref[0, :, s0 + GROUP_W + SSD_XBC:N_IN]], axis=1)
    pad = jnp.zeros((gates.shape[0], LANES - gates.shape[1]), jnp.float32)
    sm_ref[...] = jnp.concatenate([gates, pad], axis=1).astype(bf)


def _split_w_in(w_in, layer):
    _, d, n = w_in.shape
    widths = (4 * GROUP_W, G_COLS, R_COLS, GROUP_W + SSD_XBC, LANES)
    return pl.pallas_call(
        _split_w_in_kernel,
        out_shape=[jax.ShapeDtypeStruct((d, c), jnp.bfloat16) for c in widths],
        grid=(1,),
        in_specs=[pl.BlockSpec((1, d, n), lambda i: (layer, 0, 0))],
        out_specs=[pl.BlockSpec((d, c), lambda i: (0, 0)) for c in widths],
        compiler_params=pltpu.CompilerParams(vmem_limit_bytes=VMEM_LIMIT),
        name="split_w_in",
    )(w_in)


def _shift_rows(x, s, fill, row):
    return jnp.where(row >= s, pltpu.roll(x, s, 0), fill)


def _rglru_kernel(g_ref, cw_ref, cb_ref, wg_ref, bg_ref, c_ref, o_ref, xpad_ref, a_ref, u_ref, *, rc, unroll):
    seq, w = o_ref.shape
    pad = SUBLANES
    xpad_ref[0:pad, :] = jnp.zeros((pad, w), jnp.float32)
    xpad_ref[pad:, :] = g_ref[:, 0:w]
    cw = cw_ref[...]
    for c in range(seq // rc):
        r0 = c * rc
        xc = cb_ref[...] + cw[CONV_K - 1:CONV_K, :] * g_ref[r0:r0 + rc, 0:w]
        for j in range(1, CONV_K):
            xc = xc + cw[CONV_K - 1 - j:CONV_K - j, :] * xpad_ref[pad + r0 - j:pad + r0 - j + rc, :]
        z = jnp.dot(xc.astype(jnp.bfloat16), wg_ref[...], preferred_element_type=jnp.float32) + bg_ref[...]
        z = jax.nn.sigmoid(z)
        a = jnp.exp(c_ref[...] * z[:, 0:w])
        a_ref[r0:r0 + rc, :] = a
        u_ref[r0:r0 + rc, :] = jnp.sqrt(1.0 - a * a) * (z[:, w:] * xc)

    row = lax.broadcasted_iota(jnp.int32, (SUBLANES, w), 0)

    def tiles(i, h):
        for j in range(unroll):
            r = pl.multiple_of((i * unroll + j) * SUBLANES, SUBLANES)
            a = a_ref[pl.ds(r, SUBLANES), :]
            u = u_ref[pl.ds(r, SUBLANES), :]
            for s in (1, 2, 4):
                u = a * _shift_rows(u, s, 0.0, row) + u
                a = a * _shift_rows(a, s, 1.0, row)
            ht = a * h + u
            o_ref[pl.ds(r, SUBLANES), :] = ht * jax.nn.gelu(g_ref[pl.ds(r, SUBLANES), w:2 * w])
            h = jnp.broadcast_to(ht[SUBLANES - 1:SUBLANES, :], (SUBLANES, w))
        return h

    lax.fori_loop(0, seq // (SUBLANES * unroll), tiles, jnp.zeros((SUBLANES, w), jnp.float32))


def _block_diag(w):
    h, p, _ = w.shape
    eye = jnp.eye(h, dtype=w.dtype)
    return jnp.einsum('hij,hg->higj', w, eye).reshape(h * p, h * p)


def _rglru_group(g_cols, bsz, conv_w, conv_b, w_a, b_a, w_x, b_x, lam):
    t = g_cols.shape[0]
    seq = t // bsz
    w = GROUP_W
    wg = jnp.concatenate([_block_diag(w_a), _block_diag(w_x)], axis=1).astype(jnp.bfloat16)
    bg = jnp.concatenate([b_a, b_x])[None, :]
    cdec = (-RG_C * jax.nn.softplus(-lam))[None, :]
    const = lambda shape: pl.BlockSpec(shape, lambda b: (0, 0))
    return pl.pallas_call(
        functools.partial(_rglru_kernel, rc=256, unroll=8),
        out_shape=jax.ShapeDtypeStruct((t, w), jnp.float32),
        grid=(bsz,),
        in_specs=[pl.BlockSpec((seq, 2 * w), lambda b: (b, 0)), const((CONV_K, w)), const((1, w)),
                  const((w, 2 * w)), const((1, 2 * w)), const((1, w))],
        out_specs=pl.BlockSpec((seq, w), lambda b: (b, 0)),
        scratch_shapes=[pltpu.VMEM((seq + SUBLANES, w), jnp.float32),
                        pltpu.VMEM((seq, w), jnp.float32), pltpu.VMEM((seq, w), jnp.float32)],
        compiler_params=pltpu.CompilerParams(
            dimension_semantics=("parallel",), vmem_limit_bytes=VMEM_LIMIT),
        name="rglru",
    )(g_cols, conv_w, conv_b[None, :], wg, bg, cdec)


MIXER_ROWS_PER_STEP = 2
_NT = (((1,), (1,)), ((), ()))
_TN = (((0,), (0,)), ((), ()))


def _bdot(a, b, dims=(((1,), (0,)), ((), ()))):
    return lax.dot_general(a.astype(jnp.bfloat16), b.astype(jnp.bfloat16), dims,
                           preferred_element_type=jnp.float32)


def _cumsum_rows(x):
    n = x.shape[0]
    tri = (lax.broadcasted_iota(jnp.int32, (n, n), 0) >= lax.broadcasted_iota(jnp.int32, (n, n), 1))
    return jnp.dot(tri.astype(jnp.float32), x, precision=lax.Precision.HIGHEST,
                   preferred_element_type=jnp.float32)


def _log_sigmoid(x):
    return jnp.minimum(x, 0.0) - jnp.log1p(jnp.exp(-jnp.abs(x)))


def _softplus(x):
    return jnp.maximum(x, 0.0) + jnp.log1p(jnp.exp(-jnp.abs(x)))


def _mlstm_kernel(pm_ref, sm_ref, bias_ref, nw_ref, o_ref, c_ref, n_ref, m_ref):
    _, seq, w = o_ref.shape
    nh, dh, L = GROUP_HEADS, HEAD_DIM, MLSTM_CHUNK
    c_ref[...] = jnp.zeros_like(c_ref)
    n_ref[...] = jnp.zeros_like(n_ref)
    m_ref[...] = jnp.zeros_like(m_ref)
    f32 = jnp.float32
    hi = lax.Precision.HIGHEST
    col = lax.broadcasted_iota(jnp.int32, (L, LANES), 1)
    lrow = lax.broadcasted_iota(jnp.int32, (L, LANES), 0)
    scol = col & (dh - 1)
    lo = col < dh
    lo1 = lo[0:1]
    sq = lax.broadcasted_iota(jnp.int32, (LANES, LANES), 0)
    sr = lax.broadcasted_iota(jnp.int32, (LANES, LANES), 1)
    bd = (sq < dh) == (sr < dh)
    e_row = lax.broadcasted_iota(jnp.int32, (LANES, 2 * w), 0)
    e_col = lax.broadcasted_iota(jnp.int32, (LANES, 2 * w), 1)
    e_head = jnp.where(e_col < w, 0, nh) + ((e_col & (w - 1)) >> (dh.bit_length() - 1))
    spread = (e_row == e_head).astype(f32)
    tri = (lax.broadcasted_iota(jnp.int32, (L, L), 0) >= lax.broadcasted_iota(jnp.int32, (L, L), 1)).astype(f32)

    def seg_max(x, first):
        a = jnp.max(jnp.where(first, x, -jnp.inf), axis=1, keepdims=True)
        b = jnp.max(jnp.where(first, -jnp.inf, x), axis=1, keepdims=True)
        return jnp.where(first, a, b)

    def seg_sum(x):
        a = jnp.sum(jnp.where(lo, x, 0.0), axis=1, keepdims=True)
        b = jnp.sum(jnp.where(lo, 0.0, x), axis=1, keepdims=True)
        return jnp.where(lo, a, b)

    nb = pm_ref.shape[0]
    items = [(b, p) for b in range(nb) for p in range(nh // 2)]
    lanes = lambda p, sec: slice(sec * w + p * LANES, sec * w + (p + 1) * LANES)

    def chunk(c, carry):
        r0 = pl.multiple_of(c * L, L)
        rows = pl.ds(r0, L)
        each = lambda f, *xs: [f(*args) for args in zip(*xs)]
        gts = [sm_ref[b, rows, :] + bias_ref[...] for b in range(nb)]
        gts = [jnp.where(col < nh, t, jnp.where(col < 2 * nh, _log_sigmoid(t), 0.0)) for t in gts]
        rep = [jnp.dot(t, spread, precision=hi, preferred_element_type=f32) for t in gts]
        b_rep = [jnp.dot(tri, t[:, w:], precision=hi, preferred_element_type=f32) for t in rep]
        bc = [b_rep[b][:, lanes(p, 0)] for b, p in items]
        lic = [rep[b][:, lanes(p, 0)] for b, p in items]
        lfc = [rep[b][:, lanes(p, 1)] for b, p in items]
        b_row = [jnp.sum(jnp.where(lrow <= scol, t, 0.0), axis=0, keepdims=True) for t in lfc]
        li_row = [jnp.sum(jnp.where(lrow == scol, t, 0.0), axis=0, keepdims=True) for t in lic]
        g = [t[L - 1:L, :] for t in bc]
        q = [pm_ref[b, rows, lanes(p, 0)] for b, p in items]
        k = [pm_ref[b, rows, lanes(p, 1)] * (dh ** -0.5) for b, p in items]
        v = [pm_ref[b, rows, lanes(p, 2)] for b, p in items]
        cst = [c_ref[b, p] for b, p in items]
        nst = [n_ref[b, p] for b, p in items]
        m = [m_ref[b, p] for b, p in items]
        log_d = each(lambda bc_, br, lr: jnp.where(lrow >= scol, bc_ - br + lr, -jnp.inf), bc, b_row, li_row)
        m_t = each(lambda bc_, m_, ld: jnp.maximum(bc_ + m_, seg_max(ld, lo)), bc, m, log_d)
        k2 = [jnp.concatenate([t, t], axis=0) for t in k]
        qk = each(lambda q_, k2_: jnp.where(lo, _bdot(jnp.where(lo, q_, 0.0), k2_, _NT),
                                            _bdot(jnp.where(lo, 0.0, q_), k2_, _NT)), q, k2)
        s = each(lambda qk_, ld, mt: qk_ * jnp.exp(ld - mt), qk, log_d, m_t)
        inter_w = each(lambda bc_, m_, mt: jnp.exp(bc_ + m_ - mt), bc, m, m_t)
        v_bd = [jnp.where(bd, jnp.concatenate([t, t], axis=0), 0.0) for t in v]
        num = each(lambda s_, vb, iw, q_, c_: _bdot(s_, vb) + iw * _bdot(q_, c_), s, v_bd, inter_w, q, cst)
        den = each(lambda s_, iw, q_, n_: seg_sum(s_) + iw * seg_sum(q_ * n_), s, inter_w, q, nst)
        hh = each(lambda nu, de, mt: nu / jnp.maximum(jnp.abs(de), jnp.exp(-mt)), num, den, m_t)
        ms = [seg_sum(t * t) for t in hh]
        for (b, p), h_, ms_ in zip(items, hh, ms):
            o = pm_ref[b, rows, lanes(p, 3)]
            o_ref[b, rows, lanes(p, 0)] = (h_ * lax.rsqrt(ms_ * (1.0 / dh) + RMS_EPS) * nw_ref[:, lanes(p, 0)]
                                           * jax.nn.sigmoid(o))
        m_new = each(lambda g_, m_, br, lr: jnp.maximum(g_ + m_, seg_max(g_ - br + lr, lo1)), g, m, b_row, li_row)
        decay = each(lambda g_, m_, mn: jnp.exp(g_ + m_ - mn), g, m, m_new)
        w_col = each(lambda g_, bc_, li_, mn: jnp.exp(g_ - bc_ + li_ - mn), g, bc, lic, m_new)
        upd = each(lambda k_, wc, v_: jnp.where(bd, _bdot(k_, wc * v_, _TN), 0.0), k, w_col, v)
        for i, (b, p) in enumerate(items):
            c_ref[b, p] = decay[i] * cst[i] + upd[i]
            n_ref[b, p] = decay[i] * nst[i] + jnp.sum(w_col[i] * k[i], axis=0, keepdims=True)
            m_ref[b, p] = m_new[i]
        return carry

    lax.fori_loop(0, seq // L, chunk, 0)


def _mlstm_group(pm, psm, bsz, i_bias, f_bias, norm_w):
    t = pm.shape[0]
    seq = t // bsz
    w, nh, dh = GROUP_W, GROUP_HEADS, HEAD_DIM
    bias = jnp.pad(jnp.concatenate([i_bias, f_bias]), (0, LANES - 2 * nh))[None, :]
    const = lambda shape: pl.BlockSpec(shape, lambda b: (0,) * len(shape))
    nb = MIXER_ROWS_PER_STEP
    assert bsz % nb == 0
    seq_block = lambda n: pl.BlockSpec((nb, seq, n), lambda b: (b, 0, 0))
    return pl.pallas_call(
        _mlstm_kernel,
        out_shape=jax.ShapeDtypeStruct((bsz, seq, w), jnp.float32),
        grid=(bsz // nb,),
        in_specs=[seq_block(4 * w), seq_block(LANES), const((1, LANES)), const((1, w))],
        out_specs=seq_block(w),
        scratch_shapes=[pltpu.VMEM((nb, nh // 2, LANES, LANES), jnp.float32),
                        pltpu.VMEM((nb, nh // 2, 1, LANES), jnp.float32),
                        pltpu.VMEM((nb, nh // 2, 1, LANES), jnp.float32)],
        compiler_params=pltpu.CompilerParams(
            dimension_semantics=("parallel",), vmem_limit_bytes=VMEM_LIMIT),
        name="mlstm",
    )(pm.reshape(bsz, seq, 4 * w), psm.reshape(bsz, seq, LANES), bias, norm_w[None, :]).reshape(t, w)


def _ssd_kernel(ps_ref, sm_ref, cw_ref, cb_ref, dtb_ref, aneg_ref, dsk_ref, nw_ref, o_ref,
                xbc_ref, st_ref, *, rc):
    nb, seq, w = o_ref.shape
    nh, hp, ng, ns, L = GROUP_HEADS, HEAD_DIM, SSD_GROUPS, SSD_STATE, SSD_CHUNK
    cx = SSD_XBC
    cw = cw_ref[...]
    row = lax.broadcasted_iota(jnp.int32, (rc, cx), 0)
    for b in range(nb):
        for c in range(seq // rc):
            r0 = c * rc
            cur = ps_ref[b, r0:r0 + rc, w:w + cx]
            acc = cb_ref[...] + cw[CONV_K - 1:CONV_K, :] * cur
            for j in range(1, CONV_K):
                if c == 0:
                    sh = jnp.where(row >= j, pltpu.roll(cur, j, 0), 0.0)
                else:
                    sh = ps_ref[b, r0 - j:r0 - j + rc, w:w + cx]
                acc = acc + cw[CONV_K - 1 - j:CONV_K - j, :] * sh
            xbc_ref[b, r0:r0 + rc, :] = jax.nn.silu(acc)
    st_ref[...] = jnp.zeros_like(st_ref)
    causal = lax.broadcasted_iota(jnp.int32, (L, L), 0) >= lax.broadcasted_iota(jnp.int32, (L, L), 1)
    gsz = w // ng
    rep = nh // ng
    groups = [(b, g) for b in range(nb) for g in range(ng)]
    heads = [(b, h) for b in range(nb) for h in range(nh)]

    def chunk(c, carry):
        r0 = pl.multiple_of(c * L, L)
        rows = pl.ds(r0, L)
        each = lambda f, *xs: [f(*args) for args in zip(*xs)]
        dt_all = [_softplus(sm_ref[b, rows, :] + dtb_ref[...]) for b in range(nb)]
        acs_all = [_cumsum_rows(aneg_ref[...] * t) for t in dt_all]
        acs_t = [t.T for t in acs_all]
        bm = {bg: xbc_ref[bg[0], rows, w + bg[1] * ns:w + (bg[1] + 1) * ns] for bg in groups}
        cm = {bg: xbc_ref[bg[0], rows, w + (ng + bg[1]) * ns:w + (ng + bg[1] + 1) * ns] for bg in groups}
        cb = {bg: _bdot(cm[bg], bm[bg], _NT) for bg in groups}
        grp = lambda b, h: (b, h // rep)
        cidx = lambda h: 2 * nh + h
        acs_col = [acs_all[b][:, cidx(h):cidx(h) + 1] for b, h in heads]
        acs_row = [acs_t[b][cidx(h):cidx(h) + 1, :] for b, h in heads]
        dt_col = [dt_all[b][:, cidx(h):cidx(h) + 1] for b, h in heads]
        a_last = [t[L - 1:L, :] for t in acs_col]
        xh = [xbc_ref[b, rows, h * hp:(h + 1) * hp] for b, h in heads]
        xdt = each(lambda x_, d_: x_ * d_, xh, dt_col)
        state = [st_ref[b, h] for b, h in heads]
        scores = [cb[grp(b, h)] * jnp.exp(jnp.where(causal, ac - ar, -jnp.inf))
                  for (b, h), ac, ar in zip(heads, acs_col, acs_row)]
        y_diag = each(_bdot, scores, xdt)
        y_off = [_bdot(cm[grp(b, h)], st, _NT) for (b, h), st in zip(heads, state)]
        ys = [yd + jnp.exp(ac) * yo + x_ * dsk_ref[:, h * hp:(h + 1) * hp]
              for (b, h), yd, ac, yo, x_ in zip(heads, y_diag, acs_col, y_off, xh)]
        upd = [_bdot(xd * jnp.exp(al - ac), bm[grp(b, h)], _TN)
               for (b, h), xd, al, ac in zip(heads, xdt, a_last, acs_col)]
        for i, (b, h) in enumerate(heads):
            st_ref[b, h] = jnp.exp(a_last[i]) * state[i] + upd[i]
        ysd = dict(zip(heads, ys))
        ygs = [jnp.concatenate([ysd[(b, g * rep + j)] for j in range(rep)], axis=1)
               * jax.nn.silu(ps_ref[b, rows, g * gsz:(g + 1) * gsz]) for b, g in groups]
        ms = [jnp.mean(t * t, axis=1, keepdims=True) for t in ygs]
        for (b, g), yg, m_ in zip(groups, ygs, ms):
            o_ref[b, rows, g * gsz:(g + 1) * gsz] = yg * lax.rsqrt(m_ + RMS_EPS) * nw_ref[:, g * gsz:(g + 1) * gsz]
        return carry

    lax.fori_loop(0, seq // L, chunk, 0)


def _ssd_group(ps, psm, bsz, conv_w, conv_b, dt_bias, a_log, d_skip, norm_w):
    t = ps.shape[0]
    seq = t // bsz
    w, nh = GROUP_W, GROUP_HEADS
    lane_vec = lambda v: jnp.pad(v, (2 * nh, LANES - 3 * nh))[None, :]
    dsk = jnp.repeat(d_skip, HEAD_DIM)[None, :]
    const = lambda shape: pl.BlockSpec(shape, lambda b: (0,) * len(shape))
    nb = MIXER_ROWS_PER_STEP
    rc = 256
    assert bsz % nb == 0 and seq % rc == 0
    seq_block = lambda n: pl.BlockSpec((nb, seq, n), lambda b: (b, 0, 0))
    return pl.pallas_call(
        functools.partial(_ssd_kernel, rc=rc),
        out_shape=jax.ShapeDtypeStruct((bsz, seq, w), jnp.float32),
        grid=(bsz // nb,),
        in_specs=[seq_block(w + SSD_XBC), seq_block(LANES),
                  const((CONV_K, SSD_XBC)), const((1, SSD_XBC)), const((1, LANES)), const((1, LANES)),
                  const((1, w)), const((1, w))],
        out_specs=seq_block(w),
        scratch_shapes=[pltpu.VMEM((nb, seq, SSD_XBC), jnp.float32),
                        pltpu.VMEM((nb, nh, HEAD_DIM, SSD_STATE), jnp.float32)],
        compiler_params=pltpu.CompilerParams(
            dimension_semantics=("parallel",), vmem_limit_bytes=VMEM_LIMIT),
        name="ssd",
    )(ps.reshape(bsz, seq, w + SSD_XBC), psm.reshape(bsz, seq, LANES), conv_w, conv_b[None, :], lane_vec(dt_bias),
      lane_vec(-jnp.exp(a_log)), dsk, norm_w[None, :]).reshape(t, w)


def _layer_norm(x, g, b):
    xc = x - jnp.mean(x, -1, keepdims=True)
    var = jnp.mean(xc * xc, -1, keepdims=True)
    return (xc * lax.rsqrt(var + LN_EPS)) * g + b


def _outproj_ln_kernel(x_ref, m0_ref, m1_ref, m2_ref, m3_ref, w_ref, g_ref, b_ref, *rest):
    gw = m0_ref.shape[1]
    acc = ALPHA * x_ref[...]
    for j, m_ref in enumerate((m0_ref, m1_ref, m2_ref, m3_ref)):
        acc = acc + _bdot(m_ref[...], w_ref[j * gw:(j + 1) * gw, :])
    x1 = _layer_norm(acc, g_ref[...], b_ref[...])
    if len(rest) == 1:
        rest[0][...] = x1
    else:
        wr_ref, o_ref, lg_ref = rest
        o_ref[...] = x1
        lg_ref[...] = _bdot(x1, wr_ref[...])


def _outproj_ln(x, mixers, w_out, g, b, w_router=None, tm=512):
    t, d = x.shape
    gw = mixers[0].shape[1]
    row = lambda n: pl.BlockSpec((tm, n), lambda i: (i, 0))
    const = lambda shape: pl.BlockSpec(shape, lambda i: (0, 0))
    in_specs = [row(d)] + [row(gw)] * 4 + [const((d, d)), const((1, d)), const((1, d))]
    args = [x, *mixers, w_out.astype(jnp.bfloat16), g[None, :], b[None, :]]
    out_shape = [jax.ShapeDtypeStruct((t, d), jnp.float32)]
    out_specs = [row(d)]
    if w_router is not None:
        wr = jnp.pad(w_router, ((0, 0), (0, LANES - w_router.shape[1]))).astype(jnp.bfloat16)
        in_specs.append(const((d, LANES)))
        args.append(wr)
        out_shape.append(jax.ShapeDtypeStruct((t, LANES), jnp.float32))
        out_specs.append(row(LANES))
    return pl.pallas_call(
        _outproj_ln_kernel,
        out_shape=out_shape, grid=(t // tm,), in_specs=in_specs, out_specs=out_specs,
        compiler_params=pltpu.CompilerParams(
            dimension_semantics=("parallel",), vmem_limit_bytes=VMEM_LIMIT),
        name="outproj_ln",
    )(*args)


def _pe_ln_kernel(x_ref, p_ref, wp_ref, wg_ref, bg_ref, g_ref, b_ref, *rest):
    *ff_refs, o_ref = rest
    x1 = x_ref[...]
    pe = _bdot(p_ref[0], wp_ref[...]) * jax.nn.sigmoid(_bdot(x1, wg_ref[...]) + bg_ref[...])
    ff = functools.reduce(lambda a, c: a + c, [r[...] for r in ff_refs])
    o_ref[...] = _layer_norm(ALPHA * x1 + ff + pe, g_ref[...], b_ref[...])


def _pe_ln(x1, ff_parts, p, layer, pe_proj, gate_w, gate_b, g, b, tm=512):
    t, d = x1.shape
    pd = p.shape[2]
    row = lambda n: pl.BlockSpec((tm, n), lambda i: (i, 0))
    const = lambda shape: pl.BlockSpec(shape, lambda i: (0, 0))
    return pl.pallas_call(
        _pe_ln_kernel,
        out_shape=jax.ShapeDtypeStruct((t, d), jnp.float32),
        grid=(t // tm,),
        in_specs=[row(d), pl.BlockSpec((1, tm, pd), lambda i: (layer, i, 0)), const((pd, d)), const((d, d)),
                  const((1, d)), const((1, d)), const((1, d))]
        + [row(d)] * len(ff_parts),
        out_specs=row(d),
        compiler_params=pltpu.CompilerParams(
            dimension_semantics=("parallel",), vmem_limit_bytes=VMEM_LIMIT),
        name="pe_ln",
    )(x1, p, pe_proj.astype(jnp.bfloat16), gate_w.astype(jnp.bfloat16), gate_b[None, :], g[None, :], b[None, :],
      *ff_parts)


def _pe_kernel(x_ref, p_ref, wp_ref, wg_ref, bg_ref, o_ref):
    o_ref[...] = _bdot(p_ref[0], wp_ref[...]) * jax.nn.sigmoid(_bdot(x_ref[...], wg_ref[...]) + bg_ref[...])


def _ln_sum_kernel(x_ref, pe_ref, g_ref, b_ref, *rest):
    *ff_refs, o_ref = rest
    ff = functools.reduce(lambda a, c: a + c, [r[...] for r in ff_refs])
    o_ref[...] = _layer_norm(ALPHA * x_ref[...] + ff + pe_ref[...], g_ref[...], b_ref[...])


def _pe_then_ln(x1, ff_parts, p, layer, pe_proj, gate_w, gate_b, g, b, tm=512):
    t, d = x1.shape
    pd = p.shape[2]
    row = lambda n: pl.BlockSpec((tm, n), lambda i: (i, 0))
    const = lambda shape: pl.BlockSpec(shape, lambda i: (0, 0))
    params = pltpu.CompilerParams(dimension_semantics=("parallel",), vmem_limit_bytes=VMEM_LIMIT)
    pe = pl.pallas_call(
        _pe_kernel,
        out_shape=jax.ShapeDtypeStruct((t, d), jnp.float32),
        grid=(t // tm,),
        in_specs=[row(d), pl.BlockSpec((1, tm, pd), lambda i: (layer, i, 0)), const((pd, d)), const((d, d)),
                  const((1, d))],
        out_specs=row(d), compiler_params=params, name="pe_gate",
    )(x1, p, pe_proj.astype(jnp.bfloat16), gate_w.astype(jnp.bfloat16), gate_b[None, :])
    return pl.pallas_call(
        _ln_sum_kernel,
        out_shape=jax.ShapeDtypeStruct((t, d), jnp.float32),
        grid=(t // tm,),
        in_specs=[row(d), row(d), const((1, d)), const((1, d))] + [row(d)] * len(ff_parts),
        out_specs=row(d), compiler_params=params, name="ln_sum",
    )(x1, pe, g[None, :], b[None, :], *ff_parts)


def _head_sums(x):
    rows = x.shape[0]
    parts = []
    for h in range(GROUP_HEADS):
        s = jnp.sum(x[:, h * HEAD_DIM:(h + 1) * HEAD_DIM], axis=1, keepdims=True)
        parts.append(jnp.broadcast_to(s, (rows, HEAD_DIM)))
    return jnp.concatenate(parts, axis=1)


def _rwkv_pre_kernel(c_ref, mu_ref, w0_ref, w2_ref, a0_ref, a2_ref, g2_ref, kk_ref, ka_ref,
                     pa_ref, pb_ref, pc_ref, g_ref, *, rc):
    seq = c_ref.shape[0]
    w, p = GROUP_W, HEAD_DIM
    o1 = 3 * w + RWKV_DECAY_RANK
    o2 = o1 + RWKV_ICL_RANK
    nchunks = seq // rc
    blk = rc // RWKV_TB
    row = lax.broadcasted_iota(jnp.int32, (rc, c_ref.shape[1]), 0)
    lane = lax.broadcasted_iota(jnp.int32, (rc, LANES), 1)

    def unit_keys(cols):
        kk = cols[:, w:2 * w] * kk_ref[...]
        return kk * lax.rsqrt(jnp.maximum(_head_sums(kk * kk), 1e-24))

    def pack(dst, c, x, z):
        for hp in range(w // LANES):
            x2, z2 = x[:, hp * LANES:(hp + 1) * LANES], z[:, hp * LANES:(hp + 1) * LANES]
            even = jnp.where(lane < p, x2, pltpu.roll(z2, p, 1))
            odd = jnp.where(lane < p, pltpu.roll(x2, p, 1), z2)
            for h, val in ((2 * hp, even), (2 * hp + 1, odd)):
                dst[c * blk:(c + 1) * blk, h, 0] = val.reshape(blk, RWKV_TB, LANES)

    for c in range(nchunks):
        r0 = c * rc
        cur = c_ref[r0:r0 + rc, :]
        if c == 0:
            prev = jnp.where(row == 0, 0.0, pltpu.roll(cur, 1, 0))
        else:
            prev = c_ref[r0 - 1:r0 - 1 + rc, :]
        if c == nchunks - 1:
            nxt = jnp.where(row == rc - 1, 0.0, pltpu.roll(cur, rc - 1, 0))
        else:
            nxt = c_ref[r0 + 1:r0 + 1 + rc, :]
        cols = cur + (prev - cur) * mu_ref[...]
        cols_next = nxt + (cur - nxt) * mu_ref[...]
        k = cols[:, w:2 * w]
        w_log = -_softplus(-(w0_ref[...] + _bdot(jnp.tanh(cols[:, 3 * w:o1]), w2_ref[...]))) - 0.5
        a = jax.nn.sigmoid(a0_ref[...] + _bdot(cols[:, o1:o2], a2_ref[...]))
        kk = unit_keys(cols)
        pack(pa_ref, c, jnp.exp(-jnp.exp(w_log)), kk * a)
        pack(pb_ref, c, k * (1.0 + (a - 1.0) * ka_ref[...]), cols[:, 0:w])
        pack(pc_ref, c, unit_keys(cols_next), cols[:, 2 * w:3 * w])
        g_ref[r0:r0 + rc, :] = _bdot(jax.nn.sigmoid(cols[:, o2:]), g2_ref[...])


def _rwkv_post_kernel(y_ref, pb_ref, pc_ref, g_ref, rk_ref, lw_ref, lb_ref, o_ref):
    p = HEAD_DIM
    rows = o_ref.shape[0]
    inv = 1.0 / p
    outs = []
    for h in range(GROUP_HEADS):
        hs = slice(h * p, (h + 1) * p)
        y = y_ref[:, h, 0].reshape(rows, LANES)[:, 0:p]
        kr = pb_ref[:, h, 0].reshape(rows, LANES)
        v = pc_ref[:, h, 0].reshape(rows, LANES)[:, p:2 * p]
        yc = y - jnp.sum(y, axis=1, keepdims=True) * inv
        yn = yc * lax.rsqrt(jnp.sum(yc * yc, axis=1, keepdims=True) * inv + RWKV_GN_EPS) * lw_ref[:, hs] + lb_ref[:, hs]
        rk = (kr * pltpu.roll(kr, p, 1))[:, 0:p] * rk_ref[:, hs]
        outs.append((yn + jnp.sum(rk, axis=1, keepdims=True) * v) * g_ref[:, hs])
    o_ref[...] = jnp.concatenate(outs, axis=1)


def _rwkv7_group(pr, bsz, mu, w0, w2, a0, a2, g2, k_k, k_a, r_k, ln_w, ln_b, tm=512):
    t = pr.shape[0]
    seq = t // bsz
    w = GROUP_W
    bf = lambda x: x.astype(jnp.bfloat16)
    const = lambda shape: pl.BlockSpec(shape, lambda b: (0, 0))
    vec = lambda x: x.reshape(1, -1)
    nh, tb = GROUP_HEADS, RWKV_TB
    nblk = seq // tb
    nat = jax.ShapeDtypeStruct((t, w), jnp.float32)
    packed = jax.ShapeDtypeStruct((nblk, nh, bsz, tb, LANES), jnp.float32)
    pspec = pl.BlockSpec((nblk, nh, 1, tb, LANES), lambda b: (0, 0, b, 0, 0))
    pa, pb, pc, g = pl.pallas_call(
        functools.partial(_rwkv_pre_kernel, rc=256),
        out_shape=[packed] * 3 + [nat],
        grid=(bsz,),
        in_specs=[pl.BlockSpec((seq, R_COLS), lambda b: (b, 0)), const((1, R_COLS)), const((1, w)),
                  const((RWKV_DECAY_RANK, w)), const((1, w)), const((RWKV_ICL_RANK, w)),
                  const((RWKV_GATE_RANK, w)), const((1, w)), const((1, w))],
        out_specs=[pspec] * 3 + [pl.BlockSpec((seq, w), lambda b: (b, 0))],
        compiler_params=pltpu.CompilerParams(
            dimension_semantics=("parallel",), vmem_limit_bytes=VMEM_LIMIT),
        name="rwkv_pre",
    )(pr, vec(mu), vec(w0), bf(w2), vec(a0), bf(a2), bf(g2), vec(k_k), vec(k_a))
    y = _rwkv_scan(pa, pb, pc, bsz)
    gb = tm // tb
    per_b = seq // tm
    bspec = pl.BlockSpec((gb, nh, 1, tb, LANES), lambda b, i: (i, 0, b, 0, 0))
    row = pl.BlockSpec((tm, w), lambda b, i: (b * per_b + i, 0))
    const2 = lambda shape: pl.BlockSpec(shape, lambda b, i: (0, 0))
    return pl.pallas_call(
        _rwkv_post_kernel,
        out_shape=nat,
        grid=(bsz, per_b),
        in_specs=[bspec, bspec, bspec, row] + [const2((1, w))] * 3,
        out_specs=row,
        compiler_params=pltpu.CompilerParams(
            dimension_semantics=("parallel", "parallel"), vmem_limit_bytes=VMEM_LIMIT),
        name="rwkv_post",
    )(y, pb, pc, g, vec(r_k), vec(ln_w), vec(ln_b))


FF_CHUNK = 512


def _swiglu_kernel(be_ref, x_ref, w1_ref, w3_ref, w2_ref, g_ref, o_ref):
    del be_ref
    xb = x_ref[...].astype(jnp.bfloat16)
    dff = w1_ref.shape[-1]
    acc = None
    for c0 in range(0, dff, FF_CHUNK):
        c1 = min(c0 + FF_CHUNK, dff)
        h1 = jnp.dot(xb, w1_ref[0, :, c0:c1], preferred_element_type=jnp.float32)
        h3 = jnp.dot(xb, w3_ref[0, :, c0:c1], preferred_element_type=jnp.float32)
        h = (jax.nn.silu(h1) * h3).astype(jnp.bfloat16)
        part = jnp.dot(h, w2_ref[0, c0:c1, :], preferred_element_type=jnp.float32)
        acc = part if acc is None else acc + part
    o_ref[...] = acc * g_ref[...]


def _grouped_swiglu(xrows, block_e, row_gate, w1, w3, w2, bm):
    rows, d = xrows.shape
    dff = w1.shape[-1]
    grid_spec = pltpu.PrefetchScalarGridSpec(
        num_scalar_prefetch=1,
        grid=(rows // bm,),
        in_specs=[pl.BlockSpec((bm, d), lambda i, be: (i, 0)),
                  pl.BlockSpec((1, d, dff), lambda i, be: (be[i], 0, 0)),
                  pl.BlockSpec((1, d, dff), lambda i, be: (be[i], 0, 0)),
                  pl.BlockSpec((1, dff, d), lambda i, be: (be[i], 0, 0)),
                  pl.BlockSpec((bm, 1), lambda i, be: (i, 0))],
        out_specs=pl.BlockSpec((bm, d), lambda i, be: (i, 0)),
    )
    return pl.pallas_call(
        _swiglu_kernel,
        out_shape=jax.ShapeDtypeStruct((rows, d), jnp.float32),
        grid_spec=grid_spec,
        compiler_params=pltpu.CompilerParams(
            dimension_semantics=("arbitrary",), vmem_limit_bytes=VMEM_LIMIT),
        name="swiglu",
    )(block_e, xrows, w1, w3, w2, row_gate)


CAST_SPLIT = 4


def _cast_kernel(*refs):
    *x_refs, o_ref = refs
    rs = x_refs[0].shape[2]
    for q, x_ref in enumerate(x_refs):
        o_ref[0, q * rs:(q + 1) * rs, :] = x_ref[0, 0].astype(o_ref.dtype)


def _layer_weights_bf16(w, layer):
    _, ne, r, c = w.shape
    assert r % (CAST_SPLIT * 16) == 0
    rs = r // CAST_SPLIT
    slab = lambda q: pl.BlockSpec((1, 1, rs, c), lambda e: (layer, e, q, 0))
    return pl.pallas_call(
        _cast_kernel,
        out_shape=jax.ShapeDtypeStruct((ne, r, c), jnp.bfloat16),
        grid=(ne,),
        in_specs=[slab(q) for q in range(CAST_SPLIT)],
        out_specs=pl.BlockSpec((1, r, c), lambda e: (e, 0, 0)),
        compiler_params=pltpu.CompilerParams(
            dimension_semantics=("parallel",), vmem_limit_bytes=VMEM_LIMIT),
        name="cast_bf16",
    )(*([w] * CAST_SPLIT))


def _swiglu(x2d, w1, w3, w2, layer, bm=512):
    rows = x2d.shape[0]
    bf = lambda t: _layer_weights_bf16(t[:, None], layer)
    return _grouped_swiglu(x2d, jnp.zeros((rows // bm,), jnp.int32), jnp.ones((rows, 1), jnp.float32),
                           bf(w1), bf(w3), bf(w2), bm)


def _moe_swiglu(xf, logits, w1, w3, w2, layer):
    T, d = xf.shape
    top_logit, top_e = lax.top_k(logits, TOP_K)
    gate = jax.nn.softmax(top_logit, axis=-1)
    flat_e = top_e.reshape(-1).astype(jnp.int32)
    flat_tok = jnp.repeat(jnp.arange(T, dtype=jnp.int32), TOP_K)
    onehot = (flat_e[:, None] == jnp.arange(N_EXPERTS, dtype=jnp.int32)[None, :]).astype(jnp.int32)
    csum = jnp.cumsum(onehot, axis=0)
    counts = csum[-1]
    rank = jnp.sum(onehot * (csum - 1), axis=1)
    padded = (counts + MOE_BLOCK - 1) // MOE_BLOCK * MOE_BLOCK
    pad_end = jnp.cumsum(padded)
    pad_start = pad_end - padded
    start = jnp.cumsum(counts) - counts
    pos = (pad_start[flat_e] + rank).reshape(T, TOP_K)
    _, st, sg = lax.sort((flat_e, flat_tok, gate.reshape(-1)), num_keys=1, is_stable=True)
    n_blocks = -(-(T * TOP_K) // MOE_BLOCK) + N_EXPERTS
    rows = n_blocks * MOE_BLOCK
    block_e = jnp.minimum(jnp.searchsorted(pad_end, jnp.arange(n_blocks) * MOE_BLOCK, side='right'),
                          N_EXPERTS - 1).astype(jnp.int32)
    row_e = jnp.repeat(block_e, MOE_BLOCK)
    q = jnp.arange(rows, dtype=jnp.int32) - pad_start[row_e]
    valid = q < counts[row_e]
    src = jnp.clip(start[row_e] + q, 0, T * TOP_K - 1)
    slot_tok = jnp.where(valid, st[src], 0)
    slot_gate = jnp.where(valid, sg[src], 0.0)
    bf = lambda t: _layer_weights_bf16(t, layer)
    yb = _grouped_swiglu(xf[slot_tok], block_e, slot_gate[:, None], bf(w1), bf(w3), bf(w2), MOE_BLOCK)
    return [yb[pos[:, 0]], yb[pos[:, 1]]]


def kernel(x, p, w_in, m_i_bias, m_f_bias, m_norm_w, g_conv_w, g_conv_b, g_w_a, g_b_a, g_w_x, g_b_x,
           g_lambda, r_mu, r_w0, r_w2, r_a0, r_a2, r_g2, r_k_k, r_k_a, r_r_k, r_ln_w, r_ln_b,
           s_conv_w, s_conv_b, s_dt_bias, s_a_log, s_d, s_norm_w, w_out, ln1_g, ln1_b, ln2_g, ln2_b,
           f_w1, f_w3, f_w2, e_router, e_w1, e_w3, e_w2, pe_proj, pe_gate_w, pe_gate_b):
    bsz, seq, d = x.shape
    T = bsz * seq
    x = x.reshape(T, d)
    for i in range(DEPTH):
        pm, pg, pr, ps, psm = _inproj(x, _split_w_in(w_in, i))
        mixers = [
            _mlstm_group(pm, psm, bsz, m_i_bias[i], m_f_bias[i], m_norm_w[i]),
            _rglru_group(pg, bsz, g_conv_w[i], g_conv_b[i], g_w_a[i], g_b_a[i], g_w_x[i], g_b_x[i], g_lambda[i]),
            _rwkv7_group(pr, bsz, r_mu[i], r_w0[i], r_w2[i], r_a0[i], r_a2[i], r_g2[i],
                         r_k_k[i], r_k_a[i], r_r_k[i], r_ln_w[i], r_ln_b[i]),
            _ssd_group(ps, psm, bsz, s_conv_w[i], s_conv_b[i], s_dt_bias[i], s_a_log[i], s_d[i], s_norm_w[i]),
        ]
        if i % 2 == 0:
            (x,) = _outproj_ln(x, mixers, w_out[i], ln1_g[i], ln1_b[i])
            ff = [_swiglu(x, f_w1, f_w3, f_w2, i // 2)]
        else:
            x, logits = _outproj_ln(x, mixers, w_out[i], ln1_g[i], ln1_b[i], e_router[i // 2])
            ff = _moe_swiglu(x, logits[:, :N_EXPERTS], e_w1, e_w3, e_w2, i // 2)
        tail = _pe_ln if i % 2 == 0 else _pe_then_ln
        x = tail(x, ff, p.reshape(DEPTH, T, PE_DIM), i, pe_proj[i], pe_gate_w[i], pe_gate_b[i], ln2_g[i], ln2_b[i])
    return x.reshape(bsz, seq, d)
```

```python
import functools

import jax
import jax.numpy as jnp
from jax import lax
from jax.experimental import pallas as pl
from jax.experimental.pallas import tpu as pltpu

D_MODEL = 1024
DEPTH = 4
PE_DIM = 256
GROUP_W = 256
HEAD_DIM = 64
GROUP_HEADS = 4
CONV_K = 4
MLSTM_CHUNK = 64
RG_C = 8.0
RWKV_DECAY_RANK = 32
RWKV_ICL_RANK = 32
RWKV_GATE_RANK = 64
RWKV_GN_EPS = 64e-5
SSD_STATE = 64
SSD_GROUPS = 2
SSD_CHUNK = 128
SSD_XBC = GROUP_W + 2 * SSD_GROUPS * SSD_STATE
M_COLS = 4 * GROUP_W + 2 * GROUP_HEADS
G_COLS = 2 * GROUP_W
R_COLS = 3 * GROUP_W + RWKV_DECAY_RANK + RWKV_ICL_RANK + RWKV_GATE_RANK
S_COLS = GROUP_W + SSD_XBC + GROUP_HEADS
N_IN = M_COLS + G_COLS + R_COLS + S_COLS
N_EXPERTS = 8
TOP_K = 2
MOE_BLOCK = 512
ALPHA = (2 * DEPTH) ** 0.25
LN_EPS = 1e-5
RMS_EPS = 1e-6

LANES = 128
SUBLANES = 8
VMEM_LIMIT = 56 * 1024 * 1024


RWKV_TB = 32


def _rwkv_scan_kernel(pa_ref, pb_ref, pc_ref, y_ref, s_ref, sa_ref, ta_ref, tb_ref, tc_ref, yb_ref,
                      *, tb, nb, nk, nacc):
    @pl.when(pl.program_id(0) == 0)
    def _():
        s_ref[...] = jnp.zeros_like(s_ref)
        sa_ref[...] = jnp.zeros_like(sa_ref)

    nh = GROUP_HEADS
    half = nk // 2
    lane = lax.broadcasted_iota(jnp.int32, (half, LANES), 1)
    rows = lambda t, h: pl.ds(h * nb * tb + t, nb, stride=tb)

    def prepare(t, carry):
        for src, dst in ((pa_ref, ta_ref), (pb_ref, tb_ref), (pc_ref, tc_ref)):
            parts = [src[rows(t, h), :] for h in range(nh)]
            dst[t] = jnp.concatenate(parts + parts, axis=0).T
        return carry

    lax.fori_loop(0, tb, prepare, 0, unroll=16)

    def step(t, sa):
        bc = lambda ref, r: jnp.broadcast_to(ref[t, pl.ds(r, 1), :], (half, LANES))
        v_t = jnp.where(lane < nk, tc_ref[t, nk:nk + half, :], tc_ref[t, nk + half:2 * nk, :])
        y_acc = [None] * nacc
        sa_acc = [None] * nacc
        for k in range(nk):
            s_k = s_ref[k] * bc(ta_ref, k) - sa * bc(ta_ref, nk + k) + v_t * bc(tb_ref, k)
            s_ref[k] = s_k
            y_k = s_k * bc(tb_ref, nk + k)
            n_k = s_k * bc(tc_ref, k)
            a = k % nacc
            y_acc[a] = y_k if y_acc[a] is None else y_acc[a] + y_k
            sa_acc[a] = n_k if sa_acc[a] is None else sa_acc[a] + n_k
        yb_ref[t] = functools.reduce(lambda p, q: p + q, y_acc)
        return functools.reduce(lambda p, q: p + q, sa_acc)

    sa_ref[...] = lax.fori_loop(0, tb, step, sa_ref[...])

    def emit(t, carry):
        y = yb_ref[t]
        y_sw = pltpu.roll(y, nk, 1)
        y_nat = jnp.concatenate([y, y_sw, y, y_sw], axis=0).T
        for h in range(nh):
            y_ref[rows(t, h), :] = y_nat[h * nb:(h + 1) * nb, :]
        return carry

    lax.fori_loop(0, tb, emit, 0, unroll=16)


def _rwkv_scan(pa, pb, pc, bsz):
    nblk, nh, _, tb, _ = pa.shape
    n = HEAD_DIM
    assert 2 * bsz * nh == LANES and 2 * n == LANES
    flat = lambda t: t.reshape(nblk * nh * bsz * tb, LANES)
    spec = pl.BlockSpec((nh * bsz * tb, LANES), lambda i: (i, 0))
    y = pl.pallas_call(
        functools.partial(_rwkv_scan_kernel, tb=tb, nb=bsz, nk=n, nacc=2),
        out_shape=jax.ShapeDtypeStruct((nblk * nh * bsz * tb, LANES), jnp.float32),
        grid=(nblk,),
        in_specs=[spec, spec, spec],
        out_specs=spec,
        scratch_shapes=[pltpu.VMEM((n, n // 2, LANES), jnp.float32), pltpu.VMEM((n // 2, LANES), jnp.float32)]
        + [pltpu.VMEM((tb, LANES, LANES), jnp.float32)] * 3 + [pltpu.VMEM((tb, n // 2, LANES), jnp.float32)],
        compiler_params=pltpu.CompilerParams(
            dimension_semantics=("arbitrary",), vmem_limit_bytes=VMEM_LIMIT),
        name="rwkv_scan",
    )(flat(pa), flat(pb), flat(pc))
    return y.reshape(pa.shape)


def _inproj_kernel(x_ref, *refs):
    n = len(refs) // 2
    xb = x_ref[...].astype(jnp.bfloat16)
    for w_ref, o_ref in zip(refs[:n], refs[n:]):
        o_ref[...] = jnp.dot(xb, w_ref[...], preferred_element_type=jnp.float32)


def _inproj(x, weights, tm=512):
    t, d = x.shape
    return pl.pallas_call(
        _inproj_kernel,
        out_shape=[jax.ShapeDtypeStruct((t, w.shape[1]), jnp.float32) for w in weights],
        grid=(t // tm,),
        in_specs=[pl.BlockSpec((tm, d), lambda i: (i, 0))]
        + [pl.BlockSpec(w.shape, lambda i: (0, 0)) for w in weights],
        out_specs=[pl.BlockSpec((tm, w.shape[1]), lambda i: (i, 0)) for w in weights],
        compiler_params=pltpu.CompilerParams(
            dimension_semantics=("parallel",), vmem_limit_bytes=VMEM_LIMIT),
        name="inproj",
    )(x, *weights)


def _split_w_in_kernel(w_ref, m_ref, g_ref, r_ref, s_ref, sm_ref):
    m0, g0, r0, s0 = 0, M_COLS, M_COLS + G_COLS, M_COLS + G_COLS + R_COLS
    bf = jnp.bfloat16
    m_ref[...] = w_ref[0, :, m0:m0 + 4 * GROUP_W].astype(bf)
    g_ref[...] = w_ref[0, :, g0:g0 + G_COLS].astype(bf)
    r_ref[...] = w_ref[0, :, r0:r0 + R_COLS].astype(bf)
    s_ref[...] = w_ref[0, :, s0:s0 + GROUP_W + SSD_XBC].astype(bf)
    gates = jnp.concatenate([w_ref[0, :, m0 + 4 * GROUP_W:g0], w_ref[0, :, s0 + GROUP_W + SSD_XBC:N_IN]], axis=1)
    pad = jnp.zeros((gates.shape[0], LANES - gates.shape[1]), jnp.float32)
    sm_ref[...] = jnp.concatenate([gates, pad], axis=1).astype(bf)


def _split_w_in(w_in, layer):
    _, d, n = w_in.shape
    widths = (4 * GROUP_W, G_COLS, R_COLS, GROUP_W + SSD_XBC, LANES)
    return pl.pallas_call(
        _split_w_in_kernel,
        out_shape=[jax.ShapeDtypeStruct((d, c), jnp.bfloat16) for c in widths],
        grid=(1,),
        in_specs=[pl.BlockSpec((1, d, n), lambda i: (layer, 0, 0))],
        out_specs=[pl.BlockSpec((d, c), lambda i: (0, 0)) for c in widths],
        compiler_params=pltpu.CompilerParams(vmem_limit_bytes=VMEM_LIMIT),
        name="split_w_in",
    )(w_in)


def _shift_rows(x, s, fill, row):
    return jnp.where(row >= s, pltpu.roll(x, s, 0), fill)


def _rglru_kernel(g_ref, cw_ref, cb_ref, wg_ref, bg_ref, c_ref, o_ref, xpad_ref, a_ref, u_ref, *, rc, unroll):
    seq, w = o_ref.shape
    pad = SUBLANES
    xpad_ref[0:pad, :] = jnp.zeros((pad, w), jnp.float32)
    xpad_ref[pad:, :] = g_ref[:, 0:w]
    cw = cw_ref[...]
    for c in range(seq // rc):
        r0 = c * rc
        xc = cb_ref[...] + cw[CONV_K - 1:CONV_K, :] * g_ref[r0:r0 + rc, 0:w]
        for j in range(1, CONV_K):
            xc = xc + cw[CONV_K - 1 - j:CONV_K - j, :] * xpad_ref[pad + r0 - j:pad + r0 - j + rc, :]
        z = jnp.dot(xc.astype(jnp.bfloat16), wg_ref[...], preferred_element_type=jnp.float32) + bg_ref[...]
        z = jax.nn.sigmoid(z)
        a = jnp.exp(c_ref[...] * z[:, 0:w])
        a_ref[r0:r0 + rc, :] = a
        u_ref[r0:r0 + rc, :] = jnp.sqrt(1.0 - a * a) * (z[:, w:] * xc)

    row = lax.broadcasted_iota(jnp.int32, (SUBLANES, w), 0)

    def tiles(i, h):
        for j in range(unroll):
            r = pl.multiple_of((i * unroll + j) * SUBLANES, SUBLANES)
            a = a_ref[pl.ds(r, SUBLANES), :]
            u = u_ref[pl.ds(r, SUBLANES), :]
            for s in (1, 2, 4):
                u = a * _shift_rows(u, s, 0.0, row) + u
                a = a * _shift_rows(a, s, 1.0, row)
            ht = a * h + u
            o_ref[pl.ds(r, SUBLANES), :] = ht * jax.nn.gelu(g_ref[pl.ds(r, SUBLANES), w:2 * w])
            h = jnp.broadcast_to(ht[SUBLANES - 1:SUBLANES, :], (SUBLANES, w))
        return h

    lax.fori_loop(0, seq // (SUBLANES * unroll), tiles, jnp.zeros((SUBLANES, w), jnp.float32))


def _block_diag(w):
    h, p, _ = w.shape
    eye = jnp.eye(h, dtype=w.dtype)
    return jnp.einsum('hij,hg->higj', w, eye).reshape(h * p, h * p)


def _rglru_group(g_cols, bsz, conv_w, conv_b, w_a, b_a, w_x, b_x, lam):
    t = g_cols.shape[0]
    seq = t // bsz
    w = GROUP_W
    wg = jnp.concatenate([_block_diag(w_a), _block_diag(w_x)], axis=1).astype(jnp.bfloat16)
    bg = jnp.concatenate([b_a, b_x])[None, :]
    cdec = (-RG_C * jax.nn.softplus(-lam))[None, :]
    const = lambda shape: pl.BlockSpec(shape, lambda b: (0, 0))
    return pl.pallas_call(
        functools.partial(_rglru_kernel, rc=256, unroll=8),
        out_shape=jax.ShapeDtypeStruct((t, w), jnp.float32),
        grid=(bsz,),
        in_specs=[pl.BlockSpec((seq, 2 * w), lambda b: (b, 0)), const((CONV_K, w)), const((1, w)),
                  const((w, 2 * w)), const((1, 2 * w)), const((1, w))],
        out_specs=pl.BlockSpec((seq, w), lambda b: (b, 0)),
        scratch_shapes=[pltpu.VMEM((seq + SUBLANES, w), jnp.float32),
                        pltpu.VMEM((seq, w), jnp.float32), pltpu.VMEM((seq, w), jnp.float32)],
        compiler_params=pltpu.CompilerParams(
            dimension_semantics=("parallel",), vmem_limit_bytes=VMEM_LIMIT),
        name="rglru",
    )(g_cols, conv_w, conv_b[None, :], wg, bg, cdec)


MIXER_ROWS_PER_STEP = 2
MIXER_OUT_DTYPE = jnp.bfloat16
_NT = (((1,), (1,)), ((), ()))
_TN = (((0,), (0,)), ((), ()))


def _bdot(a, b, dims=(((1,), (0,)), ((), ()))):
    return lax.dot_general(a.astype(jnp.bfloat16), b.astype(jnp.bfloat16), dims,
                           preferred_element_type=jnp.float32)


def _cumsum_rows(x):
    n = x.shape[0]
    tri = (lax.broadcasted_iota(jnp.int32, (n, n), 0) >= lax.broadcasted_iota(jnp.int32, (n, n), 1))
    return jnp.dot(tri.astype(jnp.float32), x, precision=lax.Precision.HIGHEST,
                   preferred_element_type=jnp.float32)


def _log_sigmoid(x):
    return jnp.minimum(x, 0.0) - jnp.log1p(jnp.exp(-jnp.abs(x)))


def _softplus(x):
    return jnp.maximum(x, 0.0) + jnp.log1p(jnp.exp(-jnp.abs(x)))


def _mlstm_kernel(pm_ref, sm_ref, bias_ref, nw_ref, o_ref, c_ref, n_ref, m_ref):
    _, seq, w = o_ref.shape
    nh, dh, L = GROUP_HEADS, HEAD_DIM, MLSTM_CHUNK
    c_ref[...] = jnp.zeros_like(c_ref)
    n_ref[...] = jnp.zeros_like(n_ref)
    m_ref[...] = jnp.zeros_like(m_ref)
    f32 = jnp.float32
    hi = lax.Precision.HIGHEST
    col = lax.broadcasted_iota(jnp.int32, (L, LANES), 1)
    lrow = lax.broadcasted_iota(jnp.int32, (L, LANES), 0)
    scol = col & (dh - 1)
    lo = col < dh
    lo1 = lo[0:1]
    sq = lax.broadcasted_iota(jnp.int32, (LANES, LANES), 0)
    sr = lax.broadcasted_iota(jnp.int32, (LANES, LANES), 1)
    bd = (sq < dh) == (sr < dh)
    e_row = lax.broadcasted_iota(jnp.int32, (LANES, 2 * w), 0)
    e_col = lax.broadcasted_iota(jnp.int32, (LANES, 2 * w), 1)
    e_head = jnp.where(e_col < w, 0, nh) + ((e_col & (w - 1)) >> (dh.bit_length() - 1))
    spread = (e_row == e_head).astype(f32)
    tri = (lax.broadcasted_iota(jnp.int32, (L, L), 0) >= lax.broadcasted_iota(jnp.int32, (L, L), 1)).astype(f32)

    def seg_max(x, first):
        a = jnp.max(jnp.where(first, x, -jnp.inf), axis=1, keepdims=True)
        b = jnp.max(jnp.where(first, -jnp.inf, x), axis=1, keepdims=True)
        return jnp.where(first, a, b)

    def seg_sum(x):
        a = jnp.sum(jnp.where(lo, x, 0.0), axis=1, keepdims=True)
        b = jnp.sum(jnp.where(lo, 0.0, x), axis=1, keepdims=True)
        return jnp.where(lo, a, b)

    nb = pm_ref.shape[0]
    items = [(b, p) for b in range(nb) for p in range(nh // 2)]
    lanes = lambda p, sec: slice(sec * w + p * LANES, sec * w + (p + 1) * LANES)

    def chunk(c, carry):
        r0 = pl.multiple_of(c * L, L)
        rows = pl.ds(r0, L)
        each = lambda f, *xs: [f(*args) for args in zip(*xs)]
        gts = [sm_ref[b, rows, :] + bias_ref[...] for b in range(nb)]
        gts = [jnp.where(col < nh, t, jnp.where(col < 2 * nh, _log_sigmoid(t), 0.0)) for t in gts]
        rep = [jnp.dot(t, spread, precision=hi, preferred_element_type=f32) for t in gts]
        b_rep = [jnp.dot(tri, t[:, w:], precision=hi, preferred_element_type=f32) for t in rep]
        bc = [b_rep[b][:, lanes(p, 0)] for b, p in items]
        lic = [rep[b][:, lanes(p, 0)] for b, p in items]
        lfc = [rep[b][:, lanes(p, 1)] for b, p in items]
        b_row = [jnp.sum(jnp.where(lrow <= scol, t, 0.0), axis=0, keepdims=True) for t in lfc]
        li_row = [jnp.sum(jnp.where(lrow == scol, t, 0.0), axis=0, keepdims=True) for t in lic]
        g = [t[L - 1:L, :] for t in bc]
        q = [pm_ref[b, rows, lanes(p, 0)] for b, p in items]
        k = [pm_ref[b, rows, lanes(p, 1)] * (dh ** -0.5) for b, p in items]
        v = [pm_ref[b, rows, lanes(p, 2)] for b, p in items]
        cst = [c_ref[b, p] for b, p in items]
        nst = [n_ref[b, p] for b, p in items]
        m = [m_ref[b, p] for b, p in items]
        log_d = each(lambda bc_, br, lr: jnp.where(lrow >= scol, bc_ - br + lr, -jnp.inf), bc, b_row, li_row)
        m_t = each(lambda bc_, m_, ld: jnp.maximum(bc_ + m_, seg_max(ld, lo)), bc, m, log_d)
        k2 = [jnp.concatenate([t, t], axis=0) for t in k]
        qk = each(lambda q_, k2_: jnp.where(lo, _bdot(jnp.where(lo, q_, 0.0), k2_, _NT),
                                            _bdot(jnp.where(lo, 0.0, q_), k2_, _NT)), q, k2)
        s = each(lambda qk_, ld, mt: qk_ * jnp.exp(ld - mt), qk, log_d, m_t)
        inter_w = each(lambda bc_, m_, mt: jnp.exp(bc_ + m_ - mt), bc, m, m_t)
        v_bd = [jnp.where(bd, jnp.concatenate([t, t], axis=0), 0.0) for t in v]
        num = each(lambda s_, vb, iw, q_, c_: _bdot(s_, vb) + iw * _bdot(q_, c_), s, v_bd, inter_w, q, cst)
        den = each(lambda s_, iw, q_, n_: seg_sum(s_) + iw * seg_sum(q_ * n_), s, inter_w, q, nst)
        hh = each(lambda nu, de, mt: nu / jnp.maximum(jnp.abs(de), jnp.exp(-mt)), num, den, m_t)
        ms = [seg_sum(t * t) for t in hh]
        for (b, p), h_, ms_ in zip(items, hh, ms):
            o = pm_ref[b, rows, lanes(p, 3)]
            o_ref[b, rows, lanes(p, 0)] = (h_ * lax.rsqrt(ms_ * (1.0 / dh) + RMS_EPS) * nw_ref[:, lanes(p, 0)]
                                           * jax.nn.sigmoid(o)).astype(o_ref.dtype)
        m_new = each(lambda g_, m_, br, lr: jnp.maximum(g_ + m_, seg_max(g_ - br + lr, lo1)), g, m, b_row, li_row)
        decay = each(lambda g_, m_, mn: jnp.exp(g_ + m_ - mn), g, m, m_new)
        w_col = each(lambda g_, bc_, li_, mn: jnp.exp(g_ - bc_ + li_ - mn), g, bc, lic, m_new)
        upd = each(lambda k_, wc, v_: jnp.where(bd, _bdot(k_, wc * v_, _TN), 0.0), k, w_col, v)
        for i, (b, p) in enumerate(items):
            c_ref[b, p] = decay[i] * cst[i] + upd[i]
            n_ref[b, p] = decay[i] * nst[i] + jnp.sum(w_col[i] * k[i], axis=0, keepdims=True)
            m_ref[b, p] = m_new[i]
        return carry

    lax.fori_loop(0, seq // L, chunk, 0)


def _mlstm_group(pm, psm, bsz, i_bias, f_bias, norm_w):
    t = pm.shape[0]
    seq = t // bsz
    w, nh, dh = GROUP_W, GROUP_HEADS, HEAD_DIM
    bias = jnp.pad(jnp.concatenate([i_bias, f_bias]), (0, LANES - 2 * nh))[None, :]
    const = lambda shape: pl.BlockSpec(shape, lambda b: (0,) * len(shape))
    nb = MIXER_ROWS_PER_STEP
    assert bsz % nb == 0
    seq_block = lambda n: pl.BlockSpec((nb, seq, n), lambda b: (b, 0, 0))
    return pl.pallas_call(
        _mlstm_kernel,
        out_shape=jax.ShapeDtypeStruct((bsz, seq, w), MIXER_OUT_DTYPE),
        grid=(bsz // nb,),
        in_specs=[seq_block(4 * w), seq_block(LANES), const((1, LANES)), const((1, w))],
        out_specs=seq_block(w),
        scratch_shapes=[pltpu.VMEM((nb, nh // 2, LANES, LANES), jnp.float32),
                        pltpu.VMEM((nb, nh // 2, 1, LANES), jnp.float32),
                        pltpu.VMEM((nb, nh // 2, 1, LANES), jnp.float32)],
        compiler_params=pltpu.CompilerParams(
            dimension_semantics=("parallel",), vmem_limit_bytes=VMEM_LIMIT),
        name="mlstm",
    )(pm.reshape(bsz, seq, 4 * w), psm.reshape(bsz, seq, LANES), bias, norm_w[None, :]).reshape(t, w)


def _ssd_kernel(ps_ref, sm_ref, cw_ref, cb_ref, dtb_ref, aneg_ref, dsk_ref, nw_ref, o_ref,
                xbc_ref, st_ref, *, rc):
    nb, seq, w = o_ref.shape
    nh, hp, ng, ns, L = GROUP_HEADS, HEAD_DIM, SSD_GROUPS, SSD_STATE, SSD_CHUNK
    cx = SSD_XBC
    cw = cw_ref[...]
    row = lax.broadcasted_iota(jnp.int32, (rc, cx), 0)
    for b in range(nb):
        for c in range(seq // rc):
            r0 = c * rc
            cur = ps_ref[b, r0:r0 + rc, w:w + cx]
            acc = cb_ref[...] + cw[CONV_K - 1:CONV_K, :] * cur
            for j in range(1, CONV_K):
                if c == 0:
                    sh = jnp.where(row >= j, pltpu.roll(cur, j, 0), 0.0)
                else:
                    sh = ps_ref[b, r0 - j:r0 - j + rc, w:w + cx]
                acc = acc + cw[CONV_K - 1 - j:CONV_K - j, :] * sh
            xbc_ref[b, r0:r0 + rc, :] = jax.nn.silu(acc)
    st_ref[...] = jnp.zeros_like(st_ref)
    causal = lax.broadcasted_iota(jnp.int32, (L, L), 0) >= lax.broadcasted_iota(jnp.int32, (L, L), 1)
    gsz = w // ng
    rep = nh // ng
    groups = [(b, g) for b in range(nb) for g in range(ng)]
    heads = [(b, h) for b in range(nb) for h in range(nh)]

    def chunk(c, carry):
        r0 = pl.multiple_of(c * L, L)
        rows = pl.ds(r0, L)
        each = lambda f, *xs: [f(*args) for args in zip(*xs)]
        dt_all = [_softplus(sm_ref[b, rows, :] + dtb_ref[...]) for b in range(nb)]
        acs_all = [_cumsum_rows(aneg_ref[...] * t) for t in dt_all]
        acs_t = [t.T for t in acs_all]
        bm = {bg: xbc_ref[bg[0], rows, w + bg[1] * ns:w + (bg[1] + 1) * ns] for bg in groups}
        cm = {bg: xbc_ref[bg[0], rows, w + (ng + bg[1]) * ns:w + (ng + bg[1] + 1) * ns] for bg in groups}
        cb = {bg: _bdot(cm[bg], bm[bg], _NT) for bg in groups}
        grp = lambda b, h: (b, h // rep)
        cidx = lambda h: 2 * nh + h
        acs_col = [acs_all[b][:, cidx(h):cidx(h) + 1] for b, h in heads]
        acs_row = [acs_t[b][cidx(h):cidx(h) + 1, :] for b, h in heads]
        dt_col = [dt_all[b][:, cidx(h):cidx(h) + 1] for b, h in heads]
        a_last = [t[L - 1:L, :] for t in acs_col]
        xh = [xbc_ref[b, rows, h * hp:(h + 1) * hp] for b, h in heads]
        xdt = each(lambda x_, d_: x_ * d_, xh, dt_col)
        state = [st_ref[b, h] for b, h in heads]
        scores = [cb[grp(b, h)] * jnp.exp(jnp.where(causal, ac - ar, -jnp.inf))
                  for (b, h), ac, ar in zip(heads, acs_col, acs_row)]
        y_diag = each(_bdot, scores, xdt)
        y_off = [_bdot(cm[grp(b, h)], st, _NT) for (b, h), st in zip(heads, state)]
        ys = [yd + jnp.exp(ac) * yo + x_ * dsk_ref[:, h * hp:(h + 1) * hp]
              for (b, h), yd, ac, yo, x_ in zip(heads, y_diag, acs_col, y_off, xh)]
        upd = [_bdot(xd * jnp.exp(al - ac), bm[grp(b, h)], _TN)
               for (b, h), xd, al, ac in zip(heads, xdt, a_last, acs_col)]
        for i, (b, h) in enumerate(heads):
            st_ref[b, h] = jnp.exp(a_last[i]) * state[i] + upd[i]
        ysd = dict(zip(heads, ys))
        ygs = [jnp.concatenate([ysd[(b, g * rep + j)] for j in range(rep)], axis=1)
               * jax.nn.silu(ps_ref[b, rows, g * gsz:(g + 1) * gsz]) for b, g in groups]
        ms = [jnp.mean(t * t, axis=1, keepdims=True) for t in ygs]
        for (b, g), yg, m_ in zip(groups, ygs, ms):
            o_ref[b, rows, g * gsz:(g + 1) * gsz] = (yg * lax.rsqrt(m_ + RMS_EPS)
                                                     * nw_ref[:, g * gsz:(g + 1) * gsz]).astype(o_ref.dtype)
        return carry

    lax.fori_loop(0, seq // L, chunk, 0)


def _ssd_group(ps, psm, bsz, conv_w, conv_b, dt_bias, a_log, d_skip, norm_w):
    t = ps.shape[0]
    seq = t // bsz
    w, nh = GROUP_W, GROUP_HEADS
    lane_vec = lambda v: jnp.pad(v, (2 * nh, LANES - 3 * nh))[None, :]
    dsk = jnp.repeat(d_skip, HEAD_DIM)[None, :]
    const = lambda shape: pl.BlockSpec(shape, lambda b: (0,) * len(shape))
    nb = MIXER_ROWS_PER_STEP
    rc = 256
    assert bsz % nb == 0 and seq % rc == 0
    seq_block = lambda n: pl.BlockSpec((nb, seq, n), lambda b: (b, 0, 0))
    return pl.pallas_call(
        functools.partial(_ssd_kernel, rc=rc),
        out_shape=jax.ShapeDtypeStruct((bsz, seq, w), MIXER_OUT_DTYPE),
        grid=(bsz // nb,),
        in_specs=[seq_block(w + SSD_XBC), seq_block(LANES),
                  const((CONV_K, SSD_XBC)), const((1, SSD_XBC)), const((1, LANES)), const((1, LANES)),
                  const((1, w)), const((1, w))],
        out_specs=seq_block(w),
        scratch_shapes=[pltpu.VMEM((nb, seq, SSD_XBC), jnp.float32),
                        pltpu.VMEM((nb, nh, HEAD_DIM, SSD_STATE), jnp.float32)],
        compiler_params=pltpu.CompilerParams(
            dimension_semantics=("parallel",), vmem_limit_bytes=VMEM_LIMIT),
        name="ssd",
    )(ps.reshape(bsz, seq, w + SSD_XBC), psm.reshape(bsz, seq, LANES), conv_w, conv_b[None, :], lane_vec(dt_bias),
      lane_vec(-jnp.exp(a_log)), dsk, norm_w[None, :]).reshape(t, w)


def _layer_norm(x, g, b):
    xc = x - jnp.mean(x, -1, keepdims=True)
    var = jnp.mean(xc * xc, -1, keepdims=True)
    return (xc * lax.rsqrt(var + LN_EPS)) * g + b


def _outproj_ln_kernel(x_ref, m0_ref, m1_ref, m2_ref, m3_ref, w_ref, g_ref, b_ref, *rest):
    gw = m0_ref.shape[1]
    acc = ALPHA * x_ref[...]
    for j, m_ref in enumerate((m0_ref, m1_ref, m2_ref, m3_ref)):
        acc = acc + _bdot(m_ref[...], w_ref[j * gw:(j + 1) * gw, :])
    x1 = _layer_norm(acc, g_ref[...], b_ref[...])
    if len(rest) == 1:
        rest[0][...] = x1
    else:
        wr_ref, o_ref, lg_ref = rest
        o_ref[...] = x1
        lg_ref[...] = _bdot(x1, wr_ref[...])


def _outproj_ln(x, mixers, w_out, g, b, w_router=None, tm=512):
    t, d = x.shape
    gw = mixers[0].shape[1]
    row = lambda n: pl.BlockSpec((tm, n), lambda i: (i, 0))
    const = lambda shape: pl.BlockSpec(shape, lambda i: (0, 0))
    in_specs = [row(d)] + [row(gw)] * 4 + [const((d, d)), const((1, d)), const((1, d))]
    args = [x, *mixers, w_out.astype(jnp.bfloat16), g[None, :], b[None, :]]
    out_shape = [jax.ShapeDtypeStruct((t, d), jnp.float32)]
    out_specs = [row(d)]
    if w_router is not None:
        wr = jnp.pad(w_router, ((0, 0), (0, LANES - w_router.shape[1]))).astype(jnp.bfloat16)
        in_specs.append(const((d, LANES)))
        args.append(wr)
        out_shape.append(jax.ShapeDtypeStruct((t, LANES), jnp.float32))
        out_specs.append(row(LANES))
    return pl.pallas_call(
        _outproj_ln_kernel,
        out_shape=out_shape, grid=(t // tm,), in_specs=in_specs, out_specs=out_specs,
        compiler_params=pltpu.CompilerParams(
            dimension_semantics=("parallel",), vmem_limit_bytes=VMEM_LIMIT),
        name="outproj_ln",
    )(*args)


def _pe_ln_kernel(x_ref, p_ref, wp_ref, wg_ref, bg_ref, g_ref, b_ref, *rest):
    *ff_refs, o_ref = rest
    x1 = x_ref[...]
    pe = _bdot(p_ref[0], wp_ref[...]) * jax.nn.sigmoid(_bdot(x1, wg_ref[...]) + bg_ref[...])
    ff = functools.reduce(lambda a, c: a + c, [r[...] for r in ff_refs])
    o_ref[...] = _layer_norm(ALPHA * x1 + ff + pe, g_ref[...], b_ref[...])


def _pe_ln(x1, ff_parts, p, layer, pe_proj, gate_w, gate_b, g, b, tm=512):
    t, d = x1.shape
    pd = p.shape[2]
    row = lambda n: pl.BlockSpec((tm, n), lambda i: (i, 0))
    const = lambda shape: pl.BlockSpec(shape, lambda i: (0, 0))
    return pl.pallas_call(
        _pe_ln_kernel,
        out_shape=jax.ShapeDtypeStruct((t, d), jnp.float32),
        grid=(t // tm,),
        in_specs=[row(d), pl.BlockSpec((1, tm, pd), lambda i: (layer, i, 0)), const((pd, d)), const((d, d)),
                  const((1, d)), const((1, d)), const((1, d))]
        + [row(d)] * len(ff_parts),
        out_specs=row(d),
        compiler_params=pltpu.CompilerParams(
            dimension_semantics=("parallel",), vmem_limit_bytes=VMEM_LIMIT),
        name="pe_ln",
    )(x1, p, pe_proj.astype(jnp.bfloat16), gate_w.astype(jnp.bfloat16), gate_b[None, :], g[None, :], b[None, :],
      *ff_parts)


def _pair_head_sums(x):
    lo = lax.broadcasted_iota(jnp.int32, (x.shape[0], LANES), 1) < HEAD_DIM
    tiles = []
    for hp in range(x.shape[1] // LANES):
        t = x[:, hp * LANES:(hp + 1) * LANES]
        a = jnp.sum(jnp.where(lo, t, 0.0), axis=1, keepdims=True)
        b = jnp.sum(jnp.where(lo, 0.0, t), axis=1, keepdims=True)
        tiles.append(jnp.where(lo, a, b))
    return jnp.concatenate(tiles, axis=1)


def _rwkv_pre_kernel(c_ref, mu_ref, w0_ref, w2_ref, a0_ref, a2_ref, g2_ref, kk_ref, ka_ref, rk_ref,
                     pa_ref, pb_ref, pc_ref, g_ref, bonus_ref, *, rc):
    seq = c_ref.shape[0]
    w, p = GROUP_W, HEAD_DIM
    o1 = 3 * w + RWKV_DECAY_RANK
    o2 = o1 + RWKV_ICL_RANK
    nchunks = seq // rc
    blk = rc // RWKV_TB
    row = lax.broadcasted_iota(jnp.int32, (rc, c_ref.shape[1]), 0)
    lane = lax.broadcasted_iota(jnp.int32, (rc, LANES), 1)

    def unit_keys(cols):
        kk = cols[:, w:2 * w] * kk_ref[...]
        return kk * lax.rsqrt(jnp.maximum(_pair_head_sums(kk * kk), 1e-24))

    def pack(dst, c, x, z):
        for hp in range(w // LANES):
            x2, z2 = x[:, hp * LANES:(hp + 1) * LANES], z[:, hp * LANES:(hp + 1) * LANES]
            even = jnp.where(lane < p, x2, pltpu.roll(z2, p, 1))
            odd = jnp.where(lane < p, pltpu.roll(x2, p, 1), z2)
            for h, val in ((2 * hp, even), (2 * hp + 1, odd)):
                dst[c * blk:(c + 1) * blk, h, 0] = val.reshape(blk, RWKV_TB, LANES)

    for c in range(nchunks):
        r0 = c * rc
        cur = c_ref[r0:r0 + rc, :]
        if c == 0:
            prev = jnp.where(row == 0, 0.0, pltpu.roll(cur, 1, 0))
        else:
            prev = c_ref[r0 - 1:r0 - 1 + rc, :]
        if c == nchunks - 1:
            nxt = jnp.where(row == rc - 1, 0.0, pltpu.roll(cur, rc - 1, 0))
        else:
            nxt = c_ref[r0 + 1:r0 + 1 + rc, :]
        cols = cur + (prev - cur) * mu_ref[...]
        cols_next = nxt + (cur - nxt) * mu_ref[...]
        k = cols[:, w:2 * w]
        w_log = -_softplus(-(w0_ref[...] + _bdot(jnp.tanh(cols[:, 3 * w:o1]), w2_ref[...]))) - 0.5
        a = jax.nn.sigmoid(a0_ref[...] + _bdot(cols[:, o1:o2], a2_ref[...]))
        kk = unit_keys(cols)
        r, v = cols[:, 0:w], cols[:, 2 * w:3 * w]
        k_mod = k * (1.0 + (a - 1.0) * ka_ref[...])
        pack(pa_ref, c, jnp.exp(-jnp.exp(w_log)), kk * a)
        pack(pb_ref, c, k_mod, r)
        pack(pc_ref, c, unit_keys(cols_next), v)
        g_ref[r0:r0 + rc, :] = _bdot(jax.nn.sigmoid(cols[:, o2:]), g2_ref[...])
        bonus_ref[r0:r0 + rc, :] = _pair_head_sums(r * k_mod * rk_ref[...]) * v


def _rwkv_post_kernel(y_ref, bonus_ref, g_ref, lw_ref, lb_ref, o_ref):
    p = HEAD_DIM
    rows = o_ref.shape[0]
    inv = 1.0 / p
    lo = lax.broadcasted_iota(jnp.int32, (rows, LANES), 1) < p
    for hp in range(GROUP_HEADS // 2):
        ls = slice(hp * LANES, (hp + 1) * LANES)
        y = jnp.where(lo, y_ref[:, 2 * hp, 0].reshape(rows, LANES), y_ref[:, 2 * hp + 1, 0].reshape(rows, LANES))
        yc = y - _pair_head_sums(y) * inv
        yn = yc * lax.rsqrt(_pair_head_sums(yc * yc) * inv + RWKV_GN_EPS) * lw_ref[:, ls] + lb_ref[:, ls]
        o_ref[:, ls] = ((yn + bonus_ref[:, ls]) * g_ref[:, ls]).astype(o_ref.dtype)


def _rwkv7_group(pr, bsz, mu, w0, w2, a0, a2, g2, k_k, k_a, r_k, ln_w, ln_b, tm=512):
    t = pr.shape[0]
    seq = t // bsz
    w = GROUP_W
    bf = lambda x: x.astype(jnp.bfloat16)
    const = lambda shape: pl.BlockSpec(shape, lambda b: (0, 0))
    vec = lambda x: x.reshape(1, -1)
    nh, tb = GROUP_HEADS, RWKV_TB
    nblk = seq // tb
    nat = jax.ShapeDtypeStruct((t, w), jnp.float32)
    packed = jax.ShapeDtypeStruct((nblk, nh, bsz, tb, LANES), jnp.float32)
    pspec = pl.BlockSpec((nblk, nh, 1, tb, LANES), lambda b: (0, 0, b, 0, 0))
    rc = 256
    assert seq % rc == 0 and seq % tm == 0 and tm % tb == 0 and rc % tb == 0
    seq_rows = pl.BlockSpec((seq, w), lambda b: (b, 0))
    pa, pb, pc, g, bonus = pl.pallas_call(
        functools.partial(_rwkv_pre_kernel, rc=rc),
        out_shape=[packed] * 3 + [nat] * 2,
        grid=(bsz,),
        in_specs=[pl.BlockSpec((seq, R_COLS), lambda b: (b, 0)), const((1, R_COLS)), const((1, w)),
                  const((RWKV_DECAY_RANK, w)), const((1, w)), const((RWKV_ICL_RANK, w)),
                  const((RWKV_GATE_RANK, w)), const((1, w)), const((1, w)), const((1, w))],
        out_specs=[pspec] * 3 + [seq_rows] * 2,
        compiler_params=pltpu.CompilerParams(
            dimension_semantics=("parallel",), vmem_limit_bytes=VMEM_LIMIT),
        name="rwkv_pre",
    )(pr, vec(mu), vec(w0), bf(w2), vec(a0), bf(a2), bf(g2), vec(k_k), vec(k_a), vec(r_k))
    y = _rwkv_scan(pa, pb, pc, bsz)
    gb = tm // tb
    per_b = seq // tm
    bspec = pl.BlockSpec((gb, nh, 1, tb, LANES), lambda b, i: (i, 0, b, 0, 0))
    row = pl.BlockSpec((tm, w), lambda b, i: (b * per_b + i, 0))
    const2 = lambda shape: pl.BlockSpec(shape, lambda b, i: (0, 0))
    return pl.pallas_call(
        _rwkv_post_kernel,
        out_shape=jax.ShapeDtypeStruct((t, w), MIXER_OUT_DTYPE),
        grid=(bsz, per_b),
        in_specs=[bspec, row, row] + [const2((1, w))] * 2,
        out_specs=row,
        compiler_params=pltpu.CompilerParams(
            dimension_semantics=("parallel", "parallel"), vmem_limit_bytes=VMEM_LIMIT),
        name="rwkv_post",
    )(y, bonus, g, vec(ln_w), vec(ln_b))


FF_CHUNK = 512


def _swiglu_kernel(be_ref, x_ref, w1_ref, w3_ref, w2_ref, g_ref, o_ref):
    del be_ref
    xb = x_ref[...].astype(jnp.bfloat16)
    dff = w1_ref.shape[-1]
    acc = None
    for c0 in range(0, dff, FF_CHUNK):
        c1 = min(c0 + FF_CHUNK, dff)
        h1 = jnp.dot(xb, w1_ref[0, :, c0:c1], preferred_element_type=jnp.float32)
        h3 = jnp.dot(xb, w3_ref[0, :, c0:c1], preferred_element_type=jnp.float32)
        h = (jax.nn.silu(h1) * h3).astype(jnp.bfloat16)
        part = jnp.dot(h, w2_ref[0, c0:c1, :], preferred_element_type=jnp.float32)
        acc = part if acc is None else acc + part
    o_ref[...] = acc * g_ref[...]


def _grouped_swiglu(xrows, block_e, row_gate, w1, w3, w2, bm):
    rows, d = xrows.shape
    dff = w1.shape[-1]
    grid_spec = pltpu.PrefetchScalarGridSpec(
        num_scalar_prefetch=1,
        grid=(rows // bm,),
        in_specs=[pl.BlockSpec((bm, d), lambda i, be: (i, 0)),
                  pl.BlockSpec((1, d, dff), lambda i, be: (be[i], 0, 0)),
                  pl.BlockSpec((1, d, dff), lambda i, be: (be[i], 0, 0)),
                  pl.BlockSpec((1, dff, d), lambda i, be: (be[i], 0, 0)),
                  pl.BlockSpec((bm, 1), lambda i, be: (i, 0))],
        out_specs=pl.BlockSpec((bm, d), lambda i, be: (i, 0)),
    )
    return pl.pallas_call(
        _swiglu_kernel,
        out_shape=jax.ShapeDtypeStruct((rows, d), jnp.float32),
        grid_spec=grid_spec,
        compiler_params=pltpu.CompilerParams(
            dimension_semantics=("arbitrary",), vmem_limit_bytes=VMEM_LIMIT),
        name="swiglu",
    )(block_e, xrows, w1, w3, w2, row_gate)


def _cast_kernel(x_ref, o_ref):
    o_ref[0] = x_ref[0, 0].astype(o_ref.dtype)


def _layer_weights_bf16(w, layer):
    _, ne, r, c = w.shape
    return pl.pallas_call(
        _cast_kernel,
        out_shape=jax.ShapeDtypeStruct((ne, r, c), jnp.bfloat16),
        grid=(ne,),
        in_specs=[pl.BlockSpec((1, 1, r, c), lambda e: (layer, e, 0, 0))],
        out_specs=pl.BlockSpec((1, r, c), lambda e: (e, 0, 0)),
        compiler_params=pltpu.CompilerParams(
            dimension_semantics=("parallel",), vmem_limit_bytes=VMEM_LIMIT),
        name="cast_bf16",
    )(w)


def _swiglu(x2d, w1, w3, w2, layer, bm=512):
    rows = x2d.shape[0]
    bf = lambda t: _layer_weights_bf16(t[:, None], layer)
    return _grouped_swiglu(x2d, jnp.zeros((rows // bm,), jnp.int32), jnp.ones((rows, 1), jnp.float32),
                           bf(w1), bf(w3), bf(w2), bm)


def _moe_swiglu(xf, logits, w1, w3, w2, layer):
    T, d = xf.shape
    top_logit, top_e = lax.top_k(logits, TOP_K)
    gate = jax.nn.softmax(top_logit, axis=-1)
    flat_e = top_e.reshape(-1).astype(jnp.int32)
    flat_tok = jnp.repeat(jnp.arange(T, dtype=jnp.int32), TOP_K)
    onehot = (jnp.arange(N_EXPERTS, dtype=jnp.int32)[:, None] == flat_e[None, :]).astype(jnp.int32)
    csum = jnp.cumsum(onehot, axis=1)
    counts = csum[:, -1]
    rank = jnp.sum(onehot * (csum - 1), axis=0)
    padded = (counts + MOE_BLOCK - 1) // MOE_BLOCK * MOE_BLOCK
    pad_end = jnp.cumsum(padded)
    pad_start = pad_end - padded
    start = jnp.cumsum(counts) - counts
    pos = (pad_start[flat_e] + rank).reshape(T, TOP_K)
    _, st, sg = lax.sort((flat_e, flat_tok, gate.reshape(-1)), num_keys=1, is_stable=True)
    n_blocks = -(-(T * TOP_K) // MOE_BLOCK) + N_EXPERTS
    rows = n_blocks * MOE_BLOCK
    block_e = jnp.minimum(jnp.searchsorted(pad_end, jnp.arange(n_blocks) * MOE_BLOCK, side='right'),
                          N_EXPERTS - 1).astype(jnp.int32)
    row_e = jnp.repeat(block_e, MOE_BLOCK)
    q = jnp.arange(rows, dtype=jnp.int32) - pad_start[row_e]
    valid = q < counts[row_e]
    src = jnp.clip(start[row_e] + q, 0, T * TOP_K - 1)
    slot_tok = jnp.where(valid, st[src], 0)
    slot_gate = jnp.where(valid, sg[src], 0.0)
    bf = lambda t: _layer_weights_bf16(t, layer)
    yb = _grouped_swiglu(xf[slot_tok], block_e, slot_gate[:, None], bf(w1), bf(w3), bf(w2), MOE_BLOCK)
    return [yb[pos[:, 0]], yb[pos[:, 1]]]


def kernel(x, p, w_in, m_i_bias, m_f_bias, m_norm_w, g_conv_w, g_conv_b, g_w_a, g_b_a, g_w_x, g_b_x,
           g_lambda, r_mu, r_w0, r_w2, r_a0, r_a2, r_g2, r_k_k, r_k_a, r_r_k, r_ln_w, r_ln_b,
           s_conv_w, s_conv_b, s_dt_bias, s_a_log, s_d, s_norm_w, w_out, ln1_g, ln1_b, ln2_g, ln2_b,
           f_w1, f_w3, f_w2, e_router, e_w1, e_w3, e_w2, pe_proj, pe_gate_w, pe_gate_b):
    bsz, seq, d = x.shape
    T = bsz * seq
    x = x.reshape(T, d)
    for i in range(DEPTH):
        pm, pg, pr, ps, psm = _inproj(x, _split_w_in(w_in, i))
        mixers = [
            _mlstm_group(pm, psm, bsz, m_i_bias[i], m_f_bias[i], m_norm_w[i]),
            _rglru_group(pg, bsz, g_conv_w[i], g_conv_b[i], g_w_a[i], g_b_a[i], g_w_x[i], g_b_x[i], g_lambda[i]),
            _rwkv7_group(pr, bsz, r_mu[i], r_w0[i], r_w2[i], r_a0[i], r_a2[i], r_g2[i],
                         r_k_k[i], r_k_a[i], r_r_k[i], r_ln_w[i], r_ln_b[i]),
            _ssd_group(ps, psm, bsz, s_conv_w[i], s_conv_b[i], s_dt_bias[i], s_a_log[i], s_d[i], s_norm_w[i]),
        ]
        if i % 2 == 0:
            (x,) = _outproj_ln(x, mixers, w_out[i], ln1_g[i], ln1_b[i])
            ff = [_swiglu(x, f_w1, f_w3, f_w2, i // 2)]
        else:
            x, logits = _outproj_ln(x, mixers, w_out[i], ln1_g[i], ln1_b[i], e_router[i // 2])
            ff = _moe_swiglu(x, logits[:, :N_EXPERTS], e_w1, e_w3, e_w2, i // 2)
        x = _pe_ln(x, ff, p.reshape(DEPTH, T, PE_DIM), i, pe_proj[i], pe_gate_w[i], pe_gate_b[i], ln2_g[i], ln2_b[i])
    return x.reshape(bsz, seq, d)
```

```python
import functools

import jax
import jax.numpy as jnp
from jax import lax
from jax.experimental import pallas as pl
from jax.experimental.pallas import tpu as pltpu

D_MODEL = 1024
DEPTH = 4
PE_DIM = 256
GROUP_W = 256
HEAD_DIM = 64
GROUP_HEADS = 4
CONV_K = 4
MLSTM_CHUNK = 64
RG_C = 8.0
RWKV_DECAY_RANK = 32
RWKV_ICL_RANK = 32
RWKV_GATE_RANK = 64
RWKV_GN_EPS = 64e-5
SSD_STATE = 64
SSD_GROUPS = 2
SSD_CHUNK = 128
SSD_XBC = GROUP_W + 2 * SSD_GROUPS * SSD_STATE
M_COLS = 4 * GROUP_W + 2 * GROUP_HEADS
G_COLS = 2 * GROUP_W
R_COLS = 3 * GROUP_W + RWKV_DECAY_RANK + RWKV_ICL_RANK + RWKV_GATE_RANK
S_COLS = GROUP_W + SSD_XBC + GROUP_HEADS
N_IN = M_COLS + G_COLS + R_COLS + S_COLS
N_EXPERTS = 8
TOP_K = 2
MOE_BLOCK = 512
ALPHA = (2 * DEPTH) ** 0.25
LN_EPS = 1e-5
RMS_EPS = 1e-6

LANES = 128
SUBLANES = 8
VMEM_LIMIT = 56 * 1024 * 1024


RWKV_TB = 32


def _rwkv_scan_kernel(pa_ref, pb_ref, pc_ref, y_ref, s_ref, sa_ref, ta_ref, tb_ref, tc_ref, yb_ref,
                      *, tb, nb, nk, nacc):
    @pl.when(pl.program_id(0) == 0)
    def _():
        s_ref[...] = jnp.zeros_like(s_ref)
        sa_ref[...] = jnp.zeros_like(sa_ref)

    nh = GROUP_HEADS
    half = nk // 2
    lane = lax.broadcasted_iota(jnp.int32, (half, LANES), 1)
    rows = lambda t, h: pl.ds(h * nb * tb + t, nb, stride=tb)

    def prepare(t, carry):
        for src, dst in ((pa_ref, ta_ref), (pb_ref, tb_ref), (pc_ref, tc_ref)):
            parts = [src[rows(t, h), :] for h in range(nh)]
            dst[t] = jnp.concatenate(parts + parts, axis=0).T
        return carry

    lax.fori_loop(0, tb, prepare, 0, unroll=16)

    def step(t, sa):
        bc = lambda ref, r: jnp.broadcast_to(ref[t, pl.ds(r, 1), :], (half, LANES))
        v_t = jnp.where(lane < nk, tc_ref[t, nk:nk + half, :], tc_ref[t, nk + half:2 * nk, :])
        y_acc = [None] * nacc
        sa_acc = [None] * nacc
        for k in range(nk):
            s_k = s_ref[k] * bc(ta_ref, k) - sa * bc(ta_ref, nk + k) + v_t * bc(tb_ref, k)
            s_ref[k] = s_k
            y_k = s_k * bc(tb_ref, nk + k)
            n_k = s_k * bc(tc_ref, k)
            a = k % nacc
            y_acc[a] = y_k if y_acc[a] is None else y_acc[a] + y_k
            sa_acc[a] = n_k if sa_acc[a] is None else sa_acc[a] + n_k
        yb_ref[t] = functools.reduce(lambda p, q: p + q, y_acc)
        return functools.reduce(lambda p, q: p + q, sa_acc)

    sa_ref[...] = lax.fori_loop(0, tb, step, sa_ref[...])

    def emit(t, carry):
        y = yb_ref[t]
        y_sw = pltpu.roll(y, nk, 1)
        y_nat = jnp.concatenate([y, y_sw, y, y_sw], axis=0).T
        for h in range(nh):
            y_ref[rows(t, h), :] = y_nat[h * nb:(h + 1) * nb, :]
        return carry

    lax.fori_loop(0, tb, emit, 0, unroll=16)


def _rwkv_scan(pa, pb, pc, bsz):
    nblk, nh, _, tb, _ = pa.shape
    n = HEAD_DIM
    assert 2 * bsz * nh == LANES and 2 * n == LANES
    flat = lambda t: t.reshape(nblk * nh * bsz * tb, LANES)
    spec = pl.BlockSpec((nh * bsz * tb, LANES), lambda i: (i, 0))
    y = pl.pallas_call(
        functools.partial(_rwkv_scan_kernel, tb=tb, nb=bsz, nk=n, nacc=2),
        out_shape=jax.ShapeDtypeStruct((nblk * nh * bsz * tb, LANES), jnp.float32),
        grid=(nblk,),
        in_specs=[spec, spec, spec],
        out_specs=spec,
        scratch_shapes=[pltpu.VMEM((n, n // 2, LANES), jnp.float32), pltpu.VMEM((n // 2, LANES), jnp.float32)]
        + [pltpu.VMEM((tb, LANES, LANES), jnp.float32)] * 3 + [pltpu.VMEM((tb, n // 2, LANES), jnp.float32)],
        compiler_params=pltpu.CompilerParams(
            dimension_semantics=("arbitrary",), vmem_limit_bytes=VMEM_LIMIT),
        name="rwkv_scan",
    )(flat(pa), flat(pb), flat(pc))
    return y.reshape(pa.shape)


def _inproj_kernel(x_ref, *refs):
    n = len(refs) // 2
    xb = x_ref[...].astype(jnp.bfloat16)
    for w_ref, o_ref in zip(refs[:n], refs[n:]):
        o_ref[...] = jnp.dot(xb, w_ref[...], preferred_element_type=jnp.float32)


def _inproj(x, weights, tm=512):
    t, d = x.shape
    return pl.pallas_call(
        _inproj_kernel,
        out_shape=[jax.ShapeDtypeStruct((t, w.shape[1]), jnp.float32) for w in weights],
        grid=(t // tm,),
        in_specs=[pl.BlockSpec((tm, d), lambda i: (i, 0))]
        + [pl.BlockSpec(w.shape, lambda i: (0, 0)) for w in weights],
        out_specs=[pl.BlockSpec((tm, w.shape[1]), lambda i: (i, 0)) for w in weights],
        compiler_params=pltpu.CompilerParams(
            dimension_semantics=("parallel",), vmem_limit_bytes=VMEM_LIMIT),
        name="inproj",
    )(x, *weights)


def _split_w_in_kernel(w_ref, m_ref, g_ref, r_ref, s_ref, sm_ref):
    m0, g0, r0, s0 = 0, M_COLS, M_COLS + G_COLS, M_COLS + G_COLS + R_COLS
    bf = jnp.bfloat16
    m_ref[...] = w_ref[0, :, m0:m0 + 4 * GROUP_W].astype(bf)
    g_ref[...] = w_ref[0, :, g0:g0 + G_COLS].astype(bf)
    r_ref[...] = w_ref[0, :, r0:r0 + R_COLS].astype(bf)
    s_ref[...] = w_ref[0, :, s0:s0 + GROUP_W + SSD_XBC].astype(bf)
    gates = jnp.concatenate([w_ref[0, :, m0 + 4 * GROUP_W:g0], w_ref[0, :, s0 + GROUP_W + SSD_XBC:N_IN]], axis=1)
    pad = jnp.zeros((gates.shape[0], LANES - gates.shape[1]), jnp.float32)
    sm_ref[...] = jnp.concatenate([gates, pad], axis=1).astype(bf)


def _split_w_in(w_in, layer):
    _, d, n = w_in.shape
    widths = (4 * GROUP_W, G_COLS, R_COLS, GROUP_W + SSD_XBC, LANES)
    return pl.pallas_call(
        _split_w_in_kernel,
        out_shape=[jax.ShapeDtypeStruct((d, c), jnp.bfloat16) for c in widths],
        grid=(1,),
        in_specs=[pl.BlockSpec((1, d, n), lambda i: (layer, 0, 0))],
        out_specs=[pl.BlockSpec((d, c), lambda i: (0, 0)) for c in widths],
        compiler_params=pltpu.CompilerParams(vmem_limit_bytes=VMEM_LIMIT),
        name="split_w_in",
    )(w_in)


def _shift_rows(x, s, fill, row):
    return jnp.where(row >= s, pltpu.roll(x, s, 0), fill)


def _rglru_kernel(g_ref, cw_ref, cb_ref, wg_ref, bg_ref, c_ref, o_ref, xpad_ref, a_ref, u_ref, *, rc, unroll):
    seq, w = o_ref.shape
    pad = SUBLANES
    xpad_ref[0:pad, :] = jnp.zeros((pad, w), jnp.float32)
    xpad_ref[pad:, :] = g_ref[:, 0:w]
    cw = cw_ref[...]
    for c in range(seq // rc):
        r0 = c * rc
        xc = cb_ref[...] + cw[CONV_K - 1:CONV_K, :] * g_ref[r0:r0 + rc, 0:w]
        for j in range(1, CONV_K):
            xc = xc + cw[CONV_K - 1 - j:CONV_K - j, :] * xpad_ref[pad + r0 - j:pad + r0 - j + rc, :]
        z = jnp.dot(xc.astype(jnp.bfloat16), wg_ref[...], preferred_element_type=jnp.float32) + bg_ref[...]
        z = jax.nn.sigmoid(z)
        a = jnp.exp(c_ref[...] * z[:, 0:w])
        a_ref[r0:r0 + rc, :] = a
        u_ref[r0:r0 + rc, :] = jnp.sqrt(1.0 - a * a) * (z[:, w:] * xc)

    row = lax.broadcasted_iota(jnp.int32, (SUBLANES, w), 0)

    def tiles(i, h):
        for j in range(unroll):
            r = pl.multiple_of((i * unroll + j) * SUBLANES, SUBLANES)
            a = a_ref[pl.ds(r, SUBLANES), :]
            u = u_ref[pl.ds(r, SUBLANES), :]
            for s in (1, 2, 4):
                u = a * _shift_rows(u, s, 0.0, row) + u
                a = a * _shift_rows(a, s, 1.0, row)
            ht = a * h + u
            o_ref[pl.ds(r, SUBLANES), :] = ht * jax.nn.gelu(g_ref[pl.ds(r, SUBLANES), w:2 * w])
            h = jnp.broadcast_to(ht[SUBLANES - 1:SUBLANES, :], (SUBLANES, w))
        return h

    lax.fori_loop(0, seq // (SUBLANES * unroll), tiles, jnp.zeros((SUBLANES, w), jnp.float32))


def _block_diag(w):
    h, p, _ = w.shape
    eye = jnp.eye(h, dtype=w.dtype)
    return jnp.einsum('hij,hg->higj', w, eye).reshape(h * p, h * p)


def _rglru_group(g_cols, bsz, conv_w, conv_b, w_a, b_a, w_x, b_x, lam):
    t = g_cols.shape[0]
    seq = t // bsz
    w = GROUP_W
    wg = jnp.concatenate([_block_diag(w_a), _block_diag(w_x)], axis=1).astype(jnp.bfloat16)
    bg = jnp.concatenate([b_a, b_x])[None, :]
    cdec = (-RG_C * jax.nn.softplus(-lam))[None, :]
    const = lambda shape: pl.BlockSpec(shape, lambda b: (0, 0))
    return pl.pallas_call(
        functools.partial(_rglru_kernel, rc=256, unroll=8),
        out_shape=jax.ShapeDtypeStruct((t, w), jnp.float32),
        grid=(bsz,),
        in_specs=[pl.BlockSpec((seq, 2 * w), lambda b: (b, 0)), const((CONV_K, w)), const((1, w)),
                  const((w, 2 * w)), const((1, 2 * w)), const((1, w))],
        out_specs=pl.BlockSpec((seq, w), lambda b: (b, 0)),
        scratch_shapes=[pltpu.VMEM((seq + SUBLANES, w), jnp.float32),
                        pltpu.VMEM((seq, w), jnp.float32), pltpu.VMEM((seq, w), jnp.float32)],
        compiler_params=pltpu.CompilerParams(
            dimension_semantics=("parallel",), vmem_limit_bytes=VMEM_LIMIT),
        name="rglru",
    )(g_cols, conv_w, conv_b[None, :], wg, bg, cdec)


MIXER_ROWS_PER_STEP = 2
MIXER_OUT_DTYPE = jnp.bfloat16
_NT = (((1,), (1,)), ((), ()))
_TN = (((0,), (0,)), ((), ()))


def _bdot(a, b, dims=(((1,), (0,)), ((), ()))):
    return lax.dot_general(a.astype(jnp.bfloat16), b.astype(jnp.bfloat16), dims,
                           preferred_element_type=jnp.float32)


def _cumsum_rows(x):
    n = x.shape[0]
    tri = (lax.broadcasted_iota(jnp.int32, (n, n), 0) >= lax.broadcasted_iota(jnp.int32, (n, n), 1))
    return jnp.dot(tri.astype(jnp.float32), x, precision=lax.Precision.HIGHEST,
                   preferred_element_type=jnp.float32)


def _log_sigmoid(x):
    return jnp.minimum(x, 0.0) - jnp.log1p(jnp.exp(-jnp.abs(x)))


def _softplus(x):
    return jnp.maximum(x, 0.0) + jnp.log1p(jnp.exp(-jnp.abs(x)))


def _mlstm_kernel(pm_ref, sm_ref, bias_ref, nw_ref, o_ref, c_ref, n_ref, m_ref):
    _, seq, w = o_ref.shape
    nh, dh, L = GROUP_HEADS, HEAD_DIM, MLSTM_CHUNK
    c_ref[...] = jnp.zeros_like(c_ref)
    n_ref[...] = jnp.zeros_like(n_ref)
    m_ref[...] = jnp.zeros_like(m_ref)
    f32 = jnp.float32
    hi = lax.Precision.HIGHEST
    col = lax.broadcasted_iota(jnp.int32, (L, LANES), 1)
    lrow = lax.broadcasted_iota(jnp.int32, (L, LANES), 0)
    scol = col & (dh - 1)
    lo = col < dh
    lo1 = lo[0:1]
    sq = lax.broadcasted_iota(jnp.int32, (LANES, LANES), 0)
    sr = lax.broadcasted_iota(jnp.int32, (LANES, LANES), 1)
    bd = (sq < dh) == (sr < dh)
    e_row = lax.broadcasted_iota(jnp.int32, (LANES, 2 * w), 0)
    e_col = lax.broadcasted_iota(jnp.int32, (LANES, 2 * w), 1)
    e_head = jnp.where(e_col < w, 0, nh) + ((e_col & (w - 1)) >> (dh.bit_length() - 1))
    spread = (e_row == e_head).astype(f32)
    tri = (lax.broadcasted_iota(jnp.int32, (L, L), 0) >= lax.broadcasted_iota(jnp.int32, (L, L), 1)).astype(f32)

    def seg_max(x, first):
        a = jnp.max(jnp.where(first, x, -jnp.inf), axis=1, keepdims=True)
        b = jnp.max(jnp.where(first, -jnp.inf, x), axis=1, keepdims=True)
        return jnp.where(first, a, b)

    def seg_sum(x):
        a = jnp.sum(jnp.where(lo, x, 0.0), axis=1, keepdims=True)
        b = jnp.sum(jnp.where(lo, 0.0, x), axis=1, keepdims=True)
        return jnp.where(lo, a, b)

    nb = pm_ref.shape[0]
    items = [(b, p) for b in range(nb) for p in range(nh // 2)]
    lanes = lambda p, sec: slice(sec * w + p * LANES, sec * w + (p + 1) * LANES)

    def chunk(c, carry):
        r0 = pl.multiple_of(c * L, L)
        rows = pl.ds(r0, L)
        each = lambda f, *xs: [f(*args) for args in zip(*xs)]
        gts = [sm_ref[b, rows, :] + bias_ref[...] for b in range(nb)]
        gts = [jnp.where(col < nh, t, jnp.where(col < 2 * nh, _log_sigmoid(t), 0.0)) for t in gts]
        rep = [jnp.dot(t, spread, precision=hi, preferred_element_type=f32) for t in gts]
        b_rep = [jnp.dot(tri, t[:, w:], precision=hi, preferred_element_type=f32) for t in rep]
        bc = [b_rep[b][:, lanes(p, 0)] for b, p in items]
        lic = [rep[b][:, lanes(p, 0)] for b, p in items]
        lfc = [rep[b][:, lanes(p, 1)] for b, p in items]
        b_row = [jnp.sum(jnp.where(lrow <= scol, t, 0.0), axis=0, keepdims=True) for t in lfc]
        li_row = [jnp.sum(jnp.where(lrow == scol, t, 0.0), axis=0, keepdims=True) for t in lic]
        g = [t[L - 1:L, :] for t in bc]
        q = [pm_ref[b, rows, lanes(p, 0)] for b, p in items]
        k = [pm_ref[b, rows, lanes(p, 1)] * (dh ** -0.5) for b, p in items]
        v = [pm_ref[b, rows, lanes(p, 2)] for b, p in items]
        cst = [c_ref[b, p] for b, p in items]
        nst = [n_ref[b, p] for b, p in items]
        m = [m_ref[b, p] for b, p in items]
        log_d = each(lambda bc_, br, lr: jnp.where(lrow >= scol, bc_ - br + lr, -jnp.inf), bc, b_row, li_row)
        m_t = each(lambda bc_, m_, ld: jnp.maximum(bc_ + m_, seg_max(ld, lo)), bc, m, log_d)
        k2 = [jnp.concatenate([t, t], axis=0) for t in k]
        qk = each(lambda q_, k2_: jnp.where(lo, _bdot(jnp.where(lo, q_, 0.0), k2_, _NT),
                                            _bdot(jnp.where(lo, 0.0, q_), k2_, _NT)), q, k2)
        s = each(lambda qk_, ld, mt: qk_ * jnp.exp(ld - mt), qk, log_d, m_t)
        inter_w = each(lambda bc_, m_, mt: jnp.exp(bc_ + m_ - mt), bc, m, m_t)
        v_bd = [jnp.where(bd, jnp.concatenate([t, t], axis=0), 0.0) for t in v]
        num = each(lambda s_, vb, iw, q_, c_: _bdot(s_, vb) + iw * _bdot(q_, c_), s, v_bd, inter_w, q, cst)
        den = each(lambda s_, iw, q_, n_: seg_sum(s_) + iw * seg_sum(q_ * n_), s, inter_w, q, nst)
        hh = each(lambda nu, de, mt: nu / jnp.maximum(jnp.abs(de), jnp.exp(-mt)), num, den, m_t)
        ms = [seg_sum(t * t) for t in hh]
        for (b, p), h_, ms_ in zip(items, hh, ms):
            o = pm_ref[b, rows, lanes(p, 3)]
            o_ref[b, rows, lanes(p, 0)] = (h_ * lax.rsqrt(ms_ * (1.0 / dh) + RMS_EPS) * nw_ref[:, lanes(p, 0)]
                                           * jax.nn.sigmoid(o)).astype(o_ref.dtype)
        m_new = each(lambda g_, m_, br, lr: jnp.maximum(g_ + m_, seg_max(g_ - br + lr, lo1)), g, m, b_row, li_row)
        decay = each(lambda g_, m_, mn: jnp.exp(g_ + m_ - mn), g, m, m_new)
        w_col = each(lambda g_, bc_, li_, mn: jnp.exp(g_ - bc_ + li_ - mn), g, bc, lic, m_new)
        upd = each(lambda k_, wc, v_: jnp.where(bd, _bdot(k_, wc * v_, _TN), 0.0), k, w_col, v)
        for i, (b, p) in enumerate(items):
            c_ref[b, p] = decay[i] * cst[i] + upd[i]
            n_ref[b, p] = decay[i] * nst[i] + jnp.sum(w_col[i] * k[i], axis=0, keepdims=True)
            m_ref[b, p] = m_new[i]
        return carry

    lax.fori_loop(0, seq // L, chunk, 0)


def _mlstm_group(pm, psm, bsz, i_bias, f_bias, norm_w):
    t = pm.shape[0]
    seq = t // bsz
    w, nh, dh = GROUP_W, GROUP_HEADS, HEAD_DIM
    bias = jnp.pad(jnp.concatenate([i_bias, f_bias]), (0, LANES - 2 * nh))[None, :]
    const = lambda shape: pl.BlockSpec(shape, lambda b: (0,) * len(shape))
    nb = MIXER_ROWS_PER_STEP
    assert bsz % nb == 0
    seq_block = lambda n: pl.BlockSpec((nb, seq, n), lambda b: (b, 0, 0))
    return pl.pallas_call(
        _mlstm_kernel,
        out_shape=jax.ShapeDtypeStruct((bsz, seq, w), MIXER_OUT_DTYPE),
        grid=(bsz // nb,),
        in_specs=[seq_block(4 * w), seq_block(LANES), const((1, LANES)), const((1, w))],
        out_specs=seq_block(w),
        scratch_shapes=[pltpu.VMEM((nb, nh // 2, LANES, LANES), jnp.float32),
                        pltpu.VMEM((nb, nh // 2, 1, LANES), jnp.float32),
                        pltpu.VMEM((nb, nh // 2, 1, LANES), jnp.float32)],
        compiler_params=pltpu.CompilerParams(
            dimension_semantics=("parallel",), vmem_limit_bytes=VMEM_LIMIT),
        name="mlstm",
    )(pm.reshape(bsz, seq, 4 * w), psm.reshape(bsz, seq, LANES), bias, norm_w[None, :]).reshape(t, w)


def _ssd_kernel(ps_ref, sm_ref, cw_ref, cb_ref, dtb_ref, aneg_ref, dsk_ref, nw_ref, o_ref,
                xbc_ref, st_ref, *, rc):
    nb, seq, w = o_ref.shape
    nh, hp, ng, ns, L = GROUP_HEADS, HEAD_DIM, SSD_GROUPS, SSD_STATE, SSD_CHUNK
    cx = SSD_XBC
    cw = cw_ref[...]
    row = lax.broadcasted_iota(jnp.int32, (rc, cx), 0)
    for b in range(nb):
        for c in range(seq // rc):
            r0 = c * rc
            cur = ps_ref[b, r0:r0 + rc, w:w + cx]
            acc = cb_ref[...] + cw[CONV_K - 1:CONV_K, :] * cur
            for j in range(1, CONV_K):
                if c == 0:
                    sh = jnp.where(row >= j, pltpu.roll(cur, j, 0), 0.0)
                else:
                    sh = ps_ref[b, r0 - j:r0 - j + rc, w:w + cx]
                acc = acc + cw[CONV_K - 1 - j:CONV_K - j, :] * sh
            xbc_ref[b, r0:r0 + rc, :] = jax.nn.silu(acc)
    st_ref[...] = jnp.zeros_like(st_ref)
    causal = lax.broadcasted_iota(jnp.int32, (L, L), 0) >= lax.broadcasted_iota(jnp.int32, (L, L), 1)
    gsz = w // ng
    rep = nh // ng
    groups = [(b, g) for b in range(nb) for g in range(ng)]
    heads = [(b, h) for b in range(nb) for h in range(nh)]

    def chunk(c, carry):
        r0 = pl.multiple_of(c * L, L)
        rows = pl.ds(r0, L)
        each = lambda f, *xs: [f(*args) for args in zip(*xs)]
        dt_all = [_softplus(sm_ref[b, rows, :] + dtb_ref[...]) for b in range(nb)]
        acs_all = [_cumsum_rows(aneg_ref[...] * t) for t in dt_all]
        acs_t = [t.T for t in acs_all]
        bm = {bg: xbc_ref[bg[0], rows, w + bg[1] * ns:w + (bg[1] + 1) * ns] for bg in groups}
        cm = {bg: xbc_ref[bg[0], rows, w + (ng + bg[1]) * ns:w + (ng + bg[1] + 1) * ns] for bg in groups}
        cb = {bg: _bdot(cm[bg], bm[bg], _NT) for bg in groups}
        grp = lambda b, h: (b, h // rep)
        cidx = lambda h: 2 * nh + h
        acs_col = [acs_all[b][:, cidx(h):cidx(h) + 1] for b, h in heads]
        acs_row = [acs_t[b][cidx(h):cidx(h) + 1, :] for b, h in heads]
        dt_col = [dt_all[b][:, cidx(h):cidx(h) + 1] for b, h in heads]
        a_last = [t[L - 1:L, :] for t in acs_col]
        xh = [xbc_ref[b, rows, h * hp:(h + 1) * hp] for b, h in heads]
        xdt = each(lambda x_, d_: x_ * d_, xh, dt_col)
        state = [st_ref[b, h] for b, h in heads]
        scores = [cb[grp(b, h)] * jnp.exp(jnp.where(causal, ac - ar, -jnp.inf))
                  for (b, h), ac, ar in zip(heads, acs_col, acs_row)]
        y_diag = each(_bdot, scores, xdt)
        y_off = [_bdot(cm[grp(b, h)], st, _NT) for (b, h), st in zip(heads, state)]
        ys = [yd + jnp.exp(ac) * yo + x_ * dsk_ref[:, h * hp:(h + 1) * hp]
              for (b, h), yd, ac, yo, x_ in zip(heads, y_diag, acs_col, y_off, xh)]
        upd = [_bdot(xd * jnp.exp(al - ac), bm[grp(b, h)], _TN)
               for (b, h), xd, al, ac in zip(heads, xdt, a_last, acs_col)]
        for i, (b, h) in enumerate(heads):
            st_ref[b, h] = jnp.exp(a_last[i]) * state[i] + upd[i]
        ysd = dict(zip(heads, ys))
        ygs = [jnp.concatenate([ysd[(b, g * rep + j)] for j in range(rep)], axis=1)
               * jax.nn.silu(ps_ref[b, rows, g * gsz:(g + 1) * gsz]) for b, g in groups]
        ms = [jnp.mean(t * t, axis=1, keepdims=True) for t in ygs]
        for (b, g), yg, m_ in zip(groups, ygs, ms):
            o_ref[b, rows, g * gsz:(g + 1) * gsz] = (yg * lax.rsqrt(m_ + RMS_EPS)
                                                     * nw_ref[:, g * gsz:(g + 1) * gsz]).astype(o_ref.dtype)
        return carry

    lax.fori_loop(0, seq // L, chunk, 0)


def _ssd_group(ps, psm, bsz, conv_w, conv_b, dt_bias, a_log, d_skip, norm_w):
    t = ps.shape[0]
    seq = t // bsz
    w, nh = GROUP_W, GROUP_HEADS
    lane_vec = lambda v: jnp.pad(v, (2 * nh, LANES - 3 * nh))[None, :]
    dsk = jnp.repeat(d_skip, HEAD_DIM)[None, :]
    const = lambda shape: pl.BlockSpec(shape, lambda b: (0,) * len(shape))
    nb = MIXER_ROWS_PER_STEP
    rc = 256
    assert bsz % nb == 0 and seq % rc == 0
    seq_block = lambda n: pl.BlockSpec((nb, seq, n), lambda b: (b, 0, 0))
    return pl.pallas_call(
        functools.partial(_ssd_kernel, rc=rc),
        out_shape=jax.ShapeDtypeStruct((bsz, seq, w), MIXER_OUT_DTYPE),
        grid=(bsz // nb,),
        in_specs=[seq_block(w + SSD_XBC), seq_block(LANES),
                  const((CONV_K, SSD_XBC)), const((1, SSD_XBC)), const((1, LANES)), const((1, LANES)),
                  const((1, w)), const((1, w))],
        out_specs=seq_block(w),
        scratch_shapes=[pltpu.VMEM((nb, seq, SSD_XBC), jnp.float32),
                        pltpu.VMEM((nb, nh, HEAD_DIM, SSD_STATE), jnp.float32)],
        compiler_params=pltpu.CompilerParams(
            dimension_semantics=("parallel",), vmem_limit_bytes=VMEM_LIMIT),
        name="ssd",
    )(ps.reshape(bsz, seq, w + SSD_XBC), psm.reshape(bsz, seq, LANES), conv_w, conv_b[None, :], lane_vec(dt_bias),
      lane_vec(-jnp.exp(a_log)), dsk, norm_w[None, :]).reshape(t, w)


def _layer_norm(x, g, b):
    xc = x - jnp.mean(x, -1, keepdims=True)
    var = jnp.mean(xc * xc, -1, keepdims=True)
    return (xc * lax.rsqrt(var + LN_EPS)) * g + b


def _outproj_ln_kernel(x_ref, m0_ref, m1_ref, m2_ref, m3_ref, w_ref, g_ref, b_ref, *rest):
    gw = m0_ref.shape[1]
    acc = ALPHA * x_ref[...]
    for j, m_ref in enumerate((m0_ref, m1_ref, m2_ref, m3_ref)):
        acc = acc + _bdot(m_ref[...], w_ref[j * gw:(j + 1) * gw, :])
    x1 = _layer_norm(acc, g_ref[...], b_ref[...])
    if len(rest) == 1:
        rest[0][...] = x1
    else:
        wr_ref, o_ref, lg_ref = rest
        o_ref[...] = x1
        lg_ref[...] = _bdot(x1, wr_ref[...])


def _outproj_ln(x, mixers, w_out, g, b, w_router=None, tm=512):
    t, d = x.shape
    gw = mixers[0].shape[1]
    row = lambda n: pl.BlockSpec((tm, n), lambda i: (i, 0))
    const = lambda shape: pl.BlockSpec(shape, lambda i: (0, 0))
    in_specs = [row(d)] + [row(gw)] * 4 + [const((d, d)), const((1, d)), const((1, d))]
    args = [x, *mixers, w_out.astype(jnp.bfloat16), g[None, :], b[None, :]]
    out_shape = [jax.ShapeDtypeStruct((t, d), jnp.float32)]
    out_specs = [row(d)]
    if w_router is not None:
        wr = jnp.pad(w_router, ((0, 0), (0, LANES - w_router.shape[1]))).astype(jnp.bfloat16)
        in_specs.append(const((d, LANES)))
        args.append(wr)
        out_shape.append(jax.ShapeDtypeStruct((t, LANES), jnp.float32))
        out_specs.append(row(LANES))
    return pl.pallas_call(
        _outproj_ln_kernel,
        out_shape=out_shape, grid=(t // tm,), in_specs=in_specs, out_specs=out_specs,
        compiler_params=pltpu.CompilerParams(
            dimension_semantics=("parallel",), vmem_limit_bytes=VMEM_LIMIT),
        name="outproj_ln",
    )(*args)


def _pe_ln_kernel(x_ref, p_ref, wp_ref, wg_ref, bg_ref, g_ref, b_ref, *rest):
    *ff_refs, o_ref = rest
    x1 = x_ref[...]
    pe = _bdot(p_ref[0], wp_ref[...]) * jax.nn.sigmoid(_bdot(x1, wg_ref[...]) + bg_ref[...])
    ff = functools.reduce(lambda a, c: a + c, [r[...] for r in ff_refs])
    o_ref[...] = _layer_norm(ALPHA * x1 + ff + pe, g_ref[...], b_ref[...])


def _pe_ln(x1, ff_parts, p, layer, pe_proj, gate_w, gate_b, g, b, tm=512):
    t, d = x1.shape
    pd = p.shape[2]
    row = lambda n: pl.BlockSpec((tm, n), lambda i: (i, 0))
    const = lambda shape: pl.BlockSpec(shape, lambda i: (0, 0))
    return pl.pallas_call(
        _pe_ln_kernel,
        out_shape=jax.ShapeDtypeStruct((t, d), jnp.float32),
        grid=(t // tm,),
        in_specs=[row(d), pl.BlockSpec((1, tm, pd), lambda i: (layer, i, 0)), const((pd, d)), const((d, d)),
                  const((1, d)), const((1, d)), const((1, d))]
        + [row(d)] * len(ff_parts),
        out_specs=row(d),
        compiler_params=pltpu.CompilerParams(
            dimension_semantics=("parallel",), vmem_limit_bytes=VMEM_LIMIT),
        name="pe_ln",
    )(x1, p, pe_proj.astype(jnp.bfloat16), gate_w.astype(jnp.bfloat16), gate_b[None, :], g[None, :], b[None, :],
      *ff_parts)


def _pair_head_sums(x):
    lo = lax.broadcasted_iota(jnp.int32, (x.shape[0], LANES), 1) < HEAD_DIM
    tiles = []
    for hp in range(x.shape[1] // LANES):
        t = x[:, hp * LANES:(hp + 1) * LANES]
        a = jnp.sum(jnp.where(lo, t, 0.0), axis=1, keepdims=True)
        b = jnp.sum(jnp.where(lo, 0.0, t), axis=1, keepdims=True)
        tiles.append(jnp.where(lo, a, b))
    return jnp.concatenate(tiles, axis=1)


def _rwkv_pre_kernel(c_ref, mu_ref, w0_ref, w2_ref, a0_ref, a2_ref, g2_ref, kk_ref, ka_ref, rk_ref,
                     pa_ref, pb_ref, pc_ref, g_ref, bonus_ref, *, rc):
    seq = c_ref.shape[0]
    w, p = GROUP_W, HEAD_DIM
    o1 = 3 * w + RWKV_DECAY_RANK
    o2 = o1 + RWKV_ICL_RANK
    nchunks = seq // rc
    blk = rc // RWKV_TB
    row = lax.broadcasted_iota(jnp.int32, (rc, c_ref.shape[1]), 0)
    lane = lax.broadcasted_iota(jnp.int32, (rc, LANES), 1)

    def unit_keys(cols):
        kk = cols[:, w:2 * w] * kk_ref[...]
        return kk * lax.rsqrt(jnp.maximum(_pair_head_sums(kk * kk), 1e-24))

    def pack(dst, c, x, z):
        for hp in range(w // LANES):
            x2, z2 = x[:, hp * LANES:(hp + 1) * LANES], z[:, hp * LANES:(hp + 1) * LANES]
            even = jnp.where(lane < p, x2, pltpu.roll(z2, p, 1))
            odd = jnp.where(lane < p, pltpu.roll(x2, p, 1), z2)
            for h, val in ((2 * hp, even), (2 * hp + 1, odd)):
                dst[c * blk:(c + 1) * blk, h, 0] = val.reshape(blk, RWKV_TB, LANES)

    for c in range(nchunks):
        r0 = c * rc
        cur = c_ref[r0:r0 + rc, :]
        if c == 0:
            prev = jnp.where(row == 0, 0.0, pltpu.roll(cur, 1, 0))
        else:
            prev = c_ref[r0 - 1:r0 - 1 + rc, :]
        if c == nchunks - 1:
            nxt = jnp.where(row == rc - 1, 0.0, pltpu.roll(cur, rc - 1, 0))
        else:
            nxt = c_ref[r0 + 1:r0 + 1 + rc, :]
        cols = cur + (prev - cur) * mu_ref[...]
        cols_next = nxt + (cur - nxt) * mu_ref[...]
        k = cols[:, w:2 * w]
        w_log = -_softplus(-(w0_ref[...] + _bdot(jnp.tanh(cols[:, 3 * w:o1]), w2_ref[...]))) - 0.5
        a = jax.nn.sigmoid(a0_ref[...] + _bdot(cols[:, o1:o2], a2_ref[...]))
        kk = unit_keys(cols)
        r, v = cols[:, 0:w], cols[:, 2 * w:3 * w]
        k_mod = k * (1.0 + (a - 1.0) * ka_ref[...])
        pack(pa_ref, c, jnp.exp(-jnp.exp(w_log)), kk * a)
        pack(pb_ref, c, k_mod, r)
        pack(pc_ref, c, unit_keys(cols_next), v)
        g_ref[r0:r0 + rc, :] = _bdot(jax.nn.sigmoid(cols[:, o2:]), g2_ref[...])
        bonus_ref[r0:r0 + rc, :] = _pair_head_sums(r * k_mod * rk_ref[...]) * v


def _rwkv_post_kernel(y_ref, bonus_ref, g_ref, lw_ref, lb_ref, o_ref):
    p = HEAD_DIM
    rows = o_ref.shape[0]
    inv = 1.0 / p
    lo = lax.broadcasted_iota(jnp.int32, (rows, LANES), 1) < p
    for hp in range(GROUP_HEADS // 2):
        ls = slice(hp * LANES, (hp + 1) * LANES)
        y = jnp.where(lo, y_ref[:, 2 * hp, 0].reshape(rows, LANES), y_ref[:, 2 * hp + 1, 0].reshape(rows, LANES))
        yc = y - _pair_head_sums(y) * inv
        yn = yc * lax.rsqrt(_pair_head_sums(yc * yc) * inv + RWKV_GN_EPS) * lw_ref[:, ls] + lb_ref[:, ls]
        o_ref[:, ls] = ((yn + bonus_ref[:, ls]) * g_ref[:, ls]).astype(o_ref.dtype)


def _rwkv7_group(pr, bsz, mu, w0, w2, a0, a2, g2, k_k, k_a, r_k, ln_w, ln_b, tm=512):
    t = pr.shape[0]
    seq = t // bsz
    w = GROUP_W
    bf = lambda x: x.astype(jnp.bfloat16)
    const = lambda shape: pl.BlockSpec(shape, lambda b: (0, 0))
    vec = lambda x: x.reshape(1, -1)
    nh, tb = GROUP_HEADS, RWKV_TB
    nblk = seq // tb
    nat = jax.ShapeDtypeStruct((t, w), jnp.float32)
    packed = jax.ShapeDtypeStruct((nblk, nh, bsz, tb, LANES), jnp.float32)
    pspec = pl.BlockSpec((nblk, nh, 1, tb, LANES), lambda b: (0, 0, b, 0, 0))
    rc = 256
    assert seq % rc == 0 and seq % tm == 0 and tm % tb == 0 and rc % tb == 0
    seq_rows = pl.BlockSpec((seq, w), lambda b: (b, 0))
    pa, pb, pc, g, bonus = pl.pallas_call(
        functools.partial(_rwkv_pre_kernel, rc=rc),
        out_shape=[packed] * 3 + [nat] * 2,
        grid=(bsz,),
        in_specs=[pl.BlockSpec((seq, R_COLS), lambda b: (b, 0)), const((1, R_COLS)), const((1, w)),
                  const((RWKV_DECAY_RANK, w)), const((1, w)), const((RWKV_ICL_RANK, w)),
                  const((RWKV_GATE_RANK, w)), const((1, w)), const((1, w)), const((1, w))],
        out_specs=[pspec] * 3 + [seq_rows] * 2,
        compiler_params=pltpu.CompilerParams(
            dimension_semantics=("parallel",), vmem_limit_bytes=VMEM_LIMIT),
        name="rwkv_pre",
    )(pr, vec(mu), vec(w0), bf(w2), vec(a0), bf(a2), bf(g2), vec(k_k), vec(k_a), vec(r_k))
    y = _rwkv_scan(pa, pb, pc, bsz)
    gb = tm // tb
    per_b = seq // tm
    bspec = pl.BlockSpec((gb, nh, 1, tb, LANES), lambda b, i: (i, 0, b, 0, 0))
    row = pl.BlockSpec((tm, w), lambda b, i: (b * per_b + i, 0))
    const2 = lambda shape: pl.BlockSpec(shape, lambda b, i: (0, 0))
    return pl.pallas_call(
        _rwkv_post_kernel,
        out_shape=jax.ShapeDtypeStruct((t, w), MIXER_OUT_DTYPE),
        grid=(bsz, per_b),
        in_specs=[bspec, row, row] + [const2((1, w))] * 2,
        out_specs=row,
        compiler_params=pltpu.CompilerParams(
            dimension_semantics=("parallel", "parallel"), vmem_limit_bytes=VMEM_LIMIT),
        name="rwkv_post",
    )(y, bonus, g, vec(ln_w), vec(ln_b))


FF_CHUNK = 512
FF_SPLIT = 2


def _swiglu_kernel(be_ref, x_ref, *refs):
    del be_ref
    w_refs, (g_ref, o_ref) = refs[:3 * FF_SPLIT], refs[3 * FF_SPLIT:]
    xb = x_ref[...].astype(jnp.bfloat16)
    acc = None
    for s in range(FF_SPLIT):
        w1_ref, w3_ref, w2_ref = w_refs[s], w_refs[FF_SPLIT + s], w_refs[2 * FF_SPLIT + s]
        dff = w1_ref.shape[-1]
        for c0 in range(0, dff, FF_CHUNK):
            c1 = min(c0 + FF_CHUNK, dff)
            h1 = jnp.dot(xb, w1_ref[0, :, c0:c1], preferred_element_type=jnp.float32)
            h3 = jnp.dot(xb, w3_ref[0, :, c0:c1], preferred_element_type=jnp.float32)
            h = (jax.nn.silu(h1) * h3).astype(jnp.bfloat16)
            part = jnp.dot(h, w2_ref[0, c0:c1, :], preferred_element_type=jnp.float32)
            acc = part if acc is None else acc + part
    o_ref[...] = acc * g_ref[...]


def _grouped_swiglu(xrows, block_e, row_gate, w1, w3, w2, bm):
    rows, d = xrows.shape
    weights = [*w1, *w3, *w2]
    expert = lambda w: pl.BlockSpec((1,) + w.shape[1:], lambda i, be: (be[i], 0, 0))
    grid_spec = pltpu.PrefetchScalarGridSpec(
        num_scalar_prefetch=1,
        grid=(rows // bm,),
        in_specs=[pl.BlockSpec((bm, d), lambda i, be: (i, 0))] + [expert(w) for w in weights]
        + [pl.BlockSpec((bm, 1), lambda i, be: (i, 0))],
        out_specs=pl.BlockSpec((bm, d), lambda i, be: (i, 0)),
    )
    return pl.pallas_call(
        _swiglu_kernel,
        out_shape=jax.ShapeDtypeStruct((rows, d), jnp.float32),
        grid_spec=grid_spec,
        compiler_params=pltpu.CompilerParams(
            dimension_semantics=("arbitrary",), vmem_limit_bytes=VMEM_LIMIT),
        name="swiglu",
    )(block_e, xrows, *weights, row_gate)


def _cast_kernel(x_ref, *o_refs, axis):
    n = x_ref.shape[axis] // len(o_refs)
    for j, o_ref in enumerate(o_refs):
        idx = [0, 0, slice(None), slice(None)]
        idx[axis] = slice(j * n, (j + 1) * n)
        o_ref[0] = x_ref[tuple(idx)].astype(o_ref.dtype)


def _layer_weights_bf16(w, layer, ff_axis):
    _, ne, r, c = w.shape
    shape = [ne, r, c]
    shape[ff_axis - 1] //= FF_SPLIT
    assert shape[ff_axis - 1] % LANES == 0
    part = pl.BlockSpec((1, shape[1], shape[2]), lambda e: (e, 0, 0))
    return pl.pallas_call(
        functools.partial(_cast_kernel, axis=ff_axis),
        out_shape=[jax.ShapeDtypeStruct(tuple(shape), jnp.bfloat16)] * FF_SPLIT,
        grid=(ne,),
        in_specs=[pl.BlockSpec((1, 1, r, c), lambda e: (layer, e, 0, 0))],
        out_specs=[part] * FF_SPLIT,
        compiler_params=pltpu.CompilerParams(
            dimension_semantics=("parallel",), vmem_limit_bytes=VMEM_LIMIT),
        name="cast_bf16",
    )(w)


def _swiglu(x2d, w1, w3, w2, layer, bm=512):
    rows = x2d.shape[0]
    bf = lambda t, ax: _layer_weights_bf16(t[:, None], layer, ax)
    return _grouped_swiglu(x2d, jnp.zeros((rows // bm,), jnp.int32), jnp.ones((rows, 1), jnp.float32),
                           bf(w1, 3), bf(w3, 3), bf(w2, 2), bm)


def _moe_swiglu(xf, logits, w1, w3, w2, layer):
    T, d = xf.shape
    i32 = jnp.int32
    lt = logits.T
    eids = jnp.arange(N_EXPERTS, dtype=i32)[:, None]
    e0 = jnp.argmax(lt, axis=0).astype(i32)
    m0 = jnp.max(lt, axis=0)
    lt1 = jnp.where(eids == e0[None, :], -jnp.inf, lt)
    e1 = jnp.argmax(lt1, axis=0).astype(i32)
    m1 = jnp.max(lt1, axis=0)
    ex = jnp.exp(m1 - m0)
    gates = (1.0 / (1.0 + ex), ex / (1.0 + ex))
    oh = [(eids == e[None, :]).astype(i32) for e in (e0, e1)]
    both = oh[0] + oh[1]
    incl = jnp.cumsum(both, axis=1)
    counts = incl[:, -1]
    padded = (counts + MOE_BLOCK - 1) // MOE_BLOCK * MOE_BLOCK
    pad_end = jnp.cumsum(padded)
    pad_start = pad_end - padded
    start = jnp.cumsum(counts) - counts
    offs = (incl - both) + pad_start[:, None]
    pos = [jnp.sum(o * offs, axis=0) for o in oh]
    tok = jnp.arange(T, dtype=i32)
    _, st, sg = lax.sort((jnp.concatenate(pos), jnp.concatenate([tok, tok]), jnp.concatenate(gates)), num_keys=1)
    n_blocks = -(-(T * TOP_K) // MOE_BLOCK) + N_EXPERTS
    blk = jnp.arange(n_blocks, dtype=i32)
    block_e = jnp.minimum(jnp.searchsorted(pad_end, blk * MOE_BLOCK, side='right'), N_EXPERTS - 1).astype(i32)
    q = (blk * MOE_BLOCK - pad_start[block_e])[:, None] + jnp.arange(MOE_BLOCK, dtype=i32)[None, :]
    valid = q < counts[block_e][:, None]
    src = jnp.clip(start[block_e][:, None] + q, 0, T * TOP_K - 1)
    slot_tok = jnp.where(valid, st[src], 0).reshape(-1)
    slot_gate = jnp.where(valid, sg[src], 0.0).reshape(-1, 1)
    bf = lambda t, ax: _layer_weights_bf16(t, layer, ax)
    yb = _grouped_swiglu(xf[slot_tok], block_e, slot_gate, bf(w1, 3), bf(w3, 3), bf(w2, 2), MOE_BLOCK)
    return [yb[pos[0]], yb[pos[1]]]


def kernel(x, p, w_in, m_i_bias, m_f_bias, m_norm_w, g_conv_w, g_conv_b, g_w_a, g_b_a, g_w_x, g_b_x,
           g_lambda, r_mu, r_w0, r_w2, r_a0, r_a2, r_g2, r_k_k, r_k_a, r_r_k, r_ln_w, r_ln_b,
           s_conv_w, s_conv_b, s_dt_bias, s_a_log, s_d, s_norm_w, w_out, ln1_g, ln1_b, ln2_g, ln2_b,
           f_w1, f_w3, f_w2, e_router, e_w1, e_w3, e_w2, pe_proj, pe_gate_w, pe_gate_b):
    bsz, seq, d = x.shape
    T = bsz * seq
    x = x.reshape(T, d)
    for i in range(DEPTH):
        pm, pg, pr, ps, psm = _inproj(x, _split_w_in(w_in, i))
        mixers = [
            _mlstm_group(pm, psm, bsz, m_i_bias[i], m_f_bias[i], m_norm_w[i]),
            _rglru_group(pg, bsz, g_conv_w[i], g_conv_b[i], g_w_a[i], g_b_a[i], g_w_x[i], g_b_x[i], g_lambda[i]),
            _rwkv7_group(pr, bsz, r_mu[i], r_w0[i], r_w2[i], r_a0[i], r_a2[i], r_g2[i],
                         r_k_k[i], r_k_a[i], r_r_k[i], r_ln_w[i], r_ln_b[i]),
            _ssd_group(ps, psm, bsz, s_conv_w[i], s_conv_b[i], s_dt_bias[i], s_a_log[i], s_d[i], s_norm_w[i]),
        ]
        if i % 2 == 0:
            (x,) = _outproj_ln(x, mixers, w_out[i], ln1_g[i], ln1_b[i])
            ff = [_swiglu(x, f_w1, f_w3, f_w2, i // 2)]
        else:
            x, logits = _outproj_ln(x, mixers, w_out[i], ln1_g[i], ln1_b[i], e_router[i // 2])
            ff = _moe_swiglu(x, logits[:, :N_EXPERTS], e_w1, e_w3, e_w2, i // 2)
        x = _pe_ln(x, ff, p.reshape(DEPTH, T, PE_DIM), i, pe_proj[i], pe_gate_w[i], pe_gate_b[i], ln2_g[i], ln2_b[i])
    return x.reshape(bsz, seq, d)
```

```python
import functools

import jax
import jax.numpy as jnp
from jax import lax
from jax.experimental import pallas as pl
from jax.experimental.pallas import tpu as pltpu

D_MODEL = 1024
DEPTH = 4
PE_DIM = 256
GROUP_W = 256
HEAD_DIM = 64
GROUP_HEADS = 4
CONV_K = 4
MLSTM_CHUNK = 64
RG_C = 8.0
RWKV_DECAY_RANK = 32
RWKV_ICL_RANK = 32
RWKV_GATE_RANK = 64
RWKV_GN_EPS = 64e-5
SSD_STATE = 64
SSD_GROUPS = 2
SSD_CHUNK = 128
SSD_XBC = GROUP_W + 2 * SSD_GROUPS * SSD_STATE
M_COLS = 4 * GROUP_W + 2 * GROUP_HEADS
G_COLS = 2 * GROUP_W
R_COLS = 3 * GROUP_W + RWKV_DECAY_RANK + RWKV_ICL_RANK + RWKV_GATE_RANK
S_COLS = GROUP_W + SSD_XBC + GROUP_HEADS
N_IN = M_COLS + G_COLS + R_COLS + S_COLS
N_EXPERTS = 8
TOP_K = 2
MOE_BLOCK = 512
ALPHA = (2 * DEPTH) ** 0.25
LN_EPS = 1e-5
RMS_EPS = 1e-6

LANES = 128
SUBLANES = 8
VMEM_LIMIT = 56 * 1024 * 1024


RWKV_TB = 32


def _rwkv_scan_kernel(pa_ref, pb_ref, pc_ref, y_ref, s_ref, sa_ref, ta_ref, tb_ref, tc_ref, yb_ref,
                      *, tb, nb, nk, nacc):
    @pl.when(pl.program_id(0) == 0)
    def _():
        s_ref[...] = jnp.zeros_like(s_ref)
        sa_ref[...] = jnp.zeros_like(sa_ref)

    nh = GROUP_HEADS
    half = nk // 2
    lane = lax.broadcasted_iota(jnp.int32, (half, LANES), 1)
    rows = lambda t, h: pl.ds(h * nb * tb + t, nb, stride=tb)

    def prepare(t, carry):
        for src, dst in ((pa_ref, ta_ref), (pb_ref, tb_ref), (pc_ref, tc_ref)):
            parts = [src[rows(t, h), :] for h in range(nh)]
            dst[t] = jnp.concatenate(parts + parts, axis=0).T
        return carry

    lax.fori_loop(0, tb, prepare, 0, unroll=16)

    def step(t, sa):
        bc = lambda ref, r: jnp.broadcast_to(ref[t, pl.ds(r, 1), :], (half, LANES))
        v_t = jnp.where(lane < nk, tc_ref[t, nk:nk + half, :], tc_ref[t, nk + half:2 * nk, :])
        y_acc = [None] * nacc
        sa_acc = [None] * nacc
        for k in range(nk):
            s_k = s_ref[k] * bc(ta_ref, k) - sa * bc(ta_ref, nk + k) + v_t * bc(tb_ref, k)
            s_ref[k] = s_k
            y_k = s_k * bc(tb_ref, nk + k)
            n_k = s_k * bc(tc_ref, k)
            a = k % nacc
            y_acc[a] = y_k if y_acc[a] is None else y_acc[a] + y_k
            sa_acc[a] = n_k if sa_acc[a] is None else sa_acc[a] + n_k
        yb_ref[t] = functools.reduce(lambda p, q: p + q, y_acc)
        return functools.reduce(lambda p, q: p + q, sa_acc)

    sa_ref[...] = lax.fori_loop(0, tb, step, sa_ref[...])

    def emit(t, carry):
        y = yb_ref[t]
        y_sw = pltpu.roll(y, nk, 1)
        y_nat = jnp.concatenate([y, y_sw, y, y_sw], axis=0).T
        for h in range(nh):
            y_ref[rows(t, h), :] = y_nat[h * nb:(h + 1) * nb, :]
        return carry

    lax.fori_loop(0, tb, emit, 0, unroll=16)


def _rwkv_scan(pa, pb, pc, bsz):
    nblk, nh, _, tb, _ = pa.shape
    n = HEAD_DIM
    assert 2 * bsz * nh == LANES and 2 * n == LANES
    flat = lambda t: t.reshape(nblk * nh * bsz * tb, LANES)
    spec = pl.BlockSpec((nh * bsz * tb, LANES), lambda i: (i, 0))
    y = pl.pallas_call(
        functools.partial(_rwkv_scan_kernel, tb=tb, nb=bsz, nk=n, nacc=2),
        out_shape=jax.ShapeDtypeStruct((nblk * nh * bsz * tb, LANES), jnp.float32),
        grid=(nblk,),
        in_specs=[spec, spec, spec],
        out_specs=spec,
        scratch_shapes=[pltpu.VMEM((n, n // 2, LANES), jnp.float32), pltpu.VMEM((n // 2, LANES), jnp.float32)]
        + [pltpu.VMEM((tb, LANES, LANES), jnp.float32)] * 3 + [pltpu.VMEM((tb, n // 2, LANES), jnp.float32)],
        compiler_params=pltpu.CompilerParams(
            dimension_semantics=("arbitrary",), vmem_limit_bytes=VMEM_LIMIT),
        name="rwkv_scan",
    )(flat(pa), flat(pb), flat(pc))
    return y.reshape(pa.shape)


def _inproj_kernel(x_ref, *refs):
    n = len(refs) // 2
    xb = x_ref[...].astype(jnp.bfloat16)
    for w_ref, o_ref in zip(refs[:n], refs[n:]):
        o_ref[...] = jnp.dot(xb, w_ref[...], preferred_element_type=jnp.float32)


def _inproj(x, weights, tm=512):
    t, d = x.shape
    return pl.pallas_call(
        _inproj_kernel,
        out_shape=[jax.ShapeDtypeStruct((t, w.shape[1]), jnp.float32) for w in weights],
        grid=(t // tm,),
        in_specs=[pl.BlockSpec((tm, d), lambda i: (i, 0))]
        + [pl.BlockSpec(w.shape, lambda i: (0, 0)) for w in weights],
        out_specs=[pl.BlockSpec((tm, w.shape[1]), lambda i: (i, 0)) for w in weights],
        compiler_params=pltpu.CompilerParams(
            dimension_semantics=("parallel",), vmem_limit_bytes=VMEM_LIMIT),
        name="inproj",
    )(x, *weights)


def _split_w_in_kernel(w_ref, m_ref, g_ref, r_ref, s_ref, sm_ref):
    m0, g0, r0, s0 = 0, M_COLS, M_COLS + G_COLS, M_COLS + G_COLS + R_COLS
    bf = jnp.bfloat16
    m_ref[...] = w_ref[0, :, m0:m0 + 4 * GROUP_W].astype(bf)
    g_ref[...] = w_ref[0, :, g0:g0 + G_COLS].astype(bf)
    r_ref[...] = w_ref[0, :, r0:r0 + R_COLS].astype(bf)
    s_ref[...] = w_ref[0, :, s0:s0 + GROUP_W + SSD_XBC].astype(bf)
    gates = jnp.concatenate([w_ref[0, :, m0 + 4 * GROUP_W:g0], w_ref[0, :, s0 + GROUP_W + SSD_XBC:N_IN]], axis=1)
    pad = jnp.zeros((gates.shape[0], LANES - gates.shape[1]), jnp.float32)
    sm_ref[...] = jnp.concatenate([gates, pad], axis=1).astype(bf)


def _split_w_in(w_in, layer):
    _, d, n = w_in.shape
    widths = (4 * GROUP_W, G_COLS, R_COLS, GROUP_W + SSD_XBC, LANES)
    return pl.pallas_call(
        _split_w_in_kernel,
        out_shape=[jax.ShapeDtypeStruct((d, c), jnp.bfloat16) for c in widths],
        grid=(1,),
        in_specs=[pl.BlockSpec((1, d, n), lambda i: (layer, 0, 0))],
        out_specs=[pl.BlockSpec((d, c), lambda i: (0, 0)) for c in widths],
        compiler_params=pltpu.CompilerParams(vmem_limit_bytes=VMEM_LIMIT),
        name="split_w_in",
    )(w_in)


def _shift_rows(x, s, fill, row):
    return jnp.where(row >= s, pltpu.roll(x, s, 0), fill)


def _rglru_kernel(g_ref, cw_ref, cb_ref, wg_ref, bg_ref, c_ref, o_ref, xpad_ref, a_ref, u_ref, *, rc, unroll):
    seq, w = o_ref.shape
    pad = SUBLANES
    xpad_ref[0:pad, :] = jnp.zeros((pad, w), jnp.float32)
    xpad_ref[pad:, :] = g_ref[:, 0:w]
    cw = cw_ref[...]
    for c in range(seq // rc):
        r0 = c * rc
        xc = cb_ref[...] + cw[CONV_K - 1:CONV_K, :] * g_ref[r0:r0 + rc, 0:w]
        for j in range(1, CONV_K):
            xc = xc + cw[CONV_K - 1 - j:CONV_K - j, :] * xpad_ref[pad + r0 - j:pad + r0 - j + rc, :]
        z = jnp.dot(xc.astype(jnp.bfloat16), wg_ref[...], preferred_element_type=jnp.float32) + bg_ref[...]
        z = jax.nn.sigmoid(z)
        a = jnp.exp(c_ref[...] * z[:, 0:w])
        a_ref[r0:r0 + rc, :] = a
        u_ref[r0:r0 + rc, :] = jnp.sqrt(1.0 - a * a) * (z[:, w:] * xc)

    row = lax.broadcasted_iota(jnp.int32, (SUBLANES, w), 0)

    def tiles(i, h):
        for j in range(unroll):
            r = pl.multiple_of((i * unroll + j) * SUBLANES, SUBLANES)
            a = a_ref[pl.ds(r, SUBLANES), :]
            u = u_ref[pl.ds(r, SUBLANES), :]
            for s in (1, 2, 4):
                u = a * _shift_rows(u, s, 0.0, row) + u
                a = a * _shift_rows(a, s, 1.0, row)
            ht = a * h + u
            o_ref[pl.ds(r, SUBLANES), :] = ht * jax.nn.gelu(g_ref[pl.ds(r, SUBLANES), w:2 * w])
            h = jnp.broadcast_to(ht[SUBLANES - 1:SUBLANES, :], (SUBLANES, w))
        return h

    lax.fori_loop(0, seq // (SUBLANES * unroll), tiles, jnp.zeros((SUBLANES, w), jnp.float32))


def _block_diag(w):
    h, p, _ = w.shape
    eye = jnp.eye(h, dtype=w.dtype)
    return jnp.einsum('hij,hg->higj', w, eye).reshape(h * p, h * p)


def _rglru_group(g_cols, bsz, conv_w, conv_b, w_a, b_a, w_x, b_x, lam):
    t = g_cols.shape[0]
    seq = t // bsz
    w = GROUP_W
    wg = jnp.concatenate([_block_diag(w_a), _block_diag(w_x)], axis=1).astype(jnp.bfloat16)
    bg = jnp.concatenate([b_a, b_x])[None, :]
    cdec = (-RG_C * jax.nn.softplus(-lam))[None, :]
    const = lambda shape: pl.BlockSpec(shape, lambda b: (0, 0))
    return pl.pallas_call(
        functools.partial(_rglru_kernel, rc=256, unroll=8),
        out_shape=jax.ShapeDtypeStruct((t, w), jnp.float32),
        grid=(bsz,),
        in_specs=[pl.BlockSpec((seq, 2 * w), lambda b: (b, 0)), const((CONV_K, w)), const((1, w)),
                  const((w, 2 * w)), const((1, 2 * w)), const((1, w))],
        out_specs=pl.BlockSpec((seq, w), lambda b: (b, 0)),
        scratch_shapes=[pltpu.VMEM((seq + SUBLANES, w), jnp.float32),
                        pltpu.VMEM((seq, w), jnp.float32), pltpu.VMEM((seq, w), jnp.float32)],
        compiler_params=pltpu.CompilerParams(
            dimension_semantics=("parallel",), vmem_limit_bytes=VMEM_LIMIT),
        name="rglru",
    )(g_cols, conv_w, conv_b[None, :], wg, bg, cdec)


MIXER_ROWS_PER_STEP = 2
MIXER_OUT_DTYPE = jnp.bfloat16
_NT = (((1,), (1,)), ((), ()))
_TN = (((0,), (0,)), ((), ()))


def _bdot(a, b, dims=(((1,), (0,)), ((), ()))):
    return lax.dot_general(a.astype(jnp.bfloat16), b.astype(jnp.bfloat16), dims,
                           preferred_element_type=jnp.float32)


def _cumsum_rows(x):
    n = x.shape[0]
    tri = (lax.broadcasted_iota(jnp.int32, (n, n), 0) >= lax.broadcasted_iota(jnp.int32, (n, n), 1))
    return jnp.dot(tri.astype(jnp.float32), x, precision=lax.Precision.HIGHEST,
                   preferred_element_type=jnp.float32)


def _log_sigmoid(x):
    return jnp.minimum(x, 0.0) - jnp.log1p(jnp.exp(-jnp.abs(x)))


def _softplus(x):
    return jnp.maximum(x, 0.0) + jnp.log1p(jnp.exp(-jnp.abs(x)))


def _mlstm_kernel(pm_ref, sm_ref, bias_ref, nw_ref, o_ref, c_ref, n_ref, m_ref):
    _, seq, w = o_ref.shape
    nh, dh, L = GROUP_HEADS, HEAD_DIM, MLSTM_CHUNK
    c_ref[...] = jnp.zeros_like(c_ref)
    n_ref[...] = jnp.zeros_like(n_ref)
    m_ref[...] = jnp.zeros_like(m_ref)
    f32 = jnp.float32
    hi = lax.Precision.HIGHEST
    col = lax.broadcasted_iota(jnp.int32, (L, LANES), 1)
    lrow = lax.broadcasted_iota(jnp.int32, (L, LANES), 0)
    scol = col & (dh - 1)
    lo = col < dh
    lo1 = lo[0:1]
    sq = lax.broadcasted_iota(jnp.int32, (LANES, LANES), 0)
    sr = lax.broadcasted_iota(jnp.int32, (LANES, LANES), 1)
    bd = (sq < dh) == (sr < dh)
    e_row = lax.broadcasted_iota(jnp.int32, (LANES, 2 * w), 0)
    e_col = lax.broadcasted_iota(jnp.int32, (LANES, 2 * w), 1)
    e_head = jnp.where(e_col < w, 0, nh) + ((e_col & (w - 1)) >> (dh.bit_length() - 1))
    spread = (e_row == e_head).astype(f32)
    tri = (lax.broadcasted_iota(jnp.int32, (L, L), 0) >= lax.broadcasted_iota(jnp.int32, (L, L), 1)).astype(f32)

    def seg_max(x, first):
        a = jnp.max(jnp.where(first, x, -jnp.inf), axis=1, keepdims=True)
        b = jnp.max(jnp.where(first, -jnp.inf, x), axis=1, keepdims=True)
        return jnp.where(first, a, b)

    def seg_sum(x):
        a = jnp.sum(jnp.where(lo, x, 0.0), axis=1, keepdims=True)
        b = jnp.sum(jnp.where(lo, 0.0, x), axis=1, keepdims=True)
        return jnp.where(lo, a, b)

    nb = pm_ref.shape[0]
    items = [(b, p) for b in range(nb) for p in range(nh // 2)]
    lanes = lambda p, sec: slice(sec * w + p * LANES, sec * w + (p + 1) * LANES)

    def chunk(c, carry):
        r0 = pl.multiple_of(c * L, L)
        rows = pl.ds(r0, L)
        each = lambda f, *xs: [f(*args) for args in zip(*xs)]
        gts = [sm_ref[b, rows, :] + bias_ref[...] for b in range(nb)]
        gts = [jnp.where(col < nh, t, jnp.where(col < 2 * nh, _log_sigmoid(t), 0.0)) for t in gts]
        rep = [jnp.dot(t, spread, precision=hi, preferred_element_type=f32) for t in gts]
        b_rep = [jnp.dot(tri, t[:, w:], precision=hi, preferred_element_type=f32) for t in rep]
        bc = [b_rep[b][:, lanes(p, 0)] for b, p in items]
        lic = [rep[b][:, lanes(p, 0)] for b, p in items]
        lfc = [rep[b][:, lanes(p, 1)] for b, p in items]
        b_row = [jnp.sum(jnp.where(lrow <= scol, t, 0.0), axis=0, keepdims=True) for t in lfc]
        li_row = [jnp.sum(jnp.where(lrow == scol, t, 0.0), axis=0, keepdims=True) for t in lic]
        g = [t[L - 1:L, :] for t in bc]
        q = [pm_ref[b, rows, lanes(p, 0)] for b, p in items]
        k = [pm_ref[b, rows, lanes(p, 1)] * (dh ** -0.5) for b, p in items]
        v = [pm_ref[b, rows, lanes(p, 2)] for b, p in items]
        cst = [c_ref[b, p] for b, p in items]
        nst = [n_ref[b, p] for b, p in items]
        m = [m_ref[b, p] for b, p in items]
        log_d = each(lambda bc_, br, lr: jnp.where(lrow >= scol, bc_ - br + lr, -jnp.inf), bc, b_row, li_row)
        m_t = each(lambda bc_, m_, ld: jnp.maximum(bc_ + m_, seg_max(ld, lo)), bc, m, log_d)
        k2 = [jnp.concatenate([t, t], axis=0) for t in k]
        qk = each(lambda q_, k2_: jnp.where(lo, _bdot(jnp.where(lo, q_, 0.0), k2_, _NT),
                                            _bdot(jnp.where(lo, 0.0, q_), k2_, _NT)), q, k2)
        s = each(lambda qk_, ld, mt: qk_ * jnp.exp(ld - mt), qk, log_d, m_t)
        inter_w = each(lambda bc_, m_, mt: jnp.exp(bc_ + m_ - mt), bc, m, m_t)
        v_bd = [jnp.where(bd, jnp.concatenate([t, t], axis=0), 0.0) for t in v]
        num = each(lambda s_, vb, iw, q_, c_: _bdot(s_, vb) + iw * _bdot(q_, c_), s, v_bd, inter_w, q, cst)
        den = each(lambda s_, iw, q_, n_: seg_sum(s_) + iw * seg_sum(q_ * n_), s, inter_w, q, nst)
        hh = each(lambda nu, de, mt: nu / jnp.maximum(jnp.abs(de), jnp.exp(-mt)), num, den, m_t)
        ms = [seg_sum(t * t) for t in hh]
        for (b, p), h_, ms_ in zip(items, hh, ms):
            o = pm_ref[b, rows, lanes(p, 3)]
            o_ref[b, rows, lanes(p, 0)] = (h_ * lax.rsqrt(ms_ * (1.0 / dh) + RMS_EPS) * nw_ref[:, lanes(p, 0)]
                                           * jax.nn.sigmoid(o)).astype(o_ref.dtype)
        m_new = each(lambda g_, m_, br, lr: jnp.maximum(g_ + m_, seg_max(g_ - br + lr, lo1)), g, m, b_row, li_row)
        decay = each(lambda g_, m_, mn: jnp.exp(g_ + m_ - mn), g, m, m_new)
        w_col = each(lambda g_, bc_, li_, mn: jnp.exp(g_ - bc_ + li_ - mn), g, bc, lic, m_new)
        upd = each(lambda k_, wc, v_: jnp.where(bd, _bdot(k_, wc * v_, _TN), 0.0), k, w_col, v)
        for i, (b, p) in enumerate(items):
            c_ref[b, p] = decay[i] * cst[i] + upd[i]
            n_ref[b, p] = decay[i] * nst[i] + jnp.sum(w_col[i] * k[i], axis=0, keepdims=True)
            m_ref[b, p] = m_new[i]
        return carry

    lax.fori_loop(0, seq // L, chunk, 0)


def _mlstm_group(pm, psm, bsz, i_bias, f_bias, norm_w):
    t = pm.shape[0]
    seq = t // bsz
    w, nh, dh = GROUP_W, GROUP_HEADS, HEAD_DIM
    bias = jnp.pad(jnp.concatenate([i_bias, f_bias]), (0, LANES - 2 * nh))[None, :]
    const = lambda shape: pl.BlockSpec(shape, lambda b: (0,) * len(shape))
    nb = MIXER_ROWS_PER_STEP
    assert bsz % nb == 0
    seq_block = lambda n: pl.BlockSpec((nb, seq, n), lambda b: (b, 0, 0))
    return pl.pallas_call(
        _mlstm_kernel,
        out_shape=jax.ShapeDtypeStruct((bsz, seq, w), MIXER_OUT_DTYPE),
        grid=(bsz // nb,),
        in_specs=[seq_block(4 * w), seq_block(LANES), const((1, LANES)), const((1, w))],
        out_specs=seq_block(w),
        scratch_shapes=[pltpu.VMEM((nb, nh // 2, LANES, LANES), jnp.float32),
                        pltpu.VMEM((nb, nh // 2, 1, LANES), jnp.float32),
                        pltpu.VMEM((nb, nh // 2, 1, LANES), jnp.float32)],
        compiler_params=pltpu.CompilerParams(
            dimension_semantics=("parallel",), vmem_limit_bytes=VMEM_LIMIT),
        name="mlstm",
    )(pm.reshape(bsz, seq, 4 * w), psm.reshape(bsz, seq, LANES), bias, norm_w[None, :]).reshape(t, w)


def _ssd_kernel(ps_ref, sm_ref, cw_ref, cb_ref, dtb_ref, aneg_ref, dsk_ref, nw_ref, o_ref,
                xbc_ref, st_ref, *, rc):
    nb, seq, w = o_ref.shape
    nh, hp, ng, ns, L = GROUP_HEADS, HEAD_DIM, SSD_GROUPS, SSD_STATE, SSD_CHUNK
    cx = SSD_XBC
    cw = cw_ref[...]
    row = lax.broadcasted_iota(jnp.int32, (rc, cx), 0)
    for b in range(nb):
        for c in range(seq // rc):
            r0 = c * rc
            cur = ps_ref[b, r0:r0 + rc, w:w + cx]
            acc = cb_ref[...] + cw[CONV_K - 1:CONV_K, :] * cur
            for j in range(1, CONV_K):
                if c == 0:
                    sh = jnp.where(row >= j, pltpu.roll(cur, j, 0), 0.0)
                else:
                    sh = ps_ref[b, r0 - j:r0 - j + rc, w:w + cx]
                acc = acc + cw[CONV_K - 1 - j:CONV_K - j, :] * sh
            xbc_ref[b, r0:r0 + rc, :] = jax.nn.silu(acc)
    st_ref[...] = jnp.zeros_like(st_ref)
    causal = lax.broadcasted_iota(jnp.int32, (L, L), 0) >= lax.broadcasted_iota(jnp.int32, (L, L), 1)
    gsz = w // ng
    rep = nh // ng
    groups = [(b, g) for b in range(nb) for g in range(ng)]
    heads = [(b, h) for b in range(nb) for h in range(nh)]

    def chunk(c, carry):
        r0 = pl.multiple_of(c * L, L)
        rows = pl.ds(r0, L)
        each = lambda f, *xs: [f(*args) for args in zip(*xs)]
        dt_all = [_softplus(sm_ref[b, rows, :] + dtb_ref[...]) for b in range(nb)]
        acs_all = [_cumsum_rows(aneg_ref[...] * t) for t in dt_all]
        acs_t = [t.T for t in acs_all]
        bm = {bg: xbc_ref[bg[0], rows, w + bg[1] * ns:w + (bg[1] + 1) * ns] for bg in groups}
        cm = {bg: xbc_ref[bg[0], rows, w + (ng + bg[1]) * ns:w + (ng + bg[1] + 1) * ns] for bg in groups}
        cb = {bg: _bdot(cm[bg], bm[bg], _NT) for bg in groups}
        grp = lambda b, h: (b, h // rep)
        cidx = lambda h: 2 * nh + h
        acs_col = [acs_all[b][:, cidx(h):cidx(h) + 1] for b, h in heads]
        acs_row = [acs_t[b][cidx(h):cidx(h) + 1, :] for b, h in heads]
        dt_col = [dt_all[b][:, cidx(h):cidx(h) + 1] for b, h in heads]
        a_last = [t[L - 1:L, :] for t in acs_col]
        xh = [xbc_ref[b, rows, h * hp:(h + 1) * hp] for b, h in heads]
        xdt = each(lambda x_, d_: x_ * d_, xh, dt_col)
        state = [st_ref[b, h] for b, h in heads]
        scores = [cb[grp(b, h)] * jnp.exp(jnp.where(causal, ac - ar, -jnp.inf))
                  for (b, h), ac, ar in zip(heads, acs_col, acs_row)]
        y_diag = each(_bdot, scores, xdt)
        y_off = [_bdot(cm[grp(b, h)], st, _NT) for (b, h), st in zip(heads, state)]
        ys = [yd + jnp.exp(ac) * yo + x_ * dsk_ref[:, h * hp:(h + 1) * hp]
              for (b, h), yd, ac, yo, x_ in zip(heads, y_diag, acs_col, y_off, xh)]
        upd = [_bdot(xd * jnp.exp(al - ac), bm[grp(b, h)], _TN)
               for (b, h), xd, al, ac in zip(heads, xdt, a_last, acs_col)]
        for i, (b, h) in enumerate(heads):
            st_ref[b, h] = jnp.exp(a_last[i]) * state[i] + upd[i]
        ysd = dict(zip(heads, ys))
        ygs = [jnp.concatenate([ysd[(b, g * rep + j)] for j in range(rep)], axis=1)
               * jax.nn.silu(ps_ref[b, rows, g * gsz:(g + 1) * gsz]) for b, g in groups]
        ms = [jnp.mean(t * t, axis=1, keepdims=True) for t in ygs]
        for (b, g), yg, m_ in zip(groups, ygs, ms):
            o_ref[b, rows, g * gsz:(g + 1) * gsz] = (yg * lax.rsqrt(m_ + RMS_EPS)
                                                     * nw_ref[:, g * gsz:(g + 1) * gsz]).astype(o_ref.dtype)
        return carry

    lax.fori_loop(0, seq // L, chunk, 0)


def _ssd_group(ps, psm, bsz, conv_w, conv_b, dt_bias, a_log, d_skip, norm_w):
    t = ps.shape[0]
    seq = t // bsz
    w, nh = GROUP_W, GROUP_HEADS
    lane_vec = lambda v: jnp.pad(v, (2 * nh, LANES - 3 * nh))[None, :]
    dsk = jnp.repeat(d_skip, HEAD_DIM)[None, :]
    const = lambda shape: pl.BlockSpec(shape, lambda b: (0,) * len(shape))
    nb = MIXER_ROWS_PER_STEP
    rc = 256
    assert bsz % nb == 0 and seq % rc == 0
    seq_block = lambda n: pl.BlockSpec((nb, seq, n), lambda b: (b, 0, 0))
    return pl.pallas_call(
        functools.partial(_ssd_kernel, rc=rc),
        out_shape=jax.ShapeDtypeStruct((bsz, seq, w), MIXER_OUT_DTYPE),
        grid=(bsz // nb,),
        in_specs=[seq_block(w + SSD_XBC), seq_block(LANES),
                  const((CONV_K, SSD_XBC)), const((1, SSD_XBC)), const((1, LANES)), const((1, LANES)),
                  const((1, w)), const((1, w))],
        out_specs=seq_block(w),
        scratch_shapes=[pltpu.VMEM((nb, seq, SSD_XBC), jnp.float32),
                        pltpu.VMEM((nb, nh, HEAD_DIM, SSD_STATE), jnp.float32)],
        compiler_params=pltpu.CompilerParams(
            dimension_semantics=("parallel",), vmem_limit_bytes=VMEM_LIMIT),
        name="ssd",
    )(ps.reshape(bsz, seq, w + SSD_XBC), psm.reshape(bsz, seq, LANES), conv_w, conv_b[None, :], lane_vec(dt_bias),
      lane_vec(-jnp.exp(a_log)), dsk, norm_w[None, :]).reshape(t, w)


def _layer_norm(x, g, b):
    xc = x - jnp.mean(x, -1, keepdims=True)
    var = jnp.mean(xc * xc, -1, keepdims=True)
    return (xc * lax.rsqrt(var + LN_EPS)) * g + b


def _outproj_ln_kernel(x_ref, m0_ref, m1_ref, m2_ref, m3_ref, w_ref, g_ref, b_ref, *rest):
    gw = m0_ref.shape[1]
    acc = ALPHA * x_ref[...]
    for j, m_ref in enumerate((m0_ref, m1_ref, m2_ref, m3_ref)):
        acc = acc + _bdot(m_ref[...], w_ref[j * gw:(j + 1) * gw, :])
    x1 = _layer_norm(acc, g_ref[...], b_ref[...])
    if len(rest) == 1:
        rest[0][...] = x1
    else:
        wr_ref, o_ref, lg_ref, xb_ref = rest
        o_ref[...] = x1
        xb = x1.astype(jnp.bfloat16)
        xb_ref[...] = xb
        lg_ref[...] = jnp.dot(xb, wr_ref[...], preferred_element_type=jnp.float32)


def _outproj_ln(x, mixers, w_out, g, b, w_router=None, tm=512):
    t, d = x.shape
    gw = mixers[0].shape[1]
    row = lambda n: pl.BlockSpec((tm, n), lambda i: (i, 0))
    const = lambda shape: pl.BlockSpec(shape, lambda i: (0, 0))
    in_specs = [row(d)] + [row(gw)] * 4 + [const((d, d)), const((1, d)), const((1, d))]
    args = [x, *mixers, w_out.astype(jnp.bfloat16), g[None, :], b[None, :]]
    out_shape = [jax.ShapeDtypeStruct((t, d), jnp.float32)]
    out_specs = [row(d)]
    if w_router is not None:
        wr = jnp.pad(w_router, ((0, 0), (0, LANES - w_router.shape[1]))).astype(jnp.bfloat16)
        in_specs.append(const((d, LANES)))
        args.append(wr)
        out_shape += [jax.ShapeDtypeStruct((t, LANES), jnp.float32), jax.ShapeDtypeStruct((t, d), jnp.bfloat16)]
        out_specs += [row(LANES), row(d)]
    return pl.pallas_call(
        _outproj_ln_kernel,
        out_shape=out_shape, grid=(t // tm,), in_specs=in_specs, out_specs=out_specs,
        compiler_params=pltpu.CompilerParams(
            dimension_semantics=("parallel",), vmem_limit_bytes=VMEM_LIMIT),
        name="outproj_ln",
    )(*args)


def _pe_ln_kernel(x_ref, p_ref, wp_ref, wg_ref, bg_ref, g_ref, b_ref, *rest):
    *ff_refs, o_ref = rest
    x1 = x_ref[...]
    pe = _bdot(p_ref[0], wp_ref[...]) * jax.nn.sigmoid(_bdot(x1, wg_ref[...]) + bg_ref[...])
    ff = functools.reduce(lambda a, c: a + c, [r[...] for r in ff_refs])
    o_ref[...] = _layer_norm(ALPHA * x1 + ff + pe, g_ref[...], b_ref[...])


def _pe_ln(x1, ff_parts, p, layer, pe_proj, gate_w, gate_b, g, b, tm=512):
    t, d = x1.shape
    pd = p.shape[2]
    row = lambda n: pl.BlockSpec((tm, n), lambda i: (i, 0))
    const = lambda shape: pl.BlockSpec(shape, lambda i: (0, 0))
    return pl.pallas_call(
        _pe_ln_kernel,
        out_shape=jax.ShapeDtypeStruct((t, d), jnp.float32),
        grid=(t // tm,),
        in_specs=[row(d), pl.BlockSpec((1, tm, pd), lambda i: (layer, i, 0)), const((pd, d)), const((d, d)),
                  const((1, d)), const((1, d)), const((1, d))]
        + [row(d)] * len(ff_parts),
        out_specs=row(d),
        compiler_params=pltpu.CompilerParams(
            dimension_semantics=("parallel",), vmem_limit_bytes=VMEM_LIMIT),
        name="pe_ln",
    )(x1, p, pe_proj.astype(jnp.bfloat16), gate_w.astype(jnp.bfloat16), gate_b[None, :], g[None, :], b[None, :],
      *ff_parts)


def _pair_head_sums(x):
    lo = lax.broadcasted_iota(jnp.int32, (x.shape[0], LANES), 1) < HEAD_DIM
    tiles = []
    for hp in range(x.shape[1] // LANES):
        t = x[:, hp * LANES:(hp + 1) * LANES]
        a = jnp.sum(jnp.where(lo, t, 0.0), axis=1, keepdims=True)
        b = jnp.sum(jnp.where(lo, 0.0, t), axis=1, keepdims=True)
        tiles.append(jnp.where(lo, a, b))
    return jnp.concatenate(tiles, axis=1)


def _rwkv_pre_kernel(c_ref, mu_ref, w0_ref, w2_ref, a0_ref, a2_ref, g2_ref, kk_ref, ka_ref, rk_ref,
                     pa_ref, pb_ref, pc_ref, g_ref, bonus_ref, *, rc):
    seq = c_ref.shape[0]
    w, p = GROUP_W, HEAD_DIM
    o1 = 3 * w + RWKV_DECAY_RANK
    o2 = o1 + RWKV_ICL_RANK
    nchunks = seq // rc
    blk = rc // RWKV_TB
    row = lax.broadcasted_iota(jnp.int32, (rc, c_ref.shape[1]), 0)
    lane = lax.broadcasted_iota(jnp.int32, (rc, LANES), 1)

    def unit_keys(cols):
        kk = cols[:, w:2 * w] * kk_ref[...]
        return kk * lax.rsqrt(jnp.maximum(_pair_head_sums(kk * kk), 1e-24))

    def pack(dst, c, x, z):
        for hp in range(w // LANES):
            x2, z2 = x[:, hp * LANES:(hp + 1) * LANES], z[:, hp * LANES:(hp + 1) * LANES]
            even = jnp.where(lane < p, x2, pltpu.roll(z2, p, 1))
            odd = jnp.where(lane < p, pltpu.roll(x2, p, 1), z2)
            for h, val in ((2 * hp, even), (2 * hp + 1, odd)):
                dst[c * blk:(c + 1) * blk, h, 0] = val.reshape(blk, RWKV_TB, LANES)

    for c in range(nchunks):
        r0 = c * rc
        cur = c_ref[r0:r0 + rc, :]
        if c == 0:
            prev = jnp.where(row == 0, 0.0, pltpu.roll(cur, 1, 0))
        else:
            prev = c_ref[r0 - 1:r0 - 1 + rc, :]
        if c == nchunks - 1:
            nxt = jnp.where(row == rc - 1, 0.0, pltpu.roll(cur, rc - 1, 0))
        else:
            nxt = c_ref[r0 + 1:r0 + 1 + rc, :]
        cols = cur + (prev - cur) * mu_ref[...]
        cols_next = nxt + (cur - nxt) * mu_ref[...]
        k = cols[:, w:2 * w]
        w_log = -_softplus(-(w0_ref[...] + _bdot(jnp.tanh(cols[:, 3 * w:o1]), w2_ref[...]))) - 0.5
        a = jax.nn.sigmoid(a0_ref[...] + _bdot(cols[:, o1:o2], a2_ref[...]))
        kk = unit_keys(cols)
        r, v = cols[:, 0:w], cols[:, 2 * w:3 * w]
        k_mod = k * (1.0 + (a - 1.0) * ka_ref[...])
        pack(pa_ref, c, jnp.exp(-jnp.exp(w_log)), kk * a)
        pack(pb_ref, c, k_mod, r)
        pack(pc_ref, c, unit_keys(cols_next), v)
        g_ref[r0:r0 + rc, :] = _bdot(jax.nn.sigmoid(cols[:, o2:]), g2_ref[...])
        bonus_ref[r0:r0 + rc, :] = _pair_head_sums(r * k_mod * rk_ref[...]) * v


def _rwkv_post_kernel(y_ref, bonus_ref, g_ref, lw_ref, lb_ref, o_ref):
    p = HEAD_DIM
    rows = o_ref.shape[0]
    inv = 1.0 / p
    lo = lax.broadcasted_iota(jnp.int32, (rows, LANES), 1) < p
    for hp in range(GROUP_HEADS // 2):
        ls = slice(hp * LANES, (hp + 1) * LANES)
        y = jnp.where(lo, y_ref[:, 2 * hp, 0].reshape(rows, LANES), y_ref[:, 2 * hp + 1, 0].reshape(rows, LANES))
        yc = y - _pair_head_sums(y) * inv
        yn = yc * lax.rsqrt(_pair_head_sums(yc * yc) * inv + RWKV_GN_EPS) * lw_ref[:, ls] + lb_ref[:, ls]
        o_ref[:, ls] = ((yn + bonus_ref[:, ls]) * g_ref[:, ls]).astype(o_ref.dtype)


def _rwkv7_group(pr, bsz, mu, w0, w2, a0, a2, g2, k_k, k_a, r_k, ln_w, ln_b, tm=512):
    t = pr.shape[0]
    seq = t // bsz
    w = GROUP_W
    bf = lambda x: x.astype(jnp.bfloat16)
    const = lambda shape: pl.BlockSpec(shape, lambda b: (0, 0))
    vec = lambda x: x.reshape(1, -1)
    nh, tb = GROUP_HEADS, RWKV_TB
    nblk = seq // tb
    nat = jax.ShapeDtypeStruct((t, w), jnp.float32)
    packed = jax.ShapeDtypeStruct((nblk, nh, bsz, tb, LANES), jnp.float32)
    pspec = pl.BlockSpec((nblk, nh, 1, tb, LANES), lambda b: (0, 0, b, 0, 0))
    rc = 256
    assert seq % rc == 0 and seq % tm == 0 and tm % tb == 0 and rc % tb == 0
    seq_rows = pl.BlockSpec((seq, w), lambda b: (b, 0))
    pa, pb, pc, g, bonus = pl.pallas_call(
        functools.partial(_rwkv_pre_kernel, rc=rc),
        out_shape=[packed] * 3 + [nat] * 2,
        grid=(bsz,),
        in_specs=[pl.BlockSpec((seq, R_COLS), lambda b: (b, 0)), const((1, R_COLS)), const((1, w)),
                  const((RWKV_DECAY_RANK, w)), const((1, w)), const((RWKV_ICL_RANK, w)),
                  const((RWKV_GATE_RANK, w)), const((1, w)), const((1, w)), const((1, w))],
        out_specs=[pspec] * 3 + [seq_rows] * 2,
        compiler_params=pltpu.CompilerParams(
            dimension_semantics=("parallel",), vmem_limit_bytes=VMEM_LIMIT),
        name="rwkv_pre",
    )(pr, vec(mu), vec(w0), bf(w2), vec(a0), bf(a2), bf(g2), vec(k_k), vec(k_a), vec(r_k))
    y = _rwkv_scan(pa, pb, pc, bsz)
    gb = tm // tb
    per_b = seq // tm
    bspec = pl.BlockSpec((gb, nh, 1, tb, LANES), lambda b, i: (i, 0, b, 0, 0))
    row = pl.BlockSpec((tm, w), lambda b, i: (b * per_b + i, 0))
    const2 = lambda shape: pl.BlockSpec(shape, lambda b, i: (0, 0))
    return pl.pallas_call(
        _rwkv_post_kernel,
        out_shape=jax.ShapeDtypeStruct((t, w), MIXER_OUT_DTYPE),
        grid=(bsz, per_b),
        in_specs=[bspec, row, row] + [const2((1, w))] * 2,
        out_specs=row,
        compiler_params=pltpu.CompilerParams(
            dimension_semantics=("parallel", "parallel"), vmem_limit_bytes=VMEM_LIMIT),
        name="rwkv_post",
    )(y, bonus, g, vec(ln_w), vec(ln_b))


FF_CHUNK = 512


def _swiglu_kernel(be_ref, x_ref, w1_ref, w3_ref, w2_ref, g_ref, o_ref):
    del be_ref
    xb = x_ref[...].astype(jnp.bfloat16)
    dff = w1_ref.shape[-1]
    acc = None
    for c0 in range(0, dff, FF_CHUNK):
        c1 = min(c0 + FF_CHUNK, dff)
        h1 = jnp.dot(xb, w1_ref[0, :, c0:c1], preferred_element_type=jnp.float32)
        h3 = jnp.dot(xb, w3_ref[0, :, c0:c1], preferred_element_type=jnp.float32)
        h = (jax.nn.silu(h1) * h3).astype(jnp.bfloat16)
        part = jnp.dot(h, w2_ref[0, c0:c1, :], preferred_element_type=jnp.float32)
        acc = part if acc is None else acc + part
    o_ref[...] = acc * g_ref[...]


def _grouped_swiglu(xrows, block_e, row_gate, w1, w3, w2, bm):
    rows, d = xrows.shape
    dff = w1.shape[-1]
    grid_spec = pltpu.PrefetchScalarGridSpec(
        num_scalar_prefetch=1,
        grid=(rows // bm,),
        in_specs=[pl.BlockSpec((bm, d), lambda i, be: (i, 0)),
                  pl.BlockSpec((1, d, dff), lambda i, be: (be[i], 0, 0)),
                  pl.BlockSpec((1, d, dff), lambda i, be: (be[i], 0, 0)),
                  pl.BlockSpec((1, dff, d), lambda i, be: (be[i], 0, 0)),
                  pl.BlockSpec((bm, 1), lambda i, be: (i, 0))],
        out_specs=pl.BlockSpec((bm, d), lambda i, be: (i, 0)),
    )
    return pl.pallas_call(
        _swiglu_kernel,
        out_shape=jax.ShapeDtypeStruct((rows, d), jnp.float32),
        grid_spec=grid_spec,
        compiler_params=pltpu.CompilerParams(
            dimension_semantics=("arbitrary",), vmem_limit_bytes=VMEM_LIMIT),
        name="swiglu",
    )(block_e, xrows, w1, w3, w2, row_gate)


def _cast_kernel(x_ref, o_ref):
    o_ref[0] = x_ref[0, 0].astype(o_ref.dtype)


def _layer_weights_bf16(w, layer):
    _, ne, r, c = w.shape
    return pl.pallas_call(
        _cast_kernel,
        out_shape=jax.ShapeDtypeStruct((ne, r, c), jnp.bfloat16),
        grid=(ne,),
        in_specs=[pl.BlockSpec((1, 1, r, c), lambda e: (layer, e, 0, 0))],
        out_specs=pl.BlockSpec((1, r, c), lambda e: (e, 0, 0)),
        compiler_params=pltpu.CompilerParams(
            dimension_semantics=("parallel",), vmem_limit_bytes=VMEM_LIMIT),
        name="cast_bf16",
    )(w)


def _swiglu(x2d, w1, w3, w2, layer, bm=512):
    rows = x2d.shape[0]
    bf = lambda t: _layer_weights_bf16(t[:, None], layer)
    return _grouped_swiglu(x2d, jnp.zeros((rows // bm,), jnp.int32), jnp.ones((rows, 1), jnp.float32),
                           bf(w1), bf(w3), bf(w2), bm)


def _moe_swiglu(xf, logits, w1, w3, w2, layer):
    T, d = xf.shape
    i32 = jnp.int32
    lt = logits.T
    eids = jnp.arange(N_EXPERTS, dtype=i32)[:, None]
    e0 = jnp.argmax(lt, axis=0).astype(i32)
    m0 = jnp.max(lt, axis=0)
    lt1 = jnp.where(eids == e0[None, :], -jnp.inf, lt)
    e1 = jnp.argmax(lt1, axis=0).astype(i32)
    m1 = jnp.max(lt1, axis=0)
    ex = jnp.exp(m1 - m0)
    gates = (1.0 / (1.0 + ex), ex / (1.0 + ex))
    oh = [(eids == e[None, :]).astype(i32) for e in (e0, e1)]
    both = oh[0] + oh[1]
    incl = jnp.cumsum(both, axis=1)
    counts = incl[:, -1]
    padded = (counts + MOE_BLOCK - 1) // MOE_BLOCK * MOE_BLOCK
    pad_end = jnp.cumsum(padded)
    pad_start = pad_end - padded
    start = jnp.cumsum(counts) - counts
    offs = (incl - both) + pad_start[:, None]
    pos = [jnp.sum(o * offs, axis=0) for o in oh]
    tok = jnp.arange(T, dtype=i32)
    _, st, sg = lax.sort((jnp.concatenate(pos), jnp.concatenate([tok, tok]), jnp.concatenate(gates)), num_keys=1)
    n_blocks = -(-(T * TOP_K) // MOE_BLOCK) + N_EXPERTS
    blk = jnp.arange(n_blocks, dtype=i32)
    block_e = jnp.minimum(jnp.searchsorted(pad_end, blk * MOE_BLOCK, side='right'), N_EXPERTS - 1).astype(i32)
    q = (blk * MOE_BLOCK - pad_start[block_e])[:, None] + jnp.arange(MOE_BLOCK, dtype=i32)[None, :]
    valid = q < counts[block_e][:, None]
    src = jnp.clip(start[block_e][:, None] + q, 0, T * TOP_K - 1)
    slot_tok = jnp.where(valid, st[src], 0).reshape(-1)
    slot_gate = jnp.where(valid, sg[src], 0.0).reshape(-1, 1)
    bf = lambda t: _layer_weights_bf16(t, layer)
    yb = _grouped_swiglu(xf[slot_tok], block_e, slot_gate, bf(w1), bf(w3), bf(w2), MOE_BLOCK)
    return [yb[pos[0]], yb[pos[1]]]


def kernel(x, p, w_in, m_i_bias, m_f_bias, m_norm_w, g_conv_w, g_conv_b, g_w_a, g_b_a, g_w_x, g_b_x,
           g_lambda, r_mu, r_w0, r_w2, r_a0, r_a2, r_g2, r_k_k, r_k_a, r_r_k, r_ln_w, r_ln_b,
           s_conv_w, s_conv_b, s_dt_bias, s_a_log, s_d, s_norm_w, w_out, ln1_g, ln1_b, ln2_g, ln2_b,
           f_w1, f_w3, f_w2, e_router, e_w1, e_w3, e_w2, pe_proj, pe_gate_w, pe_gate_b):
    bsz, seq, d = x.shape
    T = bsz * seq
    x = x.reshape(T, d)
    for i in range(DEPTH):
        pm, pg, pr, ps, psm = _inproj(x, _split_w_in(w_in, i))
        mixers = [
            _mlstm_group(pm, psm, bsz, m_i_bias[i], m_f_bias[i], m_norm_w[i]),
            _rglru_group(pg, bsz, g_conv_w[i], g_conv_b[i], g_w_a[i], g_b_a[i], g_w_x[i], g_b_x[i], g_lambda[i]),
            _rwkv7_group(pr, bsz, r_mu[i], r_w0[i], r_w2[i], r_a0[i], r_a2[i], r_g2[i],
                         r_k_k[i], r_k_a[i], r_r_k[i], r_ln_w[i], r_ln_b[i]),
            _ssd_group(ps, psm, bsz, s_conv_w[i], s_conv_b[i], s_dt_bias[i], s_a_log[i], s_d[i], s_norm_w[i]),
        ]
        if i % 2 == 0:
            (x,) = _outproj_ln(x, mixers, w_out[i], ln1_g[i], ln1_b[i])
            ff = [_swiglu(x, f_w1, f_w3, f_w2, i // 2)]
        else:
            x, logits, x_bf16 = _outproj_ln(x, mixers, w_out[i], ln1_g[i], ln1_b[i], e_router[i // 2])
            ff = _moe_swiglu(x_bf16, logits[:, :N_EXPERTS], e_w1, e_w3, e_w2, i // 2)
        x = _pe_ln(x, ff, p.reshape(DEPTH, T, PE_DIM), i, pe_proj[i], pe_gate_w[i], pe_gate_b[i], ln2_g[i], ln2_b[i])
    return x.reshape(bsz, seq, d)
```

```python
import functools

import jax
import jax.numpy as jnp
from jax import lax
from jax.experimental import pallas as pl
from jax.experimental.pallas import tpu as pltpu

D_MODEL = 1024
DEPTH = 4
PE_DIM = 256
GROUP_W = 256
HEAD_DIM = 64
GROUP_HEADS = 4
CONV_K = 4
MLSTM_CHUNK = 64
RG_C = 8.0
RWKV_DECAY_RANK = 32
RWKV_ICL_RANK = 32
RWKV_GATE_RANK = 64
RWKV_GN_EPS = 64e-5
SSD_STATE = 64
SSD_GROUPS = 2
SSD_CHUNK = 128
SSD_XBC = GROUP_W + 2 * SSD_GROUPS * SSD_STATE
M_COLS = 4 * GROUP_W + 2 * GROUP_HEADS
G_COLS = 2 * GROUP_W
R_COLS = 3 * GROUP_W + RWKV_DECAY_RANK + RWKV_ICL_RANK + RWKV_GATE_RANK
S_COLS = GROUP_W + SSD_XBC + GROUP_HEADS
N_IN = M_COLS + G_COLS + R_COLS + S_COLS
N_EXPERTS = 8
TOP_K = 2
MOE_BLOCK = 512
ALPHA = (2 * DEPTH) ** 0.25
LN_EPS = 1e-5
RMS_EPS = 1e-6

LANES = 128
SUBLANES = 8
VMEM_LIMIT = 56 * 1024 * 1024


RWKV_TB = 32


def _rwkv_scan_kernel(pa_ref, pb_ref, pc_ref, y_ref, s_ref, sa_ref, ta_ref, tb_ref, tc_ref, yb_ref,
                      *, tb, nb, nk, nacc):
    @pl.when(pl.program_id(0) == 0)
    def _():
        s_ref[...] = jnp.zeros_like(s_ref)
        sa_ref[...] = jnp.zeros_like(sa_ref)

    nh = GROUP_HEADS
    half = nk // 2
    lane = lax.broadcasted_iota(jnp.int32, (half, LANES), 1)
    rows = lambda t, h: pl.ds(h * nb * tb + t, nb, stride=tb)

    def prepare(t, carry):
        for src, dst in ((pa_ref, ta_ref), (pb_ref, tb_ref), (pc_ref, tc_ref)):
            parts = [src[rows(t, h), :] for h in range(nh)]
            dst[t] = jnp.concatenate(parts + parts, axis=0).T
        return carry

    lax.fori_loop(0, tb, prepare, 0, unroll=16)

    def step(t, sa):
        bc = lambda ref, r: jnp.broadcast_to(ref[t, pl.ds(r, 1), :], (half, LANES))
        v_t = jnp.where(lane < nk, tc_ref[t, nk:nk + half, :], tc_ref[t, nk + half:2 * nk, :])
        y_acc = [None] * nacc
        sa_acc = [None] * nacc
        for k in range(nk):
            s_k = s_ref[k] * bc(ta_ref, k) - sa * bc(ta_ref, nk + k) + v_t * bc(tb_ref, k)
            s_ref[k] = s_k
            y_k = s_k * bc(tb_ref, nk + k)
            n_k = s_k * bc(tc_ref, k)
            a = k % nacc
            y_acc[a] = y_k if y_acc[a] is None else y_acc[a] + y_k
            sa_acc[a] = n_k if sa_acc[a] is None else sa_acc[a] + n_k
        yb_ref[t] = functools.reduce(lambda p, q: p + q, y_acc)
        return functools.reduce(lambda p, q: p + q, sa_acc)

    sa_ref[...] = lax.fori_loop(0, tb, step, sa_ref[...])

    def emit(t, carry):
        y = yb_ref[t]
        y_sw = pltpu.roll(y, nk, 1)
        y_nat = jnp.concatenate([y, y_sw, y, y_sw], axis=0).T
        for h in range(nh):
            y_ref[rows(t, h), :] = y_nat[h * nb:(h + 1) * nb, :]
        return carry

    lax.fori_loop(0, tb, emit, 0, unroll=16)


def _rwkv_scan(pa, pb, pc, bsz):
    nblk, nh, _, tb, _ = pa.shape
    n = HEAD_DIM
    assert 2 * bsz * nh == LANES and 2 * n == LANES
    flat = lambda t: t.reshape(nblk * nh * bsz * tb, LANES)
    spec = pl.BlockSpec((nh * bsz * tb, LANES), lambda i: (i, 0))
    y = pl.pallas_call(
        functools.partial(_rwkv_scan_kernel, tb=tb, nb=bsz, nk=n, nacc=2),
        out_shape=jax.ShapeDtypeStruct((nblk * nh * bsz * tb, LANES), jnp.float32),
        grid=(nblk,),
        in_specs=[spec, spec, spec],
        out_specs=spec,
        scratch_shapes=[pltpu.VMEM((n, n // 2, LANES), jnp.float32), pltpu.VMEM((n // 2, LANES), jnp.float32)]
        + [pltpu.VMEM((tb, LANES, LANES), jnp.float32)] * 3 + [pltpu.VMEM((tb, n // 2, LANES), jnp.float32)],
        compiler_params=pltpu.CompilerParams(
            dimension_semantics=("arbitrary",), vmem_limit_bytes=VMEM_LIMIT),
        name="rwkv_scan",
    )(flat(pa), flat(pb), flat(pc))
    return y.reshape(pa.shape)


def _inproj_kernel(x_ref, *refs):
    n = len(refs) // 2
    xb = x_ref[...].astype(jnp.bfloat16)
    for w_ref, o_ref in zip(refs[:n], refs[n:]):
        o_ref[...] = jnp.dot(xb, w_ref[...], preferred_element_type=jnp.float32)


def _inproj(x, weights, tm=512):
    t, d = x.shape
    return pl.pallas_call(
        _inproj_kernel,
        out_shape=[jax.ShapeDtypeStruct((t, w.shape[1]), jnp.float32) for w in weights],
        grid=(t // tm,),
        in_specs=[pl.BlockSpec((tm, d), lambda i: (i, 0))]
        + [pl.BlockSpec(w.shape, lambda i: (0, 0)) for w in weights],
        out_specs=[pl.BlockSpec((tm, w.shape[1]), lambda i: (i, 0)) for w in weights],
        compiler_params=pltpu.CompilerParams(
            dimension_semantics=("parallel",), vmem_limit_bytes=VMEM_LIMIT),
        name="inproj",
    )(x, *weights)


def _split_w_in_kernel(w_ref, m_ref, g_ref, r_ref, s_ref, sm_ref):
    m0, g0, r0, s0 = 0, M_COLS, M_COLS + G_COLS, M_COLS + G_COLS + R_COLS
    bf = jnp.bfloat16
    m_ref[...] = w_ref[0, :, m0:m0 + 4 * GROUP_W].astype(bf)
    g_ref[...] = w_ref[0, :, g0:g0 + G_COLS].astype(bf)
    r_ref[...] = w_ref[0, :, r0:r0 + R_COLS].astype(bf)
    s_ref[...] = w_ref[0, :, s0:s0 + GROUP_W + SSD_XBC].astype(bf)
    gates = jnp.concatenate([w_ref[0, :, m0 + 4 * GROUP_W:g0], w_ref[0, :, s0 + GROUP_W + SSD_XBC:N_IN]], axis=1)
    pad = jnp.zeros((gates.shape[0], LANES - gates.shape[1]), jnp.float32)
    sm_ref[...] = jnp.concatenate([gates, pad], axis=1).astype(bf)


def _split_w_in(w_in, layer):
    _, d, n = w_in.shape
    widths = (4 * GROUP_W, G_COLS, R_COLS, GROUP_W + SSD_XBC, LANES)
    return pl.pallas_call(
        _split_w_in_kernel,
        out_shape=[jax.ShapeDtypeStruct((d, c), jnp.bfloat16) for c in widths],
        grid=(1,),
        in_specs=[pl.BlockSpec((1, d, n), lambda i: (layer, 0, 0))],
        out_specs=[pl.BlockSpec((d, c), lambda i: (0, 0)) for c in widths],
        compiler_params=pltpu.CompilerParams(vmem_limit_bytes=VMEM_LIMIT),
        name="split_w_in",
    )(w_in)


def _shift_rows(x, s, fill, row):
    return jnp.where(row >= s, pltpu.roll(x, s, 0), fill)


def _rglru_kernel(g_ref, cw_ref, cb_ref, wg_ref, bg_ref, c_ref, o_ref, xpad_ref, a_ref, u_ref, *, rc, unroll):
    seq, w = o_ref.shape
    pad = SUBLANES
    xpad_ref[0:pad, :] = jnp.zeros((pad, w), jnp.float32)
    xpad_ref[pad:, :] = g_ref[:, 0:w]
    cw = cw_ref[...]
    for c in range(seq // rc):
        r0 = c * rc
        xc = cb_ref[...] + cw[CONV_K - 1:CONV_K, :] * g_ref[r0:r0 + rc, 0:w]
        for j in range(1, CONV_K):
            xc = xc + cw[CONV_K - 1 - j:CONV_K - j, :] * xpad_ref[pad + r0 - j:pad + r0 - j + rc, :]
        z = jnp.dot(xc.astype(jnp.bfloat16), wg_ref[...], preferred_element_type=jnp.float32) + bg_ref[...]
        z = jax.nn.sigmoid(z)
        a = jnp.exp(c_ref[...] * z[:, 0:w])
        a_ref[r0:r0 + rc, :] = a
        u_ref[r0:r0 + rc, :] = jnp.sqrt(1.0 - a * a) * (z[:, w:] * xc)

    row = lax.broadcasted_iota(jnp.int32, (SUBLANES, w), 0)

    def tiles(i, h):
        for j in range(unroll):
            r = pl.multiple_of((i * unroll + j) * SUBLANES, SUBLANES)
            a = a_ref[pl.ds(r, SUBLANES), :]
            u = u_ref[pl.ds(r, SUBLANES), :]
            for s in (1, 2, 4):
                u = a * _shift_rows(u, s, 0.0, row) + u
                a = a * _shift_rows(a, s, 1.0, row)
            ht = a * h + u
            o_ref[pl.ds(r, SUBLANES), :] = ht * jax.nn.gelu(g_ref[pl.ds(r, SUBLANES), w:2 * w])
            h = jnp.broadcast_to(ht[SUBLANES - 1:SUBLANES, :], (SUBLANES, w))
        return h

    lax.fori_loop(0, seq // (SUBLANES * unroll), tiles, jnp.zeros((SUBLANES, w), jnp.float32))


def _block_diag(w):
    h, p, _ = w.shape
    eye = jnp.eye(h, dtype=w.dtype)
    return jnp.einsum('hij,hg->higj', w, eye).reshape(h * p, h * p)


def _rglru_group(g_cols, bsz, conv_w, conv_b, w_a, b_a, w_x, b_x, lam):
    t = g_cols.shape[0]
    seq = t // bsz
    w = GROUP_W
    wg = jnp.concatenate([_block_diag(w_a), _block_diag(w_x)], axis=1).astype(jnp.bfloat16)
    bg = jnp.concatenate([b_a, b_x])[None, :]
    cdec = (-RG_C * jax.nn.softplus(-lam))[None, :]
    const = lambda shape: pl.BlockSpec(shape, lambda b: (0, 0))
    return pl.pallas_call(
        functools.partial(_rglru_kernel, rc=256, unroll=8),
        out_shape=jax.ShapeDtypeStruct((t, w), jnp.float32),
        grid=(bsz,),
        in_specs=[pl.BlockSpec((seq, 2 * w), lambda b: (b, 0)), const((CONV_K, w)), const((1, w)),
                  const((w, 2 * w)), const((1, 2 * w)), const((1, w))],
        out_specs=pl.BlockSpec((seq, w), lambda b: (b, 0)),
        scratch_shapes=[pltpu.VMEM((seq + SUBLANES, w), jnp.float32),
                        pltpu.VMEM((seq, w), jnp.float32), pltpu.VMEM((seq, w), jnp.float32)],
        compiler_params=pltpu.CompilerParams(
            dimension_semantics=("parallel",), vmem_limit_bytes=VMEM_LIMIT),
        name="rglru",
    )(g_cols, conv_w, conv_b[None, :], wg, bg, cdec)


MLSTM_ROWS_PER_STEP = 4
MLSTM_SEQ_SPLIT = 2
SSD_ROWS_PER_STEP = 4
SSD_SEQ_SPLIT = 2
MIXER_OUT_DTYPE = jnp.bfloat16
_NT = (((1,), (1,)), ((), ()))
_TN = (((0,), (0,)), ((), ()))


def _bdot(a, b, dims=(((1,), (0,)), ((), ()))):
    return lax.dot_general(a.astype(jnp.bfloat16), b.astype(jnp.bfloat16), dims,
                           preferred_element_type=jnp.float32)


def _cumsum_rows(x):
    n = x.shape[0]
    tri = (lax.broadcasted_iota(jnp.int32, (n, n), 0) >= lax.broadcasted_iota(jnp.int32, (n, n), 1))
    return jnp.dot(tri.astype(jnp.float32), x, precision=lax.Precision.HIGHEST,
                   preferred_element_type=jnp.float32)


def _log_sigmoid(x):
    return jnp.minimum(x, 0.0) - jnp.log1p(jnp.exp(-jnp.abs(x)))


def _softplus(x):
    return jnp.maximum(x, 0.0) + jnp.log1p(jnp.exp(-jnp.abs(x)))


def _mlstm_kernel(pm_ref, sm_ref, bias_ref, nw_ref, o_ref, c_ref, n_ref, m_ref):
    _, seq, w = o_ref.shape
    nh, dh, L = GROUP_HEADS, HEAD_DIM, MLSTM_CHUNK

    @pl.when(pl.program_id(1) == 0)
    def _():
        c_ref[...] = jnp.zeros_like(c_ref)
        n_ref[...] = jnp.zeros_like(n_ref)
        m_ref[...] = jnp.zeros_like(m_ref)

    f32 = jnp.float32
    hi = lax.Precision.HIGHEST
    col = lax.broadcasted_iota(jnp.int32, (L, LANES), 1)
    lrow = lax.broadcasted_iota(jnp.int32, (L, LANES), 0)
    scol = col & (dh - 1)
    lo = col < dh
    lo1 = lo[0:1]
    sq = lax.broadcasted_iota(jnp.int32, (LANES, LANES), 0)
    sr = lax.broadcasted_iota(jnp.int32, (LANES, LANES), 1)
    bd = (sq < dh) == (sr < dh)
    e_row = lax.broadcasted_iota(jnp.int32, (LANES, 2 * w), 0)
    e_col = lax.broadcasted_iota(jnp.int32, (LANES, 2 * w), 1)
    e_head = jnp.where(e_col < w, 0, nh) + ((e_col & (w - 1)) >> (dh.bit_length() - 1))
    spread = (e_row == e_head).astype(f32)
    tri = (lax.broadcasted_iota(jnp.int32, (L, L), 0) >= lax.broadcasted_iota(jnp.int32, (L, L), 1)).astype(f32)

    def seg_max(x, first):
        a = jnp.max(jnp.where(first, x, -jnp.inf), axis=1, keepdims=True)
        b = jnp.max(jnp.where(first, -jnp.inf, x), axis=1, keepdims=True)
        return jnp.where(first, a, b)

    def seg_sum(x):
        a = jnp.sum(jnp.where(lo, x, 0.0), axis=1, keepdims=True)
        b = jnp.sum(jnp.where(lo, 0.0, x), axis=1, keepdims=True)
        return jnp.where(lo, a, b)

    nb = pm_ref.shape[0]
    items = [(b, p) for b in range(nb) for p in range(nh // 2)]
    lanes = lambda p, sec: slice(sec * w + p * LANES, sec * w + (p + 1) * LANES)

    def chunk(c, carry):
        r0 = pl.multiple_of(c * L, L)
        rows = pl.ds(r0, L)
        each = lambda f, *xs: [f(*args) for args in zip(*xs)]
        gts = [sm_ref[b, rows, :] + bias_ref[...] for b in range(nb)]
        gts = [jnp.where(col < nh, t, jnp.where(col < 2 * nh, _log_sigmoid(t), 0.0)) for t in gts]
        rep = [jnp.dot(t, spread, precision=hi, preferred_element_type=f32) for t in gts]
        b_rep = [jnp.dot(tri, t[:, w:], precision=hi, preferred_element_type=f32) for t in rep]
        bc = [b_rep[b][:, lanes(p, 0)] for b, p in items]
        lic = [rep[b][:, lanes(p, 0)] for b, p in items]
        lfc = [rep[b][:, lanes(p, 1)] for b, p in items]
        b_row = [jnp.sum(jnp.where(lrow <= scol, t, 0.0), axis=0, keepdims=True) for t in lfc]
        li_row = [jnp.sum(jnp.where(lrow == scol, t, 0.0), axis=0, keepdims=True) for t in lic]
        g = [t[L - 1:L, :] for t in bc]
        q = [pm_ref[b, rows, lanes(p, 0)] for b, p in items]
        k = [pm_ref[b, rows, lanes(p, 1)] * (dh ** -0.5) for b, p in items]
        v = [pm_ref[b, rows, lanes(p, 2)] for b, p in items]
        cst = [c_ref[b, p] for b, p in items]
        nst = [n_ref[b, p] for b, p in items]
        m = [m_ref[b, p] for b, p in items]
        log_d = each(lambda bc_, br, lr: jnp.where(lrow >= scol, bc_ - br + lr, -jnp.inf), bc, b_row, li_row)
        m_t = each(lambda bc_, m_, ld: jnp.maximum(bc_ + m_, seg_max(ld, lo)), bc, m, log_d)
        k2 = [jnp.concatenate([t, t], axis=0) for t in k]
        qk = each(lambda q_, k2_: jnp.where(lo, _bdot(jnp.where(lo, q_, 0.0), k2_, _NT),
                                            _bdot(jnp.where(lo, 0.0, q_), k2_, _NT)), q, k2)
        s = each(lambda qk_, ld, mt: qk_ * jnp.exp(ld - mt), qk, log_d, m_t)
        inter_w = each(lambda bc_, m_, mt: jnp.exp(bc_ + m_ - mt), bc, m, m_t)
        v_bd = [jnp.where(bd, jnp.concatenate([t, t], axis=0), 0.0) for t in v]
        num = each(lambda s_, vb, iw, q_, c_: _bdot(s_, vb) + iw * _bdot(q_, c_), s, v_bd, inter_w, q, cst)
        den = each(lambda s_, iw, q_, n_: seg_sum(s_) + iw * seg_sum(q_ * n_), s, inter_w, q, nst)
        hh = each(lambda nu, de, mt: nu / jnp.maximum(jnp.abs(de), jnp.exp(-mt)), num, den, m_t)
        ms = [seg_sum(t * t) for t in hh]
        for (b, p), h_, ms_ in zip(items, hh, ms):
            o = pm_ref[b, rows, lanes(p, 3)]
            o_ref[b, rows, lanes(p, 0)] = (h_ * lax.rsqrt(ms_ * (1.0 / dh) + RMS_EPS) * nw_ref[:, lanes(p, 0)]
                                           * jax.nn.sigmoid(o)).astype(o_ref.dtype)
        m_new = each(lambda g_, m_, br, lr: jnp.maximum(g_ + m_, seg_max(g_ - br + lr, lo1)), g, m, b_row, li_row)
        decay = each(lambda g_, m_, mn: jnp.exp(g_ + m_ - mn), g, m, m_new)
        w_col = each(lambda g_, bc_, li_, mn: jnp.exp(g_ - bc_ + li_ - mn), g, bc, lic, m_new)
        upd = each(lambda k_, wc, v_: jnp.where(bd, _bdot(k_, wc * v_, _TN), 0.0), k, w_col, v)
        for i, (b, p) in enumerate(items):
            c_ref[b, p] = decay[i] * cst[i] + upd[i]
            n_ref[b, p] = decay[i] * nst[i] + jnp.sum(w_col[i] * k[i], axis=0, keepdims=True)
            m_ref[b, p] = m_new[i]
        return carry

    lax.fori_loop(0, seq // L, chunk, 0)


def _mlstm_group(pm, psm, bsz, i_bias, f_bias, norm_w):
    t = pm.shape[0]
    seq = t // bsz
    w, nh, dh = GROUP_W, GROUP_HEADS, HEAD_DIM
    bias = jnp.pad(jnp.concatenate([i_bias, f_bias]), (0, LANES - 2 * nh))[None, :]
    const = lambda shape: pl.BlockSpec(shape, lambda b, s: (0,) * len(shape))
    nb, ns = MLSTM_ROWS_PER_STEP, MLSTM_SEQ_SPLIT
    assert bsz % nb == 0 and seq % (ns * MLSTM_CHUNK) == 0
    seq_block = lambda n: pl.BlockSpec((nb, seq // ns, n), lambda b, s: (b, s, 0))
    return pl.pallas_call(
        _mlstm_kernel,
        out_shape=jax.ShapeDtypeStruct((bsz, seq, w), MIXER_OUT_DTYPE),
        grid=(bsz // nb, ns),
        in_specs=[seq_block(4 * w), seq_block(LANES), const((1, LANES)), const((1, w))],
        out_specs=seq_block(w),
        scratch_shapes=[pltpu.VMEM((nb, nh // 2, LANES, LANES), jnp.float32),
                        pltpu.VMEM((nb, nh // 2, 1, LANES), jnp.float32),
                        pltpu.VMEM((nb, nh // 2, 1, LANES), jnp.float32)],
        compiler_params=pltpu.CompilerParams(
            dimension_semantics=("parallel", "arbitrary"), vmem_limit_bytes=VMEM_LIMIT),
        name="mlstm",
    )(pm.reshape(bsz, seq, 4 * w), psm.reshape(bsz, seq, LANES), bias, norm_w[None, :]).reshape(t, w)


def _ssd_kernel(ps_ref, sm_ref, cw_ref, cb_ref, dtb_ref, aneg_ref, dsk_ref, nw_ref, o_ref,
                xbc_ref, st_ref, head_ref, *, rc):
    nb, seq, w = o_ref.shape
    nh, hp, ng, ns, L = GROUP_HEADS, HEAD_DIM, SSD_GROUPS, SSD_STATE, SSD_CHUNK
    cx = SSD_XBC
    pad = SUBLANES

    @pl.when(pl.program_id(1) == 0)
    def _():
        st_ref[...] = jnp.zeros_like(st_ref)
        head_ref[:, 0:pad, :] = jnp.zeros((nb, pad, cx), jnp.float32)

    cw = cw_ref[...]
    for b in range(nb):
        for c in range(seq // rc):
            r0 = c * rc
            cur = ps_ref[b, r0:r0 + rc, w:w + cx]
            if c == 0:
                head_ref[b, pad:pad + rc, :] = cur
            acc = cb_ref[...] + cw[CONV_K - 1:CONV_K, :] * cur
            for j in range(1, CONV_K):
                if c == 0:
                    sh = head_ref[b, pad - j:pad - j + rc, :]
                else:
                    sh = ps_ref[b, r0 - j:r0 - j + rc, w:w + cx]
                acc = acc + cw[CONV_K - 1 - j:CONV_K - j, :] * sh
            xbc_ref[b, r0:r0 + rc, :] = jax.nn.silu(acc)
        head_ref[b, 0:pad, :] = ps_ref[b, seq - pad:seq, w:w + cx]
    causal = lax.broadcasted_iota(jnp.int32, (L, L), 0) >= lax.broadcasted_iota(jnp.int32, (L, L), 1)
    gsz = w // ng
    rep = nh // ng
    groups = [(b, g) for b in range(nb) for g in range(ng)]
    heads = [(b, h) for b in range(nb) for h in range(nh)]

    def chunk(c, carry):
        r0 = pl.multiple_of(c * L, L)
        rows = pl.ds(r0, L)
        each = lambda f, *xs: [f(*args) for args in zip(*xs)]
        dt_all = [_softplus(sm_ref[b, rows, :] + dtb_ref[...]) for b in range(nb)]
        acs_all = [_cumsum_rows(aneg_ref[...] * t) for t in dt_all]
        acs_t = [t.T for t in acs_all]
        bm = {bg: xbc_ref[bg[0], rows, w + bg[1] * ns:w + (bg[1] + 1) * ns] for bg in groups}
        cm = {bg: xbc_ref[bg[0], rows, w + (ng + bg[1]) * ns:w + (ng + bg[1] + 1) * ns] for bg in groups}
        cb = {bg: _bdot(cm[bg], bm[bg], _NT) for bg in groups}
        grp = lambda b, h: (b, h // rep)
        cidx = lambda h: 2 * nh + h
        acs_col = [acs_all[b][:, cidx(h):cidx(h) + 1] for b, h in heads]
        acs_row = [acs_t[b][cidx(h):cidx(h) + 1, :] for b, h in heads]
        dt_col = [dt_all[b][:, cidx(h):cidx(h) + 1] for b, h in heads]
        a_last = [t[L - 1:L, :] for t in acs_col]
        xh = [xbc_ref[b, rows, h * hp:(h + 1) * hp] for b, h in heads]
        xdt = each(lambda x_, d_: x_ * d_, xh, dt_col)
        state = [st_ref[b, h] for b, h in heads]
        scores = [cb[grp(b, h)] * jnp.exp(jnp.where(causal, ac - ar, -jnp.inf))
                  for (b, h), ac, ar in zip(heads, acs_col, acs_row)]
        y_diag = each(_bdot, scores, xdt)
        y_off = [_bdot(cm[grp(b, h)], st, _NT) for (b, h), st in zip(heads, state)]
        ys = [yd + jnp.exp(ac) * yo + x_ * dsk_ref[:, h * hp:(h + 1) * hp]
              for (b, h), yd, ac, yo, x_ in zip(heads, y_diag, acs_col, y_off, xh)]
        upd = [_bdot(xd * jnp.exp(al - ac), bm[grp(b, h)], _TN)
               for (b, h), xd, al, ac in zip(heads, xdt, a_last, acs_col)]
        for i, (b, h) in enumerate(heads):
            st_ref[b, h] = jnp.exp(a_last[i]) * state[i] + upd[i]
        ysd = dict(zip(heads, ys))
        ygs = [jnp.concatenate([ysd[(b, g * rep + j)] for j in range(rep)], axis=1)
               * jax.nn.silu(ps_ref[b, rows, g * gsz:(g + 1) * gsz]) for b, g in groups]
        ms = [jnp.mean(t * t, axis=1, keepdims=True) for t in ygs]
        for (b, g), yg, m_ in zip(groups, ygs, ms):
            o_ref[b, rows, g * gsz:(g + 1) * gsz] = (yg * lax.rsqrt(m_ + RMS_EPS)
                                                     * nw_ref[:, g * gsz:(g + 1) * gsz]).astype(o_ref.dtype)
        return carry

    lax.fori_loop(0, seq // L, chunk, 0)


def _ssd_group(ps, psm, bsz, conv_w, conv_b, dt_bias, a_log, d_skip, norm_w):
    t = ps.shape[0]
    seq = t // bsz
    w, nh = GROUP_W, GROUP_HEADS
    lane_vec = lambda v: jnp.pad(v, (2 * nh, LANES - 3 * nh))[None, :]
    dsk = jnp.repeat(d_skip, HEAD_DIM)[None, :]
    const = lambda shape: pl.BlockSpec(shape, lambda b, s: (0,) * len(shape))
    nb, nsl = SSD_ROWS_PER_STEP, SSD_SEQ_SPLIT
    rc = 256
    slab = seq // nsl
    assert bsz % nb == 0 and seq % nsl == 0 and slab % rc == 0 and slab % SSD_CHUNK == 0
    seq_block = lambda n: pl.BlockSpec((nb, slab, n), lambda b, s: (b, s, 0))
    return pl.pallas_call(
        functools.partial(_ssd_kernel, rc=rc),
        out_shape=jax.ShapeDtypeStruct((bsz, seq, w), MIXER_OUT_DTYPE),
        grid=(bsz // nb, nsl),
        in_specs=[seq_block(w + SSD_XBC), seq_block(LANES),
                  const((CONV_K, SSD_XBC)), const((1, SSD_XBC)), const((1, LANES)), const((1, LANES)),
                  const((1, w)), const((1, w))],
        out_specs=seq_block(w),
        scratch_shapes=[pltpu.VMEM((nb, slab, SSD_XBC), jnp.float32),
                        pltpu.VMEM((nb, nh, HEAD_DIM, SSD_STATE), jnp.float32),
                        pltpu.VMEM((nb, SUBLANES + rc, SSD_XBC), jnp.float32)],
        compiler_params=pltpu.CompilerParams(
            dimension_semantics=("parallel", "arbitrary"), vmem_limit_bytes=VMEM_LIMIT),
        name="ssd",
    )(ps.reshape(bsz, seq, w + SSD_XBC), psm.reshape(bsz, seq, LANES), conv_w, conv_b[None, :], lane_vec(dt_bias),
      lane_vec(-jnp.exp(a_log)), dsk, norm_w[None, :]).reshape(t, w)


def _layer_norm(x, g, b):
    xc = x - jnp.mean(x, -1, keepdims=True)
    var = jnp.mean(xc * xc, -1, keepdims=True)
    return (xc * lax.rsqrt(var + LN_EPS)) * g + b


def _outproj_ln_kernel(x_ref, m0_ref, m1_ref, m2_ref, m3_ref, w_ref, g_ref, b_ref, *rest):
    gw = m0_ref.shape[1]
    acc = ALPHA * x_ref[...]
    for j, m_ref in enumerate((m0_ref, m1_ref, m2_ref, m3_ref)):
        acc = acc + _bdot(m_ref[...], w_ref[j * gw:(j + 1) * gw, :])
    x1 = _layer_norm(acc, g_ref[...], b_ref[...])
    if len(rest) == 1:
        rest[0][...] = x1
    else:
        wr_ref, o_ref, lg_ref, xb_ref = rest
        o_ref[...] = x1
        xb = x1.astype(jnp.bfloat16)
        xb_ref[...] = xb
        lg_ref[...] = jnp.dot(xb, wr_ref[...], preferred_element_type=jnp.float32)


def _outproj_ln(x, mixers, w_out, g, b, w_router=None, tm=512):
    t, d = x.shape
    gw = mixers[0].shape[1]
    row = lambda n: pl.BlockSpec((tm, n), lambda i: (i, 0))
    const = lambda shape: pl.BlockSpec(shape, lambda i: (0, 0))
    in_specs = [row(d)] + [row(gw)] * 4 + [const((d, d)), const((1, d)), const((1, d))]
    args = [x, *mixers, w_out.astype(jnp.bfloat16), g[None, :], b[None, :]]
    out_shape = [jax.ShapeDtypeStruct((t, d), jnp.float32)]
    out_specs = [row(d)]
    if w_router is not None:
        wr = jnp.pad(w_router, ((0, 0), (0, LANES - w_router.shape[1]))).astype(jnp.bfloat16)
        in_specs.append(const((d, LANES)))
        args.append(wr)
        out_shape += [jax.ShapeDtypeStruct((t, LANES), jnp.float32), jax.ShapeDtypeStruct((t, d), jnp.bfloat16)]
        out_specs += [row(LANES), row(d)]
    return pl.pallas_call(
        _outproj_ln_kernel,
        out_shape=out_shape, grid=(t // tm,), in_specs=in_specs, out_specs=out_specs,
        compiler_params=pltpu.CompilerParams(
            dimension_semantics=("parallel",), vmem_limit_bytes=VMEM_LIMIT),
        name="outproj_ln",
    )(*args)


def _pe_ln_kernel(x_ref, p_ref, wp_ref, wg_ref, bg_ref, g_ref, b_ref, *rest):
    *ff_refs, o_ref = rest
    x1 = x_ref[...]
    pe = _bdot(p_ref[0], wp_ref[...]) * jax.nn.sigmoid(_bdot(x1, wg_ref[...]) + bg_ref[...])
    ff = functools.reduce(lambda a, c: a + c, [r[...] for r in ff_refs])
    o_ref[...] = _layer_norm(ALPHA * x1 + ff + pe, g_ref[...], b_ref[...])


def _pe_ln(x1, ff_parts, p, layer, pe_proj, gate_w, gate_b, g, b, tm=512):
    t, d = x1.shape
    pd = p.shape[2]
    row = lambda n: pl.BlockSpec((tm, n), lambda i: (i, 0))
    const = lambda shape: pl.BlockSpec(shape, lambda i: (0, 0))
    return pl.pallas_call(
        _pe_ln_kernel,
        out_shape=jax.ShapeDtypeStruct((t, d), jnp.float32),
        grid=(t // tm,),
        in_specs=[row(d), pl.BlockSpec((1, tm, pd), lambda i: (layer, i, 0)), const((pd, d)), const((d, d)),
                  const((1, d)), const((1, d)), const((1, d))]
        + [row(d)] * len(ff_parts),
        out_specs=row(d),
        compiler_params=pltpu.CompilerParams(
            dimension_semantics=("parallel",), vmem_limit_bytes=VMEM_LIMIT),
        name="pe_ln",
    )(x1, p, pe_proj.astype(jnp.bfloat16), gate_w.astype(jnp.bfloat16), gate_b[None, :], g[None, :], b[None, :],
      *ff_parts)


def _pair_head_sums(x):
    lo = lax.broadcasted_iota(jnp.int32, (x.shape[0], LANES), 1) < HEAD_DIM
    tiles = []
    for hp in range(x.shape[1] // LANES):
        t = x[:, hp * LANES:(hp + 1) * LANES]
        a = jnp.sum(jnp.where(lo, t, 0.0), axis=1, keepdims=True)
        b = jnp.sum(jnp.where(lo, 0.0, t), axis=1, keepdims=True)
        tiles.append(jnp.where(lo, a, b))
    return jnp.concatenate(tiles, axis=1)


def _rwkv_pre_kernel(c_ref, mu_ref, w0_ref, w2_ref, a0_ref, a2_ref, g2_ref, kk_ref, ka_ref, rk_ref,
                     pa_ref, pb_ref, pc_ref, g_ref, bonus_ref, *, rc):
    seq = c_ref.shape[0]
    w, p = GROUP_W, HEAD_DIM
    o1 = 3 * w + RWKV_DECAY_RANK
    o2 = o1 + RWKV_ICL_RANK
    nchunks = seq // rc
    blk = rc // RWKV_TB
    row = lax.broadcasted_iota(jnp.int32, (rc, c_ref.shape[1]), 0)
    lane = lax.broadcasted_iota(jnp.int32, (rc, LANES), 1)

    def unit_keys(cols):
        kk = cols[:, w:2 * w] * kk_ref[...]
        return kk * lax.rsqrt(jnp.maximum(_pair_head_sums(kk * kk), 1e-24))

    def pack(dst, c, x, z):
        for hp in range(w // LANES):
            x2, z2 = x[:, hp * LANES:(hp + 1) * LANES], z[:, hp * LANES:(hp + 1) * LANES]
            even = jnp.where(lane < p, x2, pltpu.roll(z2, p, 1))
            odd = jnp.where(lane < p, pltpu.roll(x2, p, 1), z2)
            for h, val in ((2 * hp, even), (2 * hp + 1, odd)):
                dst[c * blk:(c + 1) * blk, h, 0] = val.reshape(blk, RWKV_TB, LANES)

    for c in range(nchunks):
        r0 = c * rc
        cur = c_ref[r0:r0 + rc, :]
        if c == 0:
            prev = jnp.where(row == 0, 0.0, pltpu.roll(cur, 1, 0))
        else:
            prev = c_ref[r0 - 1:r0 - 1 + rc, :]
        if c == nchunks - 1:
            nxt = jnp.where(row == rc - 1, 0.0, pltpu.roll(cur, rc - 1, 0))
        else:
            nxt = c_ref[r0 + 1:r0 + 1 + rc, :]
        cols = cur + (prev - cur) * mu_ref[...]
        cols_next = nxt + (cur - nxt) * mu_ref[...]
        k = cols[:, w:2 * w]
        w_log = -_softplus(-(w0_ref[...] + _bdot(jnp.tanh(cols[:, 3 * w:o1]), w2_ref[...]))) - 0.5
        a = jax.nn.sigmoid(a0_ref[...] + _bdot(cols[:, o1:o2], a2_ref[...]))
        kk = unit_keys(cols)
        r, v = cols[:, 0:w], cols[:, 2 * w:3 * w]
        k_mod = k * (1.0 + (a - 1.0) * ka_ref[...])
        pack(pa_ref, c, jnp.exp(-jnp.exp(w_log)), kk * a)
        pack(pb_ref, c, k_mod, r)
        pack(pc_ref, c, unit_keys(cols_next), v)
        g_ref[r0:r0 + rc, :] = _bdot(jax.nn.sigmoid(cols[:, o2:]), g2_ref[...])
        bonus_ref[r0:r0 + rc, :] = _pair_head_sums(r * k_mod * rk_ref[...]) * v


def _rwkv_post_kernel(y_ref, bonus_ref, g_ref, lw_ref, lb_ref, o_ref):
    p = HEAD_DIM
    rows = o_ref.shape[0]
    inv = 1.0 / p
    lo = lax.broadcasted_iota(jnp.int32, (rows, LANES), 1) < p
    for hp in range(GROUP_HEADS // 2):
        ls = slice(hp * LANES, (hp + 1) * LANES)
        y = jnp.where(lo, y_ref[:, 2 * hp, 0].reshape(rows, LANES), y_ref[:, 2 * hp + 1, 0].reshape(rows, LANES))
        yc = y - _pair_head_sums(y) * inv
        yn = yc * lax.rsqrt(_pair_head_sums(yc * yc) * inv + RWKV_GN_EPS) * lw_ref[:, ls] + lb_ref[:, ls]
        o_ref[:, ls] = ((yn + bonus_ref[:, ls]) * g_ref[:, ls]).astype(o_ref.dtype)


def _rwkv7_group(pr, bsz, mu, w0, w2, a0, a2, g2, k_k, k_a, r_k, ln_w, ln_b, tm=512):
    t = pr.shape[0]
    seq = t // bsz
    w = GROUP_W
    bf = lambda x: x.astype(jnp.bfloat16)
    const = lambda shape: pl.BlockSpec(shape, lambda b: (0, 0))
    vec = lambda x: x.reshape(1, -1)
    nh, tb = GROUP_HEADS, RWKV_TB
    nblk = seq // tb
    nat = jax.ShapeDtypeStruct((t, w), jnp.float32)
    packed = jax.ShapeDtypeStruct((nblk, nh, bsz, tb, LANES), jnp.float32)
    pspec = pl.BlockSpec((nblk, nh, 1, tb, LANES), lambda b: (0, 0, b, 0, 0))
    rc = 256
    assert seq % rc == 0 and seq % tm == 0 and tm % tb == 0 and rc % tb == 0
    seq_rows = pl.BlockSpec((seq, w), lambda b: (b, 0))
    pa, pb, pc, g, bonus = pl.pallas_call(
        functools.partial(_rwkv_pre_kernel, rc=rc),
        out_shape=[packed] * 3 + [nat] * 2,
        grid=(bsz,),
        in_specs=[pl.BlockSpec((seq, R_COLS), lambda b: (b, 0)), const((1, R_COLS)), const((1, w)),
                  const((RWKV_DECAY_RANK, w)), const((1, w)), const((RWKV_ICL_RANK, w)),
                  const((RWKV_GATE_RANK, w)), const((1, w)), const((1, w)), const((1, w))],
        out_specs=[pspec] * 3 + [seq_rows] * 2,
        compiler_params=pltpu.CompilerParams(
            dimension_semantics=("parallel",), vmem_limit_bytes=VMEM_LIMIT),
        name="rwkv_pre",
    )(pr, vec(mu), vec(w0), bf(w2), vec(a0), bf(a2), bf(g2), vec(k_k), vec(k_a), vec(r_k))
    y = _rwkv_scan(pa, pb, pc, bsz)
    gb = tm // tb
    per_b = seq // tm
    bspec = pl.BlockSpec((gb, nh, 1, tb, LANES), lambda b, i: (i, 0, b, 0, 0))
    row = pl.BlockSpec((tm, w), lambda b, i: (b * per_b + i, 0))
    const2 = lambda shape: pl.BlockSpec(shape, lambda b, i: (0, 0))
    return pl.pallas_call(
        _rwkv_post_kernel,
        out_shape=jax.ShapeDtypeStruct((t, w), MIXER_OUT_DTYPE),
        grid=(bsz, per_b),
        in_specs=[bspec, row, row] + [const2((1, w))] * 2,
        out_specs=row,
        compiler_params=pltpu.CompilerParams(
            dimension_semantics=("parallel", "parallel"), vmem_limit_bytes=VMEM_LIMIT),
        name="rwkv_post",
    )(y, bonus, g, vec(ln_w), vec(ln_b))


FF_CHUNK = 512


def _swiglu_kernel(be_ref, x_ref, w1_ref, w3_ref, w2_ref, g_ref, o_ref):
    del be_ref
    xb = x_ref[...].astype(jnp.bfloat16)
    dff = w1_ref.shape[-1]
    acc = None
    for c0 in range(0, dff, FF_CHUNK):
        c1 = min(c0 + FF_CHUNK, dff)
        h1 = jnp.dot(xb, w1_ref[0, :, c0:c1], preferred_element_type=jnp.float32)
        h3 = jnp.dot(xb, w3_ref[0, :, c0:c1], preferred_element_type=jnp.float32)
        h = (jax.nn.silu(h1) * h3).astype(jnp.bfloat16)
        part = jnp.dot(h, w2_ref[0, c0:c1, :], preferred_element_type=jnp.float32)
        acc = part if acc is None else acc + part
    o_ref[...] = acc * g_ref[...]


def _grouped_swiglu(xrows, block_e, row_gate, w1, w3, w2, bm):
    rows, d = xrows.shape
    dff = w1.shape[-1]
    grid_spec = pltpu.PrefetchScalarGridSpec(
        num_scalar_prefetch=1,
        grid=(rows // bm,),
        in_specs=[pl.BlockSpec((bm, d), lambda i, be: (i, 0)),
                  pl.BlockSpec((1, d, dff), lambda i, be: (be[i], 0, 0)),
                  pl.BlockSpec((1, d, dff), lambda i, be: (be[i], 0, 0)),
                  pl.BlockSpec((1, dff, d), lambda i, be: (be[i], 0, 0)),
                  pl.BlockSpec((bm, 1), lambda i, be: (i, 0))],
        out_specs=pl.BlockSpec((bm, d), lambda i, be: (i, 0)),
    )
    return pl.pallas_call(
        _swiglu_kernel,
        out_shape=jax.ShapeDtypeStruct((rows, d), jnp.float32),
        grid_spec=grid_spec,
        compiler_params=pltpu.CompilerParams(
            dimension_semantics=("arbitrary",), vmem_limit_bytes=VMEM_LIMIT),
        cost_estimate=pl.CostEstimate(
            flops=6 * rows * d * dff, transcendentals=rows * dff,
            bytes_accessed=rows * d * (xrows.dtype.itemsize + 4) + rows * 4 + 2 * (w1.size + w3.size + w2.size)),
        name="swiglu",
    )(block_e, xrows, w1, w3, w2, row_gate)


def _cast_kernel(x_ref, o_ref):
    o_ref[0] = x_ref[0, 0].astype(o_ref.dtype)


def _layer_weights_bf16(w, layer):
    _, ne, r, c = w.shape
    return pl.pallas_call(
        _cast_kernel,
        out_shape=jax.ShapeDtypeStruct((ne, r, c), jnp.bfloat16),
        grid=(ne,),
        in_specs=[pl.BlockSpec((1, 1, r, c), lambda e: (layer, e, 0, 0))],
        out_specs=pl.BlockSpec((1, r, c), lambda e: (e, 0, 0)),
        compiler_params=pltpu.CompilerParams(
            dimension_semantics=("parallel",), vmem_limit_bytes=VMEM_LIMIT),
        cost_estimate=pl.CostEstimate(flops=0, transcendentals=0, bytes_accessed=ne * r * c * (4 + 2)),
        name="cast_bf16",
    )(w)


def _swiglu(x2d, w1, w3, w2, layer, bm=512):
    rows = x2d.shape[0]
    bf = lambda t: _layer_weights_bf16(t[:, None], layer)
    return _grouped_swiglu(x2d, jnp.zeros((rows // bm,), jnp.int32), jnp.ones((rows, 1), jnp.float32),
                           bf(w1), bf(w3), bf(w2), bm)


def _moe_swiglu(xf, logits, w1, w3, w2, layer):
    T, d = xf.shape
    i32 = jnp.int32
    lt = logits.T
    eids = jnp.arange(N_EXPERTS, dtype=i32)[:, None]
    e0 = jnp.argmax(lt, axis=0).astype(i32)
    m0 = jnp.max(lt, axis=0)
    lt1 = jnp.where(eids == e0[None, :], -jnp.inf, lt)
    e1 = jnp.argmax(lt1, axis=0).astype(i32)
    m1 = jnp.max(lt1, axis=0)
    ex = jnp.exp(m1 - m0)
    gates = (1.0 / (1.0 + ex), ex / (1.0 + ex))
    oh = [(eids == e[None, :]).astype(i32) for e in (e0, e1)]
    both = oh[0] + oh[1]
    incl = jnp.cumsum(both, axis=1)
    counts = incl[:, -1]
    padded = (counts + MOE_BLOCK - 1) // MOE_BLOCK * MOE_BLOCK
    pad_end = jnp.cumsum(padded)
    pad_start = pad_end - padded
    start = jnp.cumsum(counts) - counts
    offs = (incl - both) + pad_start[:, None]
    pos = [jnp.sum(o * offs, axis=0) for o in oh]
    tok = jnp.arange(T, dtype=i32)
    _, st, sg = lax.sort((jnp.concatenate(pos), jnp.concatenate([tok, tok]), jnp.concatenate(gates)), num_keys=1)
    n_blocks = -(-(T * TOP_K) // MOE_BLOCK) + N_EXPERTS
    blk = jnp.arange(n_blocks, dtype=i32)
    block_e = jnp.minimum(jnp.searchsorted(pad_end, blk * MOE_BLOCK, side='right'), N_EXPERTS - 1).astype(i32)
    q = (blk * MOE_BLOCK - pad_start[block_e])[:, None] + jnp.arange(MOE_BLOCK, dtype=i32)[None, :]
    valid = q < counts[block_e][:, None]
    src = jnp.clip(start[block_e][:, None] + q, 0, T * TOP_K - 1)
    slot_tok = jnp.where(valid, st[src], 0).reshape(-1)
    slot_gate = jnp.where(valid, sg[src], 0.0).reshape(-1, 1)
    bf = lambda t: _layer_weights_bf16(t, layer)
    yb = _grouped_swiglu(xf[slot_tok], block_e, slot_gate, bf(w1), bf(w3), bf(w2), MOE_BLOCK)
    return [yb[pos[0]], yb[pos[1]]]


def kernel(x, p, w_in, m_i_bias, m_f_bias, m_norm_w, g_conv_w, g_conv_b, g_w_a, g_b_a, g_w_x, g_b_x,
           g_lambda, r_mu, r_w0, r_w2, r_a0, r_a2, r_g2, r_k_k, r_k_a, r_r_k, r_ln_w, r_ln_b,
           s_conv_w, s_conv_b, s_dt_bias, s_a_log, s_d, s_norm_w, w_out, ln1_g, ln1_b, ln2_g, ln2_b,
           f_w1, f_w3, f_w2, e_router, e_w1, e_w3, e_w2, pe_proj, pe_gate_w, pe_gate_b):
    bsz, seq, d = x.shape
    T = bsz * seq
    x = x.reshape(T, d)
    for i in range(DEPTH):
        pm, pg, pr, ps, psm = _inproj(x, _split_w_in(w_in, i))
        mixers = [
            _mlstm_group(pm, psm, bsz, m_i_bias[i], m_f_bias[i], m_norm_w[i]),
            _rglru_group(pg, bsz, g_conv_w[i], g_conv_b[i], g_w_a[i], g_b_a[i], g_w_x[i], g_b_x[i], g_lambda[i]),
            _rwkv7_group(pr, bsz, r_mu[i], r_w0[i], r_w2[i], r_a0[i], r_a2[i], r_g2[i],
                         r_k_k[i], r_k_a[i], r_r_k[i], r_ln_w[i], r_ln_b[i]),
            _ssd_group(ps, psm, bsz, s_conv_w[i], s_conv_b[i], s_dt_bias[i], s_a_log[i], s_d[i], s_norm_w[i]),
        ]
        if i % 2 == 0:
            (x,) = _outproj_ln(x, mixers, w_out[i], ln1_g[i], ln1_b[i])
            ff = [_swiglu(x, f_w1, f_w3, f_w2, i // 2)]
        else:
            x, logits, x_bf16 = _outproj_ln(x, mixers, w_out[i], ln1_g[i], ln1_b[i], e_router[i // 2])
            ff = _moe_swiglu(x_bf16, logits[:, :N_EXPERTS], e_w1, e_w3, e_w2, i // 2)
        x = _pe_ln(x, ff, p.reshape(DEPTH, T, PE_DIM), i, pe_proj[i], pe_gate_w[i], pe_gate_b[i], ln2_g[i], ln2_b[i])
    return x.reshape(bsz, seq, d)
```

```python
import functools

import jax
import jax.numpy as jnp
from jax import lax
from jax.experimental import pallas as pl
from jax.experimental.pallas import tpu as pltpu

D_MODEL = 1024
DEPTH = 4
PE_DIM = 256
GROUP_W = 256
HEAD_DIM = 64
GROUP_HEADS = 4
CONV_K = 4
MLSTM_CHUNK = 64
RG_C = 8.0
RWKV_DECAY_RANK = 32
RWKV_ICL_RANK = 32
RWKV_GATE_RANK = 64
RWKV_GN_EPS = 64e-5
SSD_STATE = 64
SSD_GROUPS = 2
SSD_CHUNK = 128
SSD_XBC = GROUP_W + 2 * SSD_GROUPS * SSD_STATE
M_COLS = 4 * GROUP_W + 2 * GROUP_HEADS
G_COLS = 2 * GROUP_W
R_COLS = 3 * GROUP_W + RWKV_DECAY_RANK + RWKV_ICL_RANK + RWKV_GATE_RANK
S_COLS = GROUP_W + SSD_XBC + GROUP_HEADS
N_IN = M_COLS + G_COLS + R_COLS + S_COLS
N_EXPERTS = 8
TOP_K = 2
MOE_BLOCK = 512
ALPHA = (2 * DEPTH) ** 0.25
LN_EPS = 1e-5
RMS_EPS = 1e-6

LANES = 128
SUBLANES = 8
VMEM_LIMIT = 56 * 1024 * 1024


RWKV_TB = 32


def _rwkv_scan_kernel(pa_ref, pb_ref, pc_ref, y_ref, s_ref, sa_ref, ta_ref, tb_ref, tc_ref, yb_ref,
                      *, tb, nb, nk, nacc):
    @pl.when(pl.program_id(0) == 0)
    def _():
        s_ref[...] = jnp.zeros_like(s_ref)
        sa_ref[...] = jnp.zeros_like(sa_ref)

    nh = GROUP_HEADS
    half = nk // 2
    lane = lax.broadcasted_iota(jnp.int32, (half, LANES), 1)
    rows = lambda t, h: pl.ds(h * nb * tb + t, nb, stride=tb)

    def prepare(t, carry):
        for src, dst in ((pa_ref, ta_ref), (pb_ref, tb_ref), (pc_ref, tc_ref)):
            parts = [src[rows(t, h), :] for h in range(nh)]
            dst[t] = jnp.concatenate(parts + parts, axis=0).T
        return carry

    lax.fori_loop(0, tb, prepare, 0, unroll=16)

    def step(t, sa):
        bc = lambda ref, r: jnp.broadcast_to(ref[t, pl.ds(r, 1), :], (half, LANES))
        v_t = jnp.where(lane < nk, tc_ref[t, nk:nk + half, :], tc_ref[t, nk + half:2 * nk, :])
        y_acc = [None] * nacc
        sa_acc = [None] * nacc
        for k in range(nk):
            s_k = s_ref[k] * bc(ta_ref, k) - sa * bc(ta_ref, nk + k) + v_t * bc(tb_ref, k)
            s_ref[k] = s_k
            y_k = s_k * bc(tb_ref, nk + k)
            n_k = s_k * bc(tc_ref, k)
            a = k % nacc
            y_acc[a] = y_k if y_acc[a] is None else y_acc[a] + y_k
            sa_acc[a] = n_k if sa_acc[a] is None else sa_acc[a] + n_k
        yb_ref[t] = functools.reduce(lambda p, q: p + q, y_acc)
        return functools.reduce(lambda p, q: p + q, sa_acc)

    sa_ref[...] = lax.fori_loop(0, tb, step, sa_ref[...])

    def emit(t, carry):
        y = yb_ref[t]
        y_sw = pltpu.roll(y, nk, 1)
        y_nat = jnp.concatenate([y, y_sw, y, y_sw], axis=0).T
        for h in range(nh):
            y_ref[rows(t, h), :] = y_nat[h * nb:(h + 1) * nb, :]
        return carry

    lax.fori_loop(0, tb, emit, 0, unroll=16)


def _rwkv_scan(pa, pb, pc, bsz):
    nblk, nh, _, tb, _ = pa.shape
    n = HEAD_DIM
    assert 2 * bsz * nh == LANES and 2 * n == LANES
    flat = lambda t: t.reshape(nblk * nh * bsz * tb, LANES)
    spec = pl.BlockSpec((nh * bsz * tb, LANES), lambda i: (i, 0))
    y = pl.pallas_call(
        functools.partial(_rwkv_scan_kernel, tb=tb, nb=bsz, nk=n, nacc=2),
        out_shape=jax.ShapeDtypeStruct((nblk * nh * bsz * tb, LANES), jnp.float32),
        grid=(nblk,),
        in_specs=[spec, spec, spec],
        out_specs=spec,
        scratch_shapes=[pltpu.VMEM((n, n // 2, LANES), jnp.float32), pltpu.VMEM((n // 2, LANES), jnp.float32)]
        + [pltpu.VMEM((tb, LANES, LANES), jnp.float32)] * 3 + [pltpu.VMEM((tb, n // 2, LANES), jnp.float32)],
        compiler_params=pltpu.CompilerParams(
            dimension_semantics=("arbitrary",), vmem_limit_bytes=VMEM_LIMIT),
        name="rwkv_scan",
    )(flat(pa), flat(pb), flat(pc))
    return y.reshape(pa.shape)


def _inproj_kernel(x_ref, *refs):
    n = len(refs) // 2
    xb = x_ref[...].astype(jnp.bfloat16)
    for w_ref, o_ref in zip(refs[:n], refs[n:]):
        o_ref[...] = jnp.dot(xb, w_ref[...], preferred_element_type=jnp.float32)


def _inproj(x, weights, tm=512):
    t, d = x.shape
    assert t % tm == 0
    return pl.pallas_call(
        _inproj_kernel,
        out_shape=[jax.ShapeDtypeStruct((t, w.shape[1]), jnp.float32) for w in weights],
        grid=(t // tm,),
        in_specs=[pl.BlockSpec((tm, d), lambda i: (i, 0))]
        + [pl.BlockSpec(w.shape, lambda i: (0, 0)) for w in weights],
        out_specs=[pl.BlockSpec((tm, w.shape[1]), lambda i: (i, 0)) for w in weights],
        compiler_params=pltpu.CompilerParams(
            dimension_semantics=("parallel",), vmem_limit_bytes=VMEM_LIMIT),
        name="inproj",
    )(x, *weights)


def _split_w_in_kernel(w_ref, m_ref, g_ref, r_ref, s_ref, sm_ref):
    m0, g0, r0, s0 = 0, M_COLS, M_COLS + G_COLS, M_COLS + G_COLS + R_COLS
    bf = jnp.bfloat16
    m_ref[...] = w_ref[0, :, m0:m0 + 4 * GROUP_W].astype(bf)
    g_ref[...] = w_ref[0, :, g0:g0 + G_COLS].astype(bf)
    r_ref[...] = w_ref[0, :, r0:r0 + R_COLS].astype(bf)
    s_ref[...] = w_ref[0, :, s0:s0 + GROUP_W + SSD_XBC].astype(bf)
    gates = jnp.concatenate([w_ref[0, :, m0 + 4 * GROUP_W:g0], w_ref[0, :, s0 + GROUP_W + SSD_XBC:N_IN]], axis=1)
    pad = jnp.zeros((gates.shape[0], LANES - gates.shape[1]), jnp.float32)
    sm_ref[...] = jnp.concatenate([gates, pad], axis=1).astype(bf)


def _split_w_in(w_in, layer):
    _, d, n = w_in.shape
    widths = (4 * GROUP_W, G_COLS, R_COLS, GROUP_W + SSD_XBC, LANES)
    return pl.pallas_call(
        _split_w_in_kernel,
        out_shape=[jax.ShapeDtypeStruct((d, c), jnp.bfloat16) for c in widths],
        grid=(1,),
        in_specs=[pl.BlockSpec((1, d, n), lambda i: (layer, 0, 0))],
        out_specs=[pl.BlockSpec((d, c), lambda i: (0, 0)) for c in widths],
        compiler_params=pltpu.CompilerParams(vmem_limit_bytes=VMEM_LIMIT),
        name="split_w_in",
    )(w_in)


def _shift_rows(x, s, fill, row):
    return jnp.where(row >= s, pltpu.roll(x, s, 0), fill)


def _rglru_kernel(g_ref, cw_ref, cb_ref, wg_ref, bg_ref, c_ref, o_ref, xpad_ref, a_ref, u_ref, *, rc, unroll):
    seq, w = o_ref.shape
    pad = SUBLANES
    xpad_ref[0:pad, :] = jnp.zeros((pad, w), jnp.float32)
    xpad_ref[pad:, :] = g_ref[:, 0:w]
    cw = cw_ref[...]
    for c in range(seq // rc):
        r0 = c * rc
        xc = cb_ref[...] + cw[CONV_K - 1:CONV_K, :] * g_ref[r0:r0 + rc, 0:w]
        for j in range(1, CONV_K):
            xc = xc + cw[CONV_K - 1 - j:CONV_K - j, :] * xpad_ref[pad + r0 - j:pad + r0 - j + rc, :]
        z = jnp.dot(xc.astype(jnp.bfloat16), wg_ref[...], preferred_element_type=jnp.float32) + bg_ref[...]
        z = jax.nn.sigmoid(z)
        a = jnp.exp(c_ref[...] * z[:, 0:w])
        a_ref[r0:r0 + rc, :] = a
        u_ref[r0:r0 + rc, :] = jnp.sqrt(1.0 - a * a) * (z[:, w:] * xc)

    row = lax.broadcasted_iota(jnp.int32, (SUBLANES, w), 0)

    def tiles(i, h):
        for j in range(unroll):
            r = pl.multiple_of((i * unroll + j) * SUBLANES, SUBLANES)
            a = a_ref[pl.ds(r, SUBLANES), :]
            u = u_ref[pl.ds(r, SUBLANES), :]
            for s in (1, 2, 4):
                u = a * _shift_rows(u, s, 0.0, row) + u
                a = a * _shift_rows(a, s, 1.0, row)
            ht = a * h + u
            o_ref[pl.ds(r, SUBLANES), :] = ht * jax.nn.gelu(g_ref[pl.ds(r, SUBLANES), w:2 * w])
            h = jnp.broadcast_to(ht[SUBLANES - 1:SUBLANES, :], (SUBLANES, w))
        return h

    lax.fori_loop(0, seq // (SUBLANES * unroll), tiles, jnp.zeros((SUBLANES, w), jnp.float32))


def _block_diag(w):
    h, p, _ = w.shape
    eye = jnp.eye(h, dtype=w.dtype)
    return jnp.einsum('hij,hg->higj', w, eye).reshape(h * p, h * p)


def _rglru_group(g_cols, bsz, conv_w, conv_b, w_a, b_a, w_x, b_x, lam):
    t = g_cols.shape[0]
    seq = t // bsz
    w = GROUP_W
    wg = jnp.concatenate([_block_diag(w_a), _block_diag(w_x)], axis=1).astype(jnp.bfloat16)
    bg = jnp.concatenate([b_a, b_x])[None, :]
    cdec = (-RG_C * jax.nn.softplus(-lam))[None, :]
    const = lambda shape: pl.BlockSpec(shape, lambda b: (0, 0))
    rc, unroll = 256, 8
    assert seq % rc == 0 and seq % (SUBLANES * unroll) == 0
    return pl.pallas_call(
        functools.partial(_rglru_kernel, rc=rc, unroll=unroll),
        out_shape=jax.ShapeDtypeStruct((t, w), jnp.float32),
        grid=(bsz,),
        in_specs=[pl.BlockSpec((seq, 2 * w), lambda b: (b, 0)), const((CONV_K, w)), const((1, w)),
                  const((w, 2 * w)), const((1, 2 * w)), const((1, w))],
        out_specs=pl.BlockSpec((seq, w), lambda b: (b, 0)),
        scratch_shapes=[pltpu.VMEM((seq + SUBLANES, w), jnp.float32),
                        pltpu.VMEM((seq, w), jnp.float32), pltpu.VMEM((seq, w), jnp.float32)],
        compiler_params=pltpu.CompilerParams(
            dimension_semantics=("parallel",), vmem_limit_bytes=VMEM_LIMIT),
        name="rglru",
    )(g_cols, conv_w, conv_b[None, :], wg, bg, cdec)


MLSTM_ROWS_PER_STEP = 4
MLSTM_SEQ_SPLIT = 2
SSD_ROWS_PER_STEP = 4
SSD_SEQ_SPLIT = 2
MIXER_OUT_DTYPE = jnp.bfloat16
_NT = (((1,), (1,)), ((), ()))
_TN = (((0,), (0,)), ((), ()))


def _bdot(a, b, dims=(((1,), (0,)), ((), ()))):
    return lax.dot_general(a.astype(jnp.bfloat16), b.astype(jnp.bfloat16), dims,
                           preferred_element_type=jnp.float32)


def _cumsum_rows(x):
    n = x.shape[0]
    tri = (lax.broadcasted_iota(jnp.int32, (n, n), 0) >= lax.broadcasted_iota(jnp.int32, (n, n), 1))
    return jnp.dot(tri.astype(jnp.float32), x, precision=lax.Precision.HIGHEST,
                   preferred_element_type=jnp.float32)


def _log_sigmoid(x):
    return jnp.minimum(x, 0.0) - jnp.log1p(jnp.exp(-jnp.abs(x)))


def _softplus(x):
    return jnp.maximum(x, 0.0) + jnp.log1p(jnp.exp(-jnp.abs(x)))


def _mlstm_kernel(pm_ref, sm_ref, bias_ref, nw_ref, o_ref, c_ref, n_ref, m_ref):
    _, seq, w = o_ref.shape
    nh, dh, L = GROUP_HEADS, HEAD_DIM, MLSTM_CHUNK

    @pl.when(pl.program_id(1) == 0)
    def _():
        c_ref[...] = jnp.zeros_like(c_ref)
        n_ref[...] = jnp.zeros_like(n_ref)
        m_ref[...] = jnp.zeros_like(m_ref)

    f32 = jnp.float32
    hi = lax.Precision.HIGHEST
    col = lax.broadcasted_iota(jnp.int32, (L, LANES), 1)
    lrow = lax.broadcasted_iota(jnp.int32, (L, LANES), 0)
    scol = col & (dh - 1)
    lo = col < dh
    lo1 = lo[0:1]
    sq = lax.broadcasted_iota(jnp.int32, (LANES, LANES), 0)
    sr = lax.broadcasted_iota(jnp.int32, (LANES, LANES), 1)
    bd = (sq < dh) == (sr < dh)
    e_row = lax.broadcasted_iota(jnp.int32, (LANES, 2 * w), 0)
    e_col = lax.broadcasted_iota(jnp.int32, (LANES, 2 * w), 1)
    e_head = jnp.where(e_col < w, 0, nh) + ((e_col & (w - 1)) >> (dh.bit_length() - 1))
    spread = (e_row == e_head).astype(f32)
    tri = (lax.broadcasted_iota(jnp.int32, (L, L), 0) >= lax.broadcasted_iota(jnp.int32, (L, L), 1)).astype(f32)

    def seg_max(x, first):
        a = jnp.max(jnp.where(first, x, -jnp.inf), axis=1, keepdims=True)
        b = jnp.max(jnp.where(first, -jnp.inf, x), axis=1, keepdims=True)
        return jnp.where(first, a, b)

    def seg_sum(x):
        a = jnp.sum(jnp.where(lo, x, 0.0), axis=1, keepdims=True)
        b = jnp.sum(jnp.where(lo, 0.0, x), axis=1, keepdims=True)
        return jnp.where(lo, a, b)

    nb = pm_ref.shape[0]
    items = [(b, p) for b in range(nb) for p in range(nh // 2)]
    lanes = lambda p, sec: slice(sec * w + p * LANES, sec * w + (p + 1) * LANES)

    def chunk(c, carry):
        r0 = pl.multiple_of(c * L, L)
        rows = pl.ds(r0, L)
        each = lambda f, *xs: [f(*args) for args in zip(*xs)]
        gts = [sm_ref[b, rows, :] + bias_ref[...] for b in range(nb)]
        gts = [jnp.where(col < nh, t, jnp.where(col < 2 * nh, _log_sigmoid(t), 0.0)) for t in gts]
        rep = [jnp.dot(t, spread, precision=hi, preferred_element_type=f32) for t in gts]
        b_rep = [jnp.dot(tri, t[:, w:], precision=hi, preferred_element_type=f32) for t in rep]
        bc = [b_rep[b][:, lanes(p, 0)] for b, p in items]
        lic = [rep[b][:, lanes(p, 0)] for b, p in items]
        lfc = [rep[b][:, lanes(p, 1)] for b, p in items]
        b_row = [jnp.sum(jnp.where(lrow <= scol, t, 0.0), axis=0, keepdims=True) for t in lfc]
        li_row = [jnp.sum(jnp.where(lrow == scol, t, 0.0), axis=0, keepdims=True) for t in lic]
        g = [t[L - 1:L, :] for t in bc]
        q = [pm_ref[b, rows, lanes(p, 0)] for b, p in items]
        k = [pm_ref[b, rows, lanes(p, 1)] * (dh ** -0.5) for b, p in items]
        v = [pm_ref[b, rows, lanes(p, 2)] for b, p in items]
        cst = [c_ref[b, p] for b, p in items]
        nst = [n_ref[b, p] for b, p in items]
        m = [m_ref[b, p] for b, p in items]
        log_d = each(lambda bc_, br, lr: jnp.where(lrow >= scol, bc_ - br + lr, -jnp.inf), bc, b_row, li_row)
        m_t = each(lambda bc_, m_, ld: jnp.maximum(bc_ + m_, seg_max(ld, lo)), bc, m, log_d)
        k2 = [jnp.concatenate([t, t], axis=0) for t in k]
        qk = each(lambda q_, k2_: jnp.where(lo, _bdot(jnp.where(lo, q_, 0.0), k2_, _NT),
                                            _bdot(jnp.where(lo, 0.0, q_), k2_, _NT)), q, k2)
        s = each(lambda qk_, ld, mt: qk_ * jnp.exp(ld - mt), qk, log_d, m_t)
        inter_w = each(lambda bc_, m_, mt: jnp.exp(bc_ + m_ - mt), bc, m, m_t)
        v_bd = [jnp.where(bd, jnp.concatenate([t, t], axis=0), 0.0) for t in v]
        num = each(lambda s_, vb, iw, q_, c_: _bdot(s_, vb) + iw * _bdot(q_, c_), s, v_bd, inter_w, q, cst)
        den = each(lambda s_, iw, q_, n_: seg_sum(s_) + iw * seg_sum(q_ * n_), s, inter_w, q, nst)
        hh = each(lambda nu, de, mt: nu / jnp.maximum(jnp.abs(de), jnp.exp(-mt)), num, den, m_t)
        ms = [seg_sum(t * t) for t in hh]
        for (b, p), h_, ms_ in zip(items, hh, ms):
            o = pm_ref[b, rows, lanes(p, 3)]
            o_ref[b, rows, lanes(p, 0)] = (h_ * lax.rsqrt(ms_ * (1.0 / dh) + RMS_EPS) * nw_ref[:, lanes(p, 0)]
                                           * jax.nn.sigmoid(o)).astype(o_ref.dtype)
        m_new = each(lambda g_, m_, br, lr: jnp.maximum(g_ + m_, seg_max(g_ - br + lr, lo1)), g, m, b_row, li_row)
        decay = each(lambda g_, m_, mn: jnp.exp(g_ + m_ - mn), g, m, m_new)
        w_col = each(lambda g_, bc_, li_, mn: jnp.exp(g_ - bc_ + li_ - mn), g, bc, lic, m_new)
        upd = each(lambda k_, wc, v_: jnp.where(bd, _bdot(k_, wc * v_, _TN), 0.0), k, w_col, v)
        for i, (b, p) in enumerate(items):
            c_ref[b, p] = decay[i] * cst[i] + upd[i]
            n_ref[b, p] = decay[i] * nst[i] + jnp.sum(w_col[i] * k[i], axis=0, keepdims=True)
            m_ref[b, p] = m_new[i]
        return carry

    lax.fori_loop(0, seq // L, chunk, 0)


def _mlstm_group(pm, psm, bsz, i_bias, f_bias, norm_w):
    t = pm.shape[0]
    seq = t // bsz
    w, nh, dh = GROUP_W, GROUP_HEADS, HEAD_DIM
    bias = jnp.pad(jnp.concatenate([i_bias, f_bias]), (0, LANES - 2 * nh))[None, :]
    const = lambda shape: pl.BlockSpec(shape, lambda b, s: (0,) * len(shape))
    nb, ns = MLSTM_ROWS_PER_STEP, MLSTM_SEQ_SPLIT
    assert bsz % nb == 0 and seq % (ns * MLSTM_CHUNK) == 0
    seq_block = lambda n: pl.BlockSpec((nb, seq // ns, n), lambda b, s: (b, s, 0))
    return pl.pallas_call(
        _mlstm_kernel,
        out_shape=jax.ShapeDtypeStruct((bsz, seq, w), MIXER_OUT_DTYPE),
        grid=(bsz // nb, ns),
        in_specs=[seq_block(4 * w), seq_block(LANES), const((1, LANES)), const((1, w))],
        out_specs=seq_block(w),
        scratch_shapes=[pltpu.VMEM((nb, nh // 2, LANES, LANES), jnp.float32),
                        pltpu.VMEM((nb, nh // 2, 1, LANES), jnp.float32),
                        pltpu.VMEM((nb, nh // 2, 1, LANES), jnp.float32)],
        compiler_params=pltpu.CompilerParams(
            dimension_semantics=("parallel", "arbitrary"), vmem_limit_bytes=VMEM_LIMIT),
        name="mlstm",
    )(pm.reshape(bsz, seq, 4 * w), psm.reshape(bsz, seq, LANES), bias, norm_w[None, :]).reshape(t, w)


def _ssd_kernel(ps_ref, sm_ref, cw_ref, cb_ref, dtb_ref, aneg_ref, dsk_ref, nw_ref, o_ref,
                xbc_ref, st_ref, head_ref, *, rc):
    nb, seq, w = o_ref.shape
    nh, hp, ng, ns, L = GROUP_HEADS, HEAD_DIM, SSD_GROUPS, SSD_STATE, SSD_CHUNK
    cx = SSD_XBC
    pad = SUBLANES

    @pl.when(pl.program_id(1) == 0)
    def _():
        st_ref[...] = jnp.zeros_like(st_ref)
        head_ref[:, 0:pad, :] = jnp.zeros((nb, pad, cx), jnp.float32)

    cw = cw_ref[...]
    for b in range(nb):
        for c in range(seq // rc):
            r0 = c * rc
            cur = ps_ref[b, r0:r0 + rc, w:w + cx]
            if c == 0:
                head_ref[b, pad:pad + rc, :] = cur
            acc = cb_ref[...] + cw[CONV_K - 1:CONV_K, :] * cur
            for j in range(1, CONV_K):
                if c == 0:
                    sh = head_ref[b, pad - j:pad - j + rc, :]
                else:
                    sh = ps_ref[b, r0 - j:r0 - j + rc, w:w + cx]
                acc = acc + cw[CONV_K - 1 - j:CONV_K - j, :] * sh
            xbc_ref[b, r0:r0 + rc, :] = jax.nn.silu(acc)
        head_ref[b, 0:pad, :] = ps_ref[b, seq - pad:seq, w:w + cx]
    causal = lax.broadcasted_iota(jnp.int32, (L, L), 0) >= lax.broadcasted_iota(jnp.int32, (L, L), 1)
    gsz = w // ng
    rep = nh // ng
    groups = [(b, g) for b in range(nb) for g in range(ng)]
    heads = [(b, h) for b in range(nb) for h in range(nh)]

    def chunk(c, carry):
        r0 = pl.multiple_of(c * L, L)
        rows = pl.ds(r0, L)
        each = lambda f, *xs: [f(*args) for args in zip(*xs)]
        dt_all = [_softplus(sm_ref[b, rows, :] + dtb_ref[...]) for b in range(nb)]
        acs_all = [_cumsum_rows(aneg_ref[...] * t) for t in dt_all]
        acs_t = [t.T for t in acs_all]
        bm = {bg: xbc_ref[bg[0], rows, w + bg[1] * ns:w + (bg[1] + 1) * ns] for bg in groups}
        cm = {bg: xbc_ref[bg[0], rows, w + (ng + bg[1]) * ns:w + (ng + bg[1] + 1) * ns] for bg in groups}
        cb = {bg: _bdot(cm[bg], bm[bg], _NT) for bg in groups}
        grp = lambda b, h: (b, h // rep)
        cidx = lambda h: 2 * nh + h
        acs_col = [acs_all[b][:, cidx(h):cidx(h) + 1] for b, h in heads]
        acs_row = [acs_t[b][cidx(h):cidx(h) + 1, :] for b, h in heads]
        dt_col = [dt_all[b][:, cidx(h):cidx(h) + 1] for b, h in heads]
        a_last = [t[L - 1:L, :] for t in acs_col]
        xh = [xbc_ref[b, rows, h * hp:(h + 1) * hp] for b, h in heads]
        xdt = each(lambda x_, d_: x_ * d_, xh, dt_col)
        state = [st_ref[b, h] for b, h in heads]
        scores = [cb[grp(b, h)] * jnp.exp(jnp.where(causal, ac - ar, -jnp.inf))
                  for (b, h), ac, ar in zip(heads, acs_col, acs_row)]
        y_diag = each(_bdot, scores, xdt)
        y_off = [_bdot(cm[grp(b, h)], st, _NT) for (b, h), st in zip(heads, state)]
        ys = [yd + jnp.exp(ac) * yo + x_ * dsk_ref[:, h * hp:(h + 1) * hp]
              for (b, h), yd, ac, yo, x_ in zip(heads, y_diag, acs_col, y_off, xh)]
        upd = [_bdot(xd * jnp.exp(al - ac), bm[grp(b, h)], _TN)
               for (b, h), xd, al, ac in zip(heads, xdt, a_last, acs_col)]
        for i, (b, h) in enumerate(heads):
            st_ref[b, h] = jnp.exp(a_last[i]) * state[i] + upd[i]
        ysd = dict(zip(heads, ys))
        ygs = [jnp.concatenate([ysd[(b, g * rep + j)] for j in range(rep)], axis=1)
               * jax.nn.silu(ps_ref[b, rows, g * gsz:(g + 1) * gsz]) for b, g in groups]
        ms = [jnp.mean(t * t, axis=1, keepdims=True) for t in ygs]
        for (b, g), yg, m_ in zip(groups, ygs, ms):
            o_ref[b, rows, g * gsz:(g + 1) * gsz] = (yg * lax.rsqrt(m_ + RMS_EPS)
                                                     * nw_ref[:, g * gsz:(g + 1) * gsz]).astype(o_ref.dtype)
        return carry

    lax.fori_loop(0, seq // L, chunk, 0)


def _ssd_group(ps, psm, bsz, conv_w, conv_b, dt_bias, a_log, d_skip, norm_w):
    t = ps.shape[0]
    seq = t // bsz
    w, nh = GROUP_W, GROUP_HEADS
    lane_vec = lambda v: jnp.pad(v, (2 * nh, LANES - 3 * nh))[None, :]
    dsk = jnp.repeat(d_skip, HEAD_DIM)[None, :]
    const = lambda shape: pl.BlockSpec(shape, lambda b, s: (0,) * len(shape))
    nb, nsl = SSD_ROWS_PER_STEP, SSD_SEQ_SPLIT
    rc = 256
    slab = seq // nsl
    assert bsz % nb == 0 and seq % nsl == 0 and slab % rc == 0 and slab % SSD_CHUNK == 0
    seq_block = lambda n: pl.BlockSpec((nb, slab, n), lambda b, s: (b, s, 0))
    return pl.pallas_call(
        functools.partial(_ssd_kernel, rc=rc),
        out_shape=jax.ShapeDtypeStruct((bsz, seq, w), MIXER_OUT_DTYPE),
        grid=(bsz // nb, nsl),
        in_specs=[seq_block(w + SSD_XBC), seq_block(LANES),
                  const((CONV_K, SSD_XBC)), const((1, SSD_XBC)), const((1, LANES)), const((1, LANES)),
                  const((1, w)), const((1, w))],
        out_specs=seq_block(w),
        scratch_shapes=[pltpu.VMEM((nb, slab, SSD_XBC), jnp.float32),
                        pltpu.VMEM((nb, nh, HEAD_DIM, SSD_STATE), jnp.float32),
                        pltpu.VMEM((nb, SUBLANES + rc, SSD_XBC), jnp.float32)],
        compiler_params=pltpu.CompilerParams(
            dimension_semantics=("parallel", "arbitrary"), vmem_limit_bytes=VMEM_LIMIT),
        name="ssd",
    )(ps.reshape(bsz, seq, w + SSD_XBC), psm.reshape(bsz, seq, LANES), conv_w, conv_b[None, :], lane_vec(dt_bias),
      lane_vec(-jnp.exp(a_log)), dsk, norm_w[None, :]).reshape(t, w)


def _layer_norm(x, g, b):
    xc = x - jnp.mean(x, -1, keepdims=True)
    var = jnp.mean(xc * xc, -1, keepdims=True)
    return (xc * lax.rsqrt(var + LN_EPS)) * g + b


def _outproj_ln_kernel(x_ref, m0_ref, m1_ref, m2_ref, m3_ref, w_ref, g_ref, b_ref, *rest):
    gw = m0_ref.shape[1]
    acc = ALPHA * x_ref[...]
    for j, m_ref in enumerate((m0_ref, m1_ref, m2_ref, m3_ref)):
        acc = acc + _bdot(m_ref[...], w_ref[j * gw:(j + 1) * gw, :])
    x1 = _layer_norm(acc, g_ref[...], b_ref[...])
    if len(rest) == 1:
        rest[0][...] = x1
    else:
        wr_ref, o_ref, lg_ref, xb_ref = rest
        o_ref[...] = x1
        xb = x1.astype(jnp.bfloat16)
        xb_ref[...] = xb
        lg_ref[...] = jnp.dot(xb, wr_ref[...], preferred_element_type=jnp.float32)


def _outproj_ln(x, mixers, w_out, g, b, w_router=None, tm=512):
    t, d = x.shape
    assert t % tm == 0
    gw = mixers[0].shape[1]
    row = lambda n: pl.BlockSpec((tm, n), lambda i: (i, 0))
    const = lambda shape: pl.BlockSpec(shape, lambda i: (0, 0))
    in_specs = [row(d)] + [row(gw)] * 4 + [const((d, d)), const((1, d)), const((1, d))]
    args = [x, *mixers, w_out.astype(jnp.bfloat16), g[None, :], b[None, :]]
    out_shape = [jax.ShapeDtypeStruct((t, d), jnp.float32)]
    out_specs = [row(d)]
    if w_router is not None:
        wr = jnp.pad(w_router, ((0, 0), (0, LANES - w_router.shape[1]))).astype(jnp.bfloat16)
        in_specs.append(const((d, LANES)))
        args.append(wr)
        out_shape += [jax.ShapeDtypeStruct((t, LANES), jnp.float32), jax.ShapeDtypeStruct((t, d), jnp.bfloat16)]
        out_specs += [row(LANES), row(d)]
    return pl.pallas_call(
        _outproj_ln_kernel,
        out_shape=out_shape, grid=(t // tm,), in_specs=in_specs, out_specs=out_specs,
        compiler_params=pltpu.CompilerParams(
            dimension_semantics=("parallel",), vmem_limit_bytes=VMEM_LIMIT),
        name="outproj_ln",
    )(*args)


def _pe_ln_kernel(x_ref, p_ref, wp_ref, wg_ref, bg_ref, g_ref, b_ref, *rest):
    *ff_refs, o_ref = rest
    x1 = x_ref[...]
    pe = _bdot(p_ref[0], wp_ref[...]) * jax.nn.sigmoid(_bdot(x1, wg_ref[...]) + bg_ref[...])
    ff = functools.reduce(lambda a, c: a + c, [r[...] for r in ff_refs])
    o_ref[...] = _layer_norm(ALPHA * x1 + ff + pe, g_ref[...], b_ref[...])


def _pe_ln(x1, ff_parts, p, layer, pe_proj, gate_w, gate_b, g, b, tm=512):
    t, d = x1.shape
    assert t % tm == 0
    pd = p.shape[2]
    row = lambda n: pl.BlockSpec((tm, n), lambda i: (i, 0))
    const = lambda shape: pl.BlockSpec(shape, lambda i: (0, 0))
    return pl.pallas_call(
        _pe_ln_kernel,
        out_shape=jax.ShapeDtypeStruct((t, d), jnp.float32),
        grid=(t // tm,),
        in_specs=[row(d), pl.BlockSpec((1, tm, pd), lambda i: (layer, i, 0)), const((pd, d)), const((d, d)),
                  const((1, d)), const((1, d)), const((1, d))]
        + [row(d)] * len(ff_parts),
        out_specs=row(d),
        compiler_params=pltpu.CompilerParams(
            dimension_semantics=("parallel",), vmem_limit_bytes=VMEM_LIMIT),
        name="pe_ln",
    )(x1, p, pe_proj.astype(jnp.bfloat16), gate_w.astype(jnp.bfloat16), gate_b[None, :], g[None, :], b[None, :],
      *ff_parts)


def _pair_head_sums(x):
    lo = lax.broadcasted_iota(jnp.int32, (x.shape[0], LANES), 1) < HEAD_DIM
    tiles = []
    for hp in range(x.shape[1] // LANES):
        t = x[:, hp * LANES:(hp + 1) * LANES]
        a = jnp.sum(jnp.where(lo, t, 0.0), axis=1, keepdims=True)
        b = jnp.sum(jnp.where(lo, 0.0, t), axis=1, keepdims=True)
        tiles.append(jnp.where(lo, a, b))
    return jnp.concatenate(tiles, axis=1)


def _rwkv_pre_kernel(c_ref, mu_ref, w0_ref, w2_ref, a0_ref, a2_ref, g2_ref, kk_ref, ka_ref, rk_ref,
                     pa_ref, pb_ref, pc_ref, g_ref, bonus_ref, *, rc):
    seq = c_ref.shape[0]
    w, p = GROUP_W, HEAD_DIM
    o1 = 3 * w + RWKV_DECAY_RANK
    o2 = o1 + RWKV_ICL_RANK
    nchunks = seq // rc
    blk = rc // RWKV_TB
    row = lax.broadcasted_iota(jnp.int32, (rc, c_ref.shape[1]), 0)
    lane = lax.broadcasted_iota(jnp.int32, (rc, LANES), 1)

    def unit_keys(cols):
        kk = cols[:, w:2 * w] * kk_ref[...]
        return kk * lax.rsqrt(jnp.maximum(_pair_head_sums(kk * kk), 1e-24))

    def pack(dst, c, x, z):
        for hp in range(w // LANES):
            x2, z2 = x[:, hp * LANES:(hp + 1) * LANES], z[:, hp * LANES:(hp + 1) * LANES]
            even = jnp.where(lane < p, x2, pltpu.roll(z2, p, 1))
            odd = jnp.where(lane < p, pltpu.roll(x2, p, 1), z2)
            for h, val in ((2 * hp, even), (2 * hp + 1, odd)):
                dst[c * blk:(c + 1) * blk, h, 0] = val.reshape(blk, RWKV_TB, LANES)

    for c in range(nchunks):
        r0 = c * rc
        cur = c_ref[r0:r0 + rc, :]
        if c == 0:
            prev = jnp.where(row == 0, 0.0, pltpu.roll(cur, 1, 0))
        else:
            prev = c_ref[r0 - 1:r0 - 1 + rc, :]
        if c == nchunks - 1:
            nxt = jnp.where(row == rc - 1, 0.0, pltpu.roll(cur, rc - 1, 0))
        else:
            nxt = c_ref[r0 + 1:r0 + 1 + rc, :]
        cols = cur + (prev - cur) * mu_ref[...]
        cols_next = nxt + (cur - nxt) * mu_ref[...]
        k = cols[:, w:2 * w]
        w_log = -_softplus(-(w0_ref[...] + _bdot(jnp.tanh(cols[:, 3 * w:o1]), w2_ref[...]))) - 0.5
        a = jax.nn.sigmoid(a0_ref[...] + _bdot(cols[:, o1:o2], a2_ref[...]))
        kk = unit_keys(cols)
        r, v = cols[:, 0:w], cols[:, 2 * w:3 * w]
        k_mod = k * (1.0 + (a - 1.0) * ka_ref[...])
        pack(pa_ref, c, jnp.exp(-jnp.exp(w_log)), kk * a)
        pack(pb_ref, c, k_mod, r)
        pack(pc_ref, c, unit_keys(cols_next), v)
        g_ref[r0:r0 + rc, :] = _bdot(jax.nn.sigmoid(cols[:, o2:]), g2_ref[...])
        bonus_ref[r0:r0 + rc, :] = _pair_head_sums(r * k_mod * rk_ref[...]) * v


def _rwkv_post_kernel(y_ref, bonus_ref, g_ref, lw_ref, lb_ref, o_ref):
    p = HEAD_DIM
    rows = o_ref.shape[0]
    inv = 1.0 / p
    lo = lax.broadcasted_iota(jnp.int32, (rows, LANES), 1) < p
    for hp in range(GROUP_HEADS // 2):
        ls = slice(hp * LANES, (hp + 1) * LANES)
        y = jnp.where(lo, y_ref[:, 2 * hp, 0].reshape(rows, LANES), y_ref[:, 2 * hp + 1, 0].reshape(rows, LANES))
        yc = y - _pair_head_sums(y) * inv
        yn = yc * lax.rsqrt(_pair_head_sums(yc * yc) * inv + RWKV_GN_EPS) * lw_ref[:, ls] + lb_ref[:, ls]
        o_ref[:, ls] = ((yn + bonus_ref[:, ls]) * g_ref[:, ls]).astype(o_ref.dtype)


def _rwkv7_group(pr, bsz, mu, w0, w2, a0, a2, g2, k_k, k_a, r_k, ln_w, ln_b, tm=512):
    t = pr.shape[0]
    seq = t // bsz
    w = GROUP_W
    bf = lambda x: x.astype(jnp.bfloat16)
    const = lambda shape: pl.BlockSpec(shape, lambda b: (0, 0))
    vec = lambda x: x.reshape(1, -1)
    nh, tb = GROUP_HEADS, RWKV_TB
    nblk = seq // tb
    nat = jax.ShapeDtypeStruct((t, w), jnp.float32)
    packed = jax.ShapeDtypeStruct((nblk, nh, bsz, tb, LANES), jnp.float32)
    pspec = pl.BlockSpec((nblk, nh, 1, tb, LANES), lambda b: (0, 0, b, 0, 0))
    rc = 256
    assert seq % rc == 0 and seq % tm == 0 and tm % tb == 0 and rc % tb == 0
    seq_rows = pl.BlockSpec((seq, w), lambda b: (b, 0))
    pa, pb, pc, g, bonus = pl.pallas_call(
        functools.partial(_rwkv_pre_kernel, rc=rc),
        out_shape=[packed] * 3 + [nat] * 2,
        grid=(bsz,),
        in_specs=[pl.BlockSpec((seq, R_COLS), lambda b: (b, 0)), const((1, R_COLS)), const((1, w)),
                  const((RWKV_DECAY_RANK, w)), const((1, w)), const((RWKV_ICL_RANK, w)),
                  const((RWKV_GATE_RANK, w)), const((1, w)), const((1, w)), const((1, w))],
        out_specs=[pspec] * 3 + [seq_rows] * 2,
        compiler_params=pltpu.CompilerParams(
            dimension_semantics=("parallel",), vmem_limit_bytes=VMEM_LIMIT),
        name="rwkv_pre",
    )(pr, vec(mu), vec(w0), bf(w2), vec(a0), bf(a2), bf(g2), vec(k_k), vec(k_a), vec(r_k))
    y = _rwkv_scan(pa, pb, pc, bsz)
    gb = tm // tb
    per_b = seq // tm
    bspec = pl.BlockSpec((gb, nh, 1, tb, LANES), lambda b, i: (i, 0, b, 0, 0))
    row = pl.BlockSpec((tm, w), lambda b, i: (b * per_b + i, 0))
    const2 = lambda shape: pl.BlockSpec(shape, lambda b, i: (0, 0))
    return pl.pallas_call(
        _rwkv_post_kernel,
        out_shape=jax.ShapeDtypeStruct((t, w), MIXER_OUT_DTYPE),
        grid=(bsz, per_b),
        in_specs=[bspec, row, row] + [const2((1, w))] * 2,
        out_specs=row,
        compiler_params=pltpu.CompilerParams(
            dimension_semantics=("parallel", "parallel"), vmem_limit_bytes=VMEM_LIMIT),
        name="rwkv_post",
    )(y, bonus, g, vec(ln_w), vec(ln_b))


FF_CHUNK = 512


def _swiglu_kernel(be_ref, x_ref, w1_ref, w3_ref, w2_ref, g_ref, o_ref):
    del be_ref
    xb = x_ref[...].astype(jnp.bfloat16)
    dff = w1_ref.shape[-1]
    acc = None
    for c0 in range(0, dff, FF_CHUNK):
        c1 = min(c0 + FF_CHUNK, dff)
        h1 = jnp.dot(xb, w1_ref[0, :, c0:c1], preferred_element_type=jnp.float32)
        h3 = jnp.dot(xb, w3_ref[0, :, c0:c1], preferred_element_type=jnp.float32)
        h = (jax.nn.silu(h1) * h3).astype(jnp.bfloat16)
        part = jnp.dot(h, w2_ref[0, c0:c1, :], preferred_element_type=jnp.float32)
        acc = part if acc is None else acc + part
    o_ref[...] = acc * g_ref[...]


def _grouped_swiglu(xrows, block_e, row_gate, w1, w3, w2, bm):
    rows, d = xrows.shape
    dff = w1.shape[-1]
    assert rows % bm == 0
    grid_spec = pltpu.PrefetchScalarGridSpec(
        num_scalar_prefetch=1,
        grid=(rows // bm,),
        in_specs=[pl.BlockSpec((bm, d), lambda i, be: (i, 0)),
                  pl.BlockSpec((1, d, dff), lambda i, be: (be[i], 0, 0)),
                  pl.BlockSpec((1, d, dff), lambda i, be: (be[i], 0, 0)),
                  pl.BlockSpec((1, dff, d), lambda i, be: (be[i], 0, 0)),
                  pl.BlockSpec((bm, 1), lambda i, be: (i, 0))],
        out_specs=pl.BlockSpec((bm, d), lambda i, be: (i, 0)),
    )
    return pl.pallas_call(
        _swiglu_kernel,
        out_shape=jax.ShapeDtypeStruct((rows, d), jnp.float32),
        grid_spec=grid_spec,
        compiler_params=pltpu.CompilerParams(
            dimension_semantics=("arbitrary",), vmem_limit_bytes=VMEM_LIMIT),
        cost_estimate=pl.CostEstimate(
            flops=6 * rows * d * dff, transcendentals=rows * dff,
            bytes_accessed=rows * d * (xrows.dtype.itemsize + 4) + rows * 4 + 2 * (w1.size + w3.size + w2.size)),
        name="swiglu",
    )(block_e, xrows, w1, w3, w2, row_gate)


def _cast_kernel(x_ref, o_ref):
    o_ref[0] = x_ref[0, 0].astype(o_ref.dtype)


def _layer_weights_bf16(w, layer):
    _, ne, r, c = w.shape
    return pl.pallas_call(
        _cast_kernel,
        out_shape=jax.ShapeDtypeStruct((ne, r, c), jnp.bfloat16),
        grid=(ne,),
        in_specs=[pl.BlockSpec((1, 1, r, c), lambda e: (layer, e, 0, 0))],
        out_specs=pl.BlockSpec((1, r, c), lambda e: (e, 0, 0)),
        compiler_params=pltpu.CompilerParams(
            dimension_semantics=("parallel",), vmem_limit_bytes=VMEM_LIMIT),
        cost_estimate=pl.CostEstimate(flops=0, transcendentals=0, bytes_accessed=ne * r * c * (4 + 2)),
        name="cast_bf16",
    )(w)


def _swiglu(x2d, w1, w3, w2, layer, bm=512):
    rows = x2d.shape[0]
    bf = lambda t: _layer_weights_bf16(t[:, None], layer)
    return _grouped_swiglu(x2d, jnp.zeros((rows // bm,), jnp.int32), jnp.ones((rows, 1), jnp.float32),
                           bf(w1), bf(w3), bf(w2), bm)


def _moe_swiglu(xf, logits, w1, w3, w2, layer):
    T, d = xf.shape
    i32 = jnp.int32
    lt = logits.T
    eids = jnp.arange(N_EXPERTS, dtype=i32)[:, None]
    e0 = jnp.argmax(lt, axis=0).astype(i32)
    m0 = jnp.max(lt, axis=0)
    lt1 = jnp.where(eids == e0[None, :], -jnp.inf, lt)
    e1 = jnp.argmax(lt1, axis=0).astype(i32)
    m1 = jnp.max(lt1, axis=0)
    ex = jnp.exp(m1 - m0)
    gates = (1.0 / (1.0 + ex), ex / (1.0 + ex))
    oh = [(eids == e[None, :]).astype(i32) for e in (e0, e1)]
    both = oh[0] + oh[1]
    incl = jnp.cumsum(both, axis=1)
    counts = incl[:, -1]
    padded = (counts + MOE_BLOCK - 1) // MOE_BLOCK * MOE_BLOCK
    pad_end = jnp.cumsum(padded)
    pad_start = pad_end - padded
    start = jnp.cumsum(counts) - counts
    offs = (incl - both) + pad_start[:, None]
    pos = [jnp.sum(o * offs, axis=0) for o in oh]
    tok = jnp.arange(T, dtype=i32)
    _, st, sg = lax.sort((jnp.concatenate(pos), jnp.concatenate([tok, tok]), jnp.concatenate(gates)), num_keys=1)
    n_blocks = -(-(T * TOP_K) // MOE_BLOCK) + N_EXPERTS
    blk = jnp.arange(n_blocks, dtype=i32)
    block_e = jnp.minimum(jnp.searchsorted(pad_end, blk * MOE_BLOCK, side='right'), N_EXPERTS - 1).astype(i32)
    q = (blk * MOE_BLOCK - pad_start[block_e])[:, None] + jnp.arange(MOE_BLOCK, dtype=i32)[None, :]
    valid = q < counts[block_e][:, None]
    src = jnp.clip(start[block_e][:, None] + q, 0, T * TOP_K - 1)
    slot_tok = jnp.where(valid, st[src], 0).reshape(-1)
    slot_gate = jnp.where(valid, sg[src], 0.0).reshape(-1, 1)
    bf = lambda t: _layer_weights_bf16(t, layer)
    yb = _grouped_swiglu(xf[slot_tok], block_e, slot_gate, bf(w1), bf(w3), bf(w2), MOE_BLOCK)
    return [yb[pos[0]], yb[pos[1]]]


def kernel(x, p, w_in, m_i_bias, m_f_bias, m_norm_w, g_conv_w, g_conv_b, g_w_a, g_b_a, g_w_x, g_b_x,
           g_lambda, r_mu, r_w0, r_w2, r_a0, r_a2, r_g2, r_k_k, r_k_a, r_r_k, r_ln_w, r_ln_b,
           s_conv_w, s_conv_b, s_dt_bias, s_a_log, s_d, s_norm_w, w_out, ln1_g, ln1_b, ln2_g, ln2_b,
           f_w1, f_w3, f_w2, e_router, e_w1, e_w3, e_w2, pe_proj, pe_gate_w, pe_gate_b):
    bsz, seq, d = x.shape
    T = bsz * seq
    x = x.reshape(T, d)
    for i in range(DEPTH):
        pm, pg, pr, ps, psm = _inproj(x, _split_w_in(w_in, i))
        mixers = [
            _mlstm_group(pm, psm, bsz, m_i_bias[i], m_f_bias[i], m_norm_w[i]),
            _rglru_group(pg, bsz, g_conv_w[i], g_conv_b[i], g_w_a[i], g_b_a[i], g_w_x[i], g_b_x[i], g_lambda[i]),
            _rwkv7_group(pr, bsz, r_mu[i], r_w0[i], r_w2[i], r_a0[i], r_a2[i], r_g2[i],
                         r_k_k[i], r_k_a[i], r_r_k[i], r_ln_w[i], r_ln_b[i]),
            _ssd_group(ps, psm, bsz, s_conv_w[i], s_conv_b[i], s_dt_bias[i], s_a_log[i], s_d[i], s_norm_w[i]),
        ]
        if i % 2 == 0:
            (x,) = _outproj_ln(x, mixers, w_out[i], ln1_g[i], ln1_b[i])
            ff = [_swiglu(x, f_w1, f_w3, f_w2, i // 2)]
        else:
            x, logits, x_bf16 = _outproj_ln(x, mixers, w_out[i], ln1_g[i], ln1_b[i], e_router[i // 2])
            ff = _moe_swiglu(x_bf16, logits[:, :N_EXPERTS], e_w1, e_w3, e_w2, i // 2)
        x = _pe_ln(x, ff, p.reshape(DEPTH, T, PE_DIM), i, pe_proj[i], pe_gate_w[i], pe_gate_b[i], ln2_g[i], ln2_b[i])
    return x.reshape(bsz, seq, d)
```

```python
import functools

import jax
import jax.numpy as jnp
from jax import lax
from jax.experimental import pallas as pl
from jax.experimental.pallas import tpu as pltpu

D_MODEL = 1024
DEPTH = 4
PE_DIM = 256
GROUP_W = 256
HEAD_DIM = 64
GROUP_HEADS = 4
CONV_K = 4
MLSTM_CHUNK = 64
RG_C = 8.0
RWKV_DECAY_RANK = 32
RWKV_ICL_RANK = 32
RWKV_GATE_RANK = 64
RWKV_GN_EPS = 64e-5
SSD_STATE = 64
SSD_GROUPS = 2
SSD_CHUNK = 128
SSD_XBC = GROUP_W + 2 * SSD_GROUPS * SSD_STATE
M_COLS = 4 * GROUP_W + 2 * GROUP_HEADS
G_COLS = 2 * GROUP_W
R_COLS = 3 * GROUP_W + RWKV_DECAY_RANK + RWKV_ICL_RANK + RWKV_GATE_RANK
S_COLS = GROUP_W + SSD_XBC + GROUP_HEADS
N_IN = M_COLS + G_COLS + R_COLS + S_COLS
N_EXPERTS = 8
TOP_K = 2
MOE_BLOCK = 512
ALPHA = (2 * DEPTH) ** 0.25
LN_EPS = 1e-5
RMS_EPS = 1e-6

LANES = 128
SUBLANES = 8
VMEM_LIMIT = 56 * 1024 * 1024


RWKV_TB = 64


def _rwkv_scan_kernel(pa_ref, pb_ref, pc_ref, y_ref, s_ref, sa_ref, ta_ref, tb_ref, tc_ref, yb_ref,
                      *, tb, nb, nk, nacc):
    @pl.when(pl.program_id(0) == 0)
    def _():
        s_ref[...] = jnp.zeros_like(s_ref)
        sa_ref[...] = jnp.zeros_like(sa_ref)

    nh = GROUP_HEADS
    half = nk // 2
    lane = lax.broadcasted_iota(jnp.int32, (half, LANES), 1)
    rows = lambda t, h: pl.ds(h * nb * tb + t, nb, stride=tb)

    def prepare(t, carry):
        for src, dst in ((pa_ref, ta_ref), (pb_ref, tb_ref), (pc_ref, tc_ref)):
            parts = [src[rows(t, h), :] for h in range(nh)]
            dst[t] = jnp.concatenate(parts + parts, axis=0).T
        return carry

    lax.fori_loop(0, tb, prepare, 0, unroll=16)

    def step(t, sa):
        bc = lambda ref, r: jnp.broadcast_to(ref[t, pl.ds(r, 1), :], (half, LANES))
        v_t = jnp.where(lane < nk, tc_ref[t, nk:nk + half, :], tc_ref[t, nk + half:2 * nk, :])
        y_acc = [None] * nacc
        sa_acc = [None] * nacc
        for k in range(nk):
            s_k = s_ref[k] * bc(ta_ref, k) - sa * bc(ta_ref, nk + k) + v_t * bc(tb_ref, k)
            s_ref[k] = s_k
            y_k = s_k * bc(tb_ref, nk + k)
            n_k = s_k * bc(tc_ref, k)
            a = k % nacc
            y_acc[a] = y_k if y_acc[a] is None else y_acc[a] + y_k
            sa_acc[a] = n_k if sa_acc[a] is None else sa_acc[a] + n_k
        yb_ref[t] = functools.reduce(lambda p, q: p + q, y_acc)
        return functools.reduce(lambda p, q: p + q, sa_acc)

    sa_ref[...] = lax.fori_loop(0, tb, step, sa_ref[...])

    def emit(t, carry):
        y = yb_ref[t]
        y_sw = pltpu.roll(y, nk, 1)
        y_nat = jnp.concatenate([y, y_sw, y, y_sw], axis=0).T
        for h in range(nh):
            y_ref[rows(t, h), :] = y_nat[h * nb:(h + 1) * nb, :]
        return carry

    lax.fori_loop(0, tb, emit, 0, unroll=16)


def _rwkv_scan(pa, pb, pc, bsz):
    nblk, nh, _, tb, _ = pa.shape
    n = HEAD_DIM
    assert 2 * bsz * nh == LANES and 2 * n == LANES
    flat = lambda t: t.reshape(nblk * nh * bsz * tb, LANES)
    spec = pl.BlockSpec((nh * bsz * tb, LANES), lambda i: (i, 0))
    y = pl.pallas_call(
        functools.partial(_rwkv_scan_kernel, tb=tb, nb=bsz, nk=n, nacc=2),
        out_shape=jax.ShapeDtypeStruct((nblk * nh * bsz * tb, LANES), jnp.float32),
        grid=(nblk,),
        in_specs=[spec, spec, spec],
        out_specs=spec,
        scratch_shapes=[pltpu.VMEM((n, n // 2, LANES), jnp.float32), pltpu.VMEM((n // 2, LANES), jnp.float32)]
        + [pltpu.VMEM((tb, LANES, LANES), jnp.float32)] * 3 + [pltpu.VMEM((tb, n // 2, LANES), jnp.float32)],
        compiler_params=pltpu.CompilerParams(
            dimension_semantics=("arbitrary",), vmem_limit_bytes=VMEM_LIMIT),
        name="rwkv_scan",
    )(flat(pa), flat(pb), flat(pc))
    return y.reshape(pa.shape)


def _inproj_kernel(x_ref, *refs):
    n = len(refs) // 2
    xb = x_ref[...].astype(jnp.bfloat16)
    for w_ref, o_ref in zip(refs[:n], refs[n:]):
        o_ref[...] = jnp.dot(xb, w_ref[...], preferred_element_type=jnp.float32)


def _inproj(x, weights, tm=512):
    t, d = x.shape
    assert t % tm == 0
    return pl.pallas_call(
        _inproj_kernel,
        out_shape=[jax.ShapeDtypeStruct((t, w.shape[1]), jnp.float32) for w in weights],
        grid=(t // tm,),
        in_specs=[pl.BlockSpec((tm, d), lambda i: (i, 0))]
        + [pl.BlockSpec(w.shape, lambda i: (0, 0)) for w in weights],
        out_specs=[pl.BlockSpec((tm, w.shape[1]), lambda i: (i, 0)) for w in weights],
        compiler_params=pltpu.CompilerParams(
            dimension_semantics=("parallel",), vmem_limit_bytes=VMEM_LIMIT),
        name="inproj",
    )(x, *weights)


def _split_w_in_kernel(w_ref, m_ref, g_ref, r_ref, s_ref, sm_ref):
    m0, g0, r0, s0 = 0, M_COLS, M_COLS + G_COLS, M_COLS + G_COLS + R_COLS
    bf = jnp.bfloat16
    m_ref[...] = w_ref[0, :, m0:m0 + 4 * GROUP_W].astype(bf)
    g_ref[...] = w_ref[0, :, g0:g0 + G_COLS].astype(bf)
    r_ref[...] = w_ref[0, :, r0:r0 + R_COLS].astype(bf)
    s_ref[...] = w_ref[0, :, s0:s0 + GROUP_W + SSD_XBC].astype(bf)
    gates = jnp.concatenate([w_ref[0, :, m0 + 4 * GROUP_W:g0], w_ref[0, :, s0 + GROUP_W + SSD_XBC:N_IN]], axis=1)
    pad = jnp.zeros((gates.shape[0], LANES - gates.shape[1]), jnp.float32)
    sm_ref[...] = jnp.concatenate([gates, pad], axis=1).astype(bf)


def _split_w_in(w_in, layer):
    _, d, n = w_in.shape
    widths = (4 * GROUP_W, G_COLS, R_COLS, GROUP_W + SSD_XBC, LANES)
    return pl.pallas_call(
        _split_w_in_kernel,
        out_shape=[jax.ShapeDtypeStruct((d, c), jnp.bfloat16) for c in widths],
        grid=(1,),
        in_specs=[pl.BlockSpec((1, d, n), lambda i: (layer, 0, 0))],
        out_specs=[pl.BlockSpec((d, c), lambda i: (0, 0)) for c in widths],
        compiler_params=pltpu.CompilerParams(vmem_limit_bytes=VMEM_LIMIT),
        name="split_w_in",
    )(w_in)


def _shift_rows(x, s, fill, row):
    return jnp.where(row >= s, pltpu.roll(x, s, 0), fill)


def _rglru_kernel(g_ref, cw_ref, cb_ref, wg_ref, bg_ref, c_ref, o_ref, xpad_ref, a_ref, u_ref, *, rc, unroll):
    seq, w = o_ref.shape
    pad = SUBLANES
    xpad_ref[0:pad, :] = jnp.zeros((pad, w), jnp.float32)
    xpad_ref[pad:, :] = g_ref[:, 0:w]
    cw = cw_ref[...]
    for c in range(seq // rc):
        r0 = c * rc
        xc = cb_ref[...] + cw[CONV_K - 1:CONV_K, :] * g_ref[r0:r0 + rc, 0:w]
        for j in range(1, CONV_K):
            xc = xc + cw[CONV_K - 1 - j:CONV_K - j, :] * xpad_ref[pad + r0 - j:pad + r0 - j + rc, :]
        z = jnp.dot(xc.astype(jnp.bfloat16), wg_ref[...], preferred_element_type=jnp.float32) + bg_ref[...]
        z = jax.nn.sigmoid(z)
        a = jnp.exp(c_ref[...] * z[:, 0:w])
        a_ref[r0:r0 + rc, :] = a
        u_ref[r0:r0 + rc, :] = jnp.sqrt(1.0 - a * a) * (z[:, w:] * xc)

    row = lax.broadcasted_iota(jnp.int32, (SUBLANES, w), 0)

    def tiles(i, h):
        for j in range(unroll):
            r = pl.multiple_of((i * unroll + j) * SUBLANES, SUBLANES)
            a = a_ref[pl.ds(r, SUBLANES), :]
            u = u_ref[pl.ds(r, SUBLANES), :]
            for s in (1, 2, 4):
                u = a * _shift_rows(u, s, 0.0, row) + u
                a = a * _shift_rows(a, s, 1.0, row)
            ht = a * h + u
            o_ref[pl.ds(r, SUBLANES), :] = ht * jax.nn.gelu(g_ref[pl.ds(r, SUBLANES), w:2 * w])
            h = jnp.broadcast_to(ht[SUBLANES - 1:SUBLANES, :], (SUBLANES, w))
        return h

    lax.fori_loop(0, seq // (SUBLANES * unroll), tiles, jnp.zeros((SUBLANES, w), jnp.float32))


def _block_diag(w):
    h, p, _ = w.shape
    eye = jnp.eye(h, dtype=w.dtype)
    return jnp.einsum('hij,hg->higj', w, eye).reshape(h * p, h * p)


def _rglru_group(g_cols, bsz, conv_w, conv_b, w_a, b_a, w_x, b_x, lam):
    t = g_cols.shape[0]
    seq = t // bsz
    w = GROUP_W
    wg = jnp.concatenate([_block_diag(w_a), _block_diag(w_x)], axis=1).astype(jnp.bfloat16)
    bg = jnp.concatenate([b_a, b_x])[None, :]
    cdec = (-RG_C * jax.nn.softplus(-lam))[None, :]
    const = lambda shape: pl.BlockSpec(shape, lambda b: (0, 0))
    rc, unroll = 256, 8
    assert seq % rc == 0 and seq % (SUBLANES * unroll) == 0
    return pl.pallas_call(
        functools.partial(_rglru_kernel, rc=rc, unroll=unroll),
        out_shape=jax.ShapeDtypeStruct((t, w), jnp.float32),
        grid=(bsz,),
        in_specs=[pl.BlockSpec((seq, 2 * w), lambda b: (b, 0)), const((CONV_K, w)), const((1, w)),
                  const((w, 2 * w)), const((1, 2 * w)), const((1, w))],
        out_specs=pl.BlockSpec((seq, w), lambda b: (b, 0)),
        scratch_shapes=[pltpu.VMEM((seq + SUBLANES, w), jnp.float32),
                        pltpu.VMEM((seq, w), jnp.float32), pltpu.VMEM((seq, w), jnp.float32)],
        compiler_params=pltpu.CompilerParams(
            dimension_semantics=("parallel",), vmem_limit_bytes=VMEM_LIMIT),
        name="rglru",
    )(g_cols, conv_w, conv_b[None, :], wg, bg, cdec)


MLSTM_ROWS_PER_STEP = 4
MLSTM_SEQ_SPLIT = 2
SSD_ROWS_PER_STEP = 4
SSD_SEQ_SPLIT = 2
MIXER_OUT_DTYPE = jnp.bfloat16
_NT = (((1,), (1,)), ((), ()))
_TN = (((0,), (0,)), ((), ()))


def _bdot(a, b, dims=(((1,), (0,)), ((), ()))):
    return lax.dot_general(a.astype(jnp.bfloat16), b.astype(jnp.bfloat16), dims,
                           preferred_element_type=jnp.float32)


def _cumsum_rows(x):
    n = x.shape[0]
    tri = (lax.broadcasted_iota(jnp.int32, (n, n), 0) >= lax.broadcasted_iota(jnp.int32, (n, n), 1))
    return jnp.dot(tri.astype(jnp.float32), x, precision=lax.Precision.HIGHEST,
                   preferred_element_type=jnp.float32)


def _log_sigmoid(x):
    return jnp.minimum(x, 0.0) - jnp.log1p(jnp.exp(-jnp.abs(x)))


def _softplus(x):
    return jnp.maximum(x, 0.0) + jnp.log1p(jnp.exp(-jnp.abs(x)))


def _mlstm_kernel(pm_ref, sm_ref, bias_ref, nw_ref, o_ref, c_ref, n_ref, m_ref):
    _, seq, w = o_ref.shape
    nh, dh, L = GROUP_HEADS, HEAD_DIM, MLSTM_CHUNK

    @pl.when(pl.program_id(1) == 0)
    def _():
        c_ref[...] = jnp.zeros_like(c_ref)
        n_ref[...] = jnp.zeros_like(n_ref)
        m_ref[...] = jnp.zeros_like(m_ref)

    f32 = jnp.float32
    hi = lax.Precision.HIGHEST
    col = lax.broadcasted_iota(jnp.int32, (L, LANES), 1)
    lrow = lax.broadcasted_iota(jnp.int32, (L, LANES), 0)
    scol = col & (dh - 1)
    lo = col < dh
    lo1 = lo[0:1]
    sq = lax.broadcasted_iota(jnp.int32, (LANES, LANES), 0)
    sr = lax.broadcasted_iota(jnp.int32, (LANES, LANES), 1)
    bd = (sq < dh) == (sr < dh)
    e_row = lax.broadcasted_iota(jnp.int32, (LANES, 2 * w), 0)
    e_col = lax.broadcasted_iota(jnp.int32, (LANES, 2 * w), 1)
    e_head = jnp.where(e_col < w, 0, nh) + ((e_col & (w - 1)) >> (dh.bit_length() - 1))
    spread = (e_row == e_head).astype(f32)
    tri = (lax.broadcasted_iota(jnp.int32, (L, L), 0) >= lax.broadcasted_iota(jnp.int32, (L, L), 1)).astype(f32)

    def seg_max(x, first):
        a = jnp.max(jnp.where(first, x, -jnp.inf), axis=1, keepdims=True)
        b = jnp.max(jnp.where(first, -jnp.inf, x), axis=1, keepdims=True)
        return jnp.where(first, a, b)

    def seg_sum(x):
        a = jnp.sum(jnp.where(lo, x, 0.0), axis=1, keepdims=True)
        b = jnp.sum(jnp.where(lo, 0.0, x), axis=1, keepdims=True)
        return jnp.where(lo, a, b)

    nb = pm_ref.shape[0]
    items = [(b, p) for b in range(nb) for p in range(nh // 2)]
    lanes = lambda p, sec: slice(sec * w + p * LANES, sec * w + (p + 1) * LANES)

    def chunk(c, carry):
        r0 = pl.multiple_of(c * L, L)
        rows = pl.ds(r0, L)
        each = lambda f, *xs: [f(*args) for args in zip(*xs)]
        gts = [sm_ref[b, rows, :] + bias_ref[...] for b in range(nb)]
        gts = [jnp.where(col < nh, t, jnp.where(col < 2 * nh, _log_sigmoid(t), 0.0)) for t in gts]
        rep = [jnp.dot(t, spread, precision=hi, preferred_element_type=f32) for t in gts]
        b_rep = [jnp.dot(tri, t[:, w:], precision=hi, preferred_element_type=f32) for t in rep]
        bc = [b_rep[b][:, lanes(p, 0)] for b, p in items]
        lic = [rep[b][:, lanes(p, 0)] for b, p in items]
        lfc = [rep[b][:, lanes(p, 1)] for b, p in items]
        b_row = [jnp.sum(jnp.where(lrow <= scol, t, 0.0), axis=0, keepdims=True) for t in lfc]
        li_row = [jnp.sum(jnp.where(lrow == scol, t, 0.0), axis=0, keepdims=True) for t in lic]
        g = [t[L - 1:L, :] for t in bc]
        q = [pm_ref[b, rows, lanes(p, 0)] for b, p in items]
        k = [pm_ref[b, rows, lanes(p, 1)] * (dh ** -0.5) for b, p in items]
        v = [pm_ref[b, rows, lanes(p, 2)] for b, p in items]
        cst = [c_ref[b, p] for b, p in items]
        nst = [n_ref[b, p] for b, p in items]
        m = [m_ref[b, p] for b, p in items]
        log_d = each(lambda bc_, br, lr: jnp.where(lrow >= scol, bc_ - br + lr, -jnp.inf), bc, b_row, li_row)
        m_t = each(lambda bc_, m_, ld: jnp.maximum(bc_ + m_, seg_max(ld, lo)), bc, m, log_d)
        k2 = [jnp.concatenate([t, t], axis=0) for t in k]
        qk = each(lambda q_, k2_: jnp.where(lo, _bdot(jnp.where(lo, q_, 0.0), k2_, _NT),
                                            _bdot(jnp.where(lo, 0.0, q_), k2_, _NT)), q, k2)
        s = each(lambda qk_, ld, mt: qk_ * jnp.exp(ld - mt), qk, log_d, m_t)
        inter_w = each(lambda bc_, m_, mt: jnp.exp(bc_ + m_ - mt), bc, m, m_t)
        v_bd = [jnp.where(bd, jnp.concatenate([t, t], axis=0), 0.0) for t in v]
        num = each(lambda s_, vb, iw, q_, c_: _bdot(s_, vb) + iw * _bdot(q_, c_), s, v_bd, inter_w, q, cst)
        den = each(lambda s_, iw, q_, n_: seg_sum(s_) + iw * seg_sum(q_ * n_), s, inter_w, q, nst)
        hh = each(lambda nu, de, mt: nu / jnp.maximum(jnp.abs(de), jnp.exp(-mt)), num, den, m_t)
        ms = [seg_sum(t * t) for t in hh]
        for (b, p), h_, ms_ in zip(items, hh, ms):
            o = pm_ref[b, rows, lanes(p, 3)]
            o_ref[b, rows, lanes(p, 0)] = (h_ * lax.rsqrt(ms_ * (1.0 / dh) + RMS_EPS) * nw_ref[:, lanes(p, 0)]
                                           * jax.nn.sigmoid(o)).astype(o_ref.dtype)
        m_new = each(lambda g_, m_, br, lr: jnp.maximum(g_ + m_, seg_max(g_ - br + lr, lo1)), g, m, b_row, li_row)
        decay = each(lambda g_, m_, mn: jnp.exp(g_ + m_ - mn), g, m, m_new)
        w_col = each(lambda g_, bc_, li_, mn: jnp.exp(g_ - bc_ + li_ - mn), g, bc, lic, m_new)
        upd = each(lambda k_, wc, v_: jnp.where(bd, _bdot(k_, wc * v_, _TN), 0.0), k, w_col, v)
        for i, (b, p) in enumerate(items):
            c_ref[b, p] = decay[i] * cst[i] + upd[i]
            n_ref[b, p] = decay[i] * nst[i] + jnp.sum(w_col[i] * k[i], axis=0, keepdims=True)
            m_ref[b, p] = m_new[i]
        return carry

    lax.fori_loop(0, seq // L, chunk, 0)


def _mlstm_group(pm, psm, bsz, i_bias, f_bias, norm_w):
    t = pm.shape[0]
    seq = t // bsz
    w, nh, dh = GROUP_W, GROUP_HEADS, HEAD_DIM
    bias = jnp.pad(jnp.concatenate([i_bias, f_bias]), (0, LANES - 2 * nh))[None, :]
    const = lambda shape: pl.BlockSpec(shape, lambda b, s: (0,) * len(shape))
    nb, ns = MLSTM_ROWS_PER_STEP, MLSTM_SEQ_SPLIT
    assert bsz % nb == 0 and seq % (ns * MLSTM_CHUNK) == 0
    seq_block = lambda n: pl.BlockSpec((nb, seq // ns, n), lambda b, s: (b, s, 0))
    return pl.pallas_call(
        _mlstm_kernel,
        out_shape=jax.ShapeDtypeStruct((bsz, seq, w), MIXER_OUT_DTYPE),
        grid=(bsz // nb, ns),
        in_specs=[seq_block(4 * w), seq_block(LANES), const((1, LANES)), const((1, w))],
        out_specs=seq_block(w),
        scratch_shapes=[pltpu.VMEM((nb, nh // 2, LANES, LANES), jnp.float32),
                        pltpu.VMEM((nb, nh // 2, 1, LANES), jnp.float32),
                        pltpu.VMEM((nb, nh // 2, 1, LANES), jnp.float32)],
        compiler_params=pltpu.CompilerParams(
            dimension_semantics=("parallel", "arbitrary"), vmem_limit_bytes=VMEM_LIMIT),
        name="mlstm",
    )(pm.reshape(bsz, seq, 4 * w), psm.reshape(bsz, seq, LANES), bias, norm_w[None, :]).reshape(t, w)


def _ssd_kernel(ps_ref, sm_ref, cw_ref, cb_ref, dtb_ref, aneg_ref, dsk_ref, nw_ref, o_ref,
                xbc_ref, st_ref, head_ref, *, rc):
    nb, seq, w = o_ref.shape
    nh, hp, ng, ns, L = GROUP_HEADS, HEAD_DIM, SSD_GROUPS, SSD_STATE, SSD_CHUNK
    cx = SSD_XBC
    pad = SUBLANES

    @pl.when(pl.program_id(1) == 0)
    def _():
        st_ref[...] = jnp.zeros_like(st_ref)
        head_ref[:, 0:pad, :] = jnp.zeros((nb, pad, cx), jnp.float32)

    cw = cw_ref[...]
    for b in range(nb):
        for c in range(seq // rc):
            r0 = c * rc
            cur = ps_ref[b, r0:r0 + rc, w:w + cx]
            if c == 0:
                head_ref[b, pad:pad + rc, :] = cur
            acc = cb_ref[...] + cw[CONV_K - 1:CONV_K, :] * cur
            for j in range(1, CONV_K):
                if c == 0:
                    sh = head_ref[b, pad - j:pad - j + rc, :]
                else:
                    sh = ps_ref[b, r0 - j:r0 - j + rc, w:w + cx]
                acc = acc + cw[CONV_K - 1 - j:CONV_K - j, :] * sh
            xbc_ref[b, r0:r0 + rc, :] = jax.nn.silu(acc)
        head_ref[b, 0:pad, :] = ps_ref[b, seq - pad:seq, w:w + cx]
    causal = lax.broadcasted_iota(jnp.int32, (L, L), 0) >= lax.broadcasted_iota(jnp.int32, (L, L), 1)
    gsz = w // ng
    rep = nh // ng
    groups = [(b, g) for b in range(nb) for g in range(ng)]
    heads = [(b, h) for b in range(nb) for h in range(nh)]

    def chunk(c, carry):
        r0 = pl.multiple_of(c * L, L)
        rows = pl.ds(r0, L)
        each = lambda f, *xs: [f(*args) for args in zip(*xs)]
        dt_all = [_softplus(sm_ref[b, rows, :] + dtb_ref[...]) for b in range(nb)]
        acs_all = [_cumsum_rows(aneg_ref[...] * t) for t in dt_all]
        acs_t = [t.T for t in acs_all]
        bm = {bg: xbc_ref[bg[0], rows, w + bg[1] * ns:w + (bg[1] + 1) * ns] for bg in groups}
        cm = {bg: xbc_ref[bg[0], rows, w + (ng + bg[1]) * ns:w + (ng + bg[1] + 1) * ns] for bg in groups}
        cb = {bg: _bdot(cm[bg], bm[bg], _NT) for bg in groups}
        grp = lambda b, h: (b, h // rep)
        cidx = lambda h: 2 * nh + h
        acs_col = [acs_all[b][:, cidx(h):cidx(h) + 1] for b, h in heads]
        acs_row = [acs_t[b][cidx(h):cidx(h) + 1, :] for b, h in heads]
        dt_col = [dt_all[b][:, cidx(h):cidx(h) + 1] for b, h in heads]
        a_last = [t[L - 1:L, :] for t in acs_col]
        xh = [xbc_ref[b, rows, h * hp:(h + 1) * hp] for b, h in heads]
        xdt = each(lambda x_, d_: x_ * d_, xh, dt_col)
        state = [st_ref[b, h] for b, h in heads]
        scores = [cb[grp(b, h)] * jnp.exp(jnp.where(causal, ac - ar, -jnp.inf))
                  for (b, h), ac, ar in zip(heads, acs_col, acs_row)]
        y_diag = each(_bdot, scores, xdt)
        y_off = [_bdot(cm[grp(b, h)], st, _NT) for (b, h), st in zip(heads, state)]
        ys = [yd + jnp.exp(ac) * yo + x_ * dsk_ref[:, h * hp:(h + 1) * hp]
              for (b, h), yd, ac, yo, x_ in zip(heads, y_diag, acs_col, y_off, xh)]
        upd = [_bdot(xd * jnp.exp(al - ac), bm[grp(b, h)], _TN)
               for (b, h), xd, al, ac in zip(heads, xdt, a_last, acs_col)]
        for i, (b, h) in enumerate(heads):
            st_ref[b, h] = jnp.exp(a_last[i]) * state[i] + upd[i]
        ysd = dict(zip(heads, ys))
        ygs = [jnp.concatenate([ysd[(b, g * rep + j)] for j in range(rep)], axis=1)
               * jax.nn.silu(ps_ref[b, rows, g * gsz:(g + 1) * gsz]) for b, g in groups]
        ms = [jnp.mean(t * t, axis=1, keepdims=True) for t in ygs]
        for (b, g), yg, m_ in zip(groups, ygs, ms):
            o_ref[b, rows, g * gsz:(g + 1) * gsz] = (yg * lax.rsqrt(m_ + RMS_EPS)
                                                     * nw_ref[:, g * gsz:(g + 1) * gsz]).astype(o_ref.dtype)
        return carry

    lax.fori_loop(0, seq // L, chunk, 0)


def _ssd_group(ps, psm, bsz, conv_w, conv_b, dt_bias, a_log, d_skip, norm_w):
    t = ps.shape[0]
    seq = t // bsz
    w, nh = GROUP_W, GROUP_HEADS
    lane_vec = lambda v: jnp.pad(v, (2 * nh, LANES - 3 * nh))[None, :]
    dsk = jnp.repeat(d_skip, HEAD_DIM)[None, :]
    const = lambda shape: pl.BlockSpec(shape, lambda b, s: (0,) * len(shape))
    nb, nsl = SSD_ROWS_PER_STEP, SSD_SEQ_SPLIT
    rc = 256
    slab = seq // nsl
    assert bsz % nb == 0 and seq % nsl == 0 and slab % rc == 0 and slab % SSD_CHUNK == 0
    seq_block = lambda n: pl.BlockSpec((nb, slab, n), lambda b, s: (b, s, 0))
    return pl.pallas_call(
        functools.partial(_ssd_kernel, rc=rc),
        out_shape=jax.ShapeDtypeStruct((bsz, seq, w), MIXER_OUT_DTYPE),
        grid=(bsz // nb, nsl),
        in_specs=[seq_block(w + SSD_XBC), seq_block(LANES),
                  const((CONV_K, SSD_XBC)), const((1, SSD_XBC)), const((1, LANES)), const((1, LANES)),
                  const((1, w)), const((1, w))],
        out_specs=seq_block(w),
        scratch_shapes=[pltpu.VMEM((nb, slab, SSD_XBC), jnp.float32),
                        pltpu.VMEM((nb, nh, HEAD_DIM, SSD_STATE), jnp.float32),
                        pltpu.VMEM((nb, SUBLANES + rc, SSD_XBC), jnp.float32)],
        compiler_params=pltpu.CompilerParams(
            dimension_semantics=("parallel", "arbitrary"), vmem_limit_bytes=VMEM_LIMIT),
        name="ssd",
    )(ps.reshape(bsz, seq, w + SSD_XBC), psm.reshape(bsz, seq, LANES), conv_w, conv_b[None, :], lane_vec(dt_bias),
      lane_vec(-jnp.exp(a_log)), dsk, norm_w[None, :]).reshape(t, w)


def _layer_norm(x, g, b):
    xc = x - jnp.mean(x, -1, keepdims=True)
    var = jnp.mean(xc * xc, -1, keepdims=True)
    return (xc * lax.rsqrt(var + LN_EPS)) * g + b


def _outproj_ln_kernel(x_ref, m0_ref, m1_ref, m2_ref, m3_ref, w_ref, g_ref, b_ref, *rest):
    gw = m0_ref.shape[1]
    acc = ALPHA * x_ref[...]
    for j, m_ref in enumerate((m0_ref, m1_ref, m2_ref, m3_ref)):
        acc = acc + _bdot(m_ref[...], w_ref[j * gw:(j + 1) * gw, :])
    x1 = _layer_norm(acc, g_ref[...], b_ref[...])
    if len(rest) == 1:
        rest[0][...] = x1
    else:
        wr_ref, o_ref, lg_ref, xb_ref = rest
        o_ref[...] = x1
        xb = x1.astype(jnp.bfloat16)
        xb_ref[...] = xb
        lg_ref[...] = jnp.dot(xb, wr_ref[...], preferred_element_type=jnp.float32)


def _outproj_ln(x, mixers, w_out, g, b, w_router=None, tm=512):
    t, d = x.shape
    assert t % tm == 0
    gw = mixers[0].shape[1]
    row = lambda n: pl.BlockSpec((tm, n), lambda i: (i, 0))
    const = lambda shape: pl.BlockSpec(shape, lambda i: (0, 0))
    in_specs = [row(d)] + [row(gw)] * 4 + [const((d, d)), const((1, d)), const((1, d))]
    args = [x, *mixers, w_out.astype(jnp.bfloat16), g[None, :], b[None, :]]
    out_shape = [jax.ShapeDtypeStruct((t, d), jnp.float32)]
    out_specs = [row(d)]
    if w_router is not None:
        wr = jnp.pad(w_router, ((0, 0), (0, LANES - w_router.shape[1]))).astype(jnp.bfloat16)
        in_specs.append(const((d, LANES)))
        args.append(wr)
        out_shape += [jax.ShapeDtypeStruct((t, LANES), jnp.float32), jax.ShapeDtypeStruct((t, d), jnp.bfloat16)]
        out_specs += [row(LANES), row(d)]
    return pl.pallas_call(
        _outproj_ln_kernel,
        out_shape=out_shape, grid=(t // tm,), in_specs=in_specs, out_specs=out_specs,
        compiler_params=pltpu.CompilerParams(
            dimension_semantics=("parallel",), vmem_limit_bytes=VMEM_LIMIT),
        name="outproj_ln",
    )(*args)


def _pe_ln_kernel(x_ref, p_ref, wp_ref, wg_ref, bg_ref, g_ref, b_ref, *rest):
    *ff_refs, o_ref = rest
    x1 = x_ref[...]
    pe = _bdot(p_ref[0], wp_ref[...]) * jax.nn.sigmoid(_bdot(x1, wg_ref[...]) + bg_ref[...])
    ff = functools.reduce(lambda a, c: a + c, [r[...] for r in ff_refs])
    o_ref[...] = _layer_norm(ALPHA * x1 + ff + pe, g_ref[...], b_ref[...])


def _pe_ln(x1, ff_parts, p, layer, pe_proj, gate_w, gate_b, g, b, tm=512):
    t, d = x1.shape
    assert t % tm == 0
    pd = p.shape[2]
    row = lambda n: pl.BlockSpec((tm, n), lambda i: (i, 0))
    const = lambda shape: pl.BlockSpec(shape, lambda i: (0, 0))
    return pl.pallas_call(
        _pe_ln_kernel,
        out_shape=jax.ShapeDtypeStruct((t, d), jnp.float32),
        grid=(t // tm,),
        in_specs=[row(d), pl.BlockSpec((1, tm, pd), lambda i: (layer, i, 0)), const((pd, d)), const((d, d)),
                  const((1, d)), const((1, d)), const((1, d))]
        + [row(d)] * len(ff_parts),
        out_specs=row(d),
        compiler_params=pltpu.CompilerParams(
            dimension_semantics=("parallel",), vmem_limit_bytes=VMEM_LIMIT),
        name="pe_ln",
    )(x1, p, pe_proj.astype(jnp.bfloat16), gate_w.astype(jnp.bfloat16), gate_b[None, :], g[None, :], b[None, :],
      *ff_parts)


def _pair_head_sums(x):
    lo = lax.broadcasted_iota(jnp.int32, (x.shape[0], LANES), 1) < HEAD_DIM
    tiles = []
    for hp in range(x.shape[1] // LANES):
        t = x[:, hp * LANES:(hp + 1) * LANES]
        a = jnp.sum(jnp.where(lo, t, 0.0), axis=1, keepdims=True)
        b = jnp.sum(jnp.where(lo, 0.0, t), axis=1, keepdims=True)
        tiles.append(jnp.where(lo, a, b))
    return jnp.concatenate(tiles, axis=1)


def _rwkv_pre_kernel(c_ref, mu_ref, w0_ref, w2_ref, a0_ref, a2_ref, g2_ref, kk_ref, ka_ref, rk_ref,
                     pa_ref, pb_ref, pc_ref, g_ref, bonus_ref, *, rc):
    seq = c_ref.shape[0]
    w, p = GROUP_W, HEAD_DIM
    o1 = 3 * w + RWKV_DECAY_RANK
    o2 = o1 + RWKV_ICL_RANK
    nchunks = seq // rc
    blk = rc // RWKV_TB
    row = lax.broadcasted_iota(jnp.int32, (rc, c_ref.shape[1]), 0)
    lane = lax.broadcasted_iota(jnp.int32, (rc, LANES), 1)

    def unit_keys(cols):
        kk = cols[:, w:2 * w] * kk_ref[...]
        return kk * lax.rsqrt(jnp.maximum(_pair_head_sums(kk * kk), 1e-24))

    def pack(dst, c, x, z):
        for hp in range(w // LANES):
            x2, z2 = x[:, hp * LANES:(hp + 1) * LANES], z[:, hp * LANES:(hp + 1) * LANES]
            even = jnp.where(lane < p, x2, pltpu.roll(z2, p, 1))
            odd = jnp.where(lane < p, pltpu.roll(x2, p, 1), z2)
            for h, val in ((2 * hp, even), (2 * hp + 1, odd)):
                dst[c * blk:(c + 1) * blk, h, 0] = val.reshape(blk, RWKV_TB, LANES)

    for c in range(nchunks):
        r0 = c * rc
        cur = c_ref[r0:r0 + rc, :]
        if c == 0:
            prev = jnp.where(row == 0, 0.0, pltpu.roll(cur, 1, 0))
        else:
            prev = c_ref[r0 - 1:r0 - 1 + rc, :]
        if c == nchunks - 1:
            nxt = jnp.where(row == rc - 1, 0.0, pltpu.roll(cur, rc - 1, 0))
        else:
            nxt = c_ref[r0 + 1:r0 + 1 + rc, :]
        cols = cur + (prev - cur) * mu_ref[...]
        cols_next = nxt + (cur - nxt) * mu_ref[...]
        k = cols[:, w:2 * w]
        w_log = -_softplus(-(w0_ref[...] + _bdot(jnp.tanh(cols[:, 3 * w:o1]), w2_ref[...]))) - 0.5
        a = jax.nn.sigmoid(a0_ref[...] + _bdot(cols[:, o1:o2], a2_ref[...]))
        kk = unit_keys(cols)
        r, v = cols[:, 0:w], cols[:, 2 * w:3 * w]
        k_mod = k * (1.0 + (a - 1.0) * ka_ref[...])
        pack(pa_ref, c, jnp.exp(-jnp.exp(w_log)), kk * a)
        pack(pb_ref, c, k_mod, r)
        pack(pc_ref, c, unit_keys(cols_next), v)
        g_ref[r0:r0 + rc, :] = _bdot(jax.nn.sigmoid(cols[:, o2:]), g2_ref[...])
        bonus_ref[r0:r0 + rc, :] = _pair_head_sums(r * k_mod * rk_ref[...]) * v


def _rwkv_post_kernel(y_ref, bonus_ref, g_ref, lw_ref, lb_ref, o_ref):
    p = HEAD_DIM
    rows = o_ref.shape[0]
    inv = 1.0 / p
    lo = lax.broadcasted_iota(jnp.int32, (rows, LANES), 1) < p
    for hp in range(GROUP_HEADS // 2):
        ls = slice(hp * LANES, (hp + 1) * LANES)
        y = jnp.where(lo, y_ref[:, 2 * hp, 0].reshape(rows, LANES), y_ref[:, 2 * hp + 1, 0].reshape(rows, LANES))
        yc = y - _pair_head_sums(y) * inv
        yn = yc * lax.rsqrt(_pair_head_sums(yc * yc) * inv + RWKV_GN_EPS) * lw_ref[:, ls] + lb_ref[:, ls]
        o_ref[:, ls] = ((yn + bonus_ref[:, ls]) * g_ref[:, ls]).astype(o_ref.dtype)


def _rwkv7_group(pr, bsz, mu, w0, w2, a0, a2, g2, k_k, k_a, r_k, ln_w, ln_b, tm=512):
    t = pr.shape[0]
    seq = t // bsz
    w = GROUP_W
    bf = lambda x: x.astype(jnp.bfloat16)
    const = lambda shape: pl.BlockSpec(shape, lambda b: (0, 0))
    vec = lambda x: x.reshape(1, -1)
    nh, tb = GROUP_HEADS, RWKV_TB
    nblk = seq // tb
    nat = jax.ShapeDtypeStruct((t, w), jnp.float32)
    packed = jax.ShapeDtypeStruct((nblk, nh, bsz, tb, LANES), jnp.float32)
    pspec = pl.BlockSpec((nblk, nh, 1, tb, LANES), lambda b: (0, 0, b, 0, 0))
    rc = 256
    assert seq % rc == 0 and seq % tm == 0 and tm % tb == 0 and rc % tb == 0
    seq_rows = pl.BlockSpec((seq, w), lambda b: (b, 0))
    pa, pb, pc, g, bonus = pl.pallas_call(
        functools.partial(_rwkv_pre_kernel, rc=rc),
        out_shape=[packed] * 3 + [nat] * 2,
        grid=(bsz,),
        in_specs=[pl.BlockSpec((seq, R_COLS), lambda b: (b, 0)), const((1, R_COLS)), const((1, w)),
                  const((RWKV_DECAY_RANK, w)), const((1, w)), const((RWKV_ICL_RANK, w)),
                  const((RWKV_GATE_RANK, w)), const((1, w)), const((1, w)), const((1, w))],
        out_specs=[pspec] * 3 + [seq_rows] * 2,
        compiler_params=pltpu.CompilerParams(
            dimension_semantics=("parallel",), vmem_limit_bytes=VMEM_LIMIT),
        name="rwkv_pre",
    )(pr, vec(mu), vec(w0), bf(w2), vec(a0), bf(a2), bf(g2), vec(k_k), vec(k_a), vec(r_k))
    y = _rwkv_scan(pa, pb, pc, bsz)
    gb = tm // tb
    per_b = seq // tm
    bspec = pl.BlockSpec((gb, nh, 1, tb, LANES), lambda b, i: (i, 0, b, 0, 0))
    row = pl.BlockSpec((tm, w), lambda b, i: (b * per_b + i, 0))
    const2 = lambda shape: pl.BlockSpec(shape, lambda b, i: (0, 0))
    return pl.pallas_call(
        _rwkv_post_kernel,
        out_shape=jax.ShapeDtypeStruct((t, w), MIXER_OUT_DTYPE),
        grid=(bsz, per_b),
        in_specs=[bspec, row, row] + [const2((1, w))] * 2,
        out_specs=row,
        compiler_params=pltpu.CompilerParams(
            dimension_semantics=("parallel", "parallel"), vmem_limit_bytes=VMEM_LIMIT),
        name="rwkv_post",
    )(y, bonus, g, vec(ln_w), vec(ln_b))


FF_CHUNK = 512


def _swiglu_kernel(be_ref, live_ref, x_ref, w1_ref, w3_ref, w2_ref, g_ref, o_ref):
    del be_ref
    live = live_ref[pl.program_id(0)] != 0

    @pl.when(live)
    def _():
        xb = x_ref[...].astype(jnp.bfloat16)
        dff = w1_ref.shape[-1]
        acc = None
        for c0 in range(0, dff, FF_CHUNK):
            c1 = min(c0 + FF_CHUNK, dff)
            h1 = jnp.dot(xb, w1_ref[0, :, c0:c1], preferred_element_type=jnp.float32)
            h3 = jnp.dot(xb, w3_ref[0, :, c0:c1], preferred_element_type=jnp.float32)
            h = (jax.nn.silu(h1) * h3).astype(jnp.bfloat16)
            part = jnp.dot(h, w2_ref[0, c0:c1, :], preferred_element_type=jnp.float32)
            acc = part if acc is None else acc + part
        o_ref[...] = acc * g_ref[...]

    @pl.when(jnp.logical_not(live))
    def _():
        o_ref[...] = jnp.zeros_like(o_ref)


def _grouped_swiglu(xrows, block_e, block_live, row_gate, w1, w3, w2, bm):
    rows, d = xrows.shape
    dff = w1.shape[-1]
    assert rows % bm == 0
    grid_spec = pltpu.PrefetchScalarGridSpec(
        num_scalar_prefetch=2,
        grid=(rows // bm,),
        in_specs=[pl.BlockSpec((bm, d), lambda i, be, lv: (i, 0)),
                  pl.BlockSpec((1, d, dff), lambda i, be, lv: (be[i], 0, 0)),
                  pl.BlockSpec((1, d, dff), lambda i, be, lv: (be[i], 0, 0)),
                  pl.BlockSpec((1, dff, d), lambda i, be, lv: (be[i], 0, 0)),
                  pl.BlockSpec((bm, 1), lambda i, be, lv: (i, 0))],
        out_specs=pl.BlockSpec((bm, d), lambda i, be, lv: (i, 0)),
    )
    return pl.pallas_call(
        _swiglu_kernel,
        out_shape=jax.ShapeDtypeStruct((rows, d), jnp.float32),
        grid_spec=grid_spec,
        compiler_params=pltpu.CompilerParams(
            dimension_semantics=("arbitrary",), vmem_limit_bytes=VMEM_LIMIT),
        cost_estimate=pl.CostEstimate(
            flops=6 * rows * d * dff, transcendentals=rows * dff,
            bytes_accessed=rows * d * (xrows.dtype.itemsize + 4) + rows * 4 + 2 * (w1.size + w3.size + w2.size)),
        name="swiglu",
    )(block_e, block_live, xrows, w1, w3, w2, row_gate)


def _cast_kernel(x_ref, o_ref):
    o_ref[0] = x_ref[0, 0].astype(o_ref.dtype)


def _layer_weights_bf16(w, layer):
    _, ne, r, c = w.shape
    return pl.pallas_call(
        _cast_kernel,
        out_shape=jax.ShapeDtypeStruct((ne, r, c), jnp.bfloat16),
        grid=(ne,),
        in_specs=[pl.BlockSpec((1, 1, r, c), lambda e: (layer, e, 0, 0))],
        out_specs=pl.BlockSpec((1, r, c), lambda e: (e, 0, 0)),
        compiler_params=pltpu.CompilerParams(
            dimension_semantics=("parallel",), vmem_limit_bytes=VMEM_LIMIT),
        cost_estimate=pl.CostEstimate(flops=0, transcendentals=0, bytes_accessed=ne * r * c * (4 + 2)),
        name="cast_bf16",
    )(w)


def _swiglu(x2d, w1, w3, w2, layer, bm=512):
    rows = x2d.shape[0]
    bf = lambda t: _layer_weights_bf16(t[:, None], layer)
    nblk = rows // bm
    return _grouped_swiglu(x2d, jnp.zeros((nblk,), jnp.int32), jnp.ones((nblk,), jnp.int32),
                           jnp.ones((rows, 1), jnp.float32), bf(w1), bf(w3), bf(w2), bm)


def _moe_swiglu(xf, logits, w1, w3, w2, layer):
    T, d = xf.shape
    i32 = jnp.int32
    lt = logits.T
    eids = jnp.arange(N_EXPERTS, dtype=i32)[:, None]
    e0 = jnp.argmax(lt, axis=0).astype(i32)
    m0 = jnp.max(lt, axis=0)
    lt1 = jnp.where(eids == e0[None, :], -jnp.inf, lt)
    e1 = jnp.argmax(lt1, axis=0).astype(i32)
    m1 = jnp.max(lt1, axis=0)
    ex = jnp.exp(m1 - m0)
    gates = (1.0 / (1.0 + ex), ex / (1.0 + ex))
    oh = [(eids == e[None, :]).astype(i32) for e in (e0, e1)]
    both = oh[0] + oh[1]
    incl = jnp.cumsum(both, axis=1)
    counts = incl[:, -1]
    padded = (counts + MOE_BLOCK - 1) // MOE_BLOCK * MOE_BLOCK
    pad_end = jnp.cumsum(padded)
    pad_start = pad_end - padded
    start = jnp.cumsum(counts) - counts
    offs = (incl - both) + pad_start[:, None]
    pos = [jnp.sum(o * offs, axis=0) for o in oh]
    tok = jnp.arange(T, dtype=i32)
    _, st, sg = lax.sort((jnp.concatenate(pos), jnp.concatenate([tok, tok]), jnp.concatenate(gates)), num_keys=1)
    n_blocks = -(-(T * TOP_K) // MOE_BLOCK) + N_EXPERTS
    blk = jnp.arange(n_blocks, dtype=i32)
    block_e = jnp.minimum(jnp.searchsorted(pad_end, blk * MOE_BLOCK, side='right'), N_EXPERTS - 1).astype(i32)
    q = (blk * MOE_BLOCK - pad_start[block_e])[:, None] + jnp.arange(MOE_BLOCK, dtype=i32)[None, :]
    valid = q < counts[block_e][:, None]
    src = jnp.clip(start[block_e][:, None] + q, 0, T * TOP_K - 1)
    slot_tok = jnp.where(valid, st[src], 0).reshape(-1)
    slot_gate = jnp.where(valid, sg[src], 0.0).reshape(-1, 1)
    bf = lambda t: _layer_weights_bf16(t, layer)
    block_live = valid[:, 0].astype(i32)
    yb = _grouped_swiglu(xf[slot_tok], block_e, block_live, slot_gate, bf(w1), bf(w3), bf(w2), MOE_BLOCK)
    return [yb[pos[0]], yb[pos[1]]]


def kernel(x, p, w_in, m_i_bias, m_f_bias, m_norm_w, g_conv_w, g_conv_b, g_w_a, g_b_a, g_w_x, g_b_x,
           g_lambda, r_mu, r_w0, r_w2, r_a0, r_a2, r_g2, r_k_k, r_k_a, r_r_k, r_ln_w, r_ln_b,
           s_conv_w, s_conv_b, s_dt_bias, s_a_log, s_d, s_norm_w, w_out, ln1_g, ln1_b, ln2_g, ln2_b,
           f_w1, f_w3, f_w2, e_router, e_w1, e_w3, e_w2, pe_proj, pe_gate_w, pe_gate_b):
    bsz, seq, d = x.shape
    T = bsz * seq
    x = x.reshape(T, d)
    for i in range(DEPTH):
        pm, pg, pr, ps, psm = _inproj(x, _split_w_in(w_in, i))
        mixers = [
            _mlstm_group(pm, psm, bsz, m_i_bias[i], m_f_bias[i], m_norm_w[i]),
            _rglru_group(pg, bsz, g_conv_w[i], g_conv_b[i], g_w_a[i], g_b_a[i], g_w_x[i], g_b_x[i], g_lambda[i]),
            _rwkv7_group(pr, bsz, r_mu[i], r_w0[i], r_w2[i], r_a0[i], r_a2[i], r_g2[i],
                         r_k_k[i], r_k_a[i], r_r_k[i], r_ln_w[i], r_ln_b[i]),
            _ssd_group(ps, psm, bsz, s_conv_w[i], s_conv_b[i], s_dt_bias[i], s_a_log[i], s_d[i], s_norm_w[i]),
        ]
        if i % 2 == 0:
            (x,) = _outproj_ln(x, mixers, w_out[i], ln1_g[i], ln1_b[i])
            ff = [_swiglu(x, f_w1, f_w3, f_w2, i // 2)]
        else:
            x, logits, x_bf16 = _outproj_ln(x, mixers, w_out[i], ln1_g[i], ln1_b[i], e_router[i // 2])
            ff = _moe_swiglu(x_bf16, logits[:, :N_EXPERTS], e_w1, e_w3, e_w2, i // 2)
        x = _pe_ln(x, ff, p.reshape(DEPTH, T, PE_DIM), i, pe_proj[i], pe_gate_w[i], pe_gate_b[i], ln2_g[i], ln2_b[i])
    return x.reshape(bsz, seq, d)
```
